```python
import jax, jax.numpy as jnp
from jax import lax
import numpy as np

D_MODEL = 1024
BATCH = 4
SEQ = 4096
DEPTH = 1

GRID_W = 64
CTX_LEN = 256
HEAD_DIM = 64
N_HEADS = 16
N_KV_HEADS = 4
GROUP = N_HEADS // N_KV_HEADS
Q_W = N_HEADS * HEAD_DIM
KV_W = N_KV_HEADS * HEAD_DIM
WINDOW = 128
BLOCK = 128
ROPE_THETA = 10000.0
POOL_WINDOWS = (2, 4, 8, 16)
POOL_GROUPS = 4
POOL_W = D_MODEL // 2
POOL_GROUP_W = POOL_W // POOL_GROUPS
N_BRANCHES = 2
REST_START = Q_W + 2 * KV_W
IN_W = REST_START + POOL_W + N_BRANCHES * D_MODEL
D_FF = ((8 * D_MODEL // 3 + 255) // 256) * 256
N_MOD = 6
EPS = 1e-6

kernel_name = 'hybrid_pool_swa_flow_block'


def rms_norm(x, g):
    xf = x.astype(jnp.float32)
    y = xf * lax.rsqrt(jnp.mean(xf * xf, axis=-1, keepdims=True) + EPS)
    return (y * g.astype(jnp.float32)).astype(x.dtype)


def modulate(x, shift, scale):
    return x * (1 + scale) + shift


def heads(t, n):
    return t.reshape(*t.shape[:-1], n, HEAD_DIM)


def axial_rope_tables(seq):
    rows = seq // GRID_W
    row, col = jnp.meshgrid(jnp.arange(rows), jnp.arange(GRID_W), indexing='ij')
    row = row.reshape(-1).astype(jnp.float32)
    col = col.reshape(-1).astype(jnp.float32)
    half = HEAD_DIM // 2
    inv_freq = 1.0 / (ROPE_THETA ** (jnp.arange(0, half, 2, dtype=jnp.float32) / half))
    ang = jnp.stack([row[:, None] * inv_freq, col[:, None] * inv_freq], axis=1)
    return jnp.cos(ang), jnp.sin(ang)


def apply_axial_rope(x, cos, sin):
    B, S, H, _ = x.shape
    xa = x.astype(jnp.float32).reshape(B, S, H, 2, HEAD_DIM // 2)
    x1, x2 = jnp.split(xa, 2, axis=-1)
    c = cos[None, :, None]
    s = sin[None, :, None]
    out = jnp.concatenate([x1 * c - x2 * s, x2 * c + x1 * s], axis=-1)
    return out.reshape(B, S, H, HEAD_DIM).astype(x.dtype)


def band_mask(nb):
    i = jnp.arange(BLOCK)[:, None]
    j = jnp.arange(3 * BLOCK)[None, :]
    rel = j - BLOCK - i
    kpos = (jnp.arange(nb)[:, None, None] - 1) * BLOCK + j[None]
    return (jnp.abs(rel)[None] <= WINDOW) & (kpos >= 0) & (kpos < nb * BLOCK)


def window_attention(q, k, v, kc, vc, sink):
    B, S = q.shape[:2]
    nb = S // BLOCK
    scale = HEAD_DIM ** -0.5
    qb = q.reshape(B, nb, BLOCK, N_KV_HEADS, GROUP, HEAD_DIM)

    def neighbours(t):
        tp = jnp.pad(t, ((0, 0), (BLOCK, BLOCK), (0, 0), (0, 0)))
        tp = tp.reshape(B, nb + 2, BLOCK, N_KV_HEADS, HEAD_DIM)
        return jnp.concatenate([tp[:, :-2], tp[:, 1:-1], tp[:, 2:]], axis=2)

    kw, vw = neighbours(k), neighbours(v)
    s_win = jnp.einsum('bnqkgd,bnjkd->bkgnqj', qb, kw, preferred_element_type=jnp.float32) * scale
    s_win = jnp.where(band_mask(nb)[None, None, None], s_win, -jnp.inf)
    s_ctx = jnp.einsum('bnqkgd,bckd->bkgnqc', qb, kc, preferred_element_type=jnp.float32) * scale
    sink_l = sink.astype(jnp.float32).reshape(1, N_KV_HEADS, GROUP, 1, 1, 1)
    m = jnp.maximum(jnp.maximum(s_win.max(-1, keepdims=True), s_ctx.max(-1, keepdims=True)), sink_l)
    p_win = jnp.exp(s_win - m)
    p_ctx = jnp.exp(s_ctx - m)
    den = p_win.sum(-1, keepdims=True) + p_ctx.sum(-1, keepdims=True) + jnp.exp(sink_l - m)
    o = (jnp.einsum('bkgnqj,bnjkd->bkgnqd', p_win, vw.astype(jnp.float32))
         + jnp.einsum('bkgnqc,bckd->bkgnqd', p_ctx, vc.astype(jnp.float32))) / den
    o = o.transpose(0, 3, 4, 1, 2, 5).reshape(B, S, Q_W)
    return o.astype(q.dtype)


def context_attention(qc, kc, vc, sink):
    B, C = qc.shape[:2]
    qg = qc.reshape(B, C, N_KV_HEADS, GROUP, HEAD_DIM)
    s = jnp.einsum('bqkgd,bckd->bkgqc', qg, kc, preferred_element_type=jnp.float32) * HEAD_DIM ** -0.5
    sink_col = jnp.broadcast_to(sink.astype(jnp.float32).reshape(1, N_KV_HEADS, GROUP, 1, 1), s.shape[:-1] + (1,))
    p = jax.nn.softmax(jnp.concatenate([s, sink_col], axis=-1), axis=-1)[..., :C]
    o = jnp.einsum('bkgqc,bckd->bqkgd', p, vc.astype(jnp.float32))
    return o.reshape(B, C, Q_W).astype(qc.dtype)


def multiscale_pool(u, pool_w, pool_scale):
    B, S, _ = u.shape
    ug = u.astype(jnp.float32).reshape(B, S, POOL_GROUPS, POOL_GROUP_W)
    cs = jnp.pad(jnp.cumsum(ug, axis=1), ((0, 0), (1, 0), (0, 0), (0, 0)))
    t = jnp.arange(S)
    pooled = []
    for g, w in enumerate(POOL_WINDOWS):
        lo = jnp.clip(t - w // 2, 0, S)
        hi = jnp.clip(t + w // 2, 0, S)
        cs_g = cs[:, :, g, :]
        win_sum = cs_g[:, hi] - cs_g[:, lo]
        pooled.append(win_sum / (hi - lo).astype(jnp.float32)[None, :, None])
    diff = jnp.stack(pooled, axis=2) - ug
    mixed = jnp.einsum('bsgc,gcd->bsgd', diff, pool_w.astype(jnp.float32))
    return (mixed.reshape(B, S, POOL_W) * pool_scale).astype(u.dtype)


def merge_branches(attn, rest, gate_b, pool_w, pool_scale, w_attn_proj, w_pool_proj, w_out):
    pool_in, gate_logits = rest[..., :POOL_W], rest[..., POOL_W:]
    a = attn @ w_attn_proj
    p = multiscale_pool(pool_in, pool_w, pool_scale) @ w_pool_proj
    ga, gp = jnp.split(jax.nn.sigmoid(gate_logits + gate_b), N_BRANCHES, axis=-1)
    return (ga * a + gp * p) @ w_out


def swiglu_sublayer(x, shift, scale, gate, g, w_up, w_down):
    h = modulate(rms_norm(x, g), shift, scale)
    a, b = jnp.split(h @ w_up, 2, axis=-1)
    return x + gate * ((jax.nn.silu(a) * b) @ w_down)


def setup_inputs(seed: int = 0) -> dict:
    key = jax.random.key(seed)
    ks = jax.random.split(key, 20)

    def nrm(k, shape, s):
        return jax.random.normal(k, shape, jnp.float32) * s

    D, L = D_MODEL, DEPTH
    return {
        'x': nrm(ks[0], (BATCH, SEQ, D), 1.0),
        'c': nrm(ks[1], (BATCH, D), 1.0),
        'ctx': nrm(ks[2], (BATCH, CTX_LEN, D), 1.0),
        'c_ctx': nrm(ks[3], (D,), 1.0),
        'mod_w': nrm(ks[4], (L, D, N_MOD * D), 0.5 * D ** -0.5),
        'mod_b': nrm(ks[5], (L, N_MOD * D), 0.01),
        'norm1_g': 1.0 + nrm(ks[6], (L, D), 0.05),
        'norm2_g': 1.0 + nrm(ks[7], (L, D), 0.05),
        'w_in': nrm(ks[8], (L, D, IN_W), D ** -0.5),
        'gate_b': nrm(ks[9], (L, N_BRANCHES * D), 0.02),
        'q_norm_g': 1.0 + nrm(ks[10], (L, HEAD_DIM), 0.05),
        'k_norm_g': 1.0 + nrm(ks[11], (L, HEAD_DIM), 0.05),
        'sink': nrm(ks[12], (L, N_HEADS), 0.5),
        'pool_w': nrm(ks[13], (L, POOL_GROUPS, POOL_GROUP_W, POOL_GROUP_W), POOL_GROUP_W ** -0.5),
        'pool_scale': 1.0 + nrm(ks[14], (L, POOL_W), 0.1),
        'w_attn_proj': nrm(ks[15], (L, Q_W, D), Q_W ** -0.5),
        'w_pool_proj': nrm(ks[16], (L, POOL_W, D), POOL_W ** -0.5),
        'w_out': nrm(ks[17], (L, D, D), D ** -0.5),
        'w_up': nrm(ks[18], (L, D, 2 * D_FF), D ** -0.5),
        'w_down': nrm(ks[19], (L, D_FF, D), D_FF ** -0.5),
    }


def reference(x, c, ctx, c_ctx, mod_w, mod_b, norm1_g, norm2_g, w_in, gate_b, q_norm_g, k_norm_g,
              sink, pool_w, pool_scale, w_attn_proj, w_pool_proj, w_out, w_up, w_down):
    cos, sin = axial_rope_tables(x.shape[1])
    for l in range(DEPTH):
        mod_x = jax.nn.silu(c) @ mod_w[l] + mod_b[l]
        mod_c = jax.nn.silu(c_ctx) @ mod_w[l] + mod_b[l]
        sh1, sc1, g1, sh2, sc2, g2 = jnp.split(mod_x[:, None, :], N_MOD, axis=-1)
        csh1, csc1, cg1, csh2, csc2, cg2 = jnp.split(mod_c, N_MOD)

        hc = modulate(rms_norm(ctx, norm1_g[l]), csh1, csc1)
        kv_c = hc @ w_in[l][:, Q_W:REST_START]
        kc = rms_norm(heads(kv_c[..., :KV_W], N_KV_HEADS), k_norm_g[l])
        vc = heads(kv_c[..., KV_W:], N_KV_HEADS)

        h = modulate(rms_norm(x, norm1_g[l]), sh1, sc1)
        proj = h @ w_in[l]
        q = apply_axial_rope(rms_norm(heads(proj[..., :Q_W], N_HEADS), q_norm_g[l]), cos, sin)
        k = apply_axial_rope(rms_norm(heads(proj[..., Q_W:Q_W + KV_W], N_KV_HEADS), k_norm_g[l]), cos, sin)
        v = heads(proj[..., Q_W + KV_W:REST_START], N_KV_HEADS)
        attn = window_attention(q, k, v, kc, vc, sink[l])
        mix = merge_branches(attn, proj[..., REST_START:], gate_b[l], pool_w[l], pool_scale[l],
                             w_attn_proj[l], w_pool_proj[l], w_out[l])
        x_next = x + g1 * mix
        x_next = swiglu_sublayer(x_next, sh2, sc2, g2, norm2_g[l], w_up[l], w_down[l])

        if l + 1 < DEPTH:
            q_c = rms_norm(heads(hc @ w_in[l][:, :Q_W], N_HEADS), q_norm_g[l])
            attn_c = context_attention(q_c, kc, vc, sink[l])
            mix_c = merge_branches(attn_c, hc @ w_in[l][:, REST_START:], gate_b[l], pool_w[l], pool_scale[l],
                                   w_attn_proj[l], w_pool_proj[l], w_out[l])
            ctx = ctx + cg1 * mix_c
            ctx = swiglu_sublayer(ctx, csh2, csc2, cg2, norm2_g[l], w_up[l], w_down[l])
        x = x_next
    return x
```

```python
import functools

import jax
import jax.numpy as jnp
import numpy as np
from jax import lax
from jax.experimental import pallas as pl
from jax.experimental.pallas import tpu as pltpu

D_MODEL = 1024
GRID_W = 64
HEAD_DIM = 64
N_HEADS = 16
N_KV_HEADS = 4
GROUP = N_HEADS // N_KV_HEADS
Q_W = N_HEADS * HEAD_DIM
KV_W = N_KV_HEADS * HEAD_DIM
WINDOW = 128
BLOCK = 128
ROPE_THETA = 10000.0
POOL_WINDOWS = (2, 4, 8, 16)
POOL_GROUPS = 4
POOL_W = D_MODEL // 2
POOL_GROUP_W = POOL_W // POOL_GROUPS
POOL_HALO = 8
REST_START = Q_W + 2 * KV_W
N_MOD = 6
EPS = 1e-6
NEG_BIG = -1e30

LANES = 128
MXU_TILE = 256
VMEM_LIMIT = 56 * 1024 * 1024

F32 = jnp.float32
BF16 = jnp.bfloat16


def _params(n_parallel):
    return pltpu.CompilerParams(
        dimension_semantics=("parallel",) * n_parallel,
        vmem_limit_bytes=VMEM_LIMIT,
    )


def _const_spec(shape):
    nd = len(shape)
    return pl.BlockSpec(shape, lambda *_: (0,) * nd, pipeline_mode=pl.Buffered(1))


def _dot(a, b):
    return jnp.dot(a, b, preferred_element_type=F32)


def _dot_nt(a, b):
    return lax.dot_general(a, b, (((1,), (1,)), ((), ())), preferred_element_type=F32)


def _rms_modulate(x, gs, sh):
    rs = lax.rsqrt(jnp.mean(x * x, axis=-1, keepdims=True) + EPS)
    return x * rs * gs + sh


def _head_rms(t, ones_bd):
    ms = _dot((t * t).astype(BF16), ones_bd) * (1.0 / HEAD_DIM)
    return lax.rsqrt(ms + EPS)


def _mod_kernel(c_ref, w_ref, b_ref, o_ref):
    c = c_ref[...]
    a = (c * jax.nn.sigmoid(c)).astype(BF16)
    o_ref[...] = _dot(a, w_ref[...].astype(BF16)) + b_ref[...]


def _mod_call(c8, mod_w, mod_b):
    n = mod_w.shape[1]
    bn = D_MODEL
    return pl.pallas_call(
        _mod_kernel,
        grid=(n // bn,),
        in_specs=[
            pl.BlockSpec((8, D_MODEL), lambda j: (0, 0)),
            pl.BlockSpec((D_MODEL, bn), lambda j: (0, j)),
            pl.BlockSpec((1, bn), lambda j: (0, j)),
        ],
        out_specs=pl.BlockSpec((8, bn), lambda j: (0, j)),
        out_shape=jax.ShapeDtypeStruct((8, n), F32),
        compiler_params=_params(1),
        name="mod",
    )(c8, mod_w, mod_b)


def _ctx_kernel(ctx_ref, mod_ref, g1_ref, w_ref, kg_ref, ones_ref, kc_ref, vc_ref):
    m = mod_ref[0]
    sh = m[:, 0:D_MODEL]
    gs = g1_ref[...] * (1.0 + m[:, D_MODEL:2 * D_MODEL])
    h = _rms_modulate(ctx_ref[0], gs, sh).astype(BF16)
    kv = _dot(h, w_ref[...])
    k = kv[:, :KV_W]
    kc_ref[0] = (k * _head_rms(k, ones_ref[...]) * kg_ref[...]).astype(BF16)
    vc_ref[0] = kv[:, KV_W:].astype(BF16)


def _ctx_call(ctx, mod3, norm1_g, w_kv, kg256, ones_bd):
    B, C, D = ctx.shape
    n_mod = mod3.shape[2]
    return pl.pallas_call(
        _ctx_kernel,
        grid=(B,),
        in_specs=[
            pl.BlockSpec((1, C, D), lambda b: (b, 0, 0)),
            pl.BlockSpec((1, 1, n_mod), lambda b: (B, 0, 0)),
            _const_spec((1, D)),
            _const_spec((D, 2 * KV_W)),
            _const_spec((1, KV_W)),
            _const_spec((MXU_TILE, MXU_TILE)),
        ],
        out_specs=[
            pl.BlockSpec((1, C, KV_W), lambda b: (b, 0, 0)),
            pl.BlockSpec((1, C, KV_W), lambda b: (b, 0, 0)),
        ],
        out_shape=[
            jax.ShapeDtypeStruct((B, C, KV_W), BF16),
            jax.ShapeDtypeStruct((B, C, KV_W), BF16),
        ],
        compiler_params=_params(1),
        name="ctx_kv",
    )(ctx, mod3, norm1_g, w_kv, kg256, ones_bd)


def _rope(t, cos, sin_signed, first_half):
    swap = jnp.where(first_half, pltpu.roll(t, LANES - 16, axis=1), pltpu.roll(t, 16, axis=1))
    return t * cos + swap * sin_signed


def _inproj_kernel(x_ref, mod_ref, g1_ref, w_ref, qg_ref, kg_ref, ones_ref, cos_ref, sin_ref,
                   q_ref, k_ref, v_ref, u_ref):
    m = mod_ref[0]
    sh = m[:, 0:D_MODEL]
    gs = g1_ref[...] * (1.0 + m[:, D_MODEL:2 * D_MODEL])
    h = _rms_modulate(x_ref[0], gs, sh).astype(BF16)
    ones_bd = ones_ref[...]
    cos = cos_ref[...]
    sin = sin_ref[...]
    lane = lax.broadcasted_iota(jnp.int32, cos.shape, 1)
    first_half = (lane & 31) < 16

    def normed_rope(col0, gain):
        t = _dot(h, w_ref[:, col0:col0 + MXU_TILE])
        t = t * _head_rms(t, ones_bd) * gain
        halves = [_rope(t[:, i * LANES:(i + 1) * LANES], cos, sin, first_half) for i in range(2)]
        return jnp.concatenate(halves, axis=1).astype(BF16)

    for j in range(Q_W // MXU_TILE):
        q_ref[0, :, j * MXU_TILE:(j + 1) * MXU_TILE] = normed_rope(j * MXU_TILE, qg_ref[...])
    k_ref[0] = normed_rope(Q_W, kg_ref[...])
    v_ref[0] = _dot(h, w_ref[:, Q_W + KV_W:REST_START]).astype(BF16)
    for j in range(POOL_W // MXU_TILE):
        c0 = REST_START + j * MXU_TILE
        u_ref[0, :, j * MXU_TILE:(j + 1) * MXU_TILE] = _dot(h, w_ref[:, c0:c0 + MXU_TILE])


def _inproj_call(x, mod3, norm1_g, w1, qg256, kg256, ones_bd, cos_t, sin_t, tile):
    B, S, D = x.shape
    n_mod = mod3.shape[2]
    n1 = w1.shape[1]
    row = lambda b, t: (b, t, 0)
    return pl.pallas_call(
        _inproj_kernel,
        grid=(B, S // tile),
        in_specs=[
            pl.BlockSpec((1, tile, D), row),
            pl.BlockSpec((1, 1, n_mod), lambda b, t: (b, 0, 0)),
            _const_spec((1, D)),
            _const_spec((D, n1)),
            _const_spec((1, MXU_TILE)),
            _const_spec((1, MXU_TILE)),
            _const_spec((MXU_TILE, MXU_TILE)),
            pl.BlockSpec((tile, LANES), lambda b, t: (t, 0)),
            pl.BlockSpec((tile, LANES), lambda b, t: (t, 0)),
        ],
        out_specs=[
            pl.BlockSpec((1, tile, Q_W), row),
            pl.BlockSpec((1, tile, KV_W), row),
            pl.BlockSpec((1, tile, KV_W), row),
            pl.BlockSpec((1, tile, POOL_W), row),
        ],
        out_shape=[
            jax.ShapeDtypeStruct((B, S, Q_W), BF16),
            jax.ShapeDtypeStruct((B, S, KV_W), BF16),
            jax.ShapeDtypeStruct((B, S, KV_W), BF16),
            jax.ShapeDtypeStruct((B, S, POOL_W), F32),
        ],
        compiler_params=_params(2),
        name="in_proj",
    )(x, mod3, norm1_g, w1, qg256, kg256, ones_bd, cos_t, sin_t)


def _attn_kernel(sink_ref, q_ref, kp_ref, kc0_ref, kn_ref, vp_ref, vc0_ref, vn_ref,
                 kctx_ref, vctx_ref, bias_ref, o_ref, *, n_blocks):
    n = pl.program_id(1)
    variant = jnp.where(n == 0, 1, jnp.where(n == n_blocks - 1, 2, 0))
    bias = bias_ref[variant]
    k_win = jnp.concatenate([kp_ref[0], kc0_ref[0], kn_ref[0]], axis=0)
    v_win = jnp.concatenate([vp_ref[0], vc0_ref[0], vn_ref[0]], axis=0)
    k_ctx = kctx_ref[0]
    v_ctx = vctx_ref[0]
    lane = lax.broadcasted_iota(jnp.int32, (BLOCK, MXU_TILE), 1)
    zero = jnp.zeros((), BF16)

    for j in range(GROUP):
        q_slab = q_ref[0, :, j * MXU_TILE:(j + 1) * MXU_TILE]
        acc = jnp.zeros((BLOCK, MXU_TILE), F32)
        for kv in range(N_KV_HEADS):
            in_head = (lane >= kv * HEAD_DIM) & (lane < (kv + 1) * HEAD_DIM)
            qz = jnp.where(in_head, q_slab, zero)
            s_win = _dot_nt(qz, k_win) + bias
            s_ctx = _dot_nt(qz, k_ctx)
            sink = sink_ref[kv * GROUP + j]
            m = jnp.maximum(
                jnp.maximum(jnp.max(s_win, axis=-1, keepdims=True),
                            jnp.max(s_ctx, axis=-1, keepdims=True)), sink)
            p_win = jnp.exp(s_win - m)
            p_ctx = jnp.exp(s_ctx - m)
            den = (jnp.sum(p_win, axis=-1, keepdims=True) + jnp.sum(p_ctx, axis=-1, keepdims=True)
                   + jnp.exp(sink - m))
            o = _dot(p_win.astype(BF16), v_win) + _dot(p_ctx.astype(BF16), v_ctx)
            acc = acc + jnp.where(in_head, o / den, 0.0)
        o_ref[0, :, j * MXU_TILE:(j + 1) * MXU_TILE] = acc.astype(BF16)


def _attn_call(sink, q, k, v, kc, vc, bias):
    B, S, _ = q.shape
    nb = S // BLOCK
    C = kc.shape[1]
    cur = lambda b, n, *_: (b, n, 0)
    prev = lambda b, n, *_: (b, jnp.maximum(n - 1, 0), 0)
    nxt = lambda b, n, *_: (b, jnp.minimum(n + 1, nb - 1), 0)
    per_b = lambda b, n, *_: (b, 0, 0)
    kv_blk = (1, BLOCK, KV_W)
    grid_spec = pltpu.PrefetchScalarGridSpec(
        num_scalar_prefetch=1,
        grid=(B, nb),
        in_specs=[
            pl.BlockSpec((1, BLOCK, Q_W), cur),
            pl.BlockSpec(kv_blk, prev), pl.BlockSpec(kv_blk, cur), pl.BlockSpec(kv_blk, nxt),
            pl.BlockSpec(kv_blk, prev), pl.BlockSpec(kv_blk, cur), pl.BlockSpec(kv_blk, nxt),
            pl.BlockSpec((1, C, KV_W), per_b),
            pl.BlockSpec((1, C, KV_W), per_b),
            pl.BlockSpec((3, BLOCK, 3 * BLOCK), lambda b, n, *_: (0, 0, 0)),
        ],
        out_specs=pl.BlockSpec((1, BLOCK, Q_W), cur),
    )
    return pl.pallas_call(
        functools.partial(_attn_kernel, n_blocks=nb),
        grid_spec=grid_spec,
        out_shape=jax.ShapeDtypeStruct((B, S, Q_W), BF16),
        compiler_params=_params(2),
        name="attn",
    )(sink, q, k, k, k, v, v, v, kc, vc, bias)


def _merge_kernel(x_ref, mod_ref, g1_ref, attn_ref, up_ref, uc_ref, un_ref,
                  wg_ref, gb_ref, wap_ref, pw_ref, ps_ref, wpp_ref, wo_ref, o_ref, *, seq, tile):
    t_idx = pl.program_id(1)
    n_tiles = pl.num_programs(1)
    x = x_ref[0]
    m = mod_ref[0]
    sh = m[:, 0:D_MODEL]
    gs = g1_ref[...] * (1.0 + m[:, D_MODEL:2 * D_MODEL])
    gate1 = m[:, 2 * D_MODEL:3 * D_MODEL]
    h = _rms_modulate(x, gs, sh).astype(BF16)

    keep_prev = (t_idx > 0).astype(F32)
    keep_next = (t_idx < n_tiles - 1).astype(F32)
    u_c = uc_ref[0]
    u_ext = jnp.concatenate([up_ref[0] * keep_prev, u_c, un_ref[0] * keep_next], axis=0)
    pos = t_idx * tile + lax.broadcasted_iota(jnp.int32, (tile, 1), 0)
    mixed = []
    for g, w in enumerate(POOL_WINDOWS):
        cols = slice(g * POOL_GROUP_W, (g + 1) * POOL_GROUP_W)
        ug = u_ext[:, cols]
        win = ug[POOL_HALO - w // 2:POOL_HALO - w // 2 + tile]
        for j in range(1 - w // 2, w // 2):
            win = win + ug[POOL_HALO + j:POOL_HALO + j + tile]
        cnt = (jnp.minimum(pos + w // 2, seq) - jnp.maximum(pos - w // 2, 0)).astype(F32)
        diff = win / cnt - u_c[:, cols]
        mixed.append(_dot(diff.astype(BF16), pw_ref[g]))
    pooled = (jnp.concatenate(mixed, axis=1) * ps_ref[...]).astype(BF16)
    p = _dot(pooled, wpp_ref[...])

    a = _dot(attn_ref[0], wap_ref[...])
    gates = jax.nn.sigmoid(_dot(h, wg_ref[...]) + gb_ref[...])
    merged = (gates[:, :D_MODEL] * a + gates[:, D_MODEL:] * p).astype(BF16)
    o_ref[0] = x + gate1 * _dot(merged, wo_ref[...])


def _merge_call(x, mod3, norm1_g, attn, u, w_gate, gate_b, w_ap, pool_w, pool_scale, w_pp, w_out, tile):
    B, S, D = x.shape
    n_mod = mod3.shape[2]
    r = tile // POOL_HALO
    n_halo = S // POOL_HALO
    row = lambda b, t: (b, t, 0)
    return pl.pallas_call(
        functools.partial(_merge_kernel, seq=S, tile=tile),
        grid=(B, S // tile),
        in_specs=[
            pl.BlockSpec((1, tile, D), row),
            pl.BlockSpec((1, 1, n_mod), lambda b, t: (b, 0, 0)),
            _const_spec((1, D)),
            pl.BlockSpec((1, tile, Q_W), row),
            pl.BlockSpec((1, POOL_HALO, POOL_W), lambda b, t: (b, jnp.maximum(t * r - 1, 0), 0)),
            pl.BlockSpec((1, tile, POOL_W), row),
            pl.BlockSpec((1, POOL_HALO, POOL_W), lambda b, t: (b, jnp.minimum((t + 1) * r, n_halo - 1), 0)),
            _const_spec(w_gate.shape),
            _const_spec(gate_b.shape),
            _const_spec(w_ap.shape),
            _const_spec(pool_w.shape),
            _const_spec(pool_scale.shape),
            _const_spec(w_pp.shape),
            _const_spec(w_out.shape),
        ],
        out_specs=pl.BlockSpec((1, tile, D), row),
        out_shape=jax.ShapeDtypeStruct((B, S, D), F32),
        compiler_params=_params(2),
        name="merge",
    )(x, mod3, norm1_g, attn, u, u, u, w_gate, gate_b, w_ap, pool_w, pool_scale, w_pp, w_out)


def _ffn_kernel(x_ref, mod_ref, g2_ref, wup_ref, wdn_ref, o_ref, hmid_ref, *, d_ff):
    x = x_ref[0]
    m = mod_ref[0]
    sh = m[:, 3 * D_MODEL:4 * D_MODEL]
    gs = g2_ref[...] * (1.0 + m[:, 4 * D_MODEL:5 * D_MODEL])
    gate2 = m[:, 5 * D_MODEL:6 * D_MODEL]
    h = _rms_modulate(x, gs, sh).astype(BF16)
    for c in range(d_ff // MXU_TILE):
        a = _dot(h, wup_ref[:, c * MXU_TILE:(c + 1) * MXU_TILE])
        b = _dot(h, wup_ref[:, d_ff + c * MXU_TILE:d_ff + (c + 1) * MXU_TILE])
        hmid_ref[:, c * MXU_TILE:(c + 1) * MXU_TILE] = (a * jax.nn.sigmoid(a) * b).astype(BF16)
    o_ref[0] = x + gate2 * _dot(hmid_ref[...], wdn_ref[...])


def _ffn_call(x, mod3, norm2_g, w_up, w_down, tile):
    B, S, D = x.shape
    n_mod = mod3.shape[2]
    d_ff = w_down.shape[0]
    row = lambda b, t: (b, t, 0)
    return pl.pallas_call(
        functools.partial(_ffn_kernel, d_ff=d_ff),
        grid=(B, S // tile),
        in_specs=[
            pl.BlockSpec((1, tile, D), row),
            pl.BlockSpec((1, 1, n_mod), lambda b, t: (b, 0, 0)),
            _const_spec((1, D)),
            _const_spec(w_up.shape),
            _const_spec(w_down.shape),
        ],
        out_specs=pl.BlockSpec((1, tile, D), row),
        out_shape=jax.ShapeDtypeStruct((B, S, D), F32),
        scratch_shapes=[pltpu.VMEM((tile, d_ff), BF16)],
        compiler_params=_params(2),
        name="ffn",
    )(x, mod3, norm2_g, w_up, w_down)


def _rope_tables(seq):
    t = jnp.arange(seq)
    row = (t // GRID_W).astype(F32)
    col = (t % GRID_W).astype(F32)
    half = HEAD_DIM // 2
    inv_freq = 1.0 / (ROPE_THETA ** (jnp.arange(0, half, 2, dtype=F32) / half))
    ang_r = row[:, None] * inv_freq
    ang_c = col[:, None] * inv_freq
    cos = jnp.concatenate([jnp.cos(ang_r)] * 2 + [jnp.cos(ang_c)] * 2, axis=1)
    sin = jnp.concatenate([-jnp.sin(ang_r), jnp.sin(ang_r), -jnp.sin(ang_c), jnp.sin(ang_c)], axis=1)
    return jnp.tile(cos, (1, LANES // HEAD_DIM)), jnp.tile(sin, (1, LANES // HEAD_DIM))


def _band_bias():
    i = np.arange(BLOCK)[:, None]
    j = np.arange(3 * BLOCK)[None, :]
    band = np.abs(j - BLOCK - i) <= WINDOW
    first = band & (j >= BLOCK)
    last = band & (j < 2 * BLOCK)
    masks = np.stack([band, first, last])
    return jnp.asarray(np.where(masks, 0.0, NEG_BIG), dtype=F32)


def _head_perm():
    idx = np.empty(Q_W, dtype=np.int32)
    for j in range(GROUP):
        for kv in range(N_KV_HEADS):
            new0 = (j * N_KV_HEADS + kv) * HEAD_DIM
            old0 = (kv * GROUP + j) * HEAD_DIM
            idx[new0:new0 + HEAD_DIM] = np.arange(old0, old0 + HEAD_DIM)
    return idx


def kernel(x, c, ctx, c_ctx, mod_w, mod_b, norm1_g, norm2_g, w_in, gate_b, q_norm_g, k_norm_g,
           sink, pool_w, pool_scale, w_attn_proj, w_pool_proj, w_out, w_up, w_down):
    B, S, D = x.shape
    depth = mod_w.shape[0]
    perm = _head_perm()
    cos_t, sin_t = _rope_tables(S)
    bias = _band_bias()
    ones_bd = jnp.asarray(np.kron(np.eye(MXU_TILE // HEAD_DIM), np.ones((HEAD_DIM, HEAD_DIM))), dtype=BF16)
    reps = MXU_TILE // HEAD_DIM
    assert depth == 1, "context-stream update between layers is not implemented"

    for l in range(depth):
        c8 = jnp.concatenate([c, c_ctx[None, :], jnp.zeros((8 - B - 1, D), F32)], axis=0)
        mod = _mod_call(c8, mod_w[l], mod_b[l][None, :])
        mod3 = mod[:, None, :]

        wl = w_in[l].astype(BF16)
        w_q = wl[:, :Q_W][:, perm]
        w1 = jnp.concatenate([w_q, wl[:, Q_W:REST_START + POOL_W]], axis=1)
        w_gate = wl[:, REST_START + POOL_W:]
        g1 = norm1_g[l][None, :]
        kg256 = jnp.tile(k_norm_g[l], reps)[None, :]
        qg256 = jnp.tile(q_norm_g[l] * (HEAD_DIM ** -0.5), reps)[None, :]

        kc, vc = _ctx_call(ctx, mod3, g1, wl[:, Q_W:REST_START], kg256, ones_bd)
        q, k, v, u = _inproj_call(x, mod3, g1, w1, qg256, kg256, ones_bd, cos_t, sin_t, tile=512)
        attn = _attn_call(sink[l], q, k, v, kc, vc, bias)
        x = _merge_call(
            x, mod3, g1, attn, u, w_gate, gate_b[l][None, :],
            w_attn_proj[l][perm, :].astype(BF16), pool_w[l].astype(BF16), pool_scale[l][None, :],
            w_pool_proj[l].astype(BF16), w_out[l].astype(BF16), tile=256)
        x = _ffn_call(x, mod3, norm2_g[l][None, :], w_up[l].astype(BF16), w_down[l].astype(BF16), tile=512)
    return x
```

```python
import functools

import jax
import jax.numpy as jnp
import numpy as np
from jax import lax
from jax.experimental import pallas as pl
from jax.experimental.pallas import tpu as pltpu

D_MODEL = 1024
GRID_W = 64
HEAD_DIM = 64
N_HEADS = 16
N_KV_HEADS = 4
GROUP = N_HEADS // N_KV_HEADS
Q_W = N_HEADS * HEAD_DIM
KV_W = N_KV_HEADS * HEAD_DIM
WINDOW = 128
BLOCK = 128
ROPE_THETA = 10000.0
POOL_WINDOWS = (2, 4, 8, 16)
POOL_GROUPS = 4
POOL_W = D_MODEL // 2
POOL_GROUP_W = POOL_W // POOL_GROUPS
POOL_HALO = 8
REST_START = Q_W + 2 * KV_W
N_MOD = 6
EPS = 1e-6
NEG_BIG = -1e30

LANES = 128
MXU_TILE = 256
VMEM_LIMIT = 56 * 1024 * 1024

F32 = jnp.float32
BF16 = jnp.bfloat16


def _params(n_parallel):
    return pltpu.CompilerParams(
        dimension_semantics=("parallel",) * n_parallel,
        vmem_limit_bytes=VMEM_LIMIT,
    )


def _const_spec(shape):
    nd = len(shape)
    return pl.BlockSpec(shape, lambda *_: (0,) * nd, pipeline_mode=pl.Buffered(1))


def _dot(a, b):
    return jnp.dot(a, b, preferred_element_type=F32)


def _dot_nt(a, b):
    return lax.dot_general(a, b, (((1,), (1,)), ((), ())), preferred_element_type=F32)


def _rms_modulate(x, gs, sh):
    rs = lax.rsqrt(jnp.mean(x * x, axis=-1, keepdims=True) + EPS)
    return x * rs * gs + sh


def _head_rms(t, ones_bd):
    ms = _dot((t * t).astype(BF16), ones_bd) * (1.0 / HEAD_DIM)
    return lax.rsqrt(ms + EPS)


def _mod_kernel(c_ref, w_ref, b_ref, o_ref):
    c = c_ref[...]
    a = (c * jax.nn.sigmoid(c)).astype(BF16)
    o_ref[...] = _dot(a, w_ref[...].astype(BF16)) + b_ref[...]


def _mod_call(c8, mod_w, mod_b):
    n = mod_w.shape[1]
    bn = D_MODEL
    return pl.pallas_call(
        _mod_kernel,
        grid=(n // bn,),
        in_specs=[
            pl.BlockSpec((8, D_MODEL), lambda j: (0, 0)),
            pl.BlockSpec((D_MODEL, bn), lambda j: (0, j)),
            pl.BlockSpec((1, bn), lambda j: (0, j)),
        ],
        out_specs=pl.BlockSpec((8, bn), lambda j: (0, j)),
        out_shape=jax.ShapeDtypeStruct((8, n), F32),
        compiler_params=_params(1),
        name="mod",
    )(c8, mod_w, mod_b)


def _ctx_kernel(ctx_ref, mod_ref, g1_ref, w_ref, kg_ref, ones_ref, kc_ref, vc_ref):
    m = mod_ref[0]
    sh = m[:, 0:D_MODEL]
    gs = g1_ref[...] * (1.0 + m[:, D_MODEL:2 * D_MODEL])
    h = _rms_modulate(ctx_ref[0], gs, sh).astype(BF16)
    kv = _dot(h, w_ref[...])
    k = kv[:, :KV_W]
    kc_ref[0] = (k * _head_rms(k, ones_ref[...]) * kg_ref[...]).astype(BF16)
    vc_ref[0] = kv[:, KV_W:].astype(BF16)


def _ctx_call(ctx, mod3, norm1_g, w_kv, kg256, ones_bd):
    B, C, D = ctx.shape
    n_mod = mod3.shape[2]
    return pl.pallas_call(
        _ctx_kernel,
        grid=(B,),
        in_specs=[
            pl.BlockSpec((1, C, D), lambda b: (b, 0, 0)),
            pl.BlockSpec((1, 1, n_mod), lambda b: (B, 0, 0)),
            _const_spec((1, D)),
            _const_spec((D, 2 * KV_W)),
            _const_spec((1, KV_W)),
            _const_spec((MXU_TILE, MXU_TILE)),
        ],
        out_specs=[
            pl.BlockSpec((1, C, KV_W), lambda b: (b, 0, 0)),
            pl.BlockSpec((1, C, KV_W), lambda b: (b, 0, 0)),
        ],
        out_shape=[
            jax.ShapeDtypeStruct((B, C, KV_W), BF16),
            jax.ShapeDtypeStruct((B, C, KV_W), BF16),
        ],
        compiler_params=_params(1),
        name="ctx_kv",
    )(ctx, mod3, norm1_g, w_kv, kg256, ones_bd)


def _rope(t, cos, sin_signed, first_half):
    swap = jnp.where(first_half, pltpu.roll(t, LANES - 16, axis=1), pltpu.roll(t, 16, axis=1))
    return t * cos + swap * sin_signed


def _inproj_kernel(x_ref, mod_ref, g1_ref, w_ref, qg_ref, kg_ref, ones_ref, cos_ref, sin_ref,
                   q_ref, k_ref, v_ref, u_ref):
    m = mod_ref[0]
    sh = m[:, 0:D_MODEL]
    gs = g1_ref[...] * (1.0 + m[:, D_MODEL:2 * D_MODEL])
    h = _rms_modulate(x_ref[0], gs, sh).astype(BF16)
    ones_bd = ones_ref[...]
    cos = cos_ref[...]
    sin = sin_ref[...]
    lane = lax.broadcasted_iota(jnp.int32, cos.shape, 1)
    first_half = (lane & 31) < 16

    def normed_rope(col0, gain):
        t = _dot(h, w_ref[:, col0:col0 + MXU_TILE])
        t = t * _head_rms(t, ones_bd) * gain
        halves = [_rope(t[:, i * LANES:(i + 1) * LANES], cos, sin, first_half) for i in range(2)]
        return jnp.concatenate(halves, axis=1).astype(BF16)

    for j in range(Q_W // MXU_TILE):
        q_ref[0, :, j * MXU_TILE:(j + 1) * MXU_TILE] = normed_rope(j * MXU_TILE, qg_ref[...])
    k_ref[0] = normed_rope(Q_W, kg_ref[...])
    v_ref[0] = _dot(h, w_ref[:, Q_W + KV_W:REST_START]).astype(BF16)
    for j in range(POOL_W // MXU_TILE):
        c0 = REST_START + j * MXU_TILE
        u_ref[0, :, j * MXU_TILE:(j + 1) * MXU_TILE] = _dot(h, w_ref[:, c0:c0 + MXU_TILE])


def _inproj_call(x, mod3, norm1_g, w1, qg256, kg256, ones_bd, cos_t, sin_t, tile):
    B, S, D = x.shape
    n_mod = mod3.shape[2]
    n1 = w1.shape[1]
    row = lambda b, t: (b, t, 0)
    return pl.pallas_call(
        _inproj_kernel,
        grid=(B, S // tile),
        in_specs=[
            pl.BlockSpec((1, tile, D), row),
            pl.BlockSpec((1, 1, n_mod), lambda b, t: (b, 0, 0)),
            _const_spec((1, D)),
            _const_spec((D, n1)),
            _const_spec((1, MXU_TILE)),
            _const_spec((1, MXU_TILE)),
            _const_spec((MXU_TILE, MXU_TILE)),
            pl.BlockSpec((tile, LANES), lambda b, t: (t, 0)),
            pl.BlockSpec((tile, LANES), lambda b, t: (t, 0)),
        ],
        out_specs=[
            pl.BlockSpec((1, tile, Q_W), row),
            pl.BlockSpec((1, tile, KV_W), row),
            pl.BlockSpec((1, tile, KV_W), row),
            pl.BlockSpec((1, tile, POOL_W), row),
        ],
        out_shape=[
            jax.ShapeDtypeStruct((B, S, Q_W), BF16),
            jax.ShapeDtypeStruct((B, S, KV_W), BF16),
            jax.ShapeDtypeStruct((B, S, KV_W), BF16),
            jax.ShapeDtypeStruct((B, S, POOL_W), F32),
        ],
        compiler_params=_params(2),
        name="in_proj",
    )(x, mod3, norm1_g, w1, qg256, kg256, ones_bd, cos_t, sin_t)


def _attn_kernel(sink_ref, q_ref, kp_ref, kc0_ref, kn_ref, vp_ref, vc0_ref, vn_ref,
                 kctx_ref, vctx_ref, bias_ref, o_ref, *, n_blocks):
    n = pl.program_id(1)
    variant = jnp.where(n == 0, 1, jnp.where(n == n_blocks - 1, 2, 0))
    bias = bias_ref[variant]
    k_win = jnp.concatenate([kp_ref[0], kc0_ref[0], kn_ref[0]], axis=0)
    v_win = jnp.concatenate([vp_ref[0], vc0_ref[0], vn_ref[0]], axis=0)
    k_ctx = kctx_ref[0]
    v_ctx = vctx_ref[0]
    lane = lax.broadcasted_iota(jnp.int32, (BLOCK, MXU_TILE), 1)
    in_head = [(lane >= kv * HEAD_DIM) & (lane < (kv + 1) * HEAD_DIM) for kv in range(N_KV_HEADS)]
    zero = jnp.zeros((), BF16)

    for j in range(GROUP):
        q_slab = q_ref[0, :, j * MXU_TILE:(j + 1) * MXU_TILE]
        qz = jnp.concatenate([jnp.where(in_head[kv], q_slab, zero) for kv in range(N_KV_HEADS)], axis=0)
        s_win = _dot_nt(qz, k_win)
        s_ctx = _dot_nt(qz, k_ctx)
        p_win, p_ctx, inv_den = [], [], []
        for kv in range(N_KV_HEADS):
            rows = slice(kv * BLOCK, (kv + 1) * BLOCK)
            sw = s_win[rows] + bias
            sc = s_ctx[rows]
            sink = sink_ref[kv * GROUP + j]
            m = jnp.maximum(
                jnp.maximum(jnp.max(sw, axis=-1, keepdims=True),
                            jnp.max(sc, axis=-1, keepdims=True)), sink)
            pw = jnp.exp(sw - m)
            pc = jnp.exp(sc - m)
            den = (jnp.sum(pw, axis=-1, keepdims=True) + jnp.sum(pc, axis=-1, keepdims=True)
                   + jnp.exp(sink - m))
            p_win.append(pw.astype(BF16))
            p_ctx.append(pc.astype(BF16))
            inv_den.append(1.0 / den)
        o = (_dot(jnp.concatenate(p_win, axis=0), v_win)
             + _dot(jnp.concatenate(p_ctx, axis=0), v_ctx))
        acc = jnp.zeros((BLOCK, MXU_TILE), F32)
        for kv in range(N_KV_HEADS):
            acc = acc + jnp.where(in_head[kv], o[kv * BLOCK:(kv + 1) * BLOCK] * inv_den[kv], 0.0)
        o_ref[0, :, j * MXU_TILE:(j + 1) * MXU_TILE] = acc.astype(BF16)


def _attn_call(sink, q, k, v, kc, vc, bias):
    B, S, _ = q.shape
    nb = S // BLOCK
    C = kc.shape[1]
    cur = lambda b, n, *_: (b, n, 0)
    prev = lambda b, n, *_: (b, jnp.maximum(n - 1, 0), 0)
    nxt = lambda b, n, *_: (b, jnp.minimum(n + 1, nb - 1), 0)
    per_b = lambda b, n, *_: (b, 0, 0)
    kv_blk = (1, BLOCK, KV_W)
    grid_spec = pltpu.PrefetchScalarGridSpec(
        num_scalar_prefetch=1,
        grid=(B, nb),
        in_specs=[
            pl.BlockSpec((1, BLOCK, Q_W), cur),
            pl.BlockSpec(kv_blk, prev), pl.BlockSpec(kv_blk, cur), pl.BlockSpec(kv_blk, nxt),
            pl.BlockSpec(kv_blk, prev), pl.BlockSpec(kv_blk, cur), pl.BlockSpec(kv_blk, nxt),
            pl.BlockSpec((1, C, KV_W), per_b),
            pl.BlockSpec((1, C, KV_W), per_b),
            pl.BlockSpec((3, BLOCK, 3 * BLOCK), lambda b, n, *_: (0, 0, 0)),
        ],
        out_specs=pl.BlockSpec((1, BLOCK, Q_W), cur),
    )
    return pl.pallas_call(
        functools.partial(_attn_kernel, n_blocks=nb),
        grid_spec=grid_spec,
        out_shape=jax.ShapeDtypeStruct((B, S, Q_W), BF16),
        compiler_params=_params(2),
        name="attn",
    )(sink, q, k, k, k, v, v, v, kc, vc, bias)


def _merge_kernel(x_ref, mod_ref, g1_ref, attn_ref, up_ref, uc_ref, un_ref,
                  wg_ref, gb_ref, wap_ref, pw_ref, ps_ref, wpp_ref, wo_ref, o_ref, *, seq, tile):
    t_idx = pl.program_id(1)
    n_tiles = pl.num_programs(1)
    x = x_ref[0]
    m = mod_ref[0]
    sh = m[:, 0:D_MODEL]
    gs = g1_ref[...] * (1.0 + m[:, D_MODEL:2 * D_MODEL])
    gate1 = m[:, 2 * D_MODEL:3 * D_MODEL]
    h = _rms_modulate(x, gs, sh).astype(BF16)

    keep_prev = (t_idx > 0).astype(F32)
    keep_next = (t_idx < n_tiles - 1).astype(F32)
    u_c = uc_ref[0]
    u_ext = jnp.concatenate([up_ref[0] * keep_prev, u_c, un_ref[0] * keep_next], axis=0)
    pos = t_idx * tile + lax.broadcasted_iota(jnp.int32, (tile, 1), 0)
    mixed = []
    for g, w in enumerate(POOL_WINDOWS):
        cols = slice(g * POOL_GROUP_W, (g + 1) * POOL_GROUP_W)
        ug = u_ext[:, cols]
        win = ug[POOL_HALO - w // 2:POOL_HALO - w // 2 + tile]
        for j in range(1 - w // 2, w // 2):
            win = win + ug[POOL_HALO + j:POOL_HALO + j + tile]
        cnt = (jnp.minimum(pos + w // 2, seq) - jnp.maximum(pos - w // 2, 0)).astype(F32)
        diff = win / cnt - u_c[:, cols]
        mixed.append(_dot(diff.astype(BF16), pw_ref[g]))
    pooled = (jnp.concatenate(mixed, axis=1) * ps_ref[...]).astype(BF16)
    p = _dot(pooled, wpp_ref[...])

    a = _dot(attn_ref[0], wap_ref[...])
    gates = jax.nn.sigmoid(_dot(h, wg_ref[...]) + gb_ref[...])
    merged = (gates[:, :D_MODEL] * a + gates[:, D_MODEL:] * p).astype(BF16)
    o_ref[0] = x + gate1 * _dot(merged, wo_ref[...])


def _merge_call(x, mod3, norm1_g, attn, u, w_gate, gate_b, w_ap, pool_w, pool_scale, w_pp, w_out, tile):
    B, S, D = x.shape
    n_mod = mod3.shape[2]
    r = tile // POOL_HALO
    n_halo = S // POOL_HALO
    row = lambda b, t: (b, t, 0)
    return pl.pallas_call(
        functools.partial(_merge_kernel, seq=S, tile=tile),
        grid=(B, S // tile),
        in_specs=[
            pl.BlockSpec((1, tile, D), row),
            pl.BlockSpec((1, 1, n_mod), lambda b, t: (b, 0, 0)),
            _const_spec((1, D)),
            pl.BlockSpec((1, tile, Q_W), row),
            pl.BlockSpec((1, POOL_HALO, POOL_W), lambda b, t: (b, jnp.maximum(t * r - 1, 0), 0)),
            pl.BlockSpec((1, tile, POOL_W), row),
            pl.BlockSpec((1, POOL_HALO, POOL_W), lambda b, t: (b, jnp.minimum((t + 1) * r, n_halo - 1), 0)),
            _const_spec(w_gate.shape),
            _const_spec(gate_b.shape),
            _const_spec(w_ap.shape),
            _const_spec(pool_w.shape),
            _const_spec(pool_scale.shape),
            _const_spec(w_pp.shape),
            _const_spec(w_out.shape),
        ],
        out_specs=pl.BlockSpec((1, tile, D), row),
        out_shape=jax.ShapeDtypeStruct((B, S, D), F32),
        compiler_params=_params(2),
        name="merge",
    )(x, mod3, norm1_g, attn, u, u, u, w_gate, gate_b, w_ap, pool_w, pool_scale, w_pp, w_out)


def _ffn_kernel(x_ref, mod_ref, g2_ref, wup_ref, wdn_ref, o_ref, hmid_ref, *, d_ff):
    x = x_ref[0]
    m = mod_ref[0]
    sh = m[:, 3 * D_MODEL:4 * D_MODEL]
    gs = g2_ref[...] * (1.0 + m[:, 4 * D_MODEL:5 * D_MODEL])
    gate2 = m[:, 5 * D_MODEL:6 * D_MODEL]
    h = _rms_modulate(x, gs, sh).astype(BF16)
    for c in range(d_ff // MXU_TILE):
        a = _dot(h, wup_ref[:, c * MXU_TILE:(c + 1) * MXU_TILE])
        b = _dot(h, wup_ref[:, d_ff + c * MXU_TILE:d_ff + (c + 1) * MXU_TILE])
        hmid_ref[:, c * MXU_TILE:(c + 1) * MXU_TILE] = (a * jax.nn.sigmoid(a) * b).astype(BF16)
    o_ref[0] = x + gate2 * _dot(hmid_ref[...], wdn_ref[...])


def _ffn_call(x, mod3, norm2_g, w_up, w_down, tile):
    B, S, D = x.shape
    n_mod = mod3.shape[2]
    d_ff = w_down.shape[0]
    row = lambda b, t: (b, t, 0)
    return pl.pallas_call(
        functools.partial(_ffn_kernel, d_ff=d_ff),
        grid=(B, S // tile),
        in_specs=[
            pl.BlockSpec((1, tile, D), row),
            pl.BlockSpec((1, 1, n_mod), lambda b, t: (b, 0, 0)),
            _const_spec((1, D)),
            _const_spec(w_up.shape),
            _const_spec(w_down.shape),
        ],
        out_specs=pl.BlockSpec((1, tile, D), row),
        out_shape=jax.ShapeDtypeStruct((B, S, D), F32),
        scratch_shapes=[pltpu.VMEM((tile, d_ff), BF16)],
        compiler_params=_params(2),
        name="ffn",
    )(x, mod3, norm2_g, w_up, w_down)


def _rope_tables(seq):
    t = jnp.arange(seq)
    row = (t // GRID_W).astype(F32)
    col = (t % GRID_W).astype(F32)
    half = HEAD_DIM // 2
    inv_freq = 1.0 / (ROPE_THETA ** (jnp.arange(0, half, 2, dtype=F32) / half))
    ang_r = row[:, None] * inv_freq
    ang_c = col[:, None] * inv_freq
    cos = jnp.concatenate([jnp.cos(ang_r)] * 2 + [jnp.cos(ang_c)] * 2, axis=1)
    sin = jnp.concatenate([-jnp.sin(ang_r), jnp.sin(ang_r), -jnp.sin(ang_c), jnp.sin(ang_c)], axis=1)
    return jnp.tile(cos, (1, LANES // HEAD_DIM)), jnp.tile(sin, (1, LANES // HEAD_DIM))


def _band_bias():
    i = np.arange(BLOCK)[:, None]
    j = np.arange(3 * BLOCK)[None, :]
    band = np.abs(j - BLOCK - i) <= WINDOW
    first = band & (j >= BLOCK)
    last = band & (j < 2 * BLOCK)
    masks = np.stack([band, first, last])
    return jnp.asarray(np.where(masks, 0.0, NEG_BIG), dtype=F32)


def _head_perm():
    idx = np.empty(Q_W, dtype=np.int32)
    for j in range(GROUP):
        for kv in range(N_KV_HEADS):
            new0 = (j * N_KV_HEADS + kv) * HEAD_DIM
            old0 = (kv * GROUP + j) * HEAD_DIM
            idx[new0:new0 + HEAD_DIM] = np.arange(old0, old0 + HEAD_DIM)
    return idx


def kernel(x, c, ctx, c_ctx, mod_w, mod_b, norm1_g, norm2_g, w_in, gate_b, q_norm_g, k_norm_g,
           sink, pool_w, pool_scale, w_attn_proj, w_pool_proj, w_out, w_up, w_down):
    B, S, D = x.shape
    depth = mod_w.shape[0]
    perm = _head_perm()
    cos_t, sin_t = _rope_tables(S)
    bias = _band_bias()
    ones_bd = jnp.asarray(np.kron(np.eye(MXU_TILE // HEAD_DIM), np.ones((HEAD_DIM, HEAD_DIM))), dtype=BF16)
    reps = MXU_TILE // HEAD_DIM
    assert depth == 1, "context-stream update between layers is not implemented"

    for l in range(depth):
        c8 = jnp.concatenate([c, c_ctx[None, :], jnp.zeros((8 - B - 1, D), F32)], axis=0)
        mod = _mod_call(c8, mod_w[l], mod_b[l][None, :])
        mod3 = mod[:, None, :]

        wl = w_in[l].astype(BF16)
        w_q = wl[:, :Q_W][:, perm]
        w1 = jnp.concatenate([w_q, wl[:, Q_W:REST_START + POOL_W]], axis=1)
        w_gate = wl[:, REST_START + POOL_W:]
        g1 = norm1_g[l][None, :]
        kg256 = jnp.tile(k_norm_g[l], reps)[None, :]
        qg256 = jnp.tile(q_norm_g[l] * (HEAD_DIM ** -0.5), reps)[None, :]

        kc, vc = _ctx_call(ctx, mod3, g1, wl[:, Q_W:REST_START], kg256, ones_bd)
        q, k, v, u = _inproj_call(x, mod3, g1, w1, qg256, kg256, ones_bd, cos_t, sin_t, tile=512)
        attn = _attn_call(sink[l], q, k, v, kc, vc, bias)
        x = _merge_call(
            x, mod3, g1, attn, u, w_gate, gate_b[l][None, :],
            w_attn_proj[l][perm, :].astype(BF16), pool_w[l].astype(BF16), pool_scale[l][None, :],
            w_pool_proj[l].astype(BF16), w_out[l].astype(BF16), tile=256)
        x = _ffn_call(x, mod3, norm2_g[l][None, :], w_up[l].astype(BF16), w_down[l].astype(BF16), tile=512)
    return x
```

```python
import functools

import jax
import jax.numpy as jnp
import numpy as np
from jax import lax
from jax.experimental import pallas as pl
from jax.experimental.pallas import tpu as pltpu

D_MODEL = 1024
GRID_W = 64
HEAD_DIM = 64
N_HEADS = 16
N_KV_HEADS = 4
GROUP = N_HEADS // N_KV_HEADS
Q_W = N_HEADS * HEAD_DIM
KV_W = N_KV_HEADS * HEAD_DIM
WINDOW = 128
BLOCK = 128
ROPE_THETA = 10000.0
POOL_WINDOWS = (2, 4, 8, 16)
POOL_GROUPS = 4
POOL_W = D_MODEL // 2
POOL_GROUP_W = POOL_W // POOL_GROUPS
POOL_HALO = 8
REST_START = Q_W + 2 * KV_W
N_MOD = 6
EPS = 1e-6
NEG_BIG = -1e30

LANES = 128
MXU_TILE = 256
VMEM_LIMIT = 56 * 1024 * 1024

F32 = jnp.float32
BF16 = jnp.bfloat16


def _params(n_parallel):
    return pltpu.CompilerParams(
        dimension_semantics=("parallel",) * n_parallel,
        vmem_limit_bytes=VMEM_LIMIT,
    )


def _const_spec(shape):
    nd = len(shape)
    return pl.BlockSpec(shape, lambda *_: (0,) * nd, pipeline_mode=pl.Buffered(1))


def _dot(a, b):
    return jnp.dot(a, b, preferred_element_type=F32)


def _dot_nt(a, b):
    return lax.dot_general(a, b, (((1,), (1,)), ((), ())), preferred_element_type=F32)


def _rms_modulate(x, gs, sh):
    rs = lax.rsqrt(jnp.mean(x * x, axis=-1, keepdims=True) + EPS)
    return x * rs * gs + sh


def _head_rms(t, ones_bd):
    ms = _dot((t * t).astype(BF16), ones_bd) * (1.0 / HEAD_DIM)
    return lax.rsqrt(ms + EPS)


def _mod_kernel(c_ref, w_ref, b_ref, o_ref):
    c = c_ref[...]
    a = (c * jax.nn.sigmoid(c)).astype(BF16)
    o_ref[...] = _dot(a, w_ref[...].astype(BF16)) + b_ref[...]


def _mod_call(c8, mod_w, mod_b):
    n = mod_w.shape[1]
    bn = D_MODEL
    return pl.pallas_call(
        _mod_kernel,
        grid=(n // bn,),
        in_specs=[
            pl.BlockSpec((8, D_MODEL), lambda j: (0, 0)),
            pl.BlockSpec((D_MODEL, bn), lambda j: (0, j)),
            pl.BlockSpec((1, bn), lambda j: (0, j)),
        ],
        out_specs=pl.BlockSpec((8, bn), lambda j: (0, j)),
        out_shape=jax.ShapeDtypeStruct((8, n), F32),
        compiler_params=_params(1),
        name="mod",
    )(c8, mod_w, mod_b)


def _ctx_kernel(ctx_ref, mod_ref, g1_ref, w_ref, kg_ref, ones_ref, kc_ref, vc_ref):
    m = mod_ref[0]
    sh = m[:, 0:D_MODEL]
    gs = g1_ref[...] * (1.0 + m[:, D_MODEL:2 * D_MODEL])
    h = _rms_modulate(ctx_ref[0], gs, sh).astype(BF16)
    kv = _dot(h, w_ref[...])
    k = kv[:, :KV_W]
    kc_ref[0] = (k * _head_rms(k, ones_ref[...]) * kg_ref[...]).astype(BF16)
    vc_ref[0] = kv[:, KV_W:].astype(BF16)


def _ctx_call(ctx, mod3, norm1_g, w_kv, kg256, ones_bd):
    B, C, D = ctx.shape
    n_mod = mod3.shape[2]
    return pl.pallas_call(
        _ctx_kernel,
        grid=(B,),
        in_specs=[
            pl.BlockSpec((1, C, D), lambda b: (b, 0, 0)),
            pl.BlockSpec((1, 1, n_mod), lambda b: (B, 0, 0)),
            _const_spec((1, D)),
            _const_spec((D, 2 * KV_W)),
            _const_spec((1, KV_W)),
            _const_spec((MXU_TILE, MXU_TILE)),
        ],
        out_specs=[
            pl.BlockSpec((1, C, KV_W), lambda b: (b, 0, 0)),
            pl.BlockSpec((1, C, KV_W), lambda b: (b, 0, 0)),
        ],
        out_shape=[
            jax.ShapeDtypeStruct((B, C, KV_W), BF16),
            jax.ShapeDtypeStruct((B, C, KV_W), BF16),
        ],
        compiler_params=_params(1),
        name="ctx_kv",
    )(ctx, mod3, norm1_g, w_kv, kg256, ones_bd)


def _rope(t, cos, sin_signed, first_half):
    swap = jnp.where(first_half, pltpu.roll(t, LANES - 16, axis=1), pltpu.roll(t, 16, axis=1))
    return t * cos + swap * sin_signed


def _inproj_kernel(x_ref, mod_ref, g1_ref, w_ref, qg_ref, kg_ref, ones_ref, cos_ref, sin_ref,
                   q_ref, k_ref, v_ref, u_ref):
    m = mod_ref[0]
    sh = m[:, 0:D_MODEL]
    gs = g1_ref[...] * (1.0 + m[:, D_MODEL:2 * D_MODEL])
    h = _rms_modulate(x_ref[0], gs, sh).astype(BF16)
    ones_bd = ones_ref[...]
    cos = cos_ref[...]
    sin = sin_ref[...]
    lane = lax.broadcasted_iota(jnp.int32, cos.shape, 1)
    first_half = (lane & 31) < 16

    def normed_rope(t, gain):
        t = t * _head_rms(t, ones_bd) * gain
        halves = [_rope(t[:, i * LANES:(i + 1) * LANES], cos, sin, first_half) for i in range(2)]
        return jnp.concatenate(halves, axis=1).astype(BF16)

    wide = 2 * MXU_TILE
    for j in range(Q_W // wide):
        t = _dot(h, w_ref[:, j * wide:(j + 1) * wide])
        for i in range(2):
            c0 = j * wide + i * MXU_TILE
            q_ref[0, :, c0:c0 + MXU_TILE] = normed_rope(t[:, i * MXU_TILE:(i + 1) * MXU_TILE], qg_ref[...])
    kv = _dot(h, w_ref[:, Q_W:REST_START])
    k_ref[0] = normed_rope(kv[:, :KV_W], kg_ref[...])
    v_ref[0] = kv[:, KV_W:].astype(BF16)
    u_ref[0] = _dot(h, w_ref[:, REST_START:REST_START + POOL_W])


def _inproj_call(x, mod3, norm1_g, w1, qg256, kg256, ones_bd, cos_t, sin_t, tile):
    B, S, D = x.shape
    n_mod = mod3.shape[2]
    n1 = w1.shape[1]
    row = lambda b, t: (b, t, 0)
    return pl.pallas_call(
        _inproj_kernel,
        grid=(B, S // tile),
        in_specs=[
            pl.BlockSpec((1, tile, D), row),
            pl.BlockSpec((1, 1, n_mod), lambda b, t: (b, 0, 0)),
            _const_spec((1, D)),
            _const_spec((D, n1)),
            _const_spec((1, MXU_TILE)),
            _const_spec((1, MXU_TILE)),
            _const_spec((MXU_TILE, MXU_TILE)),
            pl.BlockSpec((tile, LANES), lambda b, t: (t, 0)),
            pl.BlockSpec((tile, LANES), lambda b, t: (t, 0)),
        ],
        out_specs=[
            pl.BlockSpec((1, tile, Q_W), row),
            pl.BlockSpec((1, tile, KV_W), row),
            pl.BlockSpec((1, tile, KV_W), row),
            pl.BlockSpec((1, tile, POOL_W), row),
        ],
        out_shape=[
            jax.ShapeDtypeStruct((B, S, Q_W), BF16),
            jax.ShapeDtypeStruct((B, S, KV_W), BF16),
            jax.ShapeDtypeStruct((B, S, KV_W), BF16),
            jax.ShapeDtypeStruct((B, S, POOL_W), F32),
        ],
        compiler_params=_params(2),
        name="in_proj",
    )(x, mod3, norm1_g, w1, qg256, kg256, ones_bd, cos_t, sin_t)


def _attn_kernel(sink_ref, q_ref, kp_ref, kc0_ref, kn_ref, vp_ref, vc0_ref, vn_ref,
                 kctx_ref, vctx_ref, bias_ref, o_ref, *, blocks_per_tile):
    t_idx = pl.program_id(1)
    n_tiles = pl.num_programs(1)
    k_all = jnp.concatenate([kp_ref[0], kc0_ref[0], kn_ref[0]], axis=0)
    v_all = jnp.concatenate([vp_ref[0], vc0_ref[0], vn_ref[0]], axis=0)
    k_ctx = kctx_ref[0]
    v_ctx = vctx_ref[0]
    lane = lax.broadcasted_iota(jnp.int32, (BLOCK, MXU_TILE), 1)
    in_head = [(lane >= kv * HEAD_DIM) & (lane < (kv + 1) * HEAD_DIM) for kv in range(N_KV_HEADS)]
    zero = jnp.zeros((), BF16)

    for i in range(blocks_per_tile):
        variant = jnp.int32(0)
        if i == blocks_per_tile - 1:
            variant = jnp.where(t_idx == n_tiles - 1, 2, variant)
        if i == 0:
            variant = jnp.where(t_idx == 0, 1, variant)
        bias = bias_ref[variant]
        k_win = k_all[i * BLOCK:(i + 3) * BLOCK]
        v_win = v_all[i * BLOCK:(i + 3) * BLOCK]
        q_rows = slice(i * BLOCK, (i + 1) * BLOCK)
        for j in range(GROUP):
            q_slab = q_ref[0, q_rows, j * MXU_TILE:(j + 1) * MXU_TILE]
            qz = jnp.concatenate([jnp.where(in_head[kv], q_slab, zero) for kv in range(N_KV_HEADS)], axis=0)
            s_win = _dot_nt(qz, k_win)
            s_ctx = _dot_nt(qz, k_ctx)
            p_win, p_ctx, inv_den = [], [], []
            for kv in range(N_KV_HEADS):
                rows = slice(kv * BLOCK, (kv + 1) * BLOCK)
                sw = s_win[rows] + bias
                sc = s_ctx[rows]
                sink = sink_ref[kv * GROUP + j]
                m = jnp.maximum(
                    jnp.maximum(jnp.max(sw, axis=-1, keepdims=True),
                                jnp.max(sc, axis=-1, keepdims=True)), sink)
                pw = jnp.exp(sw - m)
                pc = jnp.exp(sc - m)
                den = (jnp.sum(pw, axis=-1, keepdims=True) + jnp.sum(pc, axis=-1, keepdims=True)
                       + jnp.exp(sink - m))
                p_win.append(pw.astype(BF16))
                p_ctx.append(pc.astype(BF16))
                inv_den.append(1.0 / den)
            o = (_dot(jnp.concatenate(p_win, axis=0), v_win)
                 + _dot(jnp.concatenate(p_ctx, axis=0), v_ctx))
            acc = jnp.zeros((BLOCK, MXU_TILE), F32)
            for kv in range(N_KV_HEADS):
                acc = acc + jnp.where(in_head[kv], o[kv * BLOCK:(kv + 1) * BLOCK] * inv_den[kv], 0.0)
            o_ref[0, q_rows, j * MXU_TILE:(j + 1) * MXU_TILE] = acc.astype(BF16)


def _attn_call(sink, q, k, v, kc, vc, bias, tile):
    B, S, _ = q.shape
    bpt = tile // BLOCK
    nb = S // BLOCK
    C = kc.shape[1]
    cur = lambda b, t, *_: (b, t, 0)
    prev = lambda b, t, *_: (b, jnp.maximum(t * bpt - 1, 0), 0)
    nxt = lambda b, t, *_: (b, jnp.minimum((t + 1) * bpt, nb - 1), 0)
    per_b = lambda b, t, *_: (b, 0, 0)
    halo = (1, BLOCK, KV_W)
    center = (1, tile, KV_W)
    grid_spec = pltpu.PrefetchScalarGridSpec(
        num_scalar_prefetch=1,
        grid=(B, S // tile),
        in_specs=[
            pl.BlockSpec((1, tile, Q_W), cur),
            pl.BlockSpec(halo, prev), pl.BlockSpec(center, cur), pl.BlockSpec(halo, nxt),
            pl.BlockSpec(halo, prev), pl.BlockSpec(center, cur), pl.BlockSpec(halo, nxt),
            pl.BlockSpec((1, C, KV_W), per_b),
            pl.BlockSpec((1, C, KV_W), per_b),
            pl.BlockSpec((3, BLOCK, 3 * BLOCK), lambda b, t, *_: (0, 0, 0)),
        ],
        out_specs=pl.BlockSpec((1, tile, Q_W), cur),
    )
    return pl.pallas_call(
        functools.partial(_attn_kernel, blocks_per_tile=bpt),
        grid_spec=grid_spec,
        out_shape=jax.ShapeDtypeStruct((B, S, Q_W), BF16),
        compiler_params=_params(2),
        name="attn",
    )(sink, q, k, k, k, v, v, v, kc, vc, bias)


def _merge_kernel(x_ref, mod_ref, g1_ref, attn_ref, up_ref, uc_ref, un_ref,
                  wg_ref, gb_ref, wap_ref, pw_ref, ps_ref, wpp_ref, wo_ref, o_ref, *, seq, tile):
    t_idx = pl.program_id(1)
    n_tiles = pl.num_programs(1)
    x = x_ref[0]
    m = mod_ref[0]
    sh = m[:, 0:D_MODEL]
    gs = g1_ref[...] * (1.0 + m[:, D_MODEL:2 * D_MODEL])
    gate1 = m[:, 2 * D_MODEL:3 * D_MODEL]
    h = _rms_modulate(x, gs, sh).astype(BF16)

    keep_prev = (t_idx > 0).astype(F32)
    keep_next = (t_idx < n_tiles - 1).astype(F32)
    u_c = uc_ref[0]
    u_ext = jnp.concatenate([up_ref[0] * keep_prev, u_c, un_ref[0] * keep_next], axis=0)
    pos = t_idx * tile + lax.broadcasted_iota(jnp.int32, (tile, 1), 0)
    mixed = []
    for g, w in enumerate(POOL_WINDOWS):
        cols = slice(g * POOL_GROUP_W, (g + 1) * POOL_GROUP_W)
        ug = u_ext[:, cols]
        win = ug[POOL_HALO - w // 2:POOL_HALO - w // 2 + tile]
        for j in range(1 - w // 2, w // 2):
            win = win + ug[POOL_HALO + j:POOL_HALO + j + tile]
        cnt = (jnp.minimum(pos + w // 2, seq) - jnp.maximum(pos - w // 2, 0)).astype(F32)
        diff = win / cnt - u_c[:, cols]
        mixed.append(_dot(diff.astype(BF16), pw_ref[g]))
    pooled = (jnp.concatenate(mixed, axis=1) * ps_ref[...]).astype(BF16)
    p = _dot(pooled, wpp_ref[...])

    a = _dot(attn_ref[0], wap_ref[...])
    gates = jax.nn.sigmoid(_dot(h, wg_ref[...]) + gb_ref[...])
    merged = (gates[:, :D_MODEL] * a + gates[:, D_MODEL:] * p).astype(BF16)
    o_ref[0] = x + gate1 * _dot(merged, wo_ref[...])


def _merge_call(x, mod3, norm1_g, attn, u, w_gate, gate_b, w_ap, pool_w, pool_scale, w_pp, w_out, tile):
    B, S, D = x.shape
    n_mod = mod3.shape[2]
    r = tile // POOL_HALO
    n_halo = S // POOL_HALO
    row = lambda b, t: (b, t, 0)
    return pl.pallas_call(
        functools.partial(_merge_kernel, seq=S, tile=tile),
        grid=(B, S // tile),
        in_specs=[
            pl.BlockSpec((1, tile, D), row),
            pl.BlockSpec((1, 1, n_mod), lambda b, t: (b, 0, 0)),
            _const_spec((1, D)),
            pl.BlockSpec((1, tile, Q_W), row),
            pl.BlockSpec((1, POOL_HALO, POOL_W), lambda b, t: (b, jnp.maximum(t * r - 1, 0), 0)),
            pl.BlockSpec((1, tile, POOL_W), row),
            pl.BlockSpec((1, POOL_HALO, POOL_W), lambda b, t: (b, jnp.minimum((t + 1) * r, n_halo - 1), 0)),
            _const_spec(w_gate.shape),
            _const_spec(gate_b.shape),
            _const_spec(w_ap.shape),
            _const_spec(pool_w.shape),
            _const_spec(pool_scale.shape),
            _const_spec(w_pp.shape),
            _const_spec(w_out.shape),
        ],
        out_specs=pl.BlockSpec((1, tile, D), row),
        out_shape=jax.ShapeDtypeStruct((B, S, D), F32),
        compiler_params=_params(2),
        name="merge",
    )(x, mod3, norm1_g, attn, u, u, u, w_gate, gate_b, w_ap, pool_w, pool_scale, w_pp, w_out)


def _ffn_kernel(x_ref, mod_ref, g2_ref, wup_ref, wdn_ref, o_ref, hmid_ref, *, d_ff):
    x = x_ref[0]
    m = mod_ref[0]
    sh = m[:, 3 * D_MODEL:4 * D_MODEL]
    gs = g2_ref[...] * (1.0 + m[:, 4 * D_MODEL:5 * D_MODEL])
    gate2 = m[:, 5 * D_MODEL:6 * D_MODEL]
    h = _rms_modulate(x, gs, sh).astype(BF16)
    for c in range(d_ff // MXU_TILE):
        a = _dot(h, wup_ref[:, c * MXU_TILE:(c + 1) * MXU_TILE])
        b = _dot(h, wup_ref[:, d_ff + c * MXU_TILE:d_ff + (c + 1) * MXU_TILE])
        hmid_ref[:, c * MXU_TILE:(c + 1) * MXU_TILE] = (a * jax.nn.sigmoid(a) * b).astype(BF16)
    o_ref[0] = x + gate2 * _dot(hmid_ref[...], wdn_ref[...])


def _ffn_call(x, mod3, norm2_g, w_up, w_down, tile):
    B, S, D = x.shape
    n_mod = mod3.shape[2]
    d_ff = w_down.shape[0]
    row = lambda b, t: (b, t, 0)
    return pl.pallas_call(
        functools.partial(_ffn_kernel, d_ff=d_ff),
        grid=(B, S // tile),
        in_specs=[
            pl.BlockSpec((1, tile, D), row),
            pl.BlockSpec((1, 1, n_mod), lambda b, t: (b, 0, 0)),
            _const_spec((1, D)),
            _const_spec(w_up.shape),
            _const_spec(w_down.shape),
        ],
        out_specs=pl.BlockSpec((1, tile, D), row),
        out_shape=jax.ShapeDtypeStruct((B, S, D), F32),
        scratch_shapes=[pltpu.VMEM((tile, d_ff), BF16)],
        compiler_params=_params(2),
        name="ffn",
    )(x, mod3, norm2_g, w_up, w_down)


def _rope_tables(seq):
    t = jnp.arange(seq)
    row = (t // GRID_W).astype(F32)
    col = (t % GRID_W).astype(F32)
    half = HEAD_DIM // 2
    inv_freq = 1.0 / (ROPE_THETA ** (jnp.arange(0, half, 2, dtype=F32) / half))
    ang_r = row[:, None] * inv_freq
    ang_c = col[:, None] * inv_freq
    cos = jnp.concatenate([jnp.cos(ang_r)] * 2 + [jnp.cos(ang_c)] * 2, axis=1)
    sin = jnp.concatenate([-jnp.sin(ang_r), jnp.sin(ang_r), -jnp.sin(ang_c), jnp.sin(ang_c)], axis=1)
    return jnp.tile(cos, (1, LANES // HEAD_DIM)), jnp.tile(sin, (1, LANES // HEAD_DIM))


def _band_bias():
    i = np.arange(BLOCK)[:, None]
    j = np.arange(3 * BLOCK)[None, :]
    band = np.abs(j - BLOCK - i) <= WINDOW
    first = band & (j >= BLOCK)
    last = band & (j < 2 * BLOCK)
    masks = np.stack([band, first, last])
    return jnp.asarray(np.where(masks, 0.0, NEG_BIG), dtype=F32)


def _head_perm():
    idx = np.empty(Q_W, dtype=np.int32)
    for j in range(GROUP):
        for kv in range(N_KV_HEADS):
            new0 = (j * N_KV_HEADS + kv) * HEAD_DIM
            old0 = (kv * GROUP + j) * HEAD_DIM
            idx[new0:new0 + HEAD_DIM] = np.arange(old0, old0 + HEAD_DIM)
    return idx


def kernel(x, c, ctx, c_ctx, mod_w, mod_b, norm1_g, norm2_g, w_in, gate_b, q_norm_g, k_norm_g,
           sink, pool_w, pool_scale, w_attn_proj, w_pool_proj, w_out, w_up, w_down):
    B, S, D = x.shape
    depth = mod_w.shape[0]
    perm = _head_perm()
    cos_t, sin_t = _rope_tables(S)
    bias = _band_bias()
    ones_bd = jnp.asarray(np.kron(np.eye(MXU_TILE // HEAD_DIM), np.ones((HEAD_DIM, HEAD_DIM))), dtype=BF16)
    reps = MXU_TILE // HEAD_DIM
    assert depth == 1, "context-stream update between layers is not implemented"

    for l in range(depth):
        c8 = jnp.concatenate([c, c_ctx[None, :], jnp.zeros((8 - B - 1, D), F32)], axis=0)
        mod = _mod_call(c8, mod_w[l], mod_b[l][None, :])
        mod3 = mod[:, None, :]

        wl = w_in[l].astype(BF16)
        w_q = wl[:, :Q_W][:, perm]
        w1 = jnp.concatenate([w_q, wl[:, Q_W:REST_START + POOL_W]], axis=1)
        w_gate = wl[:, REST_START + POOL_W:]
        g1 = norm1_g[l][None, :]
        kg256 = jnp.tile(k_norm_g[l], reps)[None, :]
        qg256 = jnp.tile(q_norm_g[l] * (HEAD_DIM ** -0.5), reps)[None, :]

        kc, vc = _ctx_call(ctx, mod3, g1, wl[:, Q_W:REST_START], kg256, ones_bd)
        q, k, v, u = _inproj_call(x, mod3, g1, w1, qg256, kg256, ones_bd, cos_t, sin_t, tile=512)
        attn = _attn_call(sink[l], q, k, v, kc, vc, bias, tile=512)
        x = _merge_call(
            x, mod3, g1, attn, u, w_gate, gate_b[l][None, :],
            w_attn_proj[l][perm, :].astype(BF16), pool_w[l].astype(BF16), pool_scale[l][None, :],
            w_pool_proj[l].astype(BF16), w_out[l].astype(BF16), tile=256)
        x = _ffn_call(x, mod3, norm2_g[l][None, :], w_up[l].astype(BF16), w_down[l].astype(BF16), tile=512)
    return x
```

```python
import functools

import jax
import jax.numpy as jnp
import numpy as np
from jax import lax
from jax.experimental import pallas as pl
from jax.experimental.pallas import tpu as pltpu

D_MODEL = 1024
GRID_W = 64
HEAD_DIM = 64
N_HEADS = 16
N_KV_HEADS = 4
GROUP = N_HEADS // N_KV_HEADS
Q_W = N_HEADS * HEAD_DIM
KV_W = N_KV_HEADS * HEAD_DIM
WINDOW = 128
BLOCK = 128
ROPE_THETA = 10000.0
POOL_WINDOWS = (2, 4, 8, 16)
POOL_GROUPS = 4
POOL_W = D_MODEL // 2
POOL_GROUP_W = POOL_W // POOL_GROUPS
POOL_HALO = 8
REST_START = Q_W + 2 * KV_W
N_MOD = 6
EPS = 1e-6
NEG_BIG = -1e30

LANES = 128
MXU_TILE = 256
VMEM_LIMIT = 56 * 1024 * 1024

F32 = jnp.float32
BF16 = jnp.bfloat16


def _params(n_parallel):
    return pltpu.CompilerParams(
        dimension_semantics=("parallel",) * n_parallel,
        vmem_limit_bytes=VMEM_LIMIT,
    )


def _const_spec(shape):
    nd = len(shape)
    return pl.BlockSpec(shape, lambda *_: (0,) * nd, pipeline_mode=pl.Buffered(1))


def _col_block_spec(rows, width, block_idx):
    return pl.BlockSpec((rows, width), lambda *_: (0, block_idx), pipeline_mode=pl.Buffered(1))


def _dot(a, b):
    return jnp.dot(a, b, preferred_element_type=F32)


def _dot_nt(a, b):
    return lax.dot_general(a, b, (((1,), (1,)), ((), ())), preferred_element_type=F32)


def _rms_modulate(x, gs, sh):
    rs = lax.rsqrt(jnp.mean(x * x, axis=-1, keepdims=True) + EPS)
    return x * rs * gs + sh


def _head_rms(t, ones_bd):
    ms = _dot((t * t).astype(BF16), ones_bd) * (1.0 / HEAD_DIM)
    return lax.rsqrt(ms + EPS)


def _mod_kernel(c_ref, w_ref, b_ref, o_ref):
    c = c_ref[...]
    a = (c * jax.nn.sigmoid(c)).astype(BF16)
    o_ref[...] = _dot(a, w_ref[...].astype(BF16)) + b_ref[...]


def _mod_call(c8, mod_w, mod_b):
    n = mod_w.shape[1]
    bn = D_MODEL
    return pl.pallas_call(
        _mod_kernel,
        grid=(n // bn,),
        in_specs=[
            pl.BlockSpec((8, D_MODEL), lambda j: (0, 0)),
            pl.BlockSpec((D_MODEL, bn), lambda j: (0, j)),
            pl.BlockSpec((1, bn), lambda j: (0, j)),
        ],
        out_specs=pl.BlockSpec((8, bn), lambda j: (0, j)),
        out_shape=jax.ShapeDtypeStruct((8, n), F32),
        compiler_params=_params(1),
        name="mod",
    )(c8, mod_w, mod_b)


def _pool_fold_kernel(pw_ref, ps_ref, wpp_ref, o_ref):
    scaled = pw_ref[0] * ps_ref[0]
    o_ref[...] = jnp.dot(scaled, wpp_ref[...], preferred_element_type=F32,
                         precision=lax.Precision.HIGHEST).astype(BF16)


def _pool_fold_call(pool_w, pool_scale3, w_pool_proj):
    n_out = w_pool_proj.shape[1]
    return pl.pallas_call(
        _pool_fold_kernel,
        grid=(POOL_GROUPS,),
        in_specs=[
            pl.BlockSpec((1, POOL_GROUP_W, POOL_GROUP_W), lambda g: (g, 0, 0)),
            pl.BlockSpec((1, 1, POOL_GROUP_W), lambda g: (g, 0, 0)),
            pl.BlockSpec((POOL_GROUP_W, n_out), lambda g: (g, 0)),
        ],
        out_specs=pl.BlockSpec((POOL_GROUP_W, n_out), lambda g: (g, 0)),
        out_shape=jax.ShapeDtypeStruct((POOL_W, n_out), BF16),
        compiler_params=_params(1),
        name="pool_fold",
    )(pool_w, pool_scale3, w_pool_proj)


def _ctx_kernel(ctx_ref, mod_ref, g1_ref, w_ref, kg_ref, ones_ref, kc_ref, vc_ref, *, n_batch):
    m = mod_ref[n_batch:n_batch + 1, :]
    sh = m[:, 0:D_MODEL]
    gs = g1_ref[...] * (1.0 + m[:, D_MODEL:2 * D_MODEL])
    h = _rms_modulate(ctx_ref[0], gs, sh).astype(BF16)
    kv = _dot(h, w_ref[...])
    k = kv[:, :KV_W]
    kc_ref[0] = (k * _head_rms(k, ones_ref[...]) * kg_ref[...]).astype(BF16)
    vc_ref[0] = kv[:, KV_W:].astype(BF16)


def _ctx_call(ctx, mod, norm1_g, w_all, kg256, ones_bd):
    B, C, D = ctx.shape
    return pl.pallas_call(
        functools.partial(_ctx_kernel, n_batch=B),
        grid=(B,),
        in_specs=[
            pl.BlockSpec((1, C, D), lambda b: (b, 0, 0)),
            _const_spec(mod.shape),
            _const_spec((1, D)),
            _col_block_spec(D, 2 * KV_W, Q_W // (2 * KV_W)),
            _const_spec((1, KV_W)),
            _const_spec((MXU_TILE, MXU_TILE)),
        ],
        out_specs=[
            pl.BlockSpec((1, C, KV_W), lambda b: (b, 0, 0)),
            pl.BlockSpec((1, C, KV_W), lambda b: (b, 0, 0)),
        ],
        out_shape=[
            jax.ShapeDtypeStruct((B, C, KV_W), BF16),
            jax.ShapeDtypeStruct((B, C, KV_W), BF16),
        ],
        compiler_params=_params(1),
        name="ctx_kv",
    )(ctx, mod, norm1_g, w_all, kg256, ones_bd)


def _rope(t, cos, sin_signed, first_half):
    swap = jnp.where(first_half, pltpu.roll(t, LANES - 16, axis=1), pltpu.roll(t, 16, axis=1))
    return t * cos + swap * sin_signed


def _inproj_kernel(x_ref, mod_ref, g1_ref, wq_ref, wkvu_ref, qg_ref, kg_ref, ones_ref, cos_ref, sin_ref,
                   q_ref, k_ref, v_ref, u_ref):
    m = mod_ref[pl.ds(pl.program_id(0), 1), :]
    sh = m[:, 0:D_MODEL]
    gs = g1_ref[...] * (1.0 + m[:, D_MODEL:2 * D_MODEL])
    h = _rms_modulate(x_ref[0], gs, sh).astype(BF16)
    ones_bd = ones_ref[...]
    cos = cos_ref[...]
    sin = sin_ref[...]
    lane = lax.broadcasted_iota(jnp.int32, cos.shape, 1)
    first_half = (lane & 31) < 16

    def normed_rope(t, gain):
        t = t * _head_rms(t, ones_bd) * gain
        halves = [_rope(t[:, i * LANES:(i + 1) * LANES], cos, sin, first_half) for i in range(2)]
        return jnp.concatenate(halves, axis=1).astype(BF16)

    wide = 2 * MXU_TILE
    for j in range(Q_W // wide):
        t = _dot(h, wq_ref[:, j * wide:(j + 1) * wide])
        for i in range(2):
            c0 = j * wide + i * MXU_TILE
            q_ref[0, :, c0:c0 + MXU_TILE] = normed_rope(t[:, i * MXU_TILE:(i + 1) * MXU_TILE], qg_ref[...])
    kv = _dot(h, wkvu_ref[:, :2 * KV_W])
    k_ref[0] = normed_rope(kv[:, :KV_W], kg_ref[...])
    v_ref[0] = kv[:, KV_W:].astype(BF16)
    u_ref[0] = _dot(h, wkvu_ref[:, 2 * KV_W:])


def _inproj_call(x, mod, norm1_g, w_q, w_all, qg256, kg256, ones_bd, cos_t, sin_t, tile):
    B, S, D = x.shape
    row = lambda b, t: (b, t, 0)
    return pl.pallas_call(
        _inproj_kernel,
        grid=(B, S // tile),
        in_specs=[
            pl.BlockSpec((1, tile, D), row),
            _const_spec(mod.shape),
            _const_spec((1, D)),
            _const_spec((D, Q_W)),
            _col_block_spec(D, Q_W, 1),
            _const_spec((1, MXU_TILE)),
            _const_spec((1, MXU_TILE)),
            _const_spec((MXU_TILE, MXU_TILE)),
            pl.BlockSpec((tile, LANES), lambda b, t: (t, 0)),
            pl.BlockSpec((tile, LANES), lambda b, t: (t, 0)),
        ],
        out_specs=[
            pl.BlockSpec((1, tile, Q_W), row),
            pl.BlockSpec((1, tile, KV_W), row),
            pl.BlockSpec((1, tile, KV_W), row),
            pl.BlockSpec((1, tile, POOL_W), row),
        ],
        out_shape=[
            jax.ShapeDtypeStruct((B, S, Q_W), BF16),
            jax.ShapeDtypeStruct((B, S, KV_W), BF16),
            jax.ShapeDtypeStruct((B, S, KV_W), BF16),
            jax.ShapeDtypeStruct((B, S, POOL_W), F32),
        ],
        compiler_params=_params(2),
        name="in_proj",
    )(x, mod, norm1_g, w_q, w_all, qg256, kg256, ones_bd, cos_t, sin_t)


def _attn_kernel(sink_ref, q_ref, kp_ref, kc0_ref, kn_ref, vp_ref, vc0_ref, vn_ref,
                 kctx_ref, vctx_ref, bias_ref, o_ref, *, blocks_per_tile):
    t_idx = pl.program_id(1)
    n_tiles = pl.num_programs(1)
    k_all = jnp.concatenate([kp_ref[0], kc0_ref[0], kn_ref[0]], axis=0)
    v_all = jnp.concatenate([vp_ref[0], vc0_ref[0], vn_ref[0]], axis=0)
    k_ctx = kctx_ref[0]
    v_ctx = vctx_ref[0]
    lane = lax.broadcasted_iota(jnp.int32, (BLOCK, MXU_TILE), 1)
    in_head = [(lane >= kv * HEAD_DIM) & (lane < (kv + 1) * HEAD_DIM) for kv in range(N_KV_HEADS)]
    zero = jnp.zeros((), BF16)

    for i in range(blocks_per_tile):
        variant = jnp.int32(0)
        if i == blocks_per_tile - 1:
            variant = jnp.where(t_idx == n_tiles - 1, 2, variant)
        if i == 0:
            variant = jnp.where(t_idx == 0, 1, variant)
        bias = bias_ref[variant]
        k_win = k_all[i * BLOCK:(i + 3) * BLOCK]
        v_win = v_all[i * BLOCK:(i + 3) * BLOCK]
        q_rows = slice(i * BLOCK, (i + 1) * BLOCK)
        for j in range(GROUP):
            q_slab = q_ref[0, q_rows, j * MXU_TILE:(j + 1) * MXU_TILE]
            qz = jnp.concatenate([jnp.where(in_head[kv], q_slab, zero) for kv in range(N_KV_HEADS)], axis=0)
            s_win = _dot_nt(qz, k_win)
            s_ctx = _dot_nt(qz, k_ctx)
            p_win, p_ctx, inv_den = [], [], []
            for kv in range(N_KV_HEADS):
                rows = slice(kv * BLOCK, (kv + 1) * BLOCK)
                sw = s_win[rows] + bias
                sc = s_ctx[rows]
                sink = sink_ref[kv * GROUP + j]
                m = jnp.maximum(
                    jnp.maximum(jnp.max(sw, axis=-1, keepdims=True),
                                jnp.max(sc, axis=-1, keepdims=True)), sink)
                pw = jnp.exp(sw - m)
                pc = jnp.exp(sc - m)
                den = (jnp.sum(pw, axis=-1, keepdims=True) + jnp.sum(pc, axis=-1, keepdims=True)
                       + jnp.exp(sink - m))
                p_win.append(pw.astype(BF16))
                p_ctx.append(pc.astype(BF16))
                inv_den.append(1.0 / den)
            o = (_dot(jnp.concatenate(p_win, axis=0), v_win)
                 + _dot(jnp.concatenate(p_ctx, axis=0), v_ctx))
            acc = jnp.zeros((BLOCK, MXU_TILE), F32)
            for kv in range(N_KV_HEADS):
                acc = acc + jnp.where(in_head[kv], o[kv * BLOCK:(kv + 1) * BLOCK] * inv_den[kv], 0.0)
            o_ref[0, q_rows, j * MXU_TILE:(j + 1) * MXU_TILE] = acc.astype(BF16)


def _attn_call(sink, q, k, v, kc, vc, bias, tile):
    B, S, _ = q.shape
    bpt = tile // BLOCK
    nb = S // BLOCK
    C = kc.shape[1]
    cur = lambda b, t, *_: (b, t, 0)
    prev = lambda b, t, *_: (b, jnp.maximum(t * bpt - 1, 0), 0)
    nxt = lambda b, t, *_: (b, jnp.minimum((t + 1) * bpt, nb - 1), 0)
    per_b = lambda b, t, *_: (b, 0, 0)
    halo = (1, BLOCK, KV_W)
    center = (1, tile, KV_W)
    grid_spec = pltpu.PrefetchScalarGridSpec(
        num_scalar_prefetch=1,
        grid=(B, S // tile),
        in_specs=[
            pl.BlockSpec((1, tile, Q_W), cur),
            pl.BlockSpec(halo, prev), pl.BlockSpec(center, cur), pl.BlockSpec(halo, nxt),
            pl.BlockSpec(halo, prev), pl.BlockSpec(center, cur), pl.BlockSpec(halo, nxt),
            pl.BlockSpec((1, C, KV_W), per_b),
            pl.BlockSpec((1, C, KV_W), per_b),
            pl.BlockSpec((3, BLOCK, 3 * BLOCK), lambda b, t, *_: (0, 0, 0)),
        ],
        out_specs=pl.BlockSpec((1, tile, Q_W), cur),
    )
    return pl.pallas_call(
        functools.partial(_attn_kernel, blocks_per_tile=bpt),
        grid_spec=grid_spec,
        out_shape=jax.ShapeDtypeStruct((B, S, Q_W), BF16),
        compiler_params=_params(2),
        name="attn",
    )(sink, q, k, k, k, v, v, v, kc, vc, bias)


def _merge_kernel(x_ref, mod_ref, g1_ref, attn_ref, up_ref, uc_ref, un_ref,
                  wg_ref, gb_ref, wap_ref, wpool_ref, wo_ref, o_ref, uext_ref, *, seq, tile):
    t_idx = pl.program_id(1)
    n_tiles = pl.num_programs(1)
    x = x_ref[0]
    m = mod_ref[pl.ds(pl.program_id(0), 1), :]
    sh = m[:, 0:D_MODEL]
    gs = g1_ref[...] * (1.0 + m[:, D_MODEL:2 * D_MODEL])
    gate1 = m[:, 2 * D_MODEL:3 * D_MODEL]
    a = _dot(attn_ref[0], wap_ref[...])
    h = _rms_modulate(x, gs, sh).astype(BF16)
    gates = jax.nn.sigmoid(_dot(h, wg_ref[...]) + gb_ref[...])

    keep_prev = (t_idx > 0).astype(F32)
    keep_next = (t_idx < n_tiles - 1).astype(F32)
    u_c = uc_ref[0]
    uext_ref[0:POOL_HALO] = up_ref[0] * keep_prev
    uext_ref[POOL_HALO:POOL_HALO + tile] = u_c
    uext_ref[POOL_HALO + tile:] = un_ref[0] * keep_next
    edge_rows = lax.broadcasted_iota(jnp.int32, (POOL_HALO, POOL_GROUP_W), 0)
    pos_first = t_idx * tile + edge_rows
    pos_last = pos_first + (tile - POOL_HALO)
    diffs = []
    for g, w in enumerate(POOL_WINDOWS):
        cols = slice(g * POOL_GROUP_W, (g + 1) * POOL_GROUP_W)
        win = uext_ref[pl.ds(POOL_HALO - w // 2, tile), cols]
        for j in range(1 - w // 2, w // 2):
            win = win + uext_ref[pl.ds(POOL_HALO + j, tile), cols]

        def inv_count(pos):
            return 1.0 / (jnp.minimum(pos + w // 2, seq) - jnp.maximum(pos - w // 2, 0)).astype(F32)

        pooled_avg = jnp.concatenate([
            win[:POOL_HALO] * inv_count(pos_first),
            win[POOL_HALO:tile - POOL_HALO] * (1.0 / w),
            win[tile - POOL_HALO:] * inv_count(pos_last)], axis=0)
        diffs.append((pooled_avg - u_c[:, cols]).astype(BF16))
    p = _dot(jnp.concatenate(diffs, axis=1), wpool_ref[...])

    merged = (gates[:, :D_MODEL] * a + gates[:, D_MODEL:] * p).astype(BF16)
    o_ref[0] = x + gate1 * _dot(merged, wo_ref[...])


def _merge_call(x, mod, norm1_g, attn, u, w_all, gate_b, w_ap, w_pool, w_out, tile):
    B, S, D = x.shape
    r = tile // POOL_HALO
    n_halo = S // POOL_HALO
    row = lambda b, t: (b, t, 0)
    return pl.pallas_call(
        functools.partial(_merge_kernel, seq=S, tile=tile),
        grid=(B, S // tile),
        in_specs=[
            pl.BlockSpec((1, tile, D), row),
            _const_spec(mod.shape),
            _const_spec((1, D)),
            pl.BlockSpec((1, tile, Q_W), row),
            pl.BlockSpec((1, POOL_HALO, POOL_W), lambda b, t: (b, jnp.maximum(t * r - 1, 0), 0)),
            pl.BlockSpec((1, tile, POOL_W), row),
            pl.BlockSpec((1, POOL_HALO, POOL_W), lambda b, t: (b, jnp.minimum((t + 1) * r, n_halo - 1), 0)),
            _col_block_spec(D, 2 * D, 1),
            _const_spec(gate_b.shape),
            _const_spec(w_ap.shape),
            _const_spec(w_pool.shape),
            _const_spec(w_out.shape),
        ],
        out_specs=pl.BlockSpec((1, tile, D), row),
        out_shape=jax.ShapeDtypeStruct((B, S, D), F32),
        scratch_shapes=[pltpu.VMEM((tile + 2 * POOL_HALO, POOL_W), F32)],
        compiler_params=_params(2),
        name="merge",
    )(x, mod, norm1_g, attn, u, u, u, w_all, gate_b, w_ap, w_pool, w_out)


def _ffn_kernel(x_ref, mod_ref, g2_ref, wup_ref, wdn_ref, o_ref, hmid_ref, *, d_ff):
    x = x_ref[0]
    m = mod_ref[pl.ds(pl.program_id(0), 1), :]
    sh = m[:, 3 * D_MODEL:4 * D_MODEL]
    gs = g2_ref[...] * (1.0 + m[:, 4 * D_MODEL:5 * D_MODEL])
    gate2 = m[:, 5 * D_MODEL:6 * D_MODEL]
    h = _rms_modulate(x, gs, sh).astype(BF16)
    for c in range(d_ff // MXU_TILE):
        a = _dot(h, wup_ref[:, c * MXU_TILE:(c + 1) * MXU_TILE])
        b = _dot(h, wup_ref[:, d_ff + c * MXU_TILE:d_ff + (c + 1) * MXU_TILE])
        hmid_ref[:, c * MXU_TILE:(c + 1) * MXU_TILE] = (a * jax.nn.sigmoid(a) * b).astype(BF16)
    o_ref[0] = x + gate2 * _dot(hmid_ref[...], wdn_ref[...])


def _ffn_call(x, mod, norm2_g, w_up, w_down, tile):
    B, S, D = x.shape
    d_ff = w_down.shape[0]
    row = lambda b, t: (b, t, 0)
    return pl.pallas_call(
        functools.partial(_ffn_kernel, d_ff=d_ff),
        grid=(B, S // tile),
        in_specs=[
            pl.BlockSpec((1, tile, D), row),
            _const_spec(mod.shape),
            _const_spec((1, D)),
            _const_spec(w_up.shape),
            _const_spec(w_down.shape),
        ],
        out_specs=pl.BlockSpec((1, tile, D), row),
        out_shape=jax.ShapeDtypeStruct((B, S, D), F32),
        scratch_shapes=[pltpu.VMEM((tile, d_ff), BF16)],
        compiler_params=_params(2),
        name="ffn",
    )(x, mod, norm2_g, w_up, w_down)


def _rope_tables(seq):
    t = np.arange(seq)
    row = (t // GRID_W).astype(np.float64)
    col = (t % GRID_W).astype(np.float64)
    half = HEAD_DIM // 2
    inv_freq = 1.0 / (ROPE_THETA ** (np.arange(0, half, 2, dtype=np.float64) / half))
    ang_r = row[:, None] * inv_freq
    ang_c = col[:, None] * inv_freq
    cos = np.concatenate([np.cos(ang_r)] * 2 + [np.cos(ang_c)] * 2, axis=1)
    sin = np.concatenate([-np.sin(ang_r), np.sin(ang_r), -np.sin(ang_c), np.sin(ang_c)], axis=1)
    reps = (1, LANES // HEAD_DIM)
    return jnp.asarray(np.tile(cos, reps), dtype=F32), jnp.asarray(np.tile(sin, reps), dtype=F32)


def _band_bias():
    i = np.arange(BLOCK)[:, None]
    j = np.arange(3 * BLOCK)[None, :]
    band = np.abs(j - BLOCK - i) <= WINDOW
    first = band & (j >= BLOCK)
    last = band & (j < 2 * BLOCK)
    masks = np.stack([band, first, last])
    return jnp.asarray(np.where(masks, 0.0, NEG_BIG), dtype=F32)


def kernel(x, c, ctx, c_ctx, mod_w, mod_b, norm1_g, norm2_g, w_in, gate_b, q_norm_g, k_norm_g,
           sink, pool_w, pool_scale, w_attn_proj, w_pool_proj, w_out, w_up, w_down):
    B, S, D = x.shape
    depth = mod_w.shape[0]
    cos_t, sin_t = _rope_tables(S)
    bias = _band_bias()
    ones_bd = jnp.asarray(np.kron(np.eye(MXU_TILE // HEAD_DIM), np.ones((HEAD_DIM, HEAD_DIM))), dtype=BF16)
    reps = MXU_TILE // HEAD_DIM
    assert depth == 1, "context-stream update between layers is not implemented"

    for l in range(depth):
        c8 = jnp.concatenate([c, c_ctx[None, :], jnp.zeros((8 - B - 1, D), F32)], axis=0)
        mod = _mod_call(c8, mod_w[l], mod_b[l][None, :])

        wl = w_in[l].astype(BF16)
        w_q = wl[:, :Q_W].reshape(D, N_KV_HEADS, GROUP, HEAD_DIM).transpose(0, 2, 1, 3).reshape(D, Q_W)
        w_ap = (w_attn_proj[l].reshape(N_KV_HEADS, GROUP, HEAD_DIM, D).transpose(1, 0, 2, 3)
                .reshape(Q_W, D).astype(BF16))
        g1 = norm1_g[l][None, :]
        kg256 = jnp.tile(k_norm_g[l], reps)[None, :]
        qg256 = jnp.tile(q_norm_g[l] * (HEAD_DIM ** -0.5), reps)[None, :]

        kc, vc = _ctx_call(ctx, mod, g1, wl, kg256, ones_bd)
        q, k, v, u = _inproj_call(x, mod, g1, w_q, wl, qg256, kg256, ones_bd, cos_t, sin_t, tile=512)
        attn = _attn_call(sink[l], q, k, v, kc, vc, bias, tile=512)
        w_pool = _pool_fold_call(pool_w[l], pool_scale[l].reshape(POOL_GROUPS, 1, POOL_GROUP_W), w_pool_proj[l])
        x = _merge_call(
            x, mod, g1, attn, u, wl, gate_b[l][None, :], w_ap, w_pool, w_out[l].astype(BF16), tile=512)
        x = _ffn_call(x, mod, norm2_g[l][None, :], w_up[l].astype(BF16), w_down[l].astype(BF16), tile=512)
    return x
```

```python
import functools

import jax
import jax.numpy as jnp
import numpy as np
from jax import lax
from jax.experimental import pallas as pl
from jax.experimental.pallas import tpu as pltpu

D_MODEL = 1024
GRID_W = 64
HEAD_DIM = 64
N_HEADS = 16
N_KV_HEADS = 4
GROUP = N_HEADS // N_KV_HEADS
Q_W = N_HEADS * HEAD_DIM
KV_W = N_KV_HEADS * HEAD_DIM
WINDOW = 128
BLOCK = 128
ROPE_THETA = 10000.0
POOL_WINDOWS = (2, 4, 8, 16)
POOL_GROUPS = 4
POOL_W = D_MODEL // 2
POOL_GROUP_W = POOL_W // POOL_GROUPS
POOL_HALO = 8
REST_START = Q_W + 2 * KV_W
N_MOD = 6
EPS = 1e-6
NEG_BIG = -1e30
LOG2_E = 1.4426950408889634

LANES = 128
MXU_TILE = 256
VMEM_LIMIT = 56 * 1024 * 1024

F32 = jnp.float32
BF16 = jnp.bfloat16


def _params(n_parallel):
    return pltpu.CompilerParams(
        dimension_semantics=("parallel",) * n_parallel,
        vmem_limit_bytes=VMEM_LIMIT,
    )


def _const_spec(shape):
    nd = len(shape)
    return pl.BlockSpec(shape, lambda *_: (0,) * nd, pipeline_mode=pl.Buffered(1))


def _col_block_spec(rows, width, block_idx):
    return pl.BlockSpec((rows, width), lambda *_: (0, block_idx), pipeline_mode=pl.Buffered(1))


def _dot(a, b):
    return jnp.dot(a, b, preferred_element_type=F32)


def _dot_nt(a, b):
    return lax.dot_general(a, b, (((1,), (1,)), ((), ())), preferred_element_type=F32)


def _rms_modulate(x, gs, sh):
    rs = lax.rsqrt(jnp.mean(x * x, axis=-1, keepdims=True) + EPS)
    return x * rs * gs + sh


def _head_rms(t, ones_bd):
    ms = _dot((t * t).astype(BF16), ones_bd) * (1.0 / HEAD_DIM)
    return lax.rsqrt(ms + EPS)


def _mod_kernel(c_ref, w_ref, b_ref, o_ref):
    c = c_ref[...]
    a = (c * jax.nn.sigmoid(c)).astype(BF16)
    o_ref[...] = _dot(a, w_ref[...].astype(BF16)) + b_ref[...]


def _mod_call(c8, mod_w, mod_b):
    n = mod_w.shape[1]
    bn = D_MODEL
    return pl.pallas_call(
        _mod_kernel,
        grid=(n // bn,),
        in_specs=[
            pl.BlockSpec((8, D_MODEL), lambda j: (0, 0)),
            pl.BlockSpec((D_MODEL, bn), lambda j: (0, j)),
            pl.BlockSpec((1, bn), lambda j: (0, j)),
        ],
        out_specs=pl.BlockSpec((8, bn), lambda j: (0, j)),
        out_shape=jax.ShapeDtypeStruct((8, n), F32),
        compiler_params=_params(1),
        name="mod",
    )(c8, mod_w, mod_b)


def _pool_fold_kernel(pw_ref, ps_ref, wpp_ref, o_ref):
    scaled = pw_ref[0] * ps_ref[0]
    o_ref[...] = jnp.dot(scaled, wpp_ref[...], preferred_element_type=F32,
                         precision=lax.Precision.HIGHEST).astype(BF16)


def _pool_fold_call(pool_w, pool_scale3, w_pool_proj):
    n_out = w_pool_proj.shape[1]
    return pl.pallas_call(
        _pool_fold_kernel,
        grid=(POOL_GROUPS,),
        in_specs=[
            pl.BlockSpec((1, POOL_GROUP_W, POOL_GROUP_W), lambda g: (g, 0, 0)),
            pl.BlockSpec((1, 1, POOL_GROUP_W), lambda g: (g, 0, 0)),
            pl.BlockSpec((POOL_GROUP_W, n_out), lambda g: (g, 0)),
        ],
        out_specs=pl.BlockSpec((POOL_GROUP_W, n_out), lambda g: (g, 0)),
        out_shape=jax.ShapeDtypeStruct((POOL_W, n_out), BF16),
        compiler_params=_params(1),
        name="pool_fold",
    )(pool_w, pool_scale3, w_pool_proj)


def _ctx_kernel(ctx_ref, mod_ref, g1_ref, w_ref, kg_ref, ones_ref, kc_ref, vc_ref, *, n_batch):
    m = mod_ref[n_batch:n_batch + 1, :]
    sh = m[:, 0:D_MODEL]
    gs = g1_ref[...] * (1.0 + m[:, D_MODEL:2 * D_MODEL])
    h = _rms_modulate(ctx_ref[0], gs, sh).astype(BF16)
    kv = _dot(h, w_ref[...])
    k = kv[:, :KV_W]
    kc_ref[0] = (k * _head_rms(k, ones_ref[...]) * kg_ref[...]).astype(BF16)
    vc_ref[0] = kv[:, KV_W:].astype(BF16)


def _ctx_call(ctx, mod, norm1_g, w_all, kg256, ones_bd):
    B, C, D = ctx.shape
    return pl.pallas_call(
        functools.partial(_ctx_kernel, n_batch=B),
        grid=(B,),
        in_specs=[
            pl.BlockSpec((1, C, D), lambda b: (b, 0, 0)),
            _const_spec(mod.shape),
            _const_spec((1, D)),
            _col_block_spec(D, 2 * KV_W, Q_W // (2 * KV_W)),
            _const_spec((1, KV_W)),
            _const_spec((MXU_TILE, MXU_TILE)),
        ],
        out_specs=[
            pl.BlockSpec((1, C, KV_W), lambda b: (b, 0, 0)),
            pl.BlockSpec((1, C, KV_W), lambda b: (b, 0, 0)),
        ],
        out_shape=[
            jax.ShapeDtypeStruct((B, C, KV_W), BF16),
            jax.ShapeDtypeStruct((B, C, KV_W), BF16),
        ],
        compiler_params=_params(1),
        name="ctx_kv",
    )(ctx, mod, norm1_g, w_all, kg256, ones_bd)


def _rope(t, cos, sin_signed, first_half):
    swap = jnp.where(first_half, pltpu.roll(t, LANES - 16, axis=1), pltpu.roll(t, 16, axis=1))
    return t * cos + swap * sin_signed


def _inproj_kernel(x_ref, mod_ref, g1_ref, wq_ref, wkvu_ref, qg_ref, kg_ref, ones_ref, cos_ref, sin_ref,
                   q_ref, k_ref, v_ref, u_ref):
    m = mod_ref[pl.ds(pl.program_id(0), 1), :]
    sh = m[:, 0:D_MODEL]
    gs = g1_ref[...] * (1.0 + m[:, D_MODEL:2 * D_MODEL])
    h = _rms_modulate(x_ref[0], gs, sh).astype(BF16)
    ones_bd = ones_ref[...]
    cos = cos_ref[...]
    sin = sin_ref[...]
    lane = lax.broadcasted_iota(jnp.int32, cos.shape, 1)
    first_half = (lane & 31) < 16

    def normed_rope(t, gain):
        t = t * _head_rms(t, ones_bd) * gain
        halves = [_rope(t[:, i * LANES:(i + 1) * LANES], cos, sin, first_half) for i in range(2)]
        return jnp.concatenate(halves, axis=1).astype(BF16)

    wide = 2 * MXU_TILE
    for j in range(Q_W // wide):
        t = _dot(h, wq_ref[:, j * wide:(j + 1) * wide])
        for i in range(2):
            c0 = j * wide + i * MXU_TILE
            q_ref[0, :, c0:c0 + MXU_TILE] = normed_rope(t[:, i * MXU_TILE:(i + 1) * MXU_TILE], qg_ref[...])
    kv = _dot(h, wkvu_ref[:, :2 * KV_W])
    k_ref[0] = normed_rope(kv[:, :KV_W], kg_ref[...])
    v_ref[0] = kv[:, KV_W:].astype(BF16)
    u_ref[0] = _dot(h, wkvu_ref[:, 2 * KV_W:])


def _inproj_call(x, mod, norm1_g, w_q, w_all, qg256, kg256, ones_bd, cos_t, sin_t, tile):
    B, S, D = x.shape
    row = lambda b, t: (b, t, 0)
    return pl.pallas_call(
        _inproj_kernel,
        grid=(B, S // tile),
        in_specs=[
            pl.BlockSpec((1, tile, D), row),
            _const_spec(mod.shape),
            _const_spec((1, D)),
            _const_spec((D, Q_W)),
            _col_block_spec(D, Q_W, 1),
            _const_spec((1, MXU_TILE)),
            _const_spec((1, MXU_TILE)),
            _const_spec((MXU_TILE, MXU_TILE)),
            pl.BlockSpec((tile, LANES), lambda b, t: (t, 0)),
            pl.BlockSpec((tile, LANES), lambda b, t: (t, 0)),
        ],
        out_specs=[
            pl.BlockSpec((1, tile, Q_W), row),
            pl.BlockSpec((1, tile, KV_W), row),
            pl.BlockSpec((1, tile, KV_W), row),
            pl.BlockSpec((1, tile, POOL_W), row),
        ],
        out_shape=[
            jax.ShapeDtypeStruct((B, S, Q_W), BF16),
            jax.ShapeDtypeStruct((B, S, KV_W), BF16),
            jax.ShapeDtypeStruct((B, S, KV_W), BF16),
            jax.ShapeDtypeStruct((B, S, POOL_W), F32),
        ],
        compiler_params=_params(2),
        name="in_proj",
    )(x, mod, norm1_g, w_q, w_all, qg256, kg256, ones_bd, cos_t, sin_t)


def _attn_kernel(sink_ref, q_ref, kp_ref, kc0_ref, kn_ref, vp_ref, vc0_ref, vn_ref,
                 kctx_ref, vctx_ref, bias_ref, o_ref, *, blocks_per_tile):
    t_idx = pl.program_id(1)
    n_tiles = pl.num_programs(1)
    k_all = jnp.concatenate([kp_ref[0], kc0_ref[0], kn_ref[0]], axis=0)
    v_all = jnp.concatenate([vp_ref[0], vc0_ref[0], vn_ref[0]], axis=0)
    k_ctx = kctx_ref[0]
    v_ctx = vctx_ref[0]
    lane = lax.broadcasted_iota(jnp.int32, (BLOCK, MXU_TILE), 1)
    in_head = [(lane >= kv * HEAD_DIM) & (lane < (kv + 1) * HEAD_DIM) for kv in range(N_KV_HEADS)]
    zero = jnp.zeros((), BF16)
    one = jnp.ones((), BF16)

    def with_ones(v, half):
        vl = lax.broadcasted_iota(jnp.int32, v.shape, 1)
        return jnp.where((vl < LANES) if half == 0 else (vl >= LANES), v, one)

    v_ctx_pair = [with_ones(v_ctx, half) for half in range(2)]
    low_head = lax.broadcasted_iota(jnp.int32, (BLOCK, LANES), 1) < HEAD_DIM

    def block_operands(i):
        variant = jnp.int32(0)
        if i == blocks_per_tile - 1:
            variant = jnp.where(t_idx == n_tiles - 1, 2, variant)
        if i == 0:
            variant = jnp.where(t_idx == 0, 1, variant)
        v_win = v_all[i * BLOCK:(i + 3) * BLOCK]
        return dict(bias=bias_ref[variant], k_win=k_all[i * BLOCK:(i + 3) * BLOCK],
                    v_win_pair=[with_ones(v_win, half) for half in range(2)])

    def scores(blk, i, j):
        q_slab = q_ref[0, i * BLOCK:(i + 1) * BLOCK, j * MXU_TILE:(j + 1) * MXU_TILE]
        qz = jnp.concatenate([jnp.where(in_head[kv], q_slab, zero) for kv in range(N_KV_HEADS)], axis=0)
        return _dot_nt(qz, blk["k_win"]), _dot_nt(qz, k_ctx)

    def softmax_values(blk, i, j, s_win, s_ctx):
        p_win, p_ctx, sink_term = [], [], []
        for kv in range(N_KV_HEADS):
            rows = slice(kv * BLOCK, (kv + 1) * BLOCK)
            sw = s_win[rows] + blk["bias"]
            sc = s_ctx[rows]
            sink = sink_ref[kv * GROUP + j] * LOG2_E
            m = jnp.maximum(
                jnp.maximum(jnp.max(sw, axis=-1, keepdims=True),
                            jnp.max(sc, axis=-1, keepdims=True)), sink)
            p_win.append(jnp.exp2(sw - m).astype(BF16))
            p_ctx.append(jnp.exp2(sc - m).astype(BF16))
            sink_term.append(jnp.exp2(sink - m))
        for half in range(2):
            pw = jnp.concatenate(p_win[2 * half:2 * half + 2], axis=0)
            pc = jnp.concatenate(p_ctx[2 * half:2 * half + 2], axis=0)
            o = _dot(pw, blk["v_win_pair"][half]) + _dot(pc, v_ctx_pair[half])
            vals = o[:, half * LANES:(half + 1) * LANES]
            sums = o[:, (1 - half) * LANES:(2 - half) * LANES]
            out = []
            for r in range(2):
                rows = slice(r * BLOCK, (r + 1) * BLOCK)
                out.append(vals[rows] / (sums[rows] + sink_term[2 * half + r]))
            c0 = j * MXU_TILE + half * LANES
            o_ref[0, i * BLOCK:(i + 1) * BLOCK, c0:c0 + LANES] = jnp.where(low_head, out[0], out[1]).astype(BF16)

    blocks = [block_operands(i) for i in range(blocks_per_tile)]
    items = [(i, j) for i in range(blocks_per_tile) for j in range(GROUP)]
    s_next = scores(blocks[0], *items[0])
    for idx, (i, j) in enumerate(items):
        s_cur = s_next
        if idx + 1 < len(items):
            ni, nj = items[idx + 1]
            s_next = scores(blocks[ni], ni, nj)
        softmax_values(blocks[i], i, j, *s_cur)


def _attn_call(sink, q, k, v, kc, vc, bias, tile):
    B, S, _ = q.shape
    bpt = tile // BLOCK
    nb = S // BLOCK
    C = kc.shape[1]
    cur = lambda b, t, *_: (b, t, 0)
    prev = lambda b, t, *_: (b, jnp.maximum(t * bpt - 1, 0), 0)
    nxt = lambda b, t, *_: (b, jnp.minimum((t + 1) * bpt, nb - 1), 0)
    per_b = lambda b, t, *_: (b, 0, 0)
    halo = (1, BLOCK, KV_W)
    center = (1, tile, KV_W)
    grid_spec = pltpu.PrefetchScalarGridSpec(
        num_scalar_prefetch=1,
        grid=(B, S // tile),
        in_specs=[
            pl.BlockSpec((1, tile, Q_W), cur),
            pl.BlockSpec(halo, prev), pl.BlockSpec(center, cur), pl.BlockSpec(halo, nxt),
            pl.BlockSpec(halo, prev), pl.BlockSpec(center, cur), pl.BlockSpec(halo, nxt),
            pl.BlockSpec((1, C, KV_W), per_b),
            pl.BlockSpec((1, C, KV_W), per_b),
            pl.BlockSpec((3, BLOCK, 3 * BLOCK), lambda b, t, *_: (0, 0, 0)),
        ],
        out_specs=pl.BlockSpec((1, tile, Q_W), cur),
    )
    return pl.pallas_call(
        functools.partial(_attn_kernel, blocks_per_tile=bpt),
        grid_spec=grid_spec,
        out_shape=jax.ShapeDtypeStruct((B, S, Q_W), BF16),
        compiler_params=_params(2),
        name="attn",
    )(sink, q, k, k, k, v, v, v, kc, vc, bias)


def _merge_kernel(x_ref, mod_ref, g1_ref, attn_ref, up_ref, uc_ref, un_ref,
                  wg_ref, gb_ref, wap_ref, wpool_ref, wo_ref, o_ref, uext_ref, *, seq, tile):
    t_idx = pl.program_id(1)
    n_tiles = pl.num_programs(1)
    x = x_ref[0]
    m = mod_ref[pl.ds(pl.program_id(0), 1), :]
    sh = m[:, 0:D_MODEL]
    gs = g1_ref[...] * (1.0 + m[:, D_MODEL:2 * D_MODEL])
    gate1 = m[:, 2 * D_MODEL:3 * D_MODEL]
    a = _dot(attn_ref[0], wap_ref[...])
    h = _rms_modulate(x, gs, sh).astype(BF16)
    gates = jax.nn.sigmoid(_dot(h, wg_ref[...]) + gb_ref[...])

    keep_prev = (t_idx > 0).astype(F32)
    keep_next = (t_idx < n_tiles - 1).astype(F32)
    u_c = uc_ref[0]
    uext_ref[0:POOL_HALO] = up_ref[0] * keep_prev
    uext_ref[POOL_HALO:POOL_HALO + tile] = u_c
    uext_ref[POOL_HALO + tile:] = un_ref[0] * keep_next
    edge_rows = lax.broadcasted_iota(jnp.int32, (POOL_HALO, POOL_GROUP_W), 0)
    pos_first = t_idx * tile + edge_rows
    pos_last = pos_first + (tile - POOL_HALO)
    diffs = []
    for g, w in enumerate(POOL_WINDOWS):
        cols = slice(g * POOL_GROUP_W, (g + 1) * POOL_GROUP_W)
        win = uext_ref[pl.ds(POOL_HALO - w // 2, tile), cols]
        for j in range(1 - w // 2, w // 2):
            win = win + uext_ref[pl.ds(POOL_HALO + j, tile), cols]

        def inv_count(pos):
            return 1.0 / (jnp.minimum(pos + w // 2, seq) - jnp.maximum(pos - w // 2, 0)).astype(F32)

        pooled_avg = jnp.concatenate([
            win[:POOL_HALO] * inv_count(pos_first),
            win[POOL_HALO:tile - POOL_HALO] * (1.0 / w),
            win[tile - POOL_HALO:] * inv_count(pos_last)], axis=0)
        diffs.append((pooled_avg - u_c[:, cols]).astype(BF16))
    p = _dot(jnp.concatenate(diffs, axis=1), wpool_ref[...])

    merged = (gates[:, :D_MODEL] * a + gates[:, D_MODEL:] * p).astype(BF16)
    o_ref[0] = x + gate1 * _dot(merged, wo_ref[...])


def _merge_call(x, mod, norm1_g, attn, u, w_all, gate_b, w_ap, w_pool, w_out, tile):
    B, S, D = x.shape
    r = tile // POOL_HALO
    n_halo = S // POOL_HALO
    row = lambda b, t: (b, t, 0)
    return pl.pallas_call(
        functools.partial(_merge_kernel, seq=S, tile=tile),
        grid=(B, S // tile),
        in_specs=[
            pl.BlockSpec((1, tile, D), row),
            _const_spec(mod.shape),
            _const_spec((1, D)),
            pl.BlockSpec((1, tile, Q_W), row),
            pl.BlockSpec((1, POOL_HALO, POOL_W), lambda b, t: (b, jnp.maximum(t * r - 1, 0), 0)),
            pl.BlockSpec((1, tile, POOL_W), row),
            pl.BlockSpec((1, POOL_HALO, POOL_W), lambda b, t: (b, jnp.minimum((t + 1) * r, n_halo - 1), 0)),
            _col_block_spec(D, 2 * D, 1),
            _const_spec(gate_b.shape),
            _const_spec(w_ap.shape),
            _const_spec(w_pool.shape),
            _const_spec(w_out.shape),
        ],
        out_specs=pl.BlockSpec((1, tile, D), row),
        out_shape=jax.ShapeDtypeStruct((B, S, D), F32),
        scratch_shapes=[pltpu.VMEM((tile + 2 * POOL_HALO, POOL_W), F32)],
        compiler_params=_params(2),
        name="merge",
    )(x, mod, norm1_g, attn, u, u, u, w_all, gate_b, w_ap, w_pool, w_out)


def _ffn_kernel(x_ref, mod_ref, g2_ref, wup_ref, wdn_ref, o_ref, hmid_ref, *, d_ff):
    x = x_ref[0]
    m = mod_ref[pl.ds(pl.program_id(0), 1), :]
    sh = m[:, 3 * D_MODEL:4 * D_MODEL]
    gs = g2_ref[...] * (1.0 + m[:, 4 * D_MODEL:5 * D_MODEL])
    gate2 = m[:, 5 * D_MODEL:6 * D_MODEL]
    h = _rms_modulate(x, gs, sh).astype(BF16)
    for c in range(d_ff // MXU_TILE):
        a = _dot(h, wup_ref[:, c * MXU_TILE:(c + 1) * MXU_TILE])
        b = _dot(h, wup_ref[:, d_ff + c * MXU_TILE:d_ff + (c + 1) * MXU_TILE])
        hmid_ref[:, c * MXU_TILE:(c + 1) * MXU_TILE] = (a * jax.nn.sigmoid(a) * b).astype(BF16)
    o_ref[0] = x + gate2 * _dot(hmid_ref[...], wdn_ref[...])


def _ffn_call(x, mod, norm2_g, w_up, w_down, tile):
    B, S, D = x.shape
    d_ff = w_down.shape[0]
    row = lambda b, t: (b, t, 0)
    return pl.pallas_call(
        functools.partial(_ffn_kernel, d_ff=d_ff),
        grid=(B, S // tile),
        in_specs=[
            pl.BlockSpec((1, tile, D), row),
            _const_spec(mod.shape),
            _const_spec((1, D)),
            _const_spec(w_up.shape),
            _const_spec(w_down.shape),
        ],
        out_specs=pl.BlockSpec((1, tile, D), row),
        out_shape=jax.ShapeDtypeStruct((B, S, D), F32),
        scratch_shapes=[pltpu.VMEM((tile, d_ff), BF16)],
        compiler_params=_params(2),
        name="ffn",
    )(x, mod, norm2_g, w_up, w_down)


def _rope_tables(seq):
    t = np.arange(seq)
    row = (t // GRID_W).astype(np.float64)
    col = (t % GRID_W).astype(np.float64)
    half = HEAD_DIM // 2
    inv_freq = 1.0 / (ROPE_THETA ** (np.arange(0, half, 2, dtype=np.float64) / half))
    ang_r = row[:, None] * inv_freq
    ang_c = col[:, None] * inv_freq
    cos = np.concatenate([np.cos(ang_r)] * 2 + [np.cos(ang_c)] * 2, axis=1)
    sin = np.concatenate([-np.sin(ang_r), np.sin(ang_r), -np.sin(ang_c), np.sin(ang_c)], axis=1)
    reps = (1, LANES // HEAD_DIM)
    return jnp.asarray(np.tile(cos, reps), dtype=F32), jnp.asarray(np.tile(sin, reps), dtype=F32)


def _band_bias():
    i = np.arange(BLOCK)[:, None]
    j = np.arange(3 * BLOCK)[None, :]
    band = np.abs(j - BLOCK - i) <= WINDOW
    first = band & (j >= BLOCK)
    last = band & (j < 2 * BLOCK)
    masks = np.stack([band, first, last])
    return jnp.asarray(np.where(masks, 0.0, NEG_BIG), dtype=F32)


def kernel(x, c, ctx, c_ctx, mod_w, mod_b, norm1_g, norm2_g, w_in, gate_b, q_norm_g, k_norm_g,
           sink, pool_w, pool_scale, w_attn_proj, w_pool_proj, w_out, w_up, w_down):
    B, S, D = x.shape
    depth = mod_w.shape[0]
    cos_t, sin_t = _rope_tables(S)
    bias = _band_bias()
    ones_bd = jnp.asarray(np.kron(np.eye(MXU_TILE // HEAD_DIM), np.ones((HEAD_DIM, HEAD_DIM))), dtype=BF16)
    reps = MXU_TILE // HEAD_DIM
    assert depth == 1, "context-stream update between layers is not implemented"

    for l in range(depth):
        c8 = jnp.concatenate([c, c_ctx[None, :], jnp.zeros((8 - B - 1, D), F32)], axis=0)
        mod = _mod_call(c8, mod_w[l], mod_b[l][None, :])

        wl = w_in[l].astype(BF16)
        w_q = wl[:, :Q_W].reshape(D, N_KV_HEADS, GROUP, HEAD_DIM).transpose(0, 2, 1, 3).reshape(D, Q_W)
        w_ap = (w_attn_proj[l].reshape(N_KV_HEADS, GROUP, HEAD_DIM, D).transpose(1, 0, 2, 3)
                .reshape(Q_W, D).astype(BF16))
        g1 = norm1_g[l][None, :]
        kg256 = jnp.tile(k_norm_g[l], reps)[None, :]
        qg256 = jnp.tile(q_norm_g[l] * (HEAD_DIM ** -0.5 * LOG2_E), reps)[None, :]

        kc, vc = _ctx_call(ctx, mod, g1, wl, kg256, ones_bd)
        q, k, v, u = _inproj_call(x, mod, g1, w_q, wl, qg256, kg256, ones_bd, cos_t, sin_t, tile=512)
        attn = _attn_call(sink[l], q, k, v, kc, vc, bias, tile=512)
        w_pool = _pool_fold_call(pool_w[l], pool_scale[l].reshape(POOL_GROUPS, 1, POOL_GROUP_W), w_pool_proj[l])
        x = _merge_call(
            x, mod, g1, attn, u, wl, gate_b[l][None, :], w_ap, w_pool, w_out[l].astype(BF16), tile=512)
        x = _ffn_call(x, mod, norm2_g[l][None, :], w_up[l].astype(BF16), w_down[l].astype(BF16), tile=512)
    return x
```

```python
import functools

import jax
import jax.numpy as jnp
import numpy as np
from jax import lax
from jax.experimental import pallas as pl
from jax.experimental.pallas import tpu as pltpu

D_MODEL = 1024
GRID_W = 64
HEAD_DIM = 64
N_HEADS = 16
N_KV_HEADS = 4
GROUP = N_HEADS // N_KV_HEADS
Q_W = N_HEADS * HEAD_DIM
KV_W = N_KV_HEADS * HEAD_DIM
WINDOW = 128
BLOCK = 128
ROPE_THETA = 10000.0
POOL_WINDOWS = (2, 4, 8, 16)
POOL_GROUPS = 4
POOL_W = D_MODEL // 2
POOL_GROUP_W = POOL_W // POOL_GROUPS
POOL_HALO = 8
REST_START = Q_W + 2 * KV_W
N_MOD = 6
EPS = 1e-6
NEG_BIG = -1e30
LOG2_E = 1.4426950408889634

LANES = 128
MXU_TILE = 256
VMEM_LIMIT = 56 * 1024 * 1024

F32 = jnp.float32
BF16 = jnp.bfloat16


def _params(n_parallel):
    return pltpu.CompilerParams(
        dimension_semantics=("parallel",) * n_parallel,
        vmem_limit_bytes=VMEM_LIMIT,
    )


def _const_spec(shape):
    nd = len(shape)
    return pl.BlockSpec(shape, lambda *_: (0,) * nd, pipeline_mode=pl.Buffered(1))


def _col_block_spec(rows, width, block_idx):
    return pl.BlockSpec((rows, width), lambda *_: (0, block_idx), pipeline_mode=pl.Buffered(1))


def _dot(a, b):
    return jnp.dot(a, b, preferred_element_type=F32)


def _dot_nt(a, b):
    return lax.dot_general(a, b, (((1,), (1,)), ((), ())), preferred_element_type=F32)


def _rms_modulate(x, gs, sh):
    rs = lax.rsqrt(jnp.mean(x * x, axis=-1, keepdims=True) + EPS)
    return x * rs * gs + sh


def _head_rms(t, ones_bd):
    ms = _dot((t * t).astype(BF16), ones_bd) * (1.0 / HEAD_DIM)
    return lax.rsqrt(ms + EPS)


def _mod_kernel(c_ref, w_ref, b_ref, o_ref):
    c = c_ref[...]
    a = (c * jax.nn.sigmoid(c)).astype(BF16)
    o_ref[...] = _dot(a, w_ref[...].astype(BF16)) + b_ref[...]


def _mod_call(c8, mod_w, mod_b):
    n = mod_w.shape[1]
    bn = D_MODEL
    return pl.pallas_call(
        _mod_kernel,
        grid=(n // bn,),
        in_specs=[
            pl.BlockSpec((8, D_MODEL), lambda j: (0, 0)),
            pl.BlockSpec((D_MODEL, bn), lambda j: (0, j)),
            pl.BlockSpec((1, bn), lambda j: (0, j)),
        ],
        out_specs=pl.BlockSpec((8, bn), lambda j: (0, j)),
        out_shape=jax.ShapeDtypeStruct((8, n), F32),
        compiler_params=_params(1),
        name="mod",
    )(c8, mod_w, mod_b)


def _pool_fold_kernel(pw_ref, ps_ref, wpp_ref, o_ref):
    scaled = pw_ref[0] * ps_ref[0]
    o_ref[...] = jnp.dot(scaled, wpp_ref[...], preferred_element_type=F32,
                         precision=lax.Precision.HIGHEST).astype(BF16)


def _pool_fold_call(pool_w, pool_scale3, w_pool_proj):
    n_out = w_pool_proj.shape[1]
    return pl.pallas_call(
        _pool_fold_kernel,
        grid=(POOL_GROUPS,),
        in_specs=[
            pl.BlockSpec((1, POOL_GROUP_W, POOL_GROUP_W), lambda g: (g, 0, 0)),
            pl.BlockSpec((1, 1, POOL_GROUP_W), lambda g: (g, 0, 0)),
            pl.BlockSpec((POOL_GROUP_W, n_out), lambda g: (g, 0)),
        ],
        out_specs=pl.BlockSpec((POOL_GROUP_W, n_out), lambda g: (g, 0)),
        out_shape=jax.ShapeDtypeStruct((POOL_W, n_out), BF16),
        compiler_params=_params(1),
        name="pool_fold",
    )(pool_w, pool_scale3, w_pool_proj)


def _ctx_kernel(ctx_ref, mod_ref, g1_ref, w_ref, kg_ref, ones_ref, kc_ref, vc_ref, *, n_batch):
    m = mod_ref[n_batch:n_batch + 1, :]
    sh = m[:, 0:D_MODEL]
    gs = g1_ref[...] * (1.0 + m[:, D_MODEL:2 * D_MODEL])
    h = _rms_modulate(ctx_ref[0], gs, sh).astype(BF16)
    kv = _dot(h, w_ref[...])
    k = kv[:, :KV_W]
    kc_ref[0] = (k * _head_rms(k, ones_ref[...]) * kg_ref[...]).astype(BF16)
    vc_ref[0] = kv[:, KV_W:].astype(BF16)


def _ctx_call(ctx, mod, norm1_g, w_all, kg256, ones_bd):
    B, C, D = ctx.shape
    return pl.pallas_call(
        functools.partial(_ctx_kernel, n_batch=B),
        grid=(B,),
        in_specs=[
            pl.BlockSpec((1, C, D), lambda b: (b, 0, 0)),
            _const_spec(mod.shape),
            _const_spec((1, D)),
            _col_block_spec(D, 2 * KV_W, Q_W // (2 * KV_W)),
            _const_spec((1, KV_W)),
            _const_spec((MXU_TILE, MXU_TILE)),
        ],
        out_specs=[
            pl.BlockSpec((1, C, KV_W), lambda b: (b, 0, 0)),
            pl.BlockSpec((1, C, KV_W), lambda b: (b, 0, 0)),
        ],
        out_shape=[
            jax.ShapeDtypeStruct((B, C, KV_W), BF16),
            jax.ShapeDtypeStruct((B, C, KV_W), BF16),
        ],
        compiler_params=_params(1),
        name="ctx_kv",
    )(ctx, mod, norm1_g, w_all, kg256, ones_bd)


def _rope(t, cos, sin_signed, first_half):
    swap = jnp.where(first_half, pltpu.roll(t, LANES - 16, axis=1), pltpu.roll(t, 16, axis=1))
    return t * cos + swap * sin_signed


def _inproj_kernel(x_ref, mod_ref, g1_ref, wq_ref, wkvu_ref, qg_ref, kg_ref, ones_ref, cos_ref, sin_ref,
                   q_ref, k_ref, v_ref, u_ref):
    m = mod_ref[pl.ds(pl.program_id(0), 1), :]
    sh = m[:, 0:D_MODEL]
    gs = g1_ref[...] * (1.0 + m[:, D_MODEL:2 * D_MODEL])
    h = _rms_modulate(x_ref[0], gs, sh).astype(BF16)
    ones_bd = ones_ref[...]
    cos = cos_ref[...]
    sin = sin_ref[...]
    lane = lax.broadcasted_iota(jnp.int32, cos.shape, 1)
    first_half = (lane & 31) < 16

    def normed_rope(t, gain):
        t = t * _head_rms(t, ones_bd) * gain
        halves = [_rope(t[:, i * LANES:(i + 1) * LANES], cos, sin, first_half) for i in range(2)]
        return jnp.concatenate(halves, axis=1).astype(BF16)

    wide = 2 * MXU_TILE
    for j in range(Q_W // wide):
        t = _dot(h, wq_ref[:, j * wide:(j + 1) * wide])
        for i in range(2):
            c0 = j * wide + i * MXU_TILE
            q_ref[0, :, c0:c0 + MXU_TILE] = normed_rope(t[:, i * MXU_TILE:(i + 1) * MXU_TILE], qg_ref[...])
    kv = _dot(h, wkvu_ref[:, :2 * KV_W])
    k_ref[0] = normed_rope(kv[:, :KV_W], kg_ref[...])
    v_ref[0] = kv[:, KV_W:].astype(BF16)
    u_ref[0] = _dot(h, wkvu_ref[:, 2 * KV_W:])


def _inproj_call(x, mod, norm1_g, w_q, w_all, qg256, kg256, ones_bd, cos_t, sin_t, tile):
    B, S, D = x.shape
    row = lambda b, t: (b, t, 0)
    return pl.pallas_call(
        _inproj_kernel,
        grid=(B, S // tile),
        in_specs=[
            pl.BlockSpec((1, tile, D), row),
            _const_spec(mod.shape),
            _const_spec((1, D)),
            _const_spec((D, Q_W)),
            _col_block_spec(D, Q_W, 1),
            _const_spec((1, MXU_TILE)),
            _const_spec((1, MXU_TILE)),
            _const_spec((MXU_TILE, MXU_TILE)),
            pl.BlockSpec((tile, LANES), lambda b, t: (t, 0)),
            pl.BlockSpec((tile, LANES), lambda b, t: (t, 0)),
        ],
        out_specs=[
            pl.BlockSpec((1, tile, Q_W), row),
            pl.BlockSpec((1, tile, KV_W), row),
            pl.BlockSpec((1, tile, KV_W), row),
            pl.BlockSpec((1, tile, POOL_W), row),
        ],
        out_shape=[
            jax.ShapeDtypeStruct((B, S, Q_W), BF16),
            jax.ShapeDtypeStruct((B, S, KV_W), BF16),
            jax.ShapeDtypeStruct((B, S, KV_W), BF16),
            jax.ShapeDtypeStruct((B, S, POOL_W), F32),
        ],
        compiler_params=_params(2),
        name="in_proj",
    )(x, mod, norm1_g, w_q, w_all, qg256, kg256, ones_bd, cos_t, sin_t)


def _attn_kernel(sink_ref, q_ref, kp_ref, kc0_ref, kn_ref, vp_ref, vc0_ref, vn_ref,
                 kctx_ref, vctx_ref, bias_ref, o_ref, *, blocks_per_tile):
    t_idx = pl.program_id(1)
    n_tiles = pl.num_programs(1)
    k_all = jnp.concatenate([kp_ref[0], kc0_ref[0], kn_ref[0]], axis=0)
    v_all = jnp.concatenate([vp_ref[0], vc0_ref[0], vn_ref[0]], axis=0)
    k_ctx = kctx_ref[0]
    v_ctx = vctx_ref[0]
    lane = lax.broadcasted_iota(jnp.int32, (BLOCK, MXU_TILE), 1)
    in_head = [(lane >= kv * HEAD_DIM) & (lane < (kv + 1) * HEAD_DIM) for kv in range(N_KV_HEADS)]
    zero = jnp.zeros((), BF16)
    one = jnp.ones((), BF16)

    def with_ones(v, half):
        vl = lax.broadcasted_iota(jnp.int32, v.shape, 1)
        return jnp.where((vl < LANES) if half == 0 else (vl >= LANES), v, one)

    v_ctx_pair = [with_ones(v_ctx, half) for half in range(2)]
    low_head = lax.broadcasted_iota(jnp.int32, (BLOCK, LANES), 1) < HEAD_DIM

    def block_operands(i):
        variant = jnp.int32(0)
        if i == blocks_per_tile - 1:
            variant = jnp.where(t_idx == n_tiles - 1, 2, variant)
        if i == 0:
            variant = jnp.where(t_idx == 0, 1, variant)
        v_win = v_all[i * BLOCK:(i + 3) * BLOCK]
        return dict(bias=bias_ref[variant], k_win=k_all[i * BLOCK:(i + 3) * BLOCK],
                    v_win_pair=[with_ones(v_win, half) for half in range(2)])

    def scores(blk, i, j):
        q_slab = q_ref[0, i * BLOCK:(i + 1) * BLOCK, j * MXU_TILE:(j + 1) * MXU_TILE]
        qz = jnp.concatenate([jnp.where(in_head[kv], q_slab, zero) for kv in range(N_KV_HEADS)], axis=0)
        return _dot_nt(qz, blk["k_win"]), _dot_nt(qz, k_ctx)

    def softmax_values(blk, i, j, s_win, s_ctx):
        p_win, p_ctx, sink_term = [], [], []
        for kv in range(N_KV_HEADS):
            rows = slice(kv * BLOCK, (kv + 1) * BLOCK)
            sw = s_win[rows] + blk["bias"]
            sc = s_ctx[rows]
            sink = sink_ref[kv * GROUP + j] * LOG2_E
            m = jnp.maximum(
                jnp.maximum(jnp.max(sw, axis=-1, keepdims=True),
                            jnp.max(sc, axis=-1, keepdims=True)), sink)
            p_win.append(jnp.exp2(sw - m).astype(BF16))
            p_ctx.append(jnp.exp2(sc - m).astype(BF16))
            sink_term.append(jnp.exp2(sink - m))
        for half in range(2):
            pw = jnp.concatenate(p_win[2 * half:2 * half + 2], axis=0)
            pc = jnp.concatenate(p_ctx[2 * half:2 * half + 2], axis=0)
            o = _dot(pw, blk["v_win_pair"][half]) + _dot(pc, v_ctx_pair[half])
            vals = o[:, half * LANES:(half + 1) * LANES]
            sums = o[:, (1 - half) * LANES:(2 - half) * LANES]
            out = []
            for r in range(2):
                rows = slice(r * BLOCK, (r + 1) * BLOCK)
                out.append(vals[rows] / (sums[rows] + sink_term[2 * half + r]))
            c0 = j * MXU_TILE + half * LANES
            o_ref[0, i * BLOCK:(i + 1) * BLOCK, c0:c0 + LANES] = jnp.where(low_head, out[0], out[1]).astype(BF16)

    blocks = [block_operands(i) for i in range(blocks_per_tile)]
    items = [(i, j) for i in range(blocks_per_tile) for j in range(GROUP)]
    s_next = scores(blocks[0], *items[0])
    for idx, (i, j) in enumerate(items):
        s_cur = s_next
        if idx + 1 < len(items):
            ni, nj = items[idx + 1]
            s_next = scores(blocks[ni], ni, nj)
        softmax_values(blocks[i], i, j, *s_cur)


def _attn_call(sink, q, k, v, kc, vc, bias, tile):
    B, S, _ = q.shape
    bpt = tile // BLOCK
    nb = S // BLOCK
    C = kc.shape[1]
    cur = lambda b, t, *_: (b, t, 0)
    prev = lambda b, t, *_: (b, jnp.maximum(t * bpt - 1, 0), 0)
    nxt = lambda b, t, *_: (b, jnp.minimum((t + 1) * bpt, nb - 1), 0)
    per_b = lambda b, t, *_: (b, 0, 0)
    halo = (1, BLOCK, KV_W)
    center = (1, tile, KV_W)
    grid_spec = pltpu.PrefetchScalarGridSpec(
        num_scalar_prefetch=1,
        grid=(B, S // tile),
        in_specs=[
            pl.BlockSpec((1, tile, Q_W), cur),
            pl.BlockSpec(halo, prev), pl.BlockSpec(center, cur), pl.BlockSpec(halo, nxt),
            pl.BlockSpec(halo, prev), pl.BlockSpec(center, cur), pl.BlockSpec(halo, nxt),
            pl.BlockSpec((1, C, KV_W), per_b),
            pl.BlockSpec((1, C, KV_W), per_b),
            pl.BlockSpec((3, BLOCK, 3 * BLOCK), lambda b, t, *_: (0, 0, 0)),
        ],
        out_specs=pl.BlockSpec((1, tile, Q_W), cur),
    )
    return pl.pallas_call(
        functools.partial(_attn_kernel, blocks_per_tile=bpt),
        grid_spec=grid_spec,
        out_shape=jax.ShapeDtypeStruct((B, S, Q_W), BF16),
        compiler_params=_params(2),
        name="attn",
    )(sink, q, k, k, k, v, v, v, kc, vc, bias)


def _merge_kernel(x_ref, mod_ref, g1_ref, attn_ref, up_ref, uc_ref, un_ref,
                  wg_ref, gb_ref, wap_ref, wpool_ref, wo_ref, o_ref, uext_ref, *, seq, tile, sub):
    t_idx = pl.program_id(1)
    n_tiles = pl.num_programs(1)
    m = mod_ref[pl.ds(pl.program_id(0), 1), :]
    sh = m[:, 0:D_MODEL]
    gs = g1_ref[...] * (1.0 + m[:, D_MODEL:2 * D_MODEL])
    gate1 = m[:, 2 * D_MODEL:3 * D_MODEL]

    keep_prev = (t_idx > 0).astype(F32)
    keep_next = (t_idx < n_tiles - 1).astype(F32)
    uext_ref[0:POOL_HALO] = up_ref[0] * keep_prev
    uext_ref[POOL_HALO:POOL_HALO + tile] = uc_ref[0]
    uext_ref[POOL_HALO + tile:] = un_ref[0] * keep_next
    edge_rows = lax.broadcasted_iota(jnp.int32, (POOL_HALO, POOL_GROUP_W), 0)

    def attn_dot(r):
        return _dot(attn_ref[0, r * sub:(r + 1) * sub, :], wap_ref[...])

    def gate_dot(r):
        h = _rms_modulate(x_ref[0, r * sub:(r + 1) * sub, :], gs, sh).astype(BF16)
        return jax.nn.sigmoid(_dot(h, wg_ref[...]) + gb_ref[...])

    def pool_dot(r):
        r0 = r * sub
        pos_first = t_idx * tile + r0 + edge_rows
        pos_last = pos_first + (sub - POOL_HALO)
        diffs = []
        for g, w in enumerate(POOL_WINDOWS):
            cols = slice(g * POOL_GROUP_W, (g + 1) * POOL_GROUP_W)
            acc = uext_ref[pl.ds(r0, sub + 2 * POOL_HALO), cols]
            k = 1
            while k < w:
                acc = acc[:acc.shape[0] - k] + acc[k:]
                k *= 2
            win = acc[POOL_HALO - w // 2:POOL_HALO - w // 2 + sub]

            def inv_count(pos):
                return 1.0 / (jnp.minimum(pos + w // 2, seq) - jnp.maximum(pos - w // 2, 0)).astype(F32)

            pooled_avg = jnp.concatenate([
                win[:POOL_HALO] * inv_count(pos_first),
                win[POOL_HALO:sub - POOL_HALO] * (1.0 / w),
                win[sub - POOL_HALO:] * inv_count(pos_last)], axis=0)
            diffs.append((pooled_avg - uext_ref[pl.ds(r0 + POOL_HALO, sub), cols]).astype(BF16))
        return _dot(jnp.concatenate(diffs, axis=1), wpool_ref[...])

    def output(r, a, gates, p):
        rows = slice(r * sub, (r + 1) * sub)
        merged = (gates[:, :D_MODEL] * a + gates[:, D_MODEL:] * p).astype(BF16)
        o_ref[0, rows, :] = x_ref[0, rows, :] + gate1 * _dot(merged, wo_ref[...])

    n_sub = tile // sub
    a = [attn_dot(r) for r in range(n_sub)]
    p = [pool_dot(r) for r in range(n_sub)]
    gates = [gate_dot(r) for r in range(n_sub)]
    for r in range(n_sub):
        output(r, a[r], gates[r], p[r])


def _merge_call(x, mod, norm1_g, attn, u, w_all, gate_b, w_ap, w_pool, w_out, tile):
    B, S, D = x.shape
    r = tile // POOL_HALO
    n_halo = S // POOL_HALO
    row = lambda b, t: (b, t, 0)
    return pl.pallas_call(
        functools.partial(_merge_kernel, seq=S, tile=tile, sub=MXU_TILE),
        grid=(B, S // tile),
        in_specs=[
            pl.BlockSpec((1, tile, D), row),
            _const_spec(mod.shape),
            _const_spec((1, D)),
            pl.BlockSpec((1, tile, Q_W), row),
            pl.BlockSpec((1, POOL_HALO, POOL_W), lambda b, t: (b, jnp.maximum(t * r - 1, 0), 0)),
            pl.BlockSpec((1, tile, POOL_W), row),
            pl.BlockSpec((1, POOL_HALO, POOL_W), lambda b, t: (b, jnp.minimum((t + 1) * r, n_halo - 1), 0)),
            _col_block_spec(D, 2 * D, 1),
            _const_spec(gate_b.shape),
            _const_spec(w_ap.shape),
            _const_spec(w_pool.shape),
            _const_spec(w_out.shape),
        ],
        out_specs=pl.BlockSpec((1, tile, D), row),
        out_shape=jax.ShapeDtypeStruct((B, S, D), F32),
        scratch_shapes=[pltpu.VMEM((tile + 2 * POOL_HALO, POOL_W), F32)],
        compiler_params=_params(2),
        name="merge",
    )(x, mod, norm1_g, attn, u, u, u, w_all, gate_b, w_ap, w_pool, w_out)


def _ffn_kernel(x_ref, mod_ref, g2_ref, wup_ref, wdn_ref, o_ref, hmid_ref, *, d_ff):
    x = x_ref[0]
    m = mod_ref[pl.ds(pl.program_id(0), 1), :]
    sh = m[:, 3 * D_MODEL:4 * D_MODEL]
    gs = g2_ref[...] * (1.0 + m[:, 4 * D_MODEL:5 * D_MODEL])
    gate2 = m[:, 5 * D_MODEL:6 * D_MODEL]
    h = _rms_modulate(x, gs, sh).astype(BF16)
    for c in range(d_ff // MXU_TILE):
        a = _dot(h, wup_ref[:, c * MXU_TILE:(c + 1) * MXU_TILE])
        b = _dot(h, wup_ref[:, d_ff + c * MXU_TILE:d_ff + (c + 1) * MXU_TILE])
        hmid_ref[:, c * MXU_TILE:(c + 1) * MXU_TILE] = (a * jax.nn.sigmoid(a) * b).astype(BF16)
    o_ref[0] = x + gate2 * _dot(hmid_ref[...], wdn_ref[...])


def _ffn_call(x, mod, norm2_g, w_up, w_down, tile):
    B, S, D = x.shape
    d_ff = w_down.shape[0]
    row = lambda b, t: (b, t, 0)
    return pl.pallas_call(
        functools.partial(_ffn_kernel, d_ff=d_ff),
        grid=(B, S // tile),
        in_specs=[
            pl.BlockSpec((1, tile, D), row),
            _const_spec(mod.shape),
            _const_spec((1, D)),
            _const_spec(w_up.shape),
            _const_spec(w_down.shape),
        ],
        out_specs=pl.BlockSpec((1, tile, D), row),
        out_shape=jax.ShapeDtypeStruct((B, S, D), F32),
        scratch_shapes=[pltpu.VMEM((tile, d_ff), BF16)],
        compiler_params=_params(2),
        name="ffn",
    )(x, mod, norm2_g, w_up, w_down)


def _rope_tables(seq):
    t = np.arange(seq)
    row = (t // GRID_W).astype(np.float64)
    col = (t % GRID_W).astype(np.float64)
    half = HEAD_DIM // 2
    inv_freq = 1.0 / (ROPE_THETA ** (np.arange(0, half, 2, dtype=np.float64) / half))
    ang_r = row[:, None] * inv_freq
    ang_c = col[:, None] * inv_freq
    cos = np.concatenate([np.cos(ang_r)] * 2 + [np.cos(ang_c)] * 2, axis=1)
    sin = np.concatenate([-np.sin(ang_r), np.sin(ang_r), -np.sin(ang_c), np.sin(ang_c)], axis=1)
    reps = (1, LANES // HEAD_DIM)
    return jnp.asarray(np.tile(cos, reps), dtype=F32), jnp.asarray(np.tile(sin, reps), dtype=F32)


def _band_bias():
    i = np.arange(BLOCK)[:, None]
    j = np.arange(3 * BLOCK)[None, :]
    band = np.abs(j - BLOCK - i) <= WINDOW
    first = band & (j >= BLOCK)
    last = band & (j < 2 * BLOCK)
    masks = np.stack([band, first, last])
    return jnp.asarray(np.where(masks, 0.0, NEG_BIG), dtype=F32)


def kernel(x, c, ctx, c_ctx, mod_w, mod_b, norm1_g, norm2_g, w_in, gate_b, q_norm_g, k_norm_g,
           sink, pool_w, pool_scale, w_attn_proj, w_pool_proj, w_out, w_up, w_down):
    B, S, D = x.shape
    depth = mod_w.shape[0]
    cos_t, sin_t = _rope_tables(S)
    bias = _band_bias()
    ones_bd = jnp.asarray(np.kron(np.eye(MXU_TILE // HEAD_DIM), np.ones((HEAD_DIM, HEAD_DIM))), dtype=BF16)
    reps = MXU_TILE // HEAD_DIM
    assert depth == 1, "context-stream update between layers is not implemented"

    for l in range(depth):
        c8 = jnp.concatenate([c, c_ctx[None, :], jnp.zeros((8 - B - 1, D), F32)], axis=0)
        mod = _mod_call(c8, mod_w[l], mod_b[l][None, :])

        wl = w_in[l].astype(BF16)
        w_q = wl[:, :Q_W].reshape(D, N_KV_HEADS, GROUP, HEAD_DIM).transpose(0, 2, 1, 3).reshape(D, Q_W)
        w_ap = (w_attn_proj[l].reshape(N_KV_HEADS, GROUP, HEAD_DIM, D).transpose(1, 0, 2, 3)
                .reshape(Q_W, D).astype(BF16))
        g1 = norm1_g[l][None, :]
        kg256 = jnp.tile(k_norm_g[l], reps)[None, :]
        qg256 = jnp.tile(q_norm_g[l] * (HEAD_DIM ** -0.5 * LOG2_E), reps)[None, :]

        kc, vc = _ctx_call(ctx, mod, g1, wl, kg256, ones_bd)
        q, k, v, u = _inproj_call(x, mod, g1, w_q, wl, qg256, kg256, ones_bd, cos_t, sin_t, tile=1024)
        attn = _attn_call(sink[l], q, k, v, kc, vc, bias, tile=512)
        w_pool = _pool_fold_call(pool_w[l], pool_scale[l].reshape(POOL_GROUPS, 1, POOL_GROUP_W), w_pool_proj[l])
        x = _merge_call(
            x, mod, g1, attn, u, wl, gate_b[l][None, :], w_ap, w_pool, w_out[l].astype(BF16), tile=512)
        x = _ffn_call(x, mod, norm2_g[l][None, :], w_up[l].astype(BF16), w_down[l].astype(BF16), tile=512)
    return x
```

```python
import functools

import jax
import jax.numpy as jnp
import numpy as np
from jax import lax
from jax.experimental import pallas as pl
from jax.experimental.pallas import tpu as pltpu

D_MODEL = 1024
GRID_W = 64
HEAD_DIM = 64
N_HEADS = 16
N_KV_HEADS = 4
GROUP = N_HEADS // N_KV_HEADS
Q_W = N_HEADS * HEAD_DIM
KV_W = N_KV_HEADS * HEAD_DIM
WINDOW = 128
BLOCK = 128
ROPE_THETA = 10000.0
POOL_WINDOWS = (2, 4, 8, 16)
POOL_GROUPS = 4
POOL_W = D_MODEL // 2
POOL_GROUP_W = POOL_W // POOL_GROUPS
POOL_HALO = 8
REST_START = Q_W + 2 * KV_W
N_MOD = 6
EPS = 1e-6
NEG_BIG = -1e30
LOG2_E = 1.4426950408889634

LANES = 128
MXU_TILE = 256
VMEM_LIMIT = 56 * 1024 * 1024

F32 = jnp.float32
BF16 = jnp.bfloat16


def _params(n_parallel):
    return pltpu.CompilerParams(
        dimension_semantics=("parallel",) * n_parallel,
        vmem_limit_bytes=VMEM_LIMIT,
    )


def _const_spec(shape):
    nd = len(shape)
    return pl.BlockSpec(shape, lambda *_: (0,) * nd, pipeline_mode=pl.Buffered(1))


def _col_block_spec(rows, width, block_idx):
    return pl.BlockSpec((rows, width), lambda *_: (0, block_idx), pipeline_mode=pl.Buffered(1))


def _dot(a, b):
    return jnp.dot(a, b, preferred_element_type=F32)


def _dot_nt(a, b):
    return lax.dot_general(a, b, (((1,), (1,)), ((), ())), preferred_element_type=F32)


def _rms_modulate(x, gs, sh):
    rs = lax.rsqrt(jnp.mean(x * x, axis=-1, keepdims=True) + EPS)
    return x * rs * gs + sh


def _head_rms(t, ones_bd):
    ms = _dot((t * t).astype(BF16), ones_bd) * (1.0 / HEAD_DIM)
    return lax.rsqrt(ms + EPS)


def _mod_kernel(c_ref, w_ref, b_ref, o_ref):
    c = c_ref[...]
    a = (c * jax.nn.sigmoid(c)).astype(BF16)
    o_ref[...] = _dot(a, w_ref[...].astype(BF16)) + b_ref[...]


def _mod_call(c8, mod_w, mod_b):
    n = mod_w.shape[1]
    bn = D_MODEL
    return pl.pallas_call(
        _mod_kernel,
        grid=(n // bn,),
        in_specs=[
            pl.BlockSpec((8, D_MODEL), lambda j: (0, 0)),
            pl.BlockSpec((D_MODEL, bn), lambda j: (0, j)),
            pl.BlockSpec((1, bn), lambda j: (0, j)),
        ],
        out_specs=pl.BlockSpec((8, bn), lambda j: (0, j)),
        out_shape=jax.ShapeDtypeStruct((8, n), F32),
        compiler_params=_params(1),
        name="mod",
    )(c8, mod_w, mod_b)


def _pool_fold_kernel(pw_ref, ps_ref, wpp_ref, o_ref):
    scaled = pw_ref[0] * ps_ref[0]
    o_ref[...] = jnp.dot(scaled, wpp_ref[...], preferred_element_type=F32,
                         precision=lax.Precision.HIGHEST).astype(BF16)


def _pool_fold_call(pool_w, pool_scale3, w_pool_proj):
    n_out = w_pool_proj.shape[1]
    return pl.pallas_call(
        _pool_fold_kernel,
        grid=(POOL_GROUPS,),
        in_specs=[
            pl.BlockSpec((1, POOL_GROUP_W, POOL_GROUP_W), lambda g: (g, 0, 0)),
            pl.BlockSpec((1, 1, POOL_GROUP_W), lambda g: (g, 0, 0)),
            pl.BlockSpec((POOL_GROUP_W, n_out), lambda g: (g, 0)),
        ],
        out_specs=pl.BlockSpec((POOL_GROUP_W, n_out), lambda g: (g, 0)),
        out_shape=jax.ShapeDtypeStruct((POOL_W, n_out), BF16),
        compiler_params=_params(1),
        name="pool_fold",
    )(pool_w, pool_scale3, w_pool_proj)


def _ctx_kernel(ctx_ref, mod_ref, g1_ref, w_ref, kg_ref, ones_ref, kc_ref, vc_ref, *, n_batch):
    m = mod_ref[n_batch:n_batch + 1, :]
    sh = m[:, 0:D_MODEL]
    gs = g1_ref[...] * (1.0 + m[:, D_MODEL:2 * D_MODEL])
    h = _rms_modulate(ctx_ref[0], gs, sh).astype(BF16)
    kv = _dot(h, w_ref[...])
    k = kv[:, :KV_W]
    kc_ref[0] = (k * _head_rms(k, ones_ref[...]) * kg_ref[...]).astype(BF16)
    vc_ref[0] = kv[:, KV_W:].T.astype(BF16)


def _ctx_call(ctx, mod, norm1_g, w_all, kg256, ones_bd):
    B, C, D = ctx.shape
    return pl.pallas_call(
        functools.partial(_ctx_kernel, n_batch=B),
        grid=(B,),
        in_specs=[
            pl.BlockSpec((1, C, D), lambda b: (b, 0, 0)),
            _const_spec(mod.shape),
            _const_spec((1, D)),
            _col_block_spec(D, 2 * KV_W, Q_W // (2 * KV_W)),
            _const_spec((1, KV_W)),
            _const_spec((MXU_TILE, MXU_TILE)),
        ],
        out_specs=[
            pl.BlockSpec((1, C, KV_W), lambda b: (b, 0, 0)),
            pl.BlockSpec((1, KV_W, C), lambda b: (b, 0, 0)),
        ],
        out_shape=[
            jax.ShapeDtypeStruct((B, C, KV_W), BF16),
            jax.ShapeDtypeStruct((B, KV_W, C), BF16),
        ],
        compiler_params=_params(1),
        name="ctx_kv",
    )(ctx, mod, norm1_g, w_all, kg256, ones_bd)


def _rope(t, cos, sin_signed, first_half):
    swap = jnp.where(first_half, pltpu.roll(t, LANES - 16, axis=1), pltpu.roll(t, 16, axis=1))
    return t * cos + swap * sin_signed


def _inproj_kernel(x_ref, mod_ref, g1_ref, wq_ref, wkvu_ref, qg_ref, kg_ref, ones_ref, cos_ref, sin_ref,
                   q_ref, k_ref, v_ref, u_ref):
    m = mod_ref[pl.ds(pl.program_id(0), 1), :]
    sh = m[:, 0:D_MODEL]
    gs = g1_ref[...] * (1.0 + m[:, D_MODEL:2 * D_MODEL])
    h = _rms_modulate(x_ref[0], gs, sh).astype(BF16)
    ones_bd = ones_ref[...]
    cos = cos_ref[...]
    sin = sin_ref[...]
    lane = lax.broadcasted_iota(jnp.int32, cos.shape, 1)
    first_half = (lane & 31) < 16

    def normed_rope(t, gain):
        t = t * _head_rms(t, ones_bd) * gain
        halves = [_rope(t[:, i * LANES:(i + 1) * LANES], cos, sin, first_half) for i in range(2)]
        return jnp.concatenate(halves, axis=1).astype(BF16)

    wide = 2 * MXU_TILE
    for j in range(Q_W // wide):
        t = _dot(h, wq_ref[:, j * wide:(j + 1) * wide])
        for i in range(2):
            c0 = j * wide + i * MXU_TILE
            q_ref[0, :, c0:c0 + MXU_TILE] = normed_rope(t[:, i * MXU_TILE:(i + 1) * MXU_TILE], qg_ref[...])
    kv = _dot(h, wkvu_ref[:, :2 * KV_W])
    k_ref[0] = normed_rope(kv[:, :KV_W], kg_ref[...])
    v_ref[0] = kv[:, KV_W:].T.astype(BF16)
    u_ref[0] = _dot(h, wkvu_ref[:, 2 * KV_W:])


def _inproj_call(x, mod, norm1_g, w_q, w_all, qg256, kg256, ones_bd, cos_t, sin_t, tile):
    B, S, D = x.shape
    row = lambda b, t: (b, t, 0)
    return pl.pallas_call(
        _inproj_kernel,
        grid=(B, S // tile),
        in_specs=[
            pl.BlockSpec((1, tile, D), row),
            _const_spec(mod.shape),
            _const_spec((1, D)),
            _const_spec((D, Q_W)),
            _col_block_spec(D, Q_W, 1),
            _const_spec((1, MXU_TILE)),
            _const_spec((1, MXU_TILE)),
            _const_spec((MXU_TILE, MXU_TILE)),
            pl.BlockSpec((tile, LANES), lambda b, t: (t, 0)),
            pl.BlockSpec((tile, LANES), lambda b, t: (t, 0)),
        ],
        out_specs=[
            pl.BlockSpec((1, tile, Q_W), row),
            pl.BlockSpec((1, tile, KV_W), row),
            pl.BlockSpec((1, KV_W, tile), lambda b, t: (b, 0, t)),
            pl.BlockSpec((1, tile, POOL_W), row),
        ],
        out_shape=[
            jax.ShapeDtypeStruct((B, S, Q_W), BF16),
            jax.ShapeDtypeStruct((B, S, KV_W), BF16),
            jax.ShapeDtypeStruct((B, KV_W, S), BF16),
            jax.ShapeDtypeStruct((B, S, POOL_W), F32),
        ],
        compiler_params=_params(2),
        name="in_proj",
    )(x, mod, norm1_g, w_q, w_all, qg256, kg256, ones_bd, cos_t, sin_t)


ONES_ROWS = 16
SCORE_LOOKAHEAD = 3


def _attn_kernel(sink_ref, q_ref, kp_ref, kc0_ref, kn_ref, vp_ref, vc0_ref, vn_ref,
                 kctx_ref, vctx_ref, bias_ref, o_ref, *, blocks_per_tile):
    t_idx = pl.program_id(1)
    n_tiles = pl.num_programs(1)
    n_ctx = kctx_ref.shape[1]
    k_all = jnp.concatenate([kp_ref[0], kc0_ref[0], kn_ref[0]], axis=0)
    vt_all = jnp.concatenate([vp_ref[0], vc0_ref[0], vn_ref[0]], axis=1)
    k_ctx = kctx_ref[0]
    vt_ctx = vctx_ref[0]
    lane = lax.broadcasted_iota(jnp.int32, (BLOCK, MXU_TILE), 1)
    in_head = [(lane >= kv * HEAD_DIM) & (lane < (kv + 1) * HEAD_DIM) for kv in range(N_KV_HEADS)]
    zero = jnp.zeros((), BF16)
    ones_rows = jnp.ones((ONES_ROWS, 3 * BLOCK + n_ctx), BF16)
    low_head = lax.broadcasted_iota(jnp.int32, (BLOCK, LANES), 1) < HEAD_DIM

    def block_operands(i):
        variant = jnp.int32(0)
        if i == blocks_per_tile - 1:
            variant = jnp.where(t_idx == n_tiles - 1, 2, variant)
        if i == 0:
            variant = jnp.where(t_idx == 0, 1, variant)
        keys = jnp.concatenate([k_all[i * BLOCK:(i + 3) * BLOCK], k_ctx], axis=0)
        vt = jnp.concatenate([vt_all[:, i * BLOCK:(i + 3) * BLOCK], vt_ctx], axis=1)
        vt_pair = [jnp.concatenate([vt[half * LANES:(half + 1) * LANES], ones_rows], axis=0)
                   for half in range(2)]
        return dict(bias_t=bias_ref[variant], keys=keys, vt_pair=vt_pair)

    def scores_t(blk, i, j):
        q_slab = q_ref[0, i * BLOCK:(i + 1) * BLOCK, j * MXU_TILE:(j + 1) * MXU_TILE]
        qz = jnp.concatenate([jnp.where(in_head[kv], q_slab, zero) for kv in range(N_KV_HEADS)], axis=0)
        return _dot_nt(blk["keys"], qz)

    def softmax_values(blk, i, j, s_t):
        p_t, sink_term = [], []
        for kv in range(N_KV_HEADS):
            s_h = s_t[:, kv * BLOCK:(kv + 1) * BLOCK]
            s_h = jnp.concatenate([
                s_h[:BLOCK] + blk["bias_t"][:BLOCK],
                s_h[BLOCK:2 * BLOCK],
                s_h[2 * BLOCK:3 * BLOCK] + blk["bias_t"][2 * BLOCK:],
                s_h[3 * BLOCK:]], axis=0)
            sink = sink_ref[kv * GROUP + j] * LOG2_E
            m = jnp.maximum(jnp.max(s_h, axis=0, keepdims=True), sink)
            p_t.append(jnp.exp2(s_h - m).astype(BF16))
            sink_term.append(jnp.exp2(sink - m))
        for half in range(2):
            p_pair = jnp.concatenate(p_t[2 * half:2 * half + 2], axis=1)
            o_t = _dot(blk["vt_pair"][half], p_pair)
            den = o_t[LANES:LANES + 1] + jnp.concatenate(sink_term[2 * half:2 * half + 2], axis=1)
            o = (o_t[:LANES] * (1.0 / den)).T
            c0 = j * MXU_TILE + half * LANES
            o_ref[0, i * BLOCK:(i + 1) * BLOCK, c0:c0 + LANES] = (
                jnp.where(low_head, o[:BLOCK], o[BLOCK:]).astype(BF16))

    blocks = [block_operands(i) for i in range(blocks_per_tile)]
    items = [(i, j) for i in range(blocks_per_tile) for j in range(GROUP)]
    pending = [scores_t(blocks[i], i, j) for i, j in items[:SCORE_LOOKAHEAD]]
    for idx, (i, j) in enumerate(items):
        if idx + SCORE_LOOKAHEAD < len(items):
            ni, nj = items[idx + SCORE_LOOKAHEAD]
            pending.append(scores_t(blocks[ni], ni, nj))
        softmax_values(blocks[i], i, j, pending.pop(0))


def _attn_call(sink, q, k, vt, kc, vct, bias_t, tile):
    B, S, _ = q.shape
    bpt = tile // BLOCK
    nb = S // BLOCK
    C = kc.shape[1]
    cur = lambda b, t, *_: (b, t, 0)
    prev = lambda b, t, *_: (b, jnp.maximum(t * bpt - 1, 0), 0)
    nxt = lambda b, t, *_: (b, jnp.minimum((t + 1) * bpt, nb - 1), 0)
    cur_t = lambda b, t, *_: (b, 0, t)
    prev_t = lambda b, t, *_: (b, 0, jnp.maximum(t * bpt - 1, 0))
    nxt_t = lambda b, t, *_: (b, 0, jnp.minimum((t + 1) * bpt, nb - 1))
    per_b = lambda b, t, *_: (b, 0, 0)
    grid_spec = pltpu.PrefetchScalarGridSpec(
        num_scalar_prefetch=1,
        grid=(B, S // tile),
        in_specs=[
            pl.BlockSpec((1, tile, Q_W), cur),
            pl.BlockSpec((1, BLOCK, KV_W), prev), pl.BlockSpec((1, tile, KV_W), cur),
            pl.BlockSpec((1, BLOCK, KV_W), nxt),
            pl.BlockSpec((1, KV_W, BLOCK), prev_t), pl.BlockSpec((1, KV_W, tile), cur_t),
            pl.BlockSpec((1, KV_W, BLOCK), nxt_t),
            pl.BlockSpec((1, C, KV_W), per_b),
            pl.BlockSpec((1, KV_W, C), per_b),
            pl.BlockSpec((3, 3 * BLOCK, BLOCK), lambda b, t, *_: (0, 0, 0)),
        ],
        out_specs=pl.BlockSpec((1, tile, Q_W), cur),
    )
    return pl.pallas_call(
        functools.partial(_attn_kernel, blocks_per_tile=bpt),
        grid_spec=grid_spec,
        out_shape=jax.ShapeDtypeStruct((B, S, Q_W), BF16),
        compiler_params=_params(2),
        name="attn",
    )(sink, q, k, k, k, vt, vt, vt, kc, vct, bias_t)


def _merge_kernel(x_ref, mod_ref, g1_ref, attn_ref, up_ref, uc_ref, un_ref,
                  wg_ref, gb_ref, wap_ref, wpool_ref, wo_ref, o_ref, uext_ref, *, seq, tile, sub):
    t_idx = pl.program_id(1)
    n_tiles = pl.num_programs(1)
    m = mod_ref[pl.ds(pl.program_id(0), 1), :]
    sh = m[:, 0:D_MODEL]
    gs = g1_ref[...] * (1.0 + m[:, D_MODEL:2 * D_MODEL])
    gate1 = m[:, 2 * D_MODEL:3 * D_MODEL]

    keep_prev = (t_idx > 0).astype(F32)
    keep_next = (t_idx < n_tiles - 1).astype(F32)
    uext_ref[0:POOL_HALO] = up_ref[0] * keep_prev
    uext_ref[POOL_HALO:POOL_HALO + tile] = uc_ref[0]
    uext_ref[POOL_HALO + tile:] = un_ref[0] * keep_next
    edge_rows = lax.broadcasted_iota(jnp.int32, (POOL_HALO, POOL_GROUP_W), 0)

    def attn_dot(r):
        return _dot(attn_ref[0, r * sub:(r + 1) * sub, :], wap_ref[...])

    def gate_dot(r):
        h = _rms_modulate(x_ref[0, r * sub:(r + 1) * sub, :], gs, sh).astype(BF16)
        return jax.nn.sigmoid(_dot(h, wg_ref[...]) + gb_ref[...])

    def pool_dot(r):
        r0 = r * sub
        pos_first = t_idx * tile + r0 + edge_rows
        pos_last = pos_first + (sub - POOL_HALO)
        diffs = []
        for g, w in enumerate(POOL_WINDOWS):
            cols = slice(g * POOL_GROUP_W, (g + 1) * POOL_GROUP_W)
            acc = uext_ref[pl.ds(r0, sub + 2 * POOL_HALO), cols]
            k = 1
            while k < w:
                acc = acc[:acc.shape[0] - k] + acc[k:]
                k *= 2
            win = acc[POOL_HALO - w // 2:POOL_HALO - w // 2 + sub]

            def inv_count(pos):
                return 1.0 / (jnp.minimum(pos + w // 2, seq) - jnp.maximum(pos - w // 2, 0)).astype(F32)

            pooled_avg = jnp.concatenate([
                win[:POOL_HALO] * inv_count(pos_first),
                win[POOL_HALO:sub - POOL_HALO] * (1.0 / w),
                win[sub - POOL_HALO:] * inv_count(pos_last)], axis=0)
            diffs.append((pooled_avg - uext_ref[pl.ds(r0 + POOL_HALO, sub), cols]).astype(BF16))
        return _dot(jnp.concatenate(diffs, axis=1), wpool_ref[...])

    def output(r, a, gates, p):
        rows = slice(r * sub, (r + 1) * sub)
        merged = (gates[:, :D_MODEL] * a + gates[:, D_MODEL:] * p).astype(BF16)
        o_ref[0, rows, :] = x_ref[0, rows, :] + gate1 * _dot(merged, wo_ref[...])

    n_sub = tile // sub
    a = [attn_dot(r) for r in range(n_sub)]
    p = [pool_dot(r) for r in range(n_sub)]
    gates = [gate_dot(r) for r in range(n_sub)]
    for r in range(n_sub):
        output(r, a[r], gates[r], p[r])


def _merge_call(x, mod, norm1_g, attn, u, w_all, gate_b, w_ap, w_pool, w_out, tile):
    B, S, D = x.shape
    r = tile // POOL_HALO
    n_halo = S // POOL_HALO
    row = lambda b, t: (b, t, 0)
    return pl.pallas_call(
        functools.partial(_merge_kernel, seq=S, tile=tile, sub=MXU_TILE),
        grid=(B, S // tile),
        in_specs=[
            pl.BlockSpec((1, tile, D), row),
            _const_spec(mod.shape),
            _const_spec((1, D)),
            pl.BlockSpec((1, tile, Q_W), row),
            pl.BlockSpec((1, POOL_HALO, POOL_W), lambda b, t: (b, jnp.maximum(t * r - 1, 0), 0)),
            pl.BlockSpec((1, tile, POOL_W), row),
            pl.BlockSpec((1, POOL_HALO, POOL_W), lambda b, t: (b, jnp.minimum((t + 1) * r, n_halo - 1), 0)),
            _col_block_spec(D, 2 * D, 1),
            _const_spec(gate_b.shape),
            _const_spec(w_ap.shape),
            _const_spec(w_pool.shape),
            _const_spec(w_out.shape),
        ],
        out_specs=pl.BlockSpec((1, tile, D), row),
        out_shape=jax.ShapeDtypeStruct((B, S, D), F32),
        scratch_shapes=[pltpu.VMEM((tile + 2 * POOL_HALO, POOL_W), F32)],
        compiler_params=_params(2),
        name="merge",
    )(x, mod, norm1_g, attn, u, u, u, w_all, gate_b, w_ap, w_pool, w_out)


def _ffn_kernel(x_ref, mod_ref, g2_ref, wup_ref, wdn_ref, o_ref, hmid_ref, *, d_ff):
    x = x_ref[0]
    m = mod_ref[pl.ds(pl.program_id(0), 1), :]
    sh = m[:, 3 * D_MODEL:4 * D_MODEL]
    gs = g2_ref[...] * (1.0 + m[:, 4 * D_MODEL:5 * D_MODEL])
    gate2 = m[:, 5 * D_MODEL:6 * D_MODEL]
    h = _rms_modulate(x, gs, sh).astype(BF16)
    for c in range(d_ff // MXU_TILE):
        a = _dot(h, wup_ref[:, c * MXU_TILE:(c + 1) * MXU_TILE])
        b = _dot(h, wup_ref[:, d_ff + c * MXU_TILE:d_ff + (c + 1) * MXU_TILE])
        hmid_ref[:, c * MXU_TILE:(c + 1) * MXU_TILE] = (a * jax.nn.sigmoid(a) * b).astype(BF16)
    o_ref[0] = x + gate2 * _dot(hmid_ref[...], wdn_ref[...])


def _ffn_call(x, mod, norm2_g, w_up, w_down, tile):
    B, S, D = x.shape
    d_ff = w_down.shape[0]
    row = lambda b, t: (b, t, 0)
    return pl.pallas_call(
        functools.partial(_ffn_kernel, d_ff=d_ff),
        grid=(B, S // tile),
        in_specs=[
            pl.BlockSpec((1, tile, D), row),
            _const_spec(mod.shape),
            _const_spec((1, D)),
            _const_spec(w_up.shape),
            _const_spec(w_down.shape),
        ],
        out_specs=pl.BlockSpec((1, tile, D), row),
        out_shape=jax.ShapeDtypeStruct((B, S, D), F32),
        scratch_shapes=[pltpu.VMEM((tile, d_ff), BF16)],
        compiler_params=_params(2),
        name="ffn",
    )(x, mod, norm2_g, w_up, w_down)


def _rope_tables(seq):
    t = np.arange(seq)
    row = (t // GRID_W).astype(np.float64)
    col = (t % GRID_W).astype(np.float64)
    half = HEAD_DIM // 2
    inv_freq = 1.0 / (ROPE_THETA ** (np.arange(0, half, 2, dtype=np.float64) / half))
    ang_r = row[:, None] * inv_freq
    ang_c = col[:, None] * inv_freq
    cos = np.concatenate([np.cos(ang_r)] * 2 + [np.cos(ang_c)] * 2, axis=1)
    sin = np.concatenate([-np.sin(ang_r), np.sin(ang_r), -np.sin(ang_c), np.sin(ang_c)], axis=1)
    reps = (1, LANES // HEAD_DIM)
    return jnp.asarray(np.tile(cos, reps), dtype=F32), jnp.asarray(np.tile(sin, reps), dtype=F32)


def _band_bias():
    i = np.arange(BLOCK)[:, None]
    j = np.arange(3 * BLOCK)[None, :]
    band = np.abs(j - BLOCK - i) <= WINDOW
    first = band & (j >= BLOCK)
    last = band & (j < 2 * BLOCK)
    masks = np.stack([band, first, last])
    return jnp.asarray(np.where(masks, 0.0, NEG_BIG).transpose(0, 2, 1), dtype=F32)


def kernel(x, c, ctx, c_ctx, mod_w, mod_b, norm1_g, norm2_g, w_in, gate_b, q_norm_g, k_norm_g,
           sink, pool_w, pool_scale, w_attn_proj, w_pool_proj, w_out, w_up, w_down):
    B, S, D = x.shape
    depth = mod_w.shape[0]
    cos_t, sin_t = _rope_tables(S)
    bias_t = _band_bias()
    ones_bd = jnp.asarray(np.kron(np.eye(MXU_TILE // HEAD_DIM), np.ones((HEAD_DIM, HEAD_DIM))), dtype=BF16)
    reps = MXU_TILE // HEAD_DIM
    assert depth == 1, "context-stream update between layers is not implemented"

    for l in range(depth):
        c8 = jnp.concatenate([c, c_ctx[None, :], jnp.zeros((8 - B - 1, D), F32)], axis=0)
        mod = _mod_call(c8, mod_w[l], mod_b[l][None, :])

        wl = w_in[l].astype(BF16)
        w_q = wl[:, :Q_W].reshape(D, N_KV_HEADS, GROUP, HEAD_DIM).transpose(0, 2, 1, 3).reshape(D, Q_W)
        w_ap = (w_attn_proj[l].reshape(N_KV_HEADS, GROUP, HEAD_DIM, D).transpose(1, 0, 2, 3)
                .reshape(Q_W, D).astype(BF16))
        g1 = norm1_g[l][None, :]
        kg256 = jnp.tile(k_norm_g[l], reps)[None, :]
        qg256 = jnp.tile(q_norm_g[l] * (HEAD_DIM ** -0.5 * LOG2_E), reps)[None, :]

        kc, vct = _ctx_call(ctx, mod, g1, wl, kg256, ones_bd)
        q, k, vt, u = _inproj_call(x, mod, g1, w_q, wl, qg256, kg256, ones_bd, cos_t, sin_t, tile=1024)
        attn = _attn_call(sink[l], q, k, vt, kc, vct, bias_t, tile=512)
        w_pool = _pool_fold_call(pool_w[l], pool_scale[l].reshape(POOL_GROUPS, 1, POOL_GROUP_W), w_pool_proj[l])
        x = _merge_call(
            x, mod, g1, attn, u, wl, gate_b[l][None, :], w_ap, w_pool, w_out[l].astype(BF16), tile=512)
        x = _ffn_call(x, mod, norm2_g[l][None, :], w_up[l].astype(BF16), w_down[l].astype(BF16), tile=512)
    return x
```

```python
import functools

import jax
import jax.numpy as jnp
import numpy as np
from jax import lax
from jax.experimental import pallas as pl
from jax.experimental.pallas import tpu as pltpu

D_MODEL = 1024
GRID_W = 64
HEAD_DIM = 64
N_HEADS = 16
N_KV_HEADS = 4
GROUP = N_HEADS // N_KV_HEADS
Q_W = N_HEADS * HEAD_DIM
KV_W = N_KV_HEADS * HEAD_DIM
WINDOW = 128
BLOCK = 128
ROPE_THETA = 10000.0
POOL_WINDOWS = (2, 4, 8, 16)
POOL_GROUPS = 4
POOL_W = D_MODEL // 2
POOL_GROUP_W = POOL_W // POOL_GROUPS
POOL_HALO = 8
REST_START = Q_W + 2 * KV_W
N_MOD = 6
EPS = 1e-6
NEG_BIG = -1e30
LOG2_E = 1.4426950408889634

LANES = 128
MXU_TILE = 256
VMEM_LIMIT = 56 * 1024 * 1024

F32 = jnp.float32
BF16 = jnp.bfloat16


def _params(n_parallel):
    return pltpu.CompilerParams(
        dimension_semantics=("parallel",) * n_parallel,
        vmem_limit_bytes=VMEM_LIMIT,
    )


def _const_spec(shape):
    nd = len(shape)
    return pl.BlockSpec(shape, lambda *_: (0,) * nd, pipeline_mode=pl.Buffered(1))


def _col_block_spec(rows, width, block_idx):
    return pl.BlockSpec((rows, width), lambda *_: (0, block_idx), pipeline_mode=pl.Buffered(1))


def _dot(a, b):
    return jnp.dot(a, b, preferred_element_type=F32)


def _dot_nt(a, b):
    return lax.dot_general(a, b, (((1,), (1,)), ((), ())), preferred_element_type=F32)


def _rms_modulate(x, gs, sh):
    rs = lax.rsqrt(jnp.mean(x * x, axis=-1, keepdims=True) + EPS)
    return x * rs * gs + sh


def _head_rms(t, ones_bd):
    ms = _dot((t * t).astype(BF16), ones_bd) * (1.0 / HEAD_DIM)
    return lax.rsqrt(ms + EPS)


def _mod_kernel(c_ref, w_ref, b_ref, o_ref):
    c = c_ref[...]
    a = (c * jax.nn.sigmoid(c)).astype(BF16)
    o_ref[...] = _dot(a, w_ref[...].astype(BF16)) + b_ref[...]


def _mod_call(c8, mod_w, mod_b):
    n = mod_w.shape[1]
    bn = D_MODEL
    return pl.pallas_call(
        _mod_kernel,
        grid=(n // bn,),
        in_specs=[
            pl.BlockSpec((8, D_MODEL), lambda j: (0, 0)),
            pl.BlockSpec((D_MODEL, bn), lambda j: (0, j)),
            pl.BlockSpec((1, bn), lambda j: (0, j)),
        ],
        out_specs=pl.BlockSpec((8, bn), lambda j: (0, j)),
        out_shape=jax.ShapeDtypeStruct((8, n), F32),
        compiler_params=_params(1),
        name="mod",
    )(c8, mod_w, mod_b)


def _pool_fold_kernel(pw_ref, ps_ref, wpp_ref, o_ref):
    scaled = pw_ref[0] * ps_ref[0]
    o_ref[...] = jnp.dot(scaled, wpp_ref[...], preferred_element_type=F32,
                         precision=lax.Precision.HIGHEST).astype(BF16)


def _pool_fold_call(pool_w, pool_scale3, w_pool_proj):
    n_out = w_pool_proj.shape[1]
    return pl.pallas_call(
        _pool_fold_kernel,
        grid=(POOL_GROUPS,),
        in_specs=[
            pl.BlockSpec((1, POOL_GROUP_W, POOL_GROUP_W), lambda g: (g, 0, 0)),
            pl.BlockSpec((1, 1, POOL_GROUP_W), lambda g: (g, 0, 0)),
            pl.BlockSpec((POOL_GROUP_W, n_out), lambda g: (g, 0)),
        ],
        out_specs=pl.BlockSpec((POOL_GROUP_W, n_out), lambda g: (g, 0)),
        out_shape=jax.ShapeDtypeStruct((POOL_W, n_out), BF16),
        compiler_params=_params(1),
        name="pool_fold",
    )(pool_w, pool_scale3, w_pool_proj)


def _ctx_kernel(ctx_ref, mod_ref, g1_ref, w_ref, kg_ref, ones_ref, kc_ref, vc_ref, *, n_batch):
    m = mod_ref[n_batch:n_batch + 1, :]
    sh = m[:, 0:D_MODEL]
    gs = g1_ref[...] * (1.0 + m[:, D_MODEL:2 * D_MODEL])
    h = _rms_modulate(ctx_ref[0], gs, sh).astype(BF16)
    kv = _dot(h, w_ref[...])
    k = kv[:, :KV_W]
    kc_ref[0] = (k * _head_rms(k, ones_ref[...]) * kg_ref[...]).astype(BF16)
    vc_ref[0] = kv[:, KV_W:].astype(BF16)


def _ctx_call(ctx, mod, norm1_g, w_all, kg256, ones_bd):
    B, C, D = ctx.shape
    return pl.pallas_call(
        functools.partial(_ctx_kernel, n_batch=B),
        grid=(B,),
        in_specs=[
            pl.BlockSpec((1, C, D), lambda b: (b, 0, 0)),
            _const_spec(mod.shape),
            _const_spec((1, D)),
            _col_block_spec(D, 2 * KV_W, Q_W // (2 * KV_W)),
            _const_spec((1, KV_W)),
            _const_spec((MXU_TILE, MXU_TILE)),
        ],
        out_specs=[
            pl.BlockSpec((1, C, KV_W), lambda b: (b, 0, 0)),
            pl.BlockSpec((1, C, KV_W), lambda b: (b, 0, 0)),
        ],
        out_shape=[
            jax.ShapeDtypeStruct((B, C, KV_W), BF16),
            jax.ShapeDtypeStruct((B, C, KV_W), BF16),
        ],
        compiler_params=_params(1),
        name="ctx_kv",
    )(ctx, mod, norm1_g, w_all, kg256, ones_bd)


def _rope(t, cos, sin_signed, first_half):
    swap = jnp.where(first_half, pltpu.roll(t, LANES - 16, axis=1), pltpu.roll(t, 16, axis=1))
    return t * cos + swap * sin_signed


def _inproj_kernel(x_ref, mod_ref, g1_ref, wq_ref, wkvu_ref, qg_ref, kg_ref, ones_ref, cos_ref, sin_ref,
                   q_ref, k_ref, v_ref, u_ref):
    m = mod_ref[pl.ds(pl.program_id(0), 1), :]
    sh = m[:, 0:D_MODEL]
    gs = g1_ref[...] * (1.0 + m[:, D_MODEL:2 * D_MODEL])
    h = _rms_modulate(x_ref[0], gs, sh).astype(BF16)
    ones_bd = ones_ref[...]
    cos = cos_ref[...]
    sin = sin_ref[...]
    lane = lax.broadcasted_iota(jnp.int32, cos.shape, 1)
    first_half = (lane & 31) < 16

    def normed_rope(t, gain):
        t = t * _head_rms(t, ones_bd) * gain
        halves = [_rope(t[:, i * LANES:(i + 1) * LANES], cos, sin, first_half) for i in range(2)]
        return jnp.concatenate(halves, axis=1).astype(BF16)

    wide = 2 * MXU_TILE
    for j in range(Q_W // wide):
        t = _dot(h, wq_ref[:, j * wide:(j + 1) * wide])
        for i in range(2):
            c0 = j * wide + i * MXU_TILE
            q_ref[0, :, c0:c0 + MXU_TILE] = normed_rope(t[:, i * MXU_TILE:(i + 1) * MXU_TILE], qg_ref[...])
    kv = _dot(h, wkvu_ref[:, :2 * KV_W])
    k_ref[0] = normed_rope(kv[:, :KV_W], kg_ref[...])
    v_ref[0] = kv[:, KV_W:].astype(BF16)
    u_ref[0] = _dot(h, wkvu_ref[:, 2 * KV_W:])


def _inproj_call(x, mod, norm1_g, w_q, w_all, qg256, kg256, ones_bd, cos_t, sin_t, tile):
    B, S, D = x.shape
    row = lambda b, t: (b, t, 0)
    return pl.pallas_call(
        _inproj_kernel,
        grid=(B, S // tile),
        in_specs=[
            pl.BlockSpec((1, tile, D), row),
            _const_spec(mod.shape),
            _const_spec((1, D)),
            _const_spec((D, Q_W)),
            _col_block_spec(D, Q_W, 1),
            _const_spec((1, MXU_TILE)),
            _const_spec((1, MXU_TILE)),
            _const_spec((MXU_TILE, MXU_TILE)),
            pl.BlockSpec((tile, LANES), lambda b, t: (t, 0)),
            pl.BlockSpec((tile, LANES), lambda b, t: (t, 0)),
        ],
        out_specs=[
            pl.BlockSpec((1, tile, Q_W), row),
            pl.BlockSpec((1, tile, KV_W), row),
            pl.BlockSpec((1, tile, KV_W), row),
            pl.BlockSpec((1, tile, POOL_W), row),
        ],
        out_shape=[
            jax.ShapeDtypeStruct((B, S, Q_W), BF16),
            jax.ShapeDtypeStruct((B, S, KV_W), BF16),
            jax.ShapeDtypeStruct((B, S, KV_W), BF16),
            jax.ShapeDtypeStruct((B, S, POOL_W), F32),
        ],
        compiler_params=_params(2),
        name="in_proj",
    )(x, mod, norm1_g, w_q, w_all, qg256, kg256, ones_bd, cos_t, sin_t)


def _attn_kernel(sink_ref, q_ref, kp_ref, kc0_ref, kn_ref, vp_ref, vc0_ref, vn_ref,
                 kctx_ref, vctx_ref, bias_ref, o_ref, *, blocks_per_tile):
    t_idx = pl.program_id(1)
    n_tiles = pl.num_programs(1)
    k_all = jnp.concatenate([kp_ref[0], kc0_ref[0], kn_ref[0]], axis=0)
    v_all = jnp.concatenate([vp_ref[0], vc0_ref[0], vn_ref[0]], axis=0)
    k_ctx = kctx_ref[0]
    v_ctx = vctx_ref[0]
    lane = lax.broadcasted_iota(jnp.int32, (BLOCK, MXU_TILE), 1)
    in_head = [(lane >= kv * HEAD_DIM) & (lane < (kv + 1) * HEAD_DIM) for kv in range(N_KV_HEADS)]
    zero = jnp.zeros((), BF16)
    one = jnp.ones((), BF16)

    def with_ones(v, half):
        vl = lax.broadcasted_iota(jnp.int32, v.shape, 1)
        return jnp.where((vl < LANES) if half == 0 else (vl >= LANES), v, one)

    v_ctx_pair = [with_ones(v_ctx, half) for half in range(2)]
    low_head = lax.broadcasted_iota(jnp.int32, (BLOCK, LANES), 1) < HEAD_DIM

    def block_operands(i):
        variant = jnp.int32(0)
        if i == blocks_per_tile - 1:
            variant = jnp.where(t_idx == n_tiles - 1, 2, variant)
        if i == 0:
            variant = jnp.where(t_idx == 0, 1, variant)
        v_win = v_all[i * BLOCK:(i + 3) * BLOCK]
        return dict(bias=bias_ref[variant], k_win=k_all[i * BLOCK:(i + 3) * BLOCK],
                    v_win_pair=[with_ones(v_win, half) for half in range(2)])

    def scores(blk, i, j):
        q_slab = q_ref[0, i * BLOCK:(i + 1) * BLOCK, j * MXU_TILE:(j + 1) * MXU_TILE]
        qz = jnp.concatenate([jnp.where(in_head[kv], q_slab, zero) for kv in range(N_KV_HEADS)], axis=0)
        return _dot_nt(qz, blk["k_win"]), _dot_nt(qz, k_ctx)

    def softmax_values(blk, i, j, s_win, s_ctx):
        p_win, p_ctx, sink_term = [], [], []
        for kv in range(N_KV_HEADS):
            rows = slice(kv * BLOCK, (kv + 1) * BLOCK)
            sw = s_win[rows] + blk["bias"]
            sc = s_ctx[rows]
            sink = sink_ref[kv * GROUP + j] * LOG2_E
            m = jnp.maximum(
                jnp.maximum(jnp.max(sw, axis=-1, keepdims=True),
                            jnp.max(sc, axis=-1, keepdims=True)), sink)
            p_win.append(jnp.exp2(sw - m).astype(BF16))
            p_ctx.append(jnp.exp2(sc - m).astype(BF16))
            sink_term.append(jnp.exp2(sink - m))
        for half in range(2):
            pw = jnp.concatenate(p_win[2 * half:2 * half + 2], axis=0)
            pc = jnp.concatenate(p_ctx[2 * half:2 * half + 2], axis=0)
            o = _dot(pw, blk["v_win_pair"][half]) + _dot(pc, v_ctx_pair[half])
            vals = o[:, half * LANES:(half + 1) * LANES]
            sums = o[:, (1 - half) * LANES:(2 - half) * LANES]
            out = []
            for r in range(2):
                rows = slice(r * BLOCK, (r + 1) * BLOCK)
                out.append(vals[rows] / (sums[rows] + sink_term[2 * half + r]))
            c0 = j * MXU_TILE + half * LANES
            o_ref[0, i * BLOCK:(i + 1) * BLOCK, c0:c0 + LANES] = jnp.where(low_head, out[0], out[1]).astype(BF16)

    blocks = [block_operands(i) for i in range(blocks_per_tile)]
    items = [(i, j) for i in range(blocks_per_tile) for j in range(GROUP)]
    s_next = scores(blocks[0], *items[0])
    for idx, (i, j) in enumerate(items):
        s_cur = s_next
        if idx + 1 < len(items):
            ni, nj = items[idx + 1]
            s_next = scores(blocks[ni], ni, nj)
        softmax_values(blocks[i], i, j, *s_cur)


def _attn_call(sink, q, k, v, kc, vc, bias, tile):
    B, S, _ = q.shape
    bpt = tile // BLOCK
    nb = S // BLOCK
    C = kc.shape[1]
    cur = lambda b, t, *_: (b, t, 0)
    prev = lambda b, t, *_: (b, jnp.maximum(t * bpt - 1, 0), 0)
    nxt = lambda b, t, *_: (b, jnp.minimum((t + 1) * bpt, nb - 1), 0)
    per_b = lambda b, t, *_: (b, 0, 0)
    halo = (1, BLOCK, KV_W)
    center = (1, tile, KV_W)
    grid_spec = pltpu.PrefetchScalarGridSpec(
        num_scalar_prefetch=1,
        grid=(B, S // tile),
        in_specs=[
            pl.BlockSpec((1, tile, Q_W), cur),
            pl.BlockSpec(halo, prev), pl.BlockSpec(center, cur), pl.BlockSpec(halo, nxt),
            pl.BlockSpec(halo, prev), pl.BlockSpec(center, cur), pl.BlockSpec(halo, nxt),
            pl.BlockSpec((1, C, KV_W), per_b),
            pl.BlockSpec((1, C, KV_W), per_b),
            pl.BlockSpec((3, BLOCK, 3 * BLOCK), lambda b, t, *_: (0, 0, 0)),
        ],
        out_specs=pl.BlockSpec((1, tile, Q_W), cur),
    )
    return pl.pallas_call(
        functools.partial(_attn_kernel, blocks_per_tile=bpt),
        grid_spec=grid_spec,
        out_shape=jax.ShapeDtypeStruct((B, S, Q_W), BF16),
        compiler_params=_params(2),
        name="attn",
    )(sink, q, k, k, k, v, v, v, kc, vc, bias)


def _merge_kernel(x_ref, mod_ref, g1_ref, attn_ref, up_ref, uc_ref, un_ref,
                  wg_ref, gb_ref, wap_ref, wpool_ref, wo_ref, o_ref, uext_ref, *, seq, tile, sub):
    t_idx = pl.program_id(1)
    n_tiles = pl.num_programs(1)
    m = mod_ref[pl.ds(pl.program_id(0), 1), :]
    sh = m[:, 0:D_MODEL]
    gs = g1_ref[...] * (1.0 + m[:, D_MODEL:2 * D_MODEL])
    gate1 = m[:, 2 * D_MODEL:3 * D_MODEL]

    keep_prev = (t_idx > 0).astype(F32)
    keep_next = (t_idx < n_tiles - 1).astype(F32)
    uext_ref[0:POOL_HALO] = up_ref[0] * keep_prev
    uext_ref[POOL_HALO:POOL_HALO + tile] = uc_ref[0]
    uext_ref[POOL_HALO + tile:] = un_ref[0] * keep_next
    edge_rows = lax.broadcasted_iota(jnp.int32, (POOL_HALO, POOL_GROUP_W), 0)

    def attn_dot(r):
        return _dot(attn_ref[0, r * sub:(r + 1) * sub, :], wap_ref[...])

    def gate_dot(r):
        h = _rms_modulate(x_ref[0, r * sub:(r + 1) * sub, :], gs, sh).astype(BF16)
        return jax.nn.sigmoid(_dot(h, wg_ref[...]) + gb_ref[...])

    def pool_dot(r):
        r0 = r * sub
        pos_first = t_idx * tile + r0 + edge_rows
        pos_last = pos_first + (sub - POOL_HALO)
        diffs = []
        for g, w in enumerate(POOL_WINDOWS):
            cols = slice(g * POOL_GROUP_W, (g + 1) * POOL_GROUP_W)
            acc = uext_ref[pl.ds(r0, sub + 2 * POOL_HALO), cols]
            k = 1
            while k < w:
                acc = acc[:acc.shape[0] - k] + acc[k:]
                k *= 2
            win = acc[POOL_HALO - w // 2:POOL_HALO - w // 2 + sub]

            def inv_count(pos):
                return 1.0 / (jnp.minimum(pos + w // 2, seq) - jnp.maximum(pos - w // 2, 0)).astype(F32)

            pooled_avg = jnp.concatenate([
                win[:POOL_HALO] * inv_count(pos_first),
                win[POOL_HALO:sub - POOL_HALO] * (1.0 / w),
                win[sub - POOL_HALO:] * inv_count(pos_last)], axis=0)
            diffs.append((pooled_avg - uext_ref[pl.ds(r0 + POOL_HALO, sub), cols]).astype(BF16))
        return _dot(jnp.concatenate(diffs, axis=1), wpool_ref[...])

    def output(r, a, gates, p):
        rows = slice(r * sub, (r + 1) * sub)
        merged = (gates[:, :D_MODEL] * a + gates[:, D_MODEL:] * p).astype(BF16)
        o_ref[0, rows, :] = x_ref[0, rows, :] + gate1 * _dot(merged, wo_ref[...])

    n_sub = tile // sub
    a = [attn_dot(r) for r in range(n_sub)]
    p = [pool_dot(r) for r in range(n_sub)]
    gates = [gate_dot(r) for r in range(n_sub)]
    for r in range(n_sub):
        output(r, a[r], gates[r], p[r])


def _merge_call(x, mod, norm1_g, attn, u, w_all, gate_b, w_ap, w_pool, w_out, tile):
    B, S, D = x.shape
    r = tile // POOL_HALO
    n_halo = S // POOL_HALO
    row = lambda b, t: (b, t, 0)
    return pl.pallas_call(
        functools.partial(_merge_kernel, seq=S, tile=tile, sub=MXU_TILE),
        grid=(B, S // tile),
        in_specs=[
            pl.BlockSpec((1, tile, D), row),
            _const_spec(mod.shape),
            _const_spec((1, D)),
            pl.BlockSpec((1, tile, Q_W), row),
            pl.BlockSpec((1, POOL_HALO, POOL_W), lambda b, t: (b, jnp.maximum(t * r - 1, 0), 0)),
            pl.BlockSpec((1, tile, POOL_W), row),
            pl.BlockSpec((1, POOL_HALO, POOL_W), lambda b, t: (b, jnp.minimum((t + 1) * r, n_halo - 1), 0)),
            _col_block_spec(D, 2 * D, 1),
            _const_spec(gate_b.shape),
            _const_spec(w_ap.shape),
            _const_spec(w_pool.shape),
            _const_spec(w_out.shape),
        ],
        out_specs=pl.BlockSpec((1, tile, D), row),
        out_shape=jax.ShapeDtypeStruct((B, S, D), F32),
        scratch_shapes=[pltpu.VMEM((tile + 2 * POOL_HALO, POOL_W), F32)],
        compiler_params=_params(2),
        name="merge",
    )(x, mod, norm1_g, attn, u, u, u, w_all, gate_b, w_ap, w_pool, w_out)


def _ffn_kernel(x_ref, mod_ref, g2_ref, wup_ref, wdn_ref, o_ref, hmid_ref, *, d_ff, tile, sub):
    m = mod_ref[pl.ds(pl.program_id(0), 1), :]
    sh = m[:, 3 * D_MODEL:4 * D_MODEL]
    gs = g2_ref[...] * (1.0 + m[:, 4 * D_MODEL:5 * D_MODEL])
    gate2 = m[:, 5 * D_MODEL:6 * D_MODEL]

    def up(r):
        rows = slice(r * sub, (r + 1) * sub)
        h = _rms_modulate(x_ref[0, rows, :], gs, sh).astype(BF16)
        for c in range(d_ff // MXU_TILE):
            a = _dot(h, wup_ref[:, c * MXU_TILE:(c + 1) * MXU_TILE])
            b = _dot(h, wup_ref[:, d_ff + c * MXU_TILE:d_ff + (c + 1) * MXU_TILE])
            hmid_ref[rows, c * MXU_TILE:(c + 1) * MXU_TILE] = (a * jax.nn.sigmoid(a) * b).astype(BF16)

    def down(r):
        rows = slice(r * sub, (r + 1) * sub)
        o_ref[0, rows, :] = x_ref[0, rows, :] + gate2 * _dot(hmid_ref[rows, :], wdn_ref[...])

    n_sub = tile // sub
    up(0)
    for r in range(1, n_sub):
        up(r)
        down(r - 1)
    down(n_sub - 1)


def _ffn_call(x, mod, norm2_g, w_up, w_down, tile):
    B, S, D = x.shape
    d_ff = w_down.shape[0]
    row = lambda b, t: (b, t, 0)
    return pl.pallas_call(
        functools.partial(_ffn_kernel, d_ff=d_ff, tile=tile, sub=MXU_TILE),
        grid=(B, S // tile),
        in_specs=[
            pl.BlockSpec((1, tile, D), row),
            _const_spec(mod.shape),
            _const_spec((1, D)),
            _const_spec(w_up.shape),
            _const_spec(w_down.shape),
        ],
        out_specs=pl.BlockSpec((1, tile, D), row),
        out_shape=jax.ShapeDtypeStruct((B, S, D), F32),
        scratch_shapes=[pltpu.VMEM((tile, d_ff), BF16)],
        compiler_params=_params(2),
        name="ffn",
    )(x, mod, norm2_g, w_up, w_down)


def _rope_tables(seq):
    t = np.arange(seq)
    row = (t // GRID_W).astype(np.float64)
    col = (t % GRID_W).astype(np.float64)
    half = HEAD_DIM // 2
    inv_freq = 1.0 / (ROPE_THETA ** (np.arange(0, half, 2, dtype=np.float64) / half))
    ang_r = row[:, None] * inv_freq
    ang_c = col[:, None] * inv_freq
    cos = np.concatenate([np.cos(ang_r)] * 2 + [np.cos(ang_c)] * 2, axis=1)
    sin = np.concatenate([-np.sin(ang_r), np.sin(ang_r), -np.sin(ang_c), np.sin(ang_c)], axis=1)
    reps = (1, LANES // HEAD_DIM)
    return jnp.asarray(np.tile(cos, reps), dtype=F32), jnp.asarray(np.tile(sin, reps), dtype=F32)


def _band_bias():
    i = np.arange(BLOCK)[:, None]
    j = np.arange(3 * BLOCK)[None, :]
    band = np.abs(j - BLOCK - i) <= WINDOW
    first = band & (j >= BLOCK)
    last = band & (j < 2 * BLOCK)
    masks = np.stack([band, first, last])
    return jnp.asarray(np.where(masks, 0.0, NEG_BIG), dtype=F32)


def kernel(x, c, ctx, c_ctx, mod_w, mod_b, norm1_g, norm2_g, w_in, gate_b, q_norm_g, k_norm_g,
           sink, pool_w, pool_scale, w_attn_proj, w_pool_proj, w_out, w_up, w_down):
    B, S, D = x.shape
    depth = mod_w.shape[0]
    cos_t, sin_t = _rope_tables(S)
    bias = _band_bias()
    ones_bd = jnp.asarray(np.kron(np.eye(MXU_TILE // HEAD_DIM), np.ones((HEAD_DIM, HEAD_DIM))), dtype=BF16)
    reps = MXU_TILE // HEAD_DIM
    assert depth == 1, "context-stream update between layers is not implemented"

    for l in range(depth):
        c8 = jnp.concatenate([c, c_ctx[None, :], jnp.zeros((8 - B - 1, D), F32)], axis=0)
        mod = _mod_call(c8, mod_w[l], mod_b[l][None, :])

        wl = w_in[l].astype(BF16)
        w_q = wl[:, :Q_W].reshape(D, N_KV_HEADS, GROUP, HEAD_DIM).transpose(0, 2, 1, 3).reshape(D, Q_W)
        w_ap = (w_attn_proj[l].reshape(N_KV_HEADS, GROUP, HEAD_DIM, D).transpose(1, 0, 2, 3)
                .reshape(Q_W, D).astype(BF16))
        g1 = norm1_g[l][None, :]
        kg256 = jnp.tile(k_norm_g[l], reps)[None, :]
        qg256 = jnp.tile(q_norm_g[l] * (HEAD_DIM ** -0.5 * LOG2_E), reps)[None, :]

        kc, vc = _ctx_call(ctx, mod, g1, wl, kg256, ones_bd)
        q, k, v, u = _inproj_call(x, mod, g1, w_q, wl, qg256, kg256, ones_bd, cos_t, sin_t, tile=1024)
        attn = _attn_call(sink[l], q, k, v, kc, vc, bias, tile=512)
        w_pool = _pool_fold_call(pool_w[l], pool_scale[l].reshape(POOL_GROUPS, 1, POOL_GROUP_W), w_pool_proj[l])
        x = _merge_call(
            x, mod, g1, attn, u, wl, gate_b[l][None, :], w_ap, w_pool, w_out[l].astype(BF16), tile=1024)
        x = _ffn_call(x, mod, norm2_g[l][None, :], w_up[l].astype(BF16), w_down[l].astype(BF16), tile=1024)
    return x
```

```python
import functools

import jax
import jax.numpy as jnp
import numpy as np
from jax import lax
from jax.experimental import pallas as pl
from jax.experimental.pallas import tpu as pltpu

D_MODEL = 1024
GRID_W = 64
HEAD_DIM = 64
N_HEADS = 16
N_KV_HEADS = 4
GROUP = N_HEADS // N_KV_HEADS
Q_W = N_HEADS * HEAD_DIM
KV_W = N_KV_HEADS * HEAD_DIM
WINDOW = 128
BLOCK = 128
ROPE_THETA = 10000.0
POOL_WINDOWS = (2, 4, 8, 16)
POOL_GROUPS = 4
POOL_W = D_MODEL // 2
POOL_GROUP_W = POOL_W // POOL_GROUPS
POOL_HALO = 8
REST_START = Q_W + 2 * KV_W
N_MOD = 6
EPS = 1e-6
NEG_BIG = -1e30
LOG2_E = 1.4426950408889634

LANES = 128
MXU_TILE = 256
VMEM_LIMIT = 56 * 1024 * 1024
FFN_STAGE_COLS = 2 * MXU_TILE
FFN_STAGE_ROWS = MXU_TILE

F32 = jnp.float32
BF16 = jnp.bfloat16


def _params(n_parallel):
    return pltpu.CompilerParams(
        dimension_semantics=("parallel",) * n_parallel,
        vmem_limit_bytes=VMEM_LIMIT,
    )


def _const_spec(shape):
    nd = len(shape)
    return pl.BlockSpec(shape, lambda *_: (0,) * nd, pipeline_mode=pl.Buffered(1))


def _col_block_spec(rows, width, block_idx):
    return pl.BlockSpec((rows, width), lambda *_: (0, block_idx), pipeline_mode=pl.Buffered(1))


def _dot(a, b):
    return jnp.dot(a, b, preferred_element_type=F32)


def _dot_nt(a, b):
    return lax.dot_general(a, b, (((1,), (1,)), ((), ())), preferred_element_type=F32)


def _rms_modulate(x, gs, sh):
    rs = lax.rsqrt(jnp.mean(x * x, axis=-1, keepdims=True) + EPS)
    return x * rs * gs + sh


def _head_rms(t, ones_bd):
    ms = _dot((t * t).astype(BF16), ones_bd) * (1.0 / HEAD_DIM)
    return lax.rsqrt(ms + EPS)


def _mod_kernel(c_ref, w_ref, b_ref, o_ref):
    c = c_ref[...]
    a = (c * jax.nn.sigmoid(c)).astype(BF16)
    o_ref[...] = _dot(a, w_ref[...].astype(BF16)) + b_ref[...]


def _mod_call(c8, mod_w, mod_b):
    n = mod_w.shape[1]
    bn = D_MODEL
    return pl.pallas_call(
        _mod_kernel,
        grid=(n // bn,),
        in_specs=[
            pl.BlockSpec((8, D_MODEL), lambda j: (0, 0)),
            pl.BlockSpec((D_MODEL, bn), lambda j: (0, j)),
            pl.BlockSpec((1, bn), lambda j: (0, j)),
        ],
        out_specs=pl.BlockSpec((8, bn), lambda j: (0, j)),
        out_shape=jax.ShapeDtypeStruct((8, n), F32),
        compiler_params=_params(1),
        name="mod",
    )(c8, mod_w, mod_b)


def _pool_fold_kernel(pw_ref, ps_ref, wpp_ref, o_ref):
    scaled = pw_ref[0] * ps_ref[0]
    o_ref[...] = jnp.dot(scaled, wpp_ref[...], preferred_element_type=F32,
                         precision=lax.Precision.HIGHEST).astype(BF16)


def _pool_fold_call(pool_w, pool_scale3, w_pool_proj):
    n_out = w_pool_proj.shape[1]
    return pl.pallas_call(
        _pool_fold_kernel,
        grid=(POOL_GROUPS,),
        in_specs=[
            pl.BlockSpec((1, POOL_GROUP_W, POOL_GROUP_W), lambda g: (g, 0, 0)),
            pl.BlockSpec((1, 1, POOL_GROUP_W), lambda g: (g, 0, 0)),
            pl.BlockSpec((POOL_GROUP_W, n_out), lambda g: (g, 0)),
        ],
        out_specs=pl.BlockSpec((POOL_GROUP_W, n_out), lambda g: (g, 0)),
        out_shape=jax.ShapeDtypeStruct((POOL_W, n_out), BF16),
        compiler_params=_params(1),
        name="pool_fold",
    )(pool_w, pool_scale3, w_pool_proj)


def _ctx_kernel(ctx_ref, mod_ref, g1_ref, w_ref, kg_ref, ones_ref, kc_ref, vc_ref, *, n_batch):
    m = mod_ref[n_batch:n_batch + 1, :]
    sh = m[:, 0:D_MODEL]
    gs = g1_ref[...] * (1.0 + m[:, D_MODEL:2 * D_MODEL])
    h = _rms_modulate(ctx_ref[0], gs, sh).astype(BF16)
    kv = _dot(h, w_ref[...])
    k = kv[:, :KV_W]
    kc_ref[0] = (k * _head_rms(k, ones_ref[...]) * kg_ref[...]).astype(BF16)
    vc_ref[0] = kv[:, KV_W:].astype(BF16)


def _ctx_call(ctx, mod, norm1_g, w_all, kg256, ones_bd):
    B, C, D = ctx.shape
    return pl.pallas_call(
        functools.partial(_ctx_kernel, n_batch=B),
        grid=(B,),
        in_specs=[
            pl.BlockSpec((1, C, D), lambda b: (b, 0, 0)),
            _const_spec(mod.shape),
            _const_spec((1, D)),
            _col_block_spec(D, 2 * KV_W, Q_W // (2 * KV_W)),
            _const_spec((1, KV_W)),
            _const_spec((MXU_TILE, MXU_TILE)),
        ],
        out_specs=[
            pl.BlockSpec((1, C, KV_W), lambda b: (b, 0, 0)),
            pl.BlockSpec((1, C, KV_W), lambda b: (b, 0, 0)),
        ],
        out_shape=[
            jax.ShapeDtypeStruct((B, C, KV_W), BF16),
            jax.ShapeDtypeStruct((B, C, KV_W), BF16),
        ],
        compiler_params=_params(1),
        name="ctx_kv",
    )(ctx, mod, norm1_g, w_all, kg256, ones_bd)


def _rope(t, cos, sin_signed, first_half):
    swap = jnp.where(first_half, pltpu.roll(t, LANES - 16, axis=1), pltpu.roll(t, 16, axis=1))
    return t * cos + swap * sin_signed


def _inproj_kernel(x_ref, mod_ref, g1_ref, wq_ref, wkvu_ref, qg_ref, kg_ref, ones_ref, cos_ref, sin_ref,
                   q_ref, k_ref, v_ref, u_ref):
    m = mod_ref[pl.ds(pl.program_id(0), 1), :]
    sh = m[:, 0:D_MODEL]
    gs = g1_ref[...] * (1.0 + m[:, D_MODEL:2 * D_MODEL])
    h = _rms_modulate(x_ref[0], gs, sh).astype(BF16)
    ones_bd = ones_ref[...]
    cos = cos_ref[...]
    sin = sin_ref[...]
    lane = lax.broadcasted_iota(jnp.int32, cos.shape, 1)
    first_half = (lane & 31) < 16

    def normed_rope(t, gain):
        t = t * _head_rms(t, ones_bd) * gain
        halves = [_rope(t[:, i * LANES:(i + 1) * LANES], cos, sin, first_half) for i in range(2)]
        return jnp.concatenate(halves, axis=1).astype(BF16)

    wide = 2 * MXU_TILE
    for j in range(Q_W // wide):
        t = _dot(h, wq_ref[:, j * wide:(j + 1) * wide])
        for i in range(2):
            c0 = j * wide + i * MXU_TILE
            q_ref[0, :, c0:c0 + MXU_TILE] = normed_rope(t[:, i * MXU_TILE:(i + 1) * MXU_TILE], qg_ref[...])
    kv = _dot(h, wkvu_ref[:, :2 * KV_W])
    k_ref[0] = normed_rope(kv[:, :KV_W], kg_ref[...])
    v_ref[0] = kv[:, KV_W:].astype(BF16)
    u_ref[0] = _dot(h, wkvu_ref[:, 2 * KV_W:])


def _inproj_call(x, mod, norm1_g, w_q, w_all, qg256, kg256, ones_bd, cos_t, sin_t, tile):
    B, S, D = x.shape
    row = lambda b, t: (b, t, 0)
    return pl.pallas_call(
        _inproj_kernel,
        grid=(B, S // tile),
        in_specs=[
            pl.BlockSpec((1, tile, D), row),
            _const_spec(mod.shape),
            _const_spec((1, D)),
            _const_spec((D, Q_W)),
            _col_block_spec(D, Q_W, 1),
            _const_spec((1, MXU_TILE)),
            _const_spec((1, MXU_TILE)),
            _const_spec((MXU_TILE, MXU_TILE)),
            pl.BlockSpec((tile, LANES), lambda b, t: (t, 0)),
            pl.BlockSpec((tile, LANES), lambda b, t: (t, 0)),
        ],
        out_specs=[
            pl.BlockSpec((1, tile, Q_W), row),
            pl.BlockSpec((1, tile, KV_W), row),
            pl.BlockSpec((1, tile, KV_W), row),
            pl.BlockSpec((1, tile, POOL_W), row),
        ],
        out_shape=[
            jax.ShapeDtypeStruct((B, S, Q_W), BF16),
            jax.ShapeDtypeStruct((B, S, KV_W), BF16),
            jax.ShapeDtypeStruct((B, S, KV_W), BF16),
            jax.ShapeDtypeStruct((B, S, POOL_W), F32),
        ],
        compiler_params=_params(2),
        name="in_proj",
    )(x, mod, norm1_g, w_q, w_all, qg256, kg256, ones_bd, cos_t, sin_t)


def _attn_kernel(sink_ref, q_ref, kp_ref, kc0_ref, kn_ref, vp_ref, vc0_ref, vn_ref,
                 kctx_ref, vctx_ref, bias_ref, o_ref, *, blocks_per_tile):
    t_idx = pl.program_id(1)
    n_tiles = pl.num_programs(1)
    k_all = jnp.concatenate([kp_ref[0], kc0_ref[0], kn_ref[0]], axis=0)
    v_all = jnp.concatenate([vp_ref[0], vc0_ref[0], vn_ref[0]], axis=0)
    k_ctx = kctx_ref[0]
    v_ctx = vctx_ref[0]
    lane = lax.broadcasted_iota(jnp.int32, (BLOCK, MXU_TILE), 1)
    in_head = [(lane >= kv * HEAD_DIM) & (lane < (kv + 1) * HEAD_DIM) for kv in range(N_KV_HEADS)]
    zero = jnp.zeros((), BF16)
    one = jnp.ones((), BF16)

    def with_ones(v, half):
        vl = lax.broadcasted_iota(jnp.int32, v.shape, 1)
        return jnp.where((vl < LANES) if half == 0 else (vl >= LANES), v, one)

    v_ctx_pair = [with_ones(v_ctx, half) for half in range(2)]
    low_head = lax.broadcasted_iota(jnp.int32, (BLOCK, LANES), 1) < HEAD_DIM

    def block_operands(i):
        variant = jnp.int32(0)
        if i == blocks_per_tile - 1:
            variant = jnp.where(t_idx == n_tiles - 1, 2, variant)
        if i == 0:
            variant = jnp.where(t_idx == 0, 1, variant)
        v_win = v_all[i * BLOCK:(i + 3) * BLOCK]
        return dict(bias=bias_ref[variant], k_win=k_all[i * BLOCK:(i + 3) * BLOCK],
                    v_win_pair=[with_ones(v_win, half) for half in range(2)])

    def scores(blk, i, j):
        q_slab = q_ref[0, i * BLOCK:(i + 1) * BLOCK, j * MXU_TILE:(j + 1) * MXU_TILE]
        qz = jnp.concatenate([jnp.where(in_head[kv], q_slab, zero) for kv in range(N_KV_HEADS)], axis=0)
        return _dot_nt(qz, blk["k_win"]), _dot_nt(qz, k_ctx)

    def softmax_values(blk, i, j, s_win, s_ctx):
        p_win, p_ctx, sink_term = [], [], []
        for kv in range(N_KV_HEADS):
            rows = slice(kv * BLOCK, (kv + 1) * BLOCK)
            sw = s_win[rows] + blk["bias"]
            sc = s_ctx[rows]
            sink = sink_ref[kv * GROUP + j] * LOG2_E
            m = jnp.maximum(
                jnp.maximum(jnp.max(sw, axis=-1, keepdims=True),
                            jnp.max(sc, axis=-1, keepdims=True)), sink)
            p_win.append(jnp.exp2(sw - m).astype(BF16))
            p_ctx.append(jnp.exp2(sc - m).astype(BF16))
            sink_term.append(jnp.exp2(sink - m))
        for half in range(2):
            pw = jnp.concatenate(p_win[2 * half:2 * half + 2], axis=0)
            pc = jnp.concatenate(p_ctx[2 * half:2 * half + 2], axis=0)
            o = _dot(pw, blk["v_win_pair"][half]) + _dot(pc, v_ctx_pair[half])
            vals = o[:, half * LANES:(half + 1) * LANES]
            sums = o[:, (1 - half) * LANES:(2 - half) * LANES]
            out = []
            for r in range(2):
                rows = slice(r * BLOCK, (r + 1) * BLOCK)
                out.append(vals[rows] / (sums[rows] + sink_term[2 * half + r]))
            c0 = j * MXU_TILE + half * LANES
            o_ref[0, i * BLOCK:(i + 1) * BLOCK, c0:c0 + LANES] = jnp.where(low_head, out[0], out[1]).astype(BF16)

    blocks = [block_operands(i) for i in range(blocks_per_tile)]
    items = [(i, j) for i in range(blocks_per_tile) for j in range(GROUP)]
    s_next = scores(blocks[0], *items[0])
    for idx, (i, j) in enumerate(items):
        s_cur = s_next
        if idx + 1 < len(items):
            ni, nj = items[idx + 1]
            s_next = scores(blocks[ni], ni, nj)
        softmax_values(blocks[i], i, j, *s_cur)


def _attn_call(sink, q, k, v, kc, vc, bias, tile):
    B, S, _ = q.shape
    bpt = tile // BLOCK
    nb = S // BLOCK
    C = kc.shape[1]
    cur = lambda b, t, *_: (b, t, 0)
    prev = lambda b, t, *_: (b, jnp.maximum(t * bpt - 1, 0), 0)
    nxt = lambda b, t, *_: (b, jnp.minimum((t + 1) * bpt, nb - 1), 0)
    per_b = lambda b, t, *_: (b, 0, 0)
    halo = (1, BLOCK, KV_W)
    center = (1, tile, KV_W)
    grid_spec = pltpu.PrefetchScalarGridSpec(
        num_scalar_prefetch=1,
        grid=(B, S // tile),
        in_specs=[
            pl.BlockSpec((1, tile, Q_W), cur),
            pl.BlockSpec(halo, prev), pl.BlockSpec(center, cur), pl.BlockSpec(halo, nxt),
            pl.BlockSpec(halo, prev), pl.BlockSpec(center, cur), pl.BlockSpec(halo, nxt),
            pl.BlockSpec((1, C, KV_W), per_b),
            pl.BlockSpec((1, C, KV_W), per_b),
            pl.BlockSpec((3, BLOCK, 3 * BLOCK), lambda b, t, *_: (0, 0, 0)),
        ],
        out_specs=pl.BlockSpec((1, tile, Q_W), cur),
    )
    return pl.pallas_call(
        functools.partial(_attn_kernel, blocks_per_tile=bpt),
        grid_spec=grid_spec,
        out_shape=jax.ShapeDtypeStruct((B, S, Q_W), BF16),
        compiler_params=_params(2),
        name="attn",
    )(sink, q, k, k, k, v, v, v, kc, vc, bias)


def _merge_kernel(x_ref, mod_ref, g1_ref, attn_ref, up_ref, uc_ref, un_ref,
                  wg_ref, gb_ref, wap_ref, wpool_ref, wo_ref, o_ref, uext_ref, *, seq, tile, sub):
    t_idx = pl.program_id(1)
    n_tiles = pl.num_programs(1)
    m = mod_ref[pl.ds(pl.program_id(0), 1), :]
    sh = m[:, 0:D_MODEL]
    gs = g1_ref[...] * (1.0 + m[:, D_MODEL:2 * D_MODEL])
    gate1 = m[:, 2 * D_MODEL:3 * D_MODEL]

    keep_prev = (t_idx > 0).astype(F32)
    keep_next = (t_idx < n_tiles - 1).astype(F32)
    uext_ref[0:POOL_HALO] = up_ref[0] * keep_prev
    uext_ref[POOL_HALO:POOL_HALO + tile] = uc_ref[0]
    uext_ref[POOL_HALO + tile:] = un_ref[0] * keep_next
    edge_rows = lax.broadcasted_iota(jnp.int32, (POOL_HALO, POOL_GROUP_W), 0)

    def attn_dot(r):
        return _dot(attn_ref[0, r * sub:(r + 1) * sub, :], wap_ref[...])

    def gate_dot(r):
        h = _rms_modulate(x_ref[0, r * sub:(r + 1) * sub, :], gs, sh).astype(BF16)
        return jax.nn.sigmoid(_dot(h, wg_ref[...]) + gb_ref[...])

    def pool_dot(r):
        r0 = r * sub
        pos_first = t_idx * tile + r0 + edge_rows
        pos_last = pos_first + (sub - POOL_HALO)
        diffs = []
        for g, w in enumerate(POOL_WINDOWS):
            cols = slice(g * POOL_GROUP_W, (g + 1) * POOL_GROUP_W)
            acc = uext_ref[pl.ds(r0, sub + 2 * POOL_HALO), cols]
            k = 1
            while k < w:
                acc = acc[:acc.shape[0] - k] + acc[k:]
                k *= 2
            win = acc[POOL_HALO - w // 2:POOL_HALO - w // 2 + sub]

            def inv_count(pos):
                return 1.0 / (jnp.minimum(pos + w // 2, seq) - jnp.maximum(pos - w // 2, 0)).astype(F32)

            pooled_avg = jnp.concatenate([
                win[:POOL_HALO] * inv_count(pos_first),
                win[POOL_HALO:sub - POOL_HALO] * (1.0 / w),
                win[sub - POOL_HALO:] * inv_count(pos_last)], axis=0)
            diffs.append((pooled_avg - uext_ref[pl.ds(r0 + POOL_HALO, sub), cols]).astype(BF16))
        return _dot(jnp.concatenate(diffs, axis=1), wpool_ref[...])

    def output(r, a, gates, p):
        rows = slice(r * sub, (r + 1) * sub)
        merged = (gates[:, :D_MODEL] * a + gates[:, D_MODEL:] * p).astype(BF16)
        o_ref[0, rows, :] = x_ref[0, rows, :] + gate1 * _dot(merged, wo_ref[...])

    n_sub = tile // sub
    a = [attn_dot(r) for r in range(n_sub)]
    p = [pool_dot(r) for r in range(n_sub)]
    gates = [gate_dot(r) for r in range(n_sub)]
    for r in range(n_sub):
        output(r, a[r], gates[r], p[r])


def _merge_call(x, mod, norm1_g, attn, u, w_all, gate_b, w_ap, w_pool, w_out, tile):
    B, S, D = x.shape
    r = tile // POOL_HALO
    n_halo = S // POOL_HALO
    row = lambda b, t: (b, t, 0)
    return pl.pallas_call(
        functools.partial(_merge_kernel, seq=S, tile=tile, sub=MXU_TILE),
        grid=(B, S // tile),
        in_specs=[
            pl.BlockSpec((1, tile, D), row),
            _const_spec(mod.shape),
            _const_spec((1, D)),
            pl.BlockSpec((1, tile, Q_W), row),
            pl.BlockSpec((1, POOL_HALO, POOL_W), lambda b, t: (b, jnp.maximum(t * r - 1, 0), 0)),
            pl.BlockSpec((1, tile, POOL_W), row),
            pl.BlockSpec((1, POOL_HALO, POOL_W), lambda b, t: (b, jnp.minimum((t + 1) * r, n_halo - 1), 0)),
            _col_block_spec(D, 2 * D, 1),
            _const_spec(gate_b.shape),
            _const_spec(w_ap.shape),
            _const_spec(w_pool.shape),
            _const_spec(w_out.shape),
        ],
        out_specs=pl.BlockSpec((1, tile, D), row),
        out_shape=jax.ShapeDtypeStruct((B, S, D), F32),
        scratch_shapes=[pltpu.VMEM((tile + 2 * POOL_HALO, POOL_W), F32)],
        compiler_params=_params(2),
        name="merge",
    )(x, mod, norm1_g, attn, u, u, u, w_all, gate_b, w_ap, w_pool, w_out)


def _stage_weight_bf16(w_hbm, w_ref, stage_ref, sem, chunk, axis):
    n = w_hbm.shape[axis] // chunk

    def piece(ref, i):
        return ref.at[:, pl.ds(i * chunk, chunk)] if axis == 1 else ref.at[pl.ds(i * chunk, chunk), :]

    def copy(i):
        return pltpu.make_async_copy(piece(w_hbm, i), stage_ref.at[i % 2], sem.at[i % 2])

    copy(0).start()
    for i in range(n):
        if i + 1 < n:
            copy(i + 1).start()
        copy(i).wait()
        piece(w_ref, i)[...] = stage_ref[i % 2].astype(BF16)


def _ffn_kernel(x_ref, mod_ref, g2_ref, wup_hbm, wdn_hbm, o_ref,
                wup_ref, wdn_ref, up_stage, dn_stage, up_sem, dn_sem, hmid_ref, *, d_ff, tile, sub):
    @pl.when((pl.program_id(0) == 0) & (pl.program_id(1) == 0))
    def _():
        _stage_weight_bf16(wup_hbm, wup_ref, up_stage, up_sem, up_stage.shape[2], axis=1)
        _stage_weight_bf16(wdn_hbm, wdn_ref, dn_stage, dn_sem, dn_stage.shape[1], axis=0)

    m = mod_ref[pl.ds(pl.program_id(0), 1), :]
    sh = m[:, 3 * D_MODEL:4 * D_MODEL]
    gs = g2_ref[...] * (1.0 + m[:, 4 * D_MODEL:5 * D_MODEL])
    gate2 = m[:, 5 * D_MODEL:6 * D_MODEL]

    def up(r):
        rows = slice(r * sub, (r + 1) * sub)
        h = _rms_modulate(x_ref[0, rows, :], gs, sh).astype(BF16)
        for c in range(d_ff // MXU_TILE):
            a = _dot(h, wup_ref[:, c * MXU_TILE:(c + 1) * MXU_TILE])
            b = _dot(h, wup_ref[:, d_ff + c * MXU_TILE:d_ff + (c + 1) * MXU_TILE])
            hmid_ref[rows, c * MXU_TILE:(c + 1) * MXU_TILE] = (a * jax.nn.sigmoid(a) * b).astype(BF16)

    def down(r):
        rows = slice(r * sub, (r + 1) * sub)
        o_ref[0, rows, :] = x_ref[0, rows, :] + gate2 * _dot(hmid_ref[rows, :], wdn_ref[...])

    n_sub = tile // sub
    up(0)
    for r in range(1, n_sub):
        up(r)
        down(r - 1)
    down(n_sub - 1)


def _ffn_call(x, mod, norm2_g, w_up, w_down, tile):
    B, S, D = x.shape
    d_ff = w_down.shape[0]
    row = lambda b, t: (b, t, 0)
    return pl.pallas_call(
        functools.partial(_ffn_kernel, d_ff=d_ff, tile=tile, sub=MXU_TILE),
        grid=(B, S // tile),
        in_specs=[
            pl.BlockSpec((1, tile, D), row),
            _const_spec(mod.shape),
            _const_spec((1, D)),
            pl.BlockSpec(memory_space=pl.ANY),
            pl.BlockSpec(memory_space=pl.ANY),
        ],
        out_specs=pl.BlockSpec((1, tile, D), row),
        out_shape=jax.ShapeDtypeStruct((B, S, D), F32),
        scratch_shapes=[
            pltpu.VMEM(w_up.shape, BF16),
            pltpu.VMEM(w_down.shape, BF16),
            pltpu.VMEM((2, D, FFN_STAGE_COLS), F32),
            pltpu.VMEM((2, FFN_STAGE_ROWS, D), F32),
            pltpu.SemaphoreType.DMA((2,)),
            pltpu.SemaphoreType.DMA((2,)),
            pltpu.VMEM((tile, d_ff), BF16),
        ],
        compiler_params=pltpu.CompilerParams(
            dimension_semantics=("arbitrary", "arbitrary"), vmem_limit_bytes=VMEM_LIMIT),
        name="ffn",
    )(x, mod, norm2_g, w_up, w_down)


def _rope_tables(seq):
    t = np.arange(seq)
    row = (t // GRID_W).astype(np.float64)
    col = (t % GRID_W).astype(np.float64)
    half = HEAD_DIM // 2
    inv_freq = 1.0 / (ROPE_THETA ** (np.arange(0, half, 2, dtype=np.float64) / half))
    ang_r = row[:, None] * inv_freq
    ang_c = col[:, None] * inv_freq
    cos = np.concatenate([np.cos(ang_r)] * 2 + [np.cos(ang_c)] * 2, axis=1)
    sin = np.concatenate([-np.sin(ang_r), np.sin(ang_r), -np.sin(ang_c), np.sin(ang_c)], axis=1)
    reps = (1, LANES // HEAD_DIM)
    return jnp.asarray(np.tile(cos, reps), dtype=F32), jnp.asarray(np.tile(sin, reps), dtype=F32)


def _band_bias():
    i = np.arange(BLOCK)[:, None]
    j = np.arange(3 * BLOCK)[None, :]
    band = np.abs(j - BLOCK - i) <= WINDOW
    first = band & (j >= BLOCK)
    last = band & (j < 2 * BLOCK)
    masks = np.stack([band, first, last])
    return jnp.asarray(np.where(masks, 0.0, NEG_BIG), dtype=F32)


def kernel(x, c, ctx, c_ctx, mod_w, mod_b, norm1_g, norm2_g, w_in, gate_b, q_norm_g, k_norm_g,
           sink, pool_w, pool_scale, w_attn_proj, w_pool_proj, w_out, w_up, w_down):
    B, S, D = x.shape
    depth = mod_w.shape[0]
    cos_t, sin_t = _rope_tables(S)
    bias = _band_bias()
    ones_bd = jnp.asarray(np.kron(np.eye(MXU_TILE // HEAD_DIM), np.ones((HEAD_DIM, HEAD_DIM))), dtype=BF16)
    reps = MXU_TILE // HEAD_DIM
    assert depth == 1, "context-stream update between layers is not implemented"

    for l in range(depth):
        c8 = jnp.concatenate([c, c_ctx[None, :], jnp.zeros((8 - B - 1, D), F32)], axis=0)
        mod = _mod_call(c8, mod_w[l], mod_b[l][None, :])

        wl = w_in[l].astype(BF16)
        w_q = wl[:, :Q_W].reshape(D, N_KV_HEADS, GROUP, HEAD_DIM).transpose(0, 2, 1, 3).reshape(D, Q_W)
        w_ap = (w_attn_proj[l].reshape(N_KV_HEADS, GROUP, HEAD_DIM, D).transpose(1, 0, 2, 3)
                .reshape(Q_W, D).astype(BF16))
        g1 = norm1_g[l][None, :]
        kg256 = jnp.tile(k_norm_g[l], reps)[None, :]
        qg256 = jnp.tile(q_norm_g[l] * (HEAD_DIM ** -0.5 * LOG2_E), reps)[None, :]

        kc, vc = _ctx_call(ctx, mod, g1, wl, kg256, ones_bd)
        q, k, v, u = _inproj_call(x, mod, g1, w_q, wl, qg256, kg256, ones_bd, cos_t, sin_t, tile=1024)
        attn = _attn_call(sink[l], q, k, v, kc, vc, bias, tile=512)
        w_pool = _pool_fold_call(pool_w[l], pool_scale[l].reshape(POOL_GROUPS, 1, POOL_GROUP_W), w_pool_proj[l])
        x = _merge_call(
            x, mod, g1, attn, u, wl, gate_b[l][None, :], w_ap, w_pool, w_out[l].astype(BF16), tile=1024)
        x = _ffn_call(x, mod, norm2_g[l][None, :], w_up[l], w_down[l], tile=1024)
    return x
```

```python
import functools

import jax
import jax.numpy as jnp
import numpy as np
from jax import lax
from jax.experimental import pallas as pl
from jax.experimental.pallas import tpu as pltpu

D_MODEL = 1024
GRID_W = 64
HEAD_DIM = 64
N_HEADS = 16
N_KV_HEADS = 4
GROUP = N_HEADS // N_KV_HEADS
Q_W = N_HEADS * HEAD_DIM
KV_W = N_KV_HEADS * HEAD_DIM
WINDOW = 128
BLOCK = 128
ROPE_THETA = 10000.0
POOL_WINDOWS = (2, 4, 8, 16)
POOL_GROUPS = 4
POOL_W = D_MODEL // 2
POOL_GROUP_W = POOL_W // POOL_GROUPS
POOL_HALO = 8
REST_START = Q_W + 2 * KV_W
N_MOD = 6
EPS = 1e-6
NEG_BIG = -1e30
LOG2_E = 1.4426950408889634

LANES = 128
MXU_TILE = 256
VMEM_LIMIT = 56 * 1024 * 1024
STAGE_SLOTS = 4
STAGE_CHUNK = MXU_TILE

F32 = jnp.float32
BF16 = jnp.bfloat16


def _params(n_parallel):
    return pltpu.CompilerParams(
        dimension_semantics=("parallel",) * n_parallel,
        vmem_limit_bytes=VMEM_LIMIT,
    )


def _const_spec(shape):
    nd = len(shape)
    return pl.BlockSpec(shape, lambda *_: (0,) * nd, pipeline_mode=pl.Buffered(1))


def _dot(a, b):
    return jnp.dot(a, b, preferred_element_type=F32)


def _dot_nt(a, b):
    return lax.dot_general(a, b, (((1,), (1,)), ((), ())), preferred_element_type=F32)


def _rms_modulate(x, gs, sh):
    rs = lax.rsqrt(jnp.mean(x * x, axis=-1, keepdims=True) + EPS)
    return x * rs * gs + sh


def _head_rms(t, ones_bd):
    ms = _dot((t * t).astype(BF16), ones_bd) * (1.0 / HEAD_DIM)
    return lax.rsqrt(ms + EPS)


def _stage_weight_bf16(w_hbm, w_ref, stage_ref, sem, axis, store=None):
    slots = stage_ref.shape[0]
    chunk = stage_ref.shape[1 + axis]
    n = w_hbm.shape[axis] // chunk

    def piece(ref, i):
        return ref.at[:, pl.ds(i * chunk, chunk)] if axis == 1 else ref.at[pl.ds(i * chunk, chunk), :]

    def copy(i):
        return pltpu.make_async_copy(piece(w_hbm, i), stage_ref.at[i % slots], sem.at[i % slots])

    for i in range(min(slots - 1, n)):
        copy(i).start()
    for i in range(n):
        if i + slots - 1 < n:
            copy(i + slots - 1).start()
        copy(i).wait()
        staged = stage_ref[i % slots].astype(BF16)
        if store is None:
            piece(w_ref, i)[...] = staged
        else:
            store(i, staged)


def _sequential_params(n_axes):
    return pltpu.CompilerParams(dimension_semantics=("arbitrary",) * n_axes, vmem_limit_bytes=VMEM_LIMIT)


def _mod_kernel(c_ref, w_ref, b_ref, o_ref):
    c = c_ref[...]
    a = (c * jax.nn.sigmoid(c)).astype(BF16)
    o_ref[...] = _dot(a, w_ref[...].astype(BF16)) + b_ref[...]


def _mod_call(c8, mod_w, mod_b):
    n = mod_w.shape[1]
    bn = D_MODEL
    return pl.pallas_call(
        _mod_kernel,
        grid=(n // bn,),
        in_specs=[
            pl.BlockSpec((8, D_MODEL), lambda j: (0, 0)),
            pl.BlockSpec((D_MODEL, bn), lambda j: (0, j)),
            pl.BlockSpec((1, bn), lambda j: (0, j)),
        ],
        out_specs=pl.BlockSpec((8, bn), lambda j: (0, j)),
        out_shape=jax.ShapeDtypeStruct((8, n), F32),
        compiler_params=_params(1),
        name="mod",
    )(c8, mod_w, mod_b)


def _pool_fold_kernel(pw_ref, ps_ref, wpp_ref, o_ref):
    scaled = pw_ref[0] * ps_ref[0]
    o_ref[...] = jnp.dot(scaled, wpp_ref[...], preferred_element_type=F32,
                         precision=lax.Precision.HIGHEST).astype(BF16)


def _pool_fold_call(pool_w, pool_scale3, w_pool_proj):
    n_out = w_pool_proj.shape[1]
    return pl.pallas_call(
        _pool_fold_kernel,
        grid=(POOL_GROUPS,),
        in_specs=[
            pl.BlockSpec((1, POOL_GROUP_W, POOL_GROUP_W), lambda g: (g, 0, 0)),
            pl.BlockSpec((1, 1, POOL_GROUP_W), lambda g: (g, 0, 0)),
            pl.BlockSpec((POOL_GROUP_W, n_out), lambda g: (g, 0)),
        ],
        out_specs=pl.BlockSpec((POOL_GROUP_W, n_out), lambda g: (g, 0)),
        out_shape=jax.ShapeDtypeStruct((POOL_W, n_out), BF16),
        compiler_params=_params(1),
        name="pool_fold",
    )(pool_w, pool_scale3, w_pool_proj)


def _ctx_kernel(ctx_ref, mod_ref, g1_ref, wall_hbm, kg_ref, ones_ref, kc_ref, vc_ref,
                w_ref, col_stage, col_sem, *, n_batch):
    @pl.when(pl.program_id(0) == 0)
    def _():
        _stage_weight_bf16(wall_hbm.at[:, pl.ds(Q_W, 2 * KV_W)], w_ref, col_stage, col_sem, axis=1)

    m = mod_ref[n_batch:n_batch + 1, :]
    sh = m[:, 0:D_MODEL]
    gs = g1_ref[...] * (1.0 + m[:, D_MODEL:2 * D_MODEL])
    h = _rms_modulate(ctx_ref[0], gs, sh).astype(BF16)
    kv = _dot(h, w_ref[...])
    k = kv[:, :KV_W]
    kc_ref[0] = (k * _head_rms(k, ones_ref[...]) * kg_ref[...]).astype(BF16)
    vc_ref[0] = kv[:, KV_W:].astype(BF16)


def _ctx_call(ctx, mod, norm1_g, w_all, kg256, ones_bd):
    B, C, D = ctx.shape
    return pl.pallas_call(
        functools.partial(_ctx_kernel, n_batch=B),
        grid=(B,),
        in_specs=[
            pl.BlockSpec((1, C, D), lambda b: (b, 0, 0)),
            _const_spec(mod.shape),
            _const_spec((1, D)),
            pl.BlockSpec(memory_space=pl.ANY),
            _const_spec((1, KV_W)),
            _const_spec((MXU_TILE, MXU_TILE)),
        ],
        out_specs=[
            pl.BlockSpec((1, C, KV_W), lambda b: (b, 0, 0)),
            pl.BlockSpec((1, C, KV_W), lambda b: (b, 0, 0)),
        ],
        out_shape=[
            jax.ShapeDtypeStruct((B, C, KV_W), BF16),
            jax.ShapeDtypeStruct((B, C, KV_W), BF16),
        ],
        scratch_shapes=[
            pltpu.VMEM((D, 2 * KV_W), BF16),
            pltpu.VMEM((STAGE_SLOTS, D, STAGE_CHUNK), F32),
            pltpu.SemaphoreType.DMA((STAGE_SLOTS,)),
        ],
        compiler_params=_sequential_params(1),
        name="ctx_kv",
    )(ctx, mod, norm1_g, w_all, kg256, ones_bd)


def _rope(t, cos, sin_signed, first_half):
    swap = jnp.where(first_half, pltpu.roll(t, LANES - 16, axis=1), pltpu.roll(t, 16, axis=1))
    return t * cos + swap * sin_signed


def _inproj_kernel(x_ref, mod_ref, g1_ref, wq_ref, wall_hbm, qg_ref, kg_ref, ones_ref, cos_ref, sin_ref,
                   q_ref, k_ref, v_ref, u_ref, wkvu_ref, col_stage, col_sem):
    @pl.when((pl.program_id(0) == 0) & (pl.program_id(1) == 0))
    def _():
        _stage_weight_bf16(wall_hbm.at[:, pl.ds(Q_W, Q_W)], wkvu_ref, col_stage, col_sem, axis=1)

    m = mod_ref[pl.ds(pl.program_id(0), 1), :]
    sh = m[:, 0:D_MODEL]
    gs = g1_ref[...] * (1.0 + m[:, D_MODEL:2 * D_MODEL])
    h = _rms_modulate(x_ref[0], gs, sh).astype(BF16)
    ones_bd = ones_ref[...]
    cos = cos_ref[...]
    sin = sin_ref[...]
    lane = lax.broadcasted_iota(jnp.int32, cos.shape, 1)
    first_half = (lane & 31) < 16

    def normed_rope(t, gain):
        t = t * _head_rms(t, ones_bd) * gain
        halves = [_rope(t[:, i * LANES:(i + 1) * LANES], cos, sin, first_half) for i in range(2)]
        return jnp.concatenate(halves, axis=1).astype(BF16)

    wide = 2 * MXU_TILE
    for j in range(Q_W // wide):
        t = _dot(h, wq_ref[:, j * wide:(j + 1) * wide])
        for i in range(2):
            c0 = j * wide + i * MXU_TILE
            q_ref[0, :, c0:c0 + MXU_TILE] = normed_rope(t[:, i * MXU_TILE:(i + 1) * MXU_TILE], qg_ref[...])
    kv = _dot(h, wkvu_ref[:, :2 * KV_W])
    k_ref[0] = normed_rope(kv[:, :KV_W], kg_ref[...])
    v_ref[0] = kv[:, KV_W:].astype(BF16)
    u_ref[0] = _dot(h, wkvu_ref[:, 2 * KV_W:])


def _inproj_call(x, mod, norm1_g, w_q, w_all, qg256, kg256, ones_bd, cos_t, sin_t, tile):
    B, S, D = x.shape
    row = lambda b, t: (b, t, 0)
    return pl.pallas_call(
        _inproj_kernel,
        grid=(B, S // tile),
        in_specs=[
            pl.BlockSpec((1, tile, D), row),
            _const_spec(mod.shape),
            _const_spec((1, D)),
            _const_spec((D, Q_W)),
            pl.BlockSpec(memory_space=pl.ANY),
            _const_spec((1, MXU_TILE)),
            _const_spec((1, MXU_TILE)),
            _const_spec((MXU_TILE, MXU_TILE)),
            pl.BlockSpec((tile, LANES), lambda b, t: (t, 0)),
            pl.BlockSpec((tile, LANES), lambda b, t: (t, 0)),
        ],
        out_specs=[
            pl.BlockSpec((1, tile, Q_W), row),
            pl.BlockSpec((1, tile, KV_W), row),
            pl.BlockSpec((1, tile, KV_W), row),
            pl.BlockSpec((1, tile, POOL_W), row),
        ],
        out_shape=[
            jax.ShapeDtypeStruct((B, S, Q_W), BF16),
            jax.ShapeDtypeStruct((B, S, KV_W), BF16),
            jax.ShapeDtypeStruct((B, S, KV_W), BF16),
            jax.ShapeDtypeStruct((B, S, POOL_W), F32),
        ],
        scratch_shapes=[
            pltpu.VMEM((D, Q_W), BF16),
            pltpu.VMEM((STAGE_SLOTS, D, STAGE_CHUNK), F32),
            pltpu.SemaphoreType.DMA((STAGE_SLOTS,)),
        ],
        compiler_params=_sequential_params(2),
        name="in_proj",
    )(x, mod, norm1_g, w_q, w_all, qg256, kg256, ones_bd, cos_t, sin_t)


def _attn_kernel(sink_ref, q_ref, kp_ref, kc0_ref, kn_ref, vp_ref, vc0_ref, vn_ref,
                 kctx_ref, vctx_ref, bias_ref, o_ref, *, blocks_per_tile):
    t_idx = pl.program_id(1)
    n_tiles = pl.num_programs(1)
    k_all = jnp.concatenate([kp_ref[0], kc0_ref[0], kn_ref[0]], axis=0)
    v_all = jnp.concatenate([vp_ref[0], vc0_ref[0], vn_ref[0]], axis=0)
    k_ctx = kctx_ref[0]
    v_ctx = vctx_ref[0]
    lane = lax.broadcasted_iota(jnp.int32, (BLOCK, MXU_TILE), 1)
    in_head = [(lane >= kv * HEAD_DIM) & (lane < (kv + 1) * HEAD_DIM) for kv in range(N_KV_HEADS)]
    zero = jnp.zeros((), BF16)
    one = jnp.ones((), BF16)

    def with_ones(v, half):
        vl = lax.broadcasted_iota(jnp.int32, v.shape, 1)
        return jnp.where((vl < LANES) if half == 0 else (vl >= LANES), v, one)

    v_ctx_pair = [with_ones(v_ctx, half) for half in range(2)]
    low_head = lax.broadcasted_iota(jnp.int32, (BLOCK, LANES), 1) < HEAD_DIM

    def block_operands(i):
        variant = jnp.int32(0)
        if i == blocks_per_tile - 1:
            variant = jnp.where(t_idx == n_tiles - 1, 2, variant)
        if i == 0:
            variant = jnp.where(t_idx == 0, 1, variant)
        v_win = v_all[i * BLOCK:(i + 3) * BLOCK]
        return dict(bias=bias_ref[variant], k_win=k_all[i * BLOCK:(i + 3) * BLOCK],
                    v_win_pair=[with_ones(v_win, half) for half in range(2)])

    def scores(blk, i, j):
        q_slab = q_ref[0, i * BLOCK:(i + 1) * BLOCK, j * MXU_TILE:(j + 1) * MXU_TILE]
        qz = jnp.concatenate([jnp.where(in_head[kv], q_slab, zero) for kv in range(N_KV_HEADS)], axis=0)
        return _dot_nt(qz, blk["k_win"]), _dot_nt(qz, k_ctx)

    def softmax_values(blk, i, j, s_win, s_ctx):
        p_win, p_ctx, sink_term = [], [], []
        for kv in range(N_KV_HEADS):
            rows = slice(kv * BLOCK, (kv + 1) * BLOCK)
            sw = s_win[rows] + blk["bias"]
            sc = s_ctx[rows]
            sink = sink_ref[kv * GROUP + j] * LOG2_E
            m = jnp.maximum(
                jnp.maximum(jnp.max(sw, axis=-1, keepdims=True),
                            jnp.max(sc, axis=-1, keepdims=True)), sink)
            p_win.append(jnp.exp2(sw - m).astype(BF16))
            p_ctx.append(jnp.exp2(sc - m).astype(BF16))
            sink_term.append(jnp.exp2(sink - m))
        for half in range(2):
            pw = jnp.concatenate(p_win[2 * half:2 * half + 2], axis=0)
            pc = jnp.concatenate(p_ctx[2 * half:2 * half + 2], axis=0)
            o = _dot(pw, blk["v_win_pair"][half]) + _dot(pc, v_ctx_pair[half])
            vals = o[:, half * LANES:(half + 1) * LANES]
            sums = o[:, (1 - half) * LANES:(2 - half) * LANES]
            out = []
            for r in range(2):
                rows = slice(r * BLOCK, (r + 1) * BLOCK)
                out.append(vals[rows] / (sums[rows] + sink_term[2 * half + r]))
            c0 = j * MXU_TILE + half * LANES
            o_ref[0, i * BLOCK:(i + 1) * BLOCK, c0:c0 + LANES] = jnp.where(low_head, out[0], out[1]).astype(BF16)

    blocks = [block_operands(i) for i in range(blocks_per_tile)]
    items = [(i, j) for i in range(blocks_per_tile) for j in range(GROUP)]
    s_next = scores(blocks[0], *items[0])
    for idx, (i, j) in enumerate(items):
        s_cur = s_next
        if idx + 1 < len(items):
            ni, nj = items[idx + 1]
            s_next = scores(blocks[ni], ni, nj)
        softmax_values(blocks[i], i, j, *s_cur)


def _attn_call(sink, q, k, v, kc, vc, bias, tile):
    B, S, _ = q.shape
    bpt = tile // BLOCK
    nb = S // BLOCK
    C = kc.shape[1]
    cur = lambda b, t, *_: (b, t, 0)
    prev = lambda b, t, *_: (b, jnp.maximum(t * bpt - 1, 0), 0)
    nxt = lambda b, t, *_: (b, jnp.minimum((t + 1) * bpt, nb - 1), 0)
    per_b = lambda b, t, *_: (b, 0, 0)
    halo = (1, BLOCK, KV_W)
    center = (1, tile, KV_W)
    grid_spec = pltpu.PrefetchScalarGridSpec(
        num_scalar_prefetch=1,
        grid=(B, S // tile),
        in_specs=[
            pl.BlockSpec((1, tile, Q_W), cur),
            pl.BlockSpec(halo, prev), pl.BlockSpec(center, cur), pl.BlockSpec(halo, nxt),
            pl.BlockSpec(halo, prev), pl.BlockSpec(center, cur), pl.BlockSpec(halo, nxt),
            pl.BlockSpec((1, C, KV_W), per_b),
            pl.BlockSpec((1, C, KV_W), per_b),
            pl.BlockSpec((3, BLOCK, 3 * BLOCK), lambda b, t, *_: (0, 0, 0)),
        ],
        out_specs=pl.BlockSpec((1, tile, Q_W), cur),
    )
    return pl.pallas_call(
        functools.partial(_attn_kernel, blocks_per_tile=bpt),
        grid_spec=grid_spec,
        out_shape=jax.ShapeDtypeStruct((B, S, Q_W), BF16),
        compiler_params=_params(2),
        name="attn",
    )(sink, q, k, k, k, v, v, v, kc, vc, bias)


def _merge_kernel(x_ref, mod_ref, g1_ref, attn_ref, up_ref, uc_ref, un_ref,
                  wall_hbm, gb_ref, wap_hbm, wpool_ref, wo_hbm, o_ref,
                  uext_ref, wg_ref, wap_ref, wo_ref, col_stage, row_stage, col_sem, row_sem, *, seq, tile, sub):
    @pl.when((pl.program_id(0) == 0) & (pl.program_id(1) == 0))
    def _():
        _stage_weight_bf16(wall_hbm.at[:, pl.ds(2 * D_MODEL, 2 * D_MODEL)], wg_ref, col_stage, col_sem, axis=1)
        _stage_weight_bf16(wo_hbm, wo_ref, col_stage, col_sem, axis=1)

        def store_regrouped(kv, rows_bf16):
            for j in range(GROUP):
                dst = (j * N_KV_HEADS + kv) * HEAD_DIM
                wap_ref[dst:dst + HEAD_DIM, :] = rows_bf16[j * HEAD_DIM:(j + 1) * HEAD_DIM, :]

        _stage_weight_bf16(wap_hbm, None, row_stage, row_sem, axis=0, store=store_regrouped)

    t_idx = pl.program_id(1)
    n_tiles = pl.num_programs(1)
    m = mod_ref[pl.ds(pl.program_id(0), 1), :]
    sh = m[:, 0:D_MODEL]
    gs = g1_ref[...] * (1.0 + m[:, D_MODEL:2 * D_MODEL])
    gate1 = m[:, 2 * D_MODEL:3 * D_MODEL]

    keep_prev = (t_idx > 0).astype(F32)
    keep_next = (t_idx < n_tiles - 1).astype(F32)
    uext_ref[0:POOL_HALO] = up_ref[0] * keep_prev
    uext_ref[POOL_HALO:POOL_HALO + tile] = uc_ref[0]
    uext_ref[POOL_HALO + tile:] = un_ref[0] * keep_next
    edge_rows = lax.broadcasted_iota(jnp.int32, (POOL_HALO, POOL_GROUP_W), 0)

    def attn_dot(r):
        return _dot(attn_ref[0, r * sub:(r + 1) * sub, :], wap_ref[...])

    def gate_dot(r):
        h = _rms_modulate(x_ref[0, r * sub:(r + 1) * sub, :], gs, sh).astype(BF16)
        return jax.nn.sigmoid(_dot(h, wg_ref[...]) + gb_ref[...])

    def pool_dot(r):
        r0 = r * sub
        pos_first = t_idx * tile + r0 + edge_rows
        pos_last = pos_first + (sub - POOL_HALO)
        diffs = []
        for g, w in enumerate(POOL_WINDOWS):
            cols = slice(g * POOL_GROUP_W, (g + 1) * POOL_GROUP_W)
            acc = uext_ref[pl.ds(r0, sub + 2 * POOL_HALO), cols]
            k = 1
            while k < w:
                acc = acc[:acc.shape[0] - k] + acc[k:]
                k *= 2
            win = acc[POOL_HALO - w // 2:POOL_HALO - w // 2 + sub]

            def inv_count(pos):
                return 1.0 / (jnp.minimum(pos + w // 2, seq) - jnp.maximum(pos - w // 2, 0)).astype(F32)

            pooled_avg = jnp.concatenate([
                win[:POOL_HALO] * inv_count(pos_first),
                win[POOL_HALO:sub - POOL_HALO] * (1.0 / w),
                win[sub - POOL_HALO:] * inv_count(pos_last)], axis=0)
            diffs.append((pooled_avg - uext_ref[pl.ds(r0 + POOL_HALO, sub), cols]).astype(BF16))
        return _dot(jnp.concatenate(diffs, axis=1), wpool_ref[...])

    def output(r, a, gates, p):
        rows = slice(r * sub, (r + 1) * sub)
        merged = (gates[:, :D_MODEL] * a + gates[:, D_MODEL:] * p).astype(BF16)
        o_ref[0, rows, :] = x_ref[0, rows, :] + gate1 * _dot(merged, wo_ref[...])

    n_sub = tile // sub
    a = [attn_dot(r) for r in range(n_sub)]
    p = [pool_dot(r) for r in range(n_sub)]
    gates = [gate_dot(r) for r in range(n_sub)]
    for r in range(n_sub):
        output(r, a[r], gates[r], p[r])


def _merge_call(x, mod, norm1_g, attn, u, w_all, gate_b, w_ap, w_pool, w_out, tile):
    B, S, D = x.shape
    r = tile // POOL_HALO
    n_halo = S // POOL_HALO
    row = lambda b, t: (b, t, 0)
    return pl.pallas_call(
        functools.partial(_merge_kernel, seq=S, tile=tile, sub=MXU_TILE),
        grid=(B, S // tile),
        in_specs=[
            pl.BlockSpec((1, tile, D), row),
            _const_spec(mod.shape),
            _const_spec((1, D)),
            pl.BlockSpec((1, tile, Q_W), row),
            pl.BlockSpec((1, POOL_HALO, POOL_W), lambda b, t: (b, jnp.maximum(t * r - 1, 0), 0)),
            pl.BlockSpec((1, tile, POOL_W), row),
            pl.BlockSpec((1, POOL_HALO, POOL_W), lambda b, t: (b, jnp.minimum((t + 1) * r, n_halo - 1), 0)),
            pl.BlockSpec(memory_space=pl.ANY),
            _const_spec(gate_b.shape),
            pl.BlockSpec(memory_space=pl.ANY),
            _const_spec(w_pool.shape),
            pl.BlockSpec(memory_space=pl.ANY),
        ],
        out_specs=pl.BlockSpec((1, tile, D), row),
        out_shape=jax.ShapeDtypeStruct((B, S, D), F32),
        scratch_shapes=[
            pltpu.VMEM((tile + 2 * POOL_HALO, POOL_W), F32),
            pltpu.VMEM((D, 2 * D), BF16),
            pltpu.VMEM(w_ap.shape, BF16),
            pltpu.VMEM(w_out.shape, BF16),
            pltpu.VMEM((STAGE_SLOTS, D, STAGE_CHUNK), F32),
            pltpu.VMEM((STAGE_SLOTS, STAGE_CHUNK, D), F32),
            pltpu.SemaphoreType.DMA((STAGE_SLOTS,)),
            pltpu.SemaphoreType.DMA((STAGE_SLOTS,)),
        ],
        compiler_params=_sequential_params(2),
        name="merge",
    )(x, mod, norm1_g, attn, u, u, u, w_all, gate_b, w_ap, w_pool, w_out)


def _ffn_kernel(x_ref, mod_ref, g2_ref, wup_hbm, wdn_hbm, o_ref,
                wup_ref, wdn_ref, col_stage, row_stage, col_sem, row_sem, hmid_ref, *, d_ff, tile, sub):
    @pl.when((pl.program_id(0) == 0) & (pl.program_id(1) == 0))
    def _():
        _stage_weight_bf16(wup_hbm, wup_ref, col_stage, col_sem, axis=1)
        _stage_weight_bf16(wdn_hbm, wdn_ref, row_stage, row_sem, axis=0)

    m = mod_ref[pl.ds(pl.program_id(0), 1), :]
    sh = m[:, 3 * D_MODEL:4 * D_MODEL]
    gs = g2_ref[...] * (1.0 + m[:, 4 * D_MODEL:5 * D_MODEL])
    gate2 = m[:, 5 * D_MODEL:6 * D_MODEL]

    def up(r):
        rows = slice(r * sub, (r + 1) * sub)
        h = _rms_modulate(x_ref[0, rows, :], gs, sh).astype(BF16)
        for c in range(d_ff // MXU_TILE):
            a = _dot(h, wup_ref[:, c * MXU_TILE:(c + 1) * MXU_TILE])
            b = _dot(h, wup_ref[:, d_ff + c * MXU_TILE:d_ff + (c + 1) * MXU_TILE])
            hmid_ref[rows, c * MXU_TILE:(c + 1) * MXU_TILE] = (a * jax.nn.sigmoid(a) * b).astype(BF16)

    def down(r):
        rows = slice(r * sub, (r + 1) * sub)
        o_ref[0, rows, :] = x_ref[0, rows, :] + gate2 * _dot(hmid_ref[rows, :], wdn_ref[...])

    n_sub = tile // sub
    up(0)
    for r in range(1, n_sub):
        up(r)
        down(r - 1)
    down(n_sub - 1)


def _ffn_call(x, mod, norm2_g, w_up, w_down, tile):
    B, S, D = x.shape
    d_ff = w_down.shape[0]
    row = lambda b, t: (b, t, 0)
    return pl.pallas_call(
        functools.partial(_ffn_kernel, d_ff=d_ff, tile=tile, sub=MXU_TILE),
        grid=(B, S // tile),
        in_specs=[
            pl.BlockSpec((1, tile, D), row),
            _const_spec(mod.shape),
            _const_spec((1, D)),
            pl.BlockSpec(memory_space=pl.ANY),
            pl.BlockSpec(memory_space=pl.ANY),
        ],
        out_specs=pl.BlockSpec((1, tile, D), row),
        out_shape=jax.ShapeDtypeStruct((B, S, D), F32),
        scratch_shapes=[
            pltpu.VMEM(w_up.shape, BF16),
            pltpu.VMEM(w_down.shape, BF16),
            pltpu.VMEM((STAGE_SLOTS, D, STAGE_CHUNK), F32),
            pltpu.VMEM((STAGE_SLOTS, STAGE_CHUNK, D), F32),
            pltpu.SemaphoreType.DMA((STAGE_SLOTS,)),
            pltpu.SemaphoreType.DMA((STAGE_SLOTS,)),
            pltpu.VMEM((tile, d_ff), BF16),
        ],
        compiler_params=_sequential_params(2),
        name="ffn",
    )(x, mod, norm2_g, w_up, w_down)


def _rope_tables(seq):
    t = np.arange(seq)
    row = (t // GRID_W).astype(np.float64)
    col = (t % GRID_W).astype(np.float64)
    half = HEAD_DIM // 2
    inv_freq = 1.0 / (ROPE_THETA ** (np.arange(0, half, 2, dtype=np.float64) / half))
    ang_r = row[:, None] * inv_freq
    ang_c = col[:, None] * inv_freq
    cos = np.concatenate([np.cos(ang_r)] * 2 + [np.cos(ang_c)] * 2, axis=1)
    sin = np.concatenate([-np.sin(ang_r), np.sin(ang_r), -np.sin(ang_c), np.sin(ang_c)], axis=1)
    reps = (1, LANES // HEAD_DIM)
    return jnp.asarray(np.tile(cos, reps), dtype=F32), jnp.asarray(np.tile(sin, reps), dtype=F32)


def _band_bias():
    i = np.arange(BLOCK)[:, None]
    j = np.arange(3 * BLOCK)[None, :]
    band = np.abs(j - BLOCK - i) <= WINDOW
    first = band & (j >= BLOCK)
    last = band & (j < 2 * BLOCK)
    masks = np.stack([band, first, last])
    return jnp.asarray(np.where(masks, 0.0, NEG_BIG), dtype=F32)


def kernel(x, c, ctx, c_ctx, mod_w, mod_b, norm1_g, norm2_g, w_in, gate_b, q_norm_g, k_norm_g,
           sink, pool_w, pool_scale, w_attn_proj, w_pool_proj, w_out, w_up, w_down):
    B, S, D = x.shape
    depth = mod_w.shape[0]
    cos_t, sin_t = _rope_tables(S)
    bias = _band_bias()
    ones_bd = jnp.asarray(np.kron(np.eye(MXU_TILE // HEAD_DIM), np.ones((HEAD_DIM, HEAD_DIM))), dtype=BF16)
    reps = MXU_TILE // HEAD_DIM
    assert depth == 1, "context-stream update between layers is not implemented"

    for l in range(depth):
        c8 = jnp.concatenate([c, c_ctx[None, :], jnp.zeros((8 - B - 1, D), F32)], axis=0)
        mod = _mod_call(c8, mod_w[l], mod_b[l][None, :])

        wl = w_in[l]
        w_q = (wl[:, :Q_W].reshape(D, N_KV_HEADS, GROUP, HEAD_DIM).transpose(0, 2, 1, 3)
               .reshape(D, Q_W).astype(BF16))
        g1 = norm1_g[l][None, :]
        kg256 = jnp.tile(k_norm_g[l], reps)[None, :]
        qg256 = jnp.tile(q_norm_g[l] * (HEAD_DIM ** -0.5 * LOG2_E), reps)[None, :]

        kc, vc = _ctx_call(ctx, mod, g1, wl, kg256, ones_bd)
        q, k, v, u = _inproj_call(x, mod, g1, w_q, wl, qg256, kg256, ones_bd, cos_t, sin_t, tile=1024)
        attn = _attn_call(sink[l], q, k, v, kc, vc, bias, tile=512)
        w_pool = _pool_fold_call(pool_w[l], pool_scale[l].reshape(POOL_GROUPS, 1, POOL_GROUP_W), w_pool_proj[l])
        x = _merge_call(
            x, mod, g1, attn, u, wl, gate_b[l][None, :], w_attn_proj[l], w_pool, w_out[l], tile=1024)
        x = _ffn_call(x, mod, norm2_g[l][None, :], w_up[l], w_down[l], tile=1024)
    return x
```

```python
import functools

import jax
import jax.numpy as jnp
import numpy as np
from jax import lax
from jax.experimental import pallas as pl
from jax.experimental.pallas import tpu as pltpu

D_MODEL = 1024
GRID_W = 64
HEAD_DIM = 64
N_HEADS = 16
N_KV_HEADS = 4
GROUP = N_HEADS // N_KV_HEADS
Q_W = N_HEADS * HEAD_DIM
KV_W = N_KV_HEADS * HEAD_DIM
WINDOW = 128
BLOCK = 128
ROPE_THETA = 10000.0
POOL_WINDOWS = (2, 4, 8, 16)
POOL_GROUPS = 4
POOL_W = D_MODEL // 2
POOL_GROUP_W = POOL_W // POOL_GROUPS
POOL_HALO = 8
REST_START = Q_W + 2 * KV_W
N_MOD = 6
EPS = 1e-6
NEG_BIG = -1e30
LOG2_E = 1.4426950408889634

LANES = 128
MXU_TILE = 256
VMEM_LIMIT = 56 * 1024 * 1024
STAGE_SLOTS = 4
STAGE_CHUNK = MXU_TILE

F32 = jnp.float32
BF16 = jnp.bfloat16


def _params(n_parallel):
    return pltpu.CompilerParams(
        dimension_semantics=("parallel",) * n_parallel,
        vmem_limit_bytes=VMEM_LIMIT,
    )


def _const_spec(shape):
    nd = len(shape)
    return pl.BlockSpec(shape, lambda *_: (0,) * nd, pipeline_mode=pl.Buffered(1))


def _dot(a, b):
    return jnp.dot(a, b, preferred_element_type=F32)


def _dot_nt(a, b):
    return lax.dot_general(a, b, (((1,), (1,)), ((), ())), preferred_element_type=F32)


def _rms_modulate(x, gs, sh):
    rs = lax.rsqrt(jnp.mean(x * x, axis=-1, keepdims=True) + EPS)
    return x * rs * gs + sh


def _head_rms(t, ones_bd):
    ms = _dot((t * t).astype(BF16), ones_bd) * (1.0 / HEAD_DIM)
    return lax.rsqrt(ms + EPS)


def _stage_weight_bf16(w_hbm, w_ref, stage_ref, sem, axis, store=None):
    slots = stage_ref.shape[0]
    chunk = stage_ref.shape[1 + axis]
    n = w_hbm.shape[axis] // chunk

    def piece(ref, i):
        return ref.at[:, pl.ds(i * chunk, chunk)] if axis == 1 else ref.at[pl.ds(i * chunk, chunk), :]

    def copy(i):
        return pltpu.make_async_copy(piece(w_hbm, i), stage_ref.at[i % slots], sem.at[i % slots])

    for i in range(min(slots - 1, n)):
        copy(i).start()
    for i in range(n):
        if i + slots - 1 < n:
            copy(i + slots - 1).start()
        copy(i).wait()
        staged = stage_ref[i % slots].astype(BF16)
        if store is None:
            piece(w_ref, i)[...] = staged
        else:
            store(i, staged)


def _sequential_params(n_axes):
    return pltpu.CompilerParams(dimension_semantics=("arbitrary",) * n_axes, vmem_limit_bytes=VMEM_LIMIT)


def _mod_kernel(c_ref, w_ref, b_ref, o_ref):
    c = c_ref[...]
    a = (c * jax.nn.sigmoid(c)).astype(BF16)
    o_ref[...] = _dot(a, w_ref[...].astype(BF16)) + b_ref[...]


def _mod_call(c8, mod_w, mod_b):
    n = mod_w.shape[1]
    bn = D_MODEL
    return pl.pallas_call(
        _mod_kernel,
        grid=(n // bn,),
        in_specs=[
            pl.BlockSpec((8, D_MODEL), lambda j: (0, 0)),
            pl.BlockSpec((D_MODEL, bn), lambda j: (0, j)),
            pl.BlockSpec((1, bn), lambda j: (0, j)),
        ],
        out_specs=pl.BlockSpec((8, bn), lambda j: (0, j)),
        out_shape=jax.ShapeDtypeStruct((8, n), F32),
        compiler_params=_params(1),
        name="mod",
    )(c8, mod_w, mod_b)


def _pool_fold_kernel(pw_ref, ps_ref, wpp_ref, o_ref):
    scaled = pw_ref[0] * ps_ref[0]
    o_ref[...] = jnp.dot(scaled, wpp_ref[...], preferred_element_type=F32,
                         precision=lax.Precision.HIGHEST).astype(BF16)


def _pool_fold_call(pool_w, pool_scale3, w_pool_proj):
    n_out = w_pool_proj.shape[1]
    return pl.pallas_call(
        _pool_fold_kernel,
        grid=(POOL_GROUPS,),
        in_specs=[
            pl.BlockSpec((1, POOL_GROUP_W, POOL_GROUP_W), lambda g: (g, 0, 0)),
            pl.BlockSpec((1, 1, POOL_GROUP_W), lambda g: (g, 0, 0)),
            pl.BlockSpec((POOL_GROUP_W, n_out), lambda g: (g, 0)),
        ],
        out_specs=pl.BlockSpec((POOL_GROUP_W, n_out), lambda g: (g, 0)),
        out_shape=jax.ShapeDtypeStruct((POOL_W, n_out), BF16),
        compiler_params=_params(1),
        name="pool_fold",
    )(pool_w, pool_scale3, w_pool_proj)


def _ctx_kernel(ctx_ref, mod_ref, g1_ref, wall_hbm, kg_ref, ones_ref, kc_ref, vc_ref,
                w_ref, col_stage, col_sem, *, n_batch):
    @pl.when(pl.program_id(0) == 0)
    def _():
        _stage_weight_bf16(wall_hbm.at[:, pl.ds(Q_W, 2 * KV_W)], w_ref, col_stage, col_sem, axis=1)

    m = mod_ref[n_batch:n_batch + 1, :]
    sh = m[:, 0:D_MODEL]
    gs = g1_ref[...] * (1.0 + m[:, D_MODEL:2 * D_MODEL])
    h = _rms_modulate(ctx_ref[0], gs, sh).astype(BF16)
    kv = _dot(h, w_ref[...])
    k = kv[:, :KV_W]
    kc_ref[0] = (k * _head_rms(k, ones_ref[...]) * kg_ref[...]).astype(BF16)
    vc_ref[0] = kv[:, KV_W:].astype(BF16)


def _ctx_call(ctx, mod, norm1_g, w_all, kg256, ones_bd):
    B, C, D = ctx.shape
    return pl.pallas_call(
        functools.partial(_ctx_kernel, n_batch=B),
        grid=(B,),
        in_specs=[
            pl.BlockSpec((1, C, D), lambda b: (b, 0, 0)),
            _const_spec(mod.shape),
            _const_spec((1, D)),
            pl.BlockSpec(memory_space=pl.ANY),
            _const_spec((1, KV_W)),
            _const_spec((MXU_TILE, MXU_TILE)),
        ],
        out_specs=[
            pl.BlockSpec((1, C, KV_W), lambda b: (b, 0, 0)),
            pl.BlockSpec((1, C, KV_W), lambda b: (b, 0, 0)),
        ],
        out_shape=[
            jax.ShapeDtypeStruct((B, C, KV_W), BF16),
            jax.ShapeDtypeStruct((B, C, KV_W), BF16),
        ],
        scratch_shapes=[
            pltpu.VMEM((D, 2 * KV_W), BF16),
            pltpu.VMEM((STAGE_SLOTS, D, STAGE_CHUNK), F32),
            pltpu.SemaphoreType.DMA((STAGE_SLOTS,)),
        ],
        compiler_params=_sequential_params(1),
        name="ctx_kv",
    )(ctx, mod, norm1_g, w_all, kg256, ones_bd)


def _rope(t, cos, sin_signed, first_half):
    swap = jnp.where(first_half, pltpu.roll(t, LANES - 16, axis=1), pltpu.roll(t, 16, axis=1))
    return t * cos + swap * sin_signed


def _inproj_kernel(x_ref, mod_ref, g1_ref, wq_ref, wall_hbm, qg_ref, kg_ref, ones_ref, cos_ref, sin_ref,
                   q_ref, k_ref, v_ref, u_ref, wkvu_ref, col_stage, col_sem):
    @pl.when((pl.program_id(0) == 0) & (pl.program_id(1) == 0))
    def _():
        _stage_weight_bf16(wall_hbm.at[:, pl.ds(Q_W, Q_W)], wkvu_ref, col_stage, col_sem, axis=1)

    m = mod_ref[pl.ds(pl.program_id(0), 1), :]
    sh = m[:, 0:D_MODEL]
    gs = g1_ref[...] * (1.0 + m[:, D_MODEL:2 * D_MODEL])
    h = _rms_modulate(x_ref[0], gs, sh).astype(BF16)
    ones_bd = ones_ref[...]
    cos = cos_ref[...]
    sin = sin_ref[...]
    lane = lax.broadcasted_iota(jnp.int32, cos.shape, 1)
    first_half = (lane & 31) < 16

    def normed_rope(t, gain):
        t = t * _head_rms(t, ones_bd) * gain
        halves = [_rope(t[:, i * LANES:(i + 1) * LANES], cos, sin, first_half) for i in range(2)]
        return jnp.concatenate(halves, axis=1).astype(BF16)

    wide = 2 * MXU_TILE
    for j in range(Q_W // wide):
        t = _dot(h, wq_ref[:, j * wide:(j + 1) * wide])
        for i in range(2):
            c0 = j * wide + i * MXU_TILE
            q_ref[0, :, c0:c0 + MXU_TILE] = normed_rope(t[:, i * MXU_TILE:(i + 1) * MXU_TILE], qg_ref[...])
    kv = _dot(h, wkvu_ref[:, :2 * KV_W])
    k_ref[0] = normed_rope(kv[:, :KV_W], kg_ref[...])
    v_ref[0] = kv[:, KV_W:].astype(BF16)
    u_ref[0] = _dot(h, wkvu_ref[:, 2 * KV_W:])


def _inproj_call(x, mod, norm1_g, w_q, w_all, qg256, kg256, ones_bd, cos_t, sin_t, tile):
    B, S, D = x.shape
    row = lambda b, t: (b, t, 0)
    return pl.pallas_call(
        _inproj_kernel,
        grid=(B, S // tile),
        in_specs=[
            pl.BlockSpec((1, tile, D), row),
            _const_spec(mod.shape),
            _const_spec((1, D)),
            _const_spec((D, Q_W)),
            pl.BlockSpec(memory_space=pl.ANY),
            _const_spec((1, MXU_TILE)),
            _const_spec((1, MXU_TILE)),
            _const_spec((MXU_TILE, MXU_TILE)),
            pl.BlockSpec((tile, LANES), lambda b, t: (t, 0)),
            pl.BlockSpec((tile, LANES), lambda b, t: (t, 0)),
        ],
        out_specs=[
            pl.BlockSpec((1, tile, Q_W), row),
            pl.BlockSpec((1, tile, KV_W), row),
            pl.BlockSpec((1, tile, KV_W), row),
            pl.BlockSpec((1, tile, POOL_W), row),
        ],
        out_shape=[
            jax.ShapeDtypeStruct((B, S, Q_W), BF16),
            jax.ShapeDtypeStruct((B, S, KV_W), BF16),
            jax.ShapeDtypeStruct((B, S, KV_W), BF16),
            jax.ShapeDtypeStruct((B, S, POOL_W), F32),
        ],
        scratch_shapes=[
            pltpu.VMEM((D, Q_W), BF16),
            pltpu.VMEM((STAGE_SLOTS, D, STAGE_CHUNK), F32),
            pltpu.SemaphoreType.DMA((STAGE_SLOTS,)),
        ],
        compiler_params=_sequential_params(2),
        name="in_proj",
    )(x, mod, norm1_g, w_q, w_all, qg256, kg256, ones_bd, cos_t, sin_t)


def _attn_kernel(sink_ref, q_ref, kp_ref, kc0_ref, kn_ref, vp_ref, vc0_ref, vn_ref,
                 kctx_ref, vctx_ref, bias_ref, o_ref, *, blocks_per_tile):
    t_idx = pl.program_id(1)
    n_tiles = pl.num_programs(1)
    k_all = jnp.concatenate([kp_ref[0], kc0_ref[0], kn_ref[0]], axis=0)
    v_all = jnp.concatenate([vp_ref[0], vc0_ref[0], vn_ref[0]], axis=0)
    k_ctx = kctx_ref[0]
    v_ctx = vctx_ref[0]
    lane = lax.broadcasted_iota(jnp.int32, (BLOCK, MXU_TILE), 1)
    in_head = [(lane >= kv * HEAD_DIM) & (lane < (kv + 1) * HEAD_DIM) for kv in range(N_KV_HEADS)]
    zero = jnp.zeros((), BF16)
    one = jnp.ones((), BF16)

    def with_ones(v, half):
        vl = lax.broadcasted_iota(jnp.int32, v.shape, 1)
        return jnp.where((vl < LANES) if half == 0 else (vl >= LANES), v, one)

    v_ctx_pair = [with_ones(v_ctx, half) for half in range(2)]
    low_head = lax.broadcasted_iota(jnp.int32, (BLOCK, LANES), 1) < HEAD_DIM

    def block_operands(i):
        variant = jnp.int32(0)
        if i == blocks_per_tile - 1:
            variant = jnp.where(t_idx == n_tiles - 1, 2, variant)
        if i == 0:
            variant = jnp.where(t_idx == 0, 1, variant)
        v_win = v_all[i * BLOCK:(i + 3) * BLOCK]
        return dict(bias=bias_ref[variant], k_win=k_all[i * BLOCK:(i + 3) * BLOCK],
                    v_win_pair=[with_ones(v_win, half) for half in range(2)])

    def scores(blk, i, j):
        q_slab = q_ref[0, i * BLOCK:(i + 1) * BLOCK, j * MXU_TILE:(j + 1) * MXU_TILE]
        qz = jnp.concatenate([jnp.where(in_head[kv], q_slab, zero) for kv in range(N_KV_HEADS)], axis=0)
        return _dot_nt(qz, blk["k_win"]), _dot_nt(qz, k_ctx)

    def softmax_values(blk, i, j, s_win, s_ctx):
        p_win, p_ctx, sink_term = [], [], []
        for kv in range(N_KV_HEADS):
            rows = slice(kv * BLOCK, (kv + 1) * BLOCK)
            sw = s_win[rows] + blk["bias"]
            sc = s_ctx[rows]
            sink = sink_ref[kv * GROUP + j] * LOG2_E
            m = jnp.maximum(
                jnp.maximum(jnp.max(sw, axis=-1, keepdims=True),
                            jnp.max(sc, axis=-1, keepdims=True)), sink)
            p_win.append(jnp.exp2(sw - m).astype(BF16))
            p_ctx.append(jnp.exp2(sc - m).astype(BF16))
            sink_term.append(jnp.exp2(sink - m))
        for half in range(2):
            pw = jnp.concatenate(p_win[2 * half:2 * half + 2], axis=0)
            pc = jnp.concatenate(p_ctx[2 * half:2 * half + 2], axis=0)
            o = _dot(pw, blk["v_win_pair"][half]) + _dot(pc, v_ctx_pair[half])
            vals = o[:, half * LANES:(half + 1) * LANES]
            sums = o[:, (1 - half) * LANES:(2 - half) * LANES]
            out = []
            for r in range(2):
                rows = slice(r * BLOCK, (r + 1) * BLOCK)
                out.append(vals[rows] / (sums[rows] + sink_term[2 * half + r]))
            c0 = j * MXU_TILE + half * LANES
            o_ref[0, i * BLOCK:(i + 1) * BLOCK, c0:c0 + LANES] = jnp.where(low_head, out[0], out[1]).astype(BF16)

    blocks = [block_operands(i) for i in range(blocks_per_tile)]
    items = [(i, j) for i in range(blocks_per_tile) for j in range(GROUP)]
    s_next = scores(blocks[0], *items[0])
    for idx, (i, j) in enumerate(items):
        s_cur = s_next
        if idx + 1 < len(items):
            ni, nj = items[idx + 1]
            s_next = scores(blocks[ni], ni, nj)
        softmax_values(blocks[i], i, j, *s_cur)


def _attn_call(sink, q, k, v, kc, vc, bias, tile):
    B, S, _ = q.shape
    bpt = tile // BLOCK
    nb = S // BLOCK
    C = kc.shape[1]
    cur = lambda b, t, *_: (b, t, 0)
    prev = lambda b, t, *_: (b, jnp.maximum(t * bpt - 1, 0), 0)
    nxt = lambda b, t, *_: (b, jnp.minimum((t + 1) * bpt, nb - 1), 0)
    per_b = lambda b, t, *_: (b, 0, 0)
    halo = (1, BLOCK, KV_W)
    center = (1, tile, KV_W)
    grid_spec = pltpu.PrefetchScalarGridSpec(
        num_scalar_prefetch=1,
        grid=(B, S // tile),
        in_specs=[
            pl.BlockSpec((1, tile, Q_W), cur),
            pl.BlockSpec(halo, prev), pl.BlockSpec(center, cur), pl.BlockSpec(halo, nxt),
            pl.BlockSpec(halo, prev), pl.BlockSpec(center, cur), pl.BlockSpec(halo, nxt),
            pl.BlockSpec((1, C, KV_W), per_b),
            pl.BlockSpec((1, C, KV_W), per_b),
            pl.BlockSpec((3, BLOCK, 3 * BLOCK), lambda b, t, *_: (0, 0, 0)),
        ],
        out_specs=pl.BlockSpec((1, tile, Q_W), cur),
    )
    return pl.pallas_call(
        functools.partial(_attn_kernel, blocks_per_tile=bpt),
        grid_spec=grid_spec,
        out_shape=jax.ShapeDtypeStruct((B, S, Q_W), BF16),
        compiler_params=_params(2),
        name="attn",
    )(sink, q, k, k, k, v, v, v, kc, vc, bias)


def _merge_kernel(x_ref, mod_ref, g1_ref, attn_ref, up_ref, uc_ref, un_ref,
                  wall_hbm, gb_ref, wap_hbm, wpool_ref, wo_hbm, o_ref,
                  uext_ref, wg_ref, wap_ref, wo_ref, col_stage, row_stage, col_sem, row_sem, *, seq, tile, sub):
    @pl.when((pl.program_id(0) == 0) & (pl.program_id(1) == 0))
    def _():
        _stage_weight_bf16(wall_hbm.at[:, pl.ds(2 * D_MODEL, 2 * D_MODEL)], wg_ref, col_stage, col_sem, axis=1)
        _stage_weight_bf16(wo_hbm, wo_ref, col_stage, col_sem, axis=1)

        def store_regrouped(kv, rows_bf16):
            for j in range(GROUP):
                dst = (j * N_KV_HEADS + kv) * HEAD_DIM
                wap_ref[dst:dst + HEAD_DIM, :] = rows_bf16[j * HEAD_DIM:(j + 1) * HEAD_DIM, :]

        _stage_weight_bf16(wap_hbm, None, row_stage, row_sem, axis=0, store=store_regrouped)

    t_idx = pl.program_id(1)
    n_tiles = pl.num_programs(1)
    m = mod_ref[pl.ds(pl.program_id(0), 1), :]
    sh = m[:, 0:D_MODEL]
    gs = g1_ref[...] * (1.0 + m[:, D_MODEL:2 * D_MODEL])
    gate1 = m[:, 2 * D_MODEL:3 * D_MODEL]

    keep_prev = (t_idx > 0).astype(F32)
    keep_next = (t_idx < n_tiles - 1).astype(F32)
    uext_ref[0:POOL_HALO] = up_ref[0] * keep_prev
    uext_ref[POOL_HALO:POOL_HALO + tile] = uc_ref[0]
    uext_ref[POOL_HALO + tile:] = un_ref[0] * keep_next
    edge_rows = lax.broadcasted_iota(jnp.int32, (POOL_HALO, POOL_GROUP_W), 0)

    def attn_dot(r):
        return _dot(attn_ref[0, r * sub:(r + 1) * sub, :], wap_ref[...])

    def gate_dot(r):
        h = _rms_modulate(x_ref[0, r * sub:(r + 1) * sub, :], gs, sh).astype(BF16)
        return jax.nn.sigmoid(_dot(h, wg_ref[...]) + gb_ref[...])

    def pool_dot(r):
        r0 = r * sub
        pos_first = t_idx * tile + r0 + edge_rows
        pos_last = pos_first + (sub - POOL_HALO)
        diffs = []
        for g, w in enumerate(POOL_WINDOWS):
            cols = slice(g * POOL_GROUP_W, (g + 1) * POOL_GROUP_W)
            acc = uext_ref[pl.ds(r0, sub + 2 * POOL_HALO), cols]
            k = 1
            while k < w:
                acc = acc[:acc.shape[0] - k] + acc[k:]
                k *= 2
            win = acc[POOL_HALO - w // 2:POOL_HALO - w // 2 + sub]

            def inv_count(pos):
                return 1.0 / (jnp.minimum(pos + w // 2, seq) - jnp.maximum(pos - w // 2, 0)).astype(F32)

            pooled_avg = jnp.concatenate([
                win[:POOL_HALO] * inv_count(pos_first),
                win[POOL_HALO:sub - POOL_HALO] * (1.0 / w),
                win[sub - POOL_HALO:] * inv_count(pos_last)], axis=0)
            diffs.append((pooled_avg - uext_ref[pl.ds(r0 + POOL_HALO, sub), cols]).astype(BF16))
        return _dot(jnp.concatenate(diffs, axis=1), wpool_ref[...])

    def output(r, a, gates, p):
        rows = slice(r * sub, (r + 1) * sub)
        merged = (gates[:, :D_MODEL] * a + gates[:, D_MODEL:] * p).astype(BF16)
        o_ref[0, rows, :] = x_ref[0, rows, :] + gate1 * _dot(merged, wo_ref[...])

    n_sub = tile // sub
    a = [attn_dot(r) for r in range(n_sub)]
    p = [pool_dot(r) for r in range(n_sub)]
    gates = [gate_dot(r) for r in range(n_sub)]
    for r in range(n_sub):
        output(r, a[r], gates[r], p[r])


def _merge_call(x, mod, norm1_g, attn, u, w_all, gate_b, w_ap, w_pool, w_out, tile):
    B, S, D = x.shape
    r = tile // POOL_HALO
    n_halo = S // POOL_HALO
    row = lambda b, t: (b, t, 0)
    return pl.pallas_call(
        functools.partial(_merge_kernel, seq=S, tile=tile, sub=MXU_TILE),
        grid=(B, S // tile),
        in_specs=[
            pl.BlockSpec((1, tile, D), row),
            _const_spec(mod.shape),
            _const_spec((1, D)),
            pl.BlockSpec((1, tile, Q_W), row),
            pl.BlockSpec((1, POOL_HALO, POOL_W), lambda b, t: (b, jnp.maximum(t * r - 1, 0), 0)),
            pl.BlockSpec((1, tile, POOL_W), row),
            pl.BlockSpec((1, POOL_HALO, POOL_W), lambda b, t: (b, jnp.minimum((t + 1) * r, n_halo - 1), 0)),
            pl.BlockSpec(memory_space=pl.ANY),
            _const_spec(gate_b.shape),
            pl.BlockSpec(memory_space=pl.ANY),
            _const_spec(w_pool.shape),
            pl.BlockSpec(memory_space=pl.ANY),
        ],
        out_specs=pl.BlockSpec((1, tile, D), row),
        out_shape=jax.ShapeDtypeStruct((B, S, D), F32),
        scratch_shapes=[
            pltpu.VMEM((tile + 2 * POOL_HALO, POOL_W), F32),
            pltpu.VMEM((D, 2 * D), BF16),
            pltpu.VMEM(w_ap.shape, BF16),
            pltpu.VMEM(w_out.shape, BF16),
            pltpu.VMEM((STAGE_SLOTS, D, STAGE_CHUNK), F32),
            pltpu.VMEM((STAGE_SLOTS, STAGE_CHUNK, D), F32),
            pltpu.SemaphoreType.DMA((STAGE_SLOTS,)),
            pltpu.SemaphoreType.DMA((STAGE_SLOTS,)),
        ],
        compiler_params=_sequential_params(2),
        name="merge",
    )(x, mod, norm1_g, attn, u, u, u, w_all, gate_b, w_ap, w_pool, w_out)


def _ffn_kernel(x_ref, mod_ref, g2_ref, wup_hbm, wdn_hbm, o_ref,
                wup_ref, wdn_ref, col_stage, row_stage, col_sem, row_sem, hmid_ref, *, d_ff, tile, sub):
    @pl.when((pl.program_id(0) == 0) & (pl.program_id(1) == 0))
    def _():
        _stage_weight_bf16(wup_hbm, wup_ref, col_stage, col_sem, axis=1)
        _stage_weight_bf16(wdn_hbm, wdn_ref, row_stage, row_sem, axis=0)

    m = mod_ref[pl.ds(pl.program_id(0), 1), :]
    sh = m[:, 3 * D_MODEL:4 * D_MODEL]
    gs = g2_ref[...] * (1.0 + m[:, 4 * D_MODEL:5 * D_MODEL])
    gate2 = m[:, 5 * D_MODEL:6 * D_MODEL]

    def up(r):
        rows = slice(r * sub, (r + 1) * sub)
        h = _rms_modulate(x_ref[0, rows, :], gs, sh).astype(BF16)
        for c in range(d_ff // MXU_TILE):
            a = _dot(h, wup_ref[:, c * MXU_TILE:(c + 1) * MXU_TILE])
            b = _dot(h, wup_ref[:, d_ff + c * MXU_TILE:d_ff + (c + 1) * MXU_TILE])
            hmid_ref[rows, c * MXU_TILE:(c + 1) * MXU_TILE] = (a * jax.nn.sigmoid(a) * b).astype(BF16)

    def down(r):
        rows = slice(r * sub, (r + 1) * sub)
        o_ref[0, rows, :] = x_ref[0, rows, :] + gate2 * _dot(hmid_ref[rows, :], wdn_ref[...])

    n_sub = tile // sub
    up(0)
    for r in range(1, n_sub):
        up(r)
        down(r - 1)
    down(n_sub - 1)


def _ffn_call(x, mod, norm2_g, w_up, w_down, tile):
    B, S, D = x.shape
    d_ff = w_down.shape[0]
    row = lambda b, t: (b, t, 0)
    return pl.pallas_call(
        functools.partial(_ffn_kernel, d_ff=d_ff, tile=tile, sub=MXU_TILE),
        grid=(B, S // tile),
        in_specs=[
            pl.BlockSpec((1, tile, D), row),
            _const_spec(mod.shape),
            _const_spec((1, D)),
            pl.BlockSpec(memory_space=pl.ANY),
            pl.BlockSpec(memory_space=pl.ANY),
        ],
        out_specs=pl.BlockSpec((1, tile, D), row),
        out_shape=jax.ShapeDtypeStruct((B, S, D), F32),
        scratch_shapes=[
            pltpu.VMEM(w_up.shape, BF16),
            pltpu.VMEM(w_down.shape, BF16),
            pltpu.VMEM((STAGE_SLOTS, D, STAGE_CHUNK), F32),
            pltpu.VMEM((STAGE_SLOTS, STAGE_CHUNK, D), F32),
            pltpu.SemaphoreType.DMA((STAGE_SLOTS,)),
            pltpu.SemaphoreType.DMA((STAGE_SLOTS,)),
            pltpu.VMEM((tile, d_ff), BF16),
        ],
        compiler_params=_sequential_params(2),
        name="ffn",
    )(x, mod, norm2_g, w_up, w_down)


def _rope_tables(seq):
    t = np.arange(seq)
    row = (t // GRID_W).astype(np.float64)
    col = (t % GRID_W).astype(np.float64)
    half = HEAD_DIM // 2
    inv_freq = 1.0 / (ROPE_THETA ** (np.arange(0, half, 2, dtype=np.float64) / half))
    ang_r = row[:, None] * inv_freq
    ang_c = col[:, None] * inv_freq
    cos = np.concatenate([np.cos(ang_r)] * 2 + [np.cos(ang_c)] * 2, axis=1)
    sin = np.concatenate([-np.sin(ang_r), np.sin(ang_r), -np.sin(ang_c), np.sin(ang_c)], axis=1)
    reps = (1, LANES // HEAD_DIM)
    return jnp.asarray(np.tile(cos, reps), dtype=F32), jnp.asarray(np.tile(sin, reps), dtype=F32)


def _band_bias():
    i = np.arange(BLOCK)[:, None]
    j = np.arange(3 * BLOCK)[None, :]
    band = np.abs(j - BLOCK - i) <= WINDOW
    first = band & (j >= BLOCK)
    last = band & (j < 2 * BLOCK)
    masks = np.stack([band, first, last])
    return jnp.asarray(np.where(masks, 0.0, NEG_BIG), dtype=F32)


def kernel(x, c, ctx, c_ctx, mod_w, mod_b, norm1_g, norm2_g, w_in, gate_b, q_norm_g, k_norm_g,
           sink, pool_w, pool_scale, w_attn_proj, w_pool_proj, w_out, w_up, w_down):
    B, S, D = x.shape
    depth = mod_w.shape[0]
    cos_t, sin_t = _rope_tables(S)
    bias = _band_bias()
    ones_bd = jnp.asarray(np.kron(np.eye(MXU_TILE // HEAD_DIM), np.ones((HEAD_DIM, HEAD_DIM))), dtype=BF16)
    reps = MXU_TILE // HEAD_DIM
    assert depth == 1, "context-stream update between layers is not implemented"

    for l in range(depth):
        c8 = jnp.concatenate([c, c_ctx[None, :], jnp.zeros((8 - B - 1, D), F32)], axis=0)
        mod = _mod_call(c8, mod_w[l], mod_b[l][None, :])

        wl = w_in[l]
        w_q = (wl[:, :Q_W].reshape(D, N_KV_HEADS, GROUP, HEAD_DIM).transpose(0, 2, 1, 3)
               .reshape(D, Q_W).astype(BF16))
        g1 = norm1_g[l][None, :]
        kg256 = jnp.tile(k_norm_g[l], reps)[None, :]
        qg256 = jnp.tile(q_norm_g[l] * (HEAD_DIM ** -0.5 * LOG2_E), reps)[None, :]

        kc, vc = _ctx_call(ctx, mod, g1, wl, kg256, ones_bd)
        q, k, v, u = _inproj_call(x, mod, g1, w_q, wl, qg256, kg256, ones_bd, cos_t, sin_t, tile=1024)
        attn = _attn_call(sink[l], q, k, v, kc, vc, bias, tile=1024)
        w_pool = _pool_fold_call(pool_w[l], pool_scale[l].reshape(POOL_GROUPS, 1, POOL_GROUP_W), w_pool_proj[l])
        x = _merge_call(
            x, mod, g1, attn, u, wl, gate_b[l][None, :], w_attn_proj[l], w_pool, w_out[l], tile=1024)
        x = _ffn_call(x, mod, norm2_g[l][None, :], w_up[l], w_down[l], tile=1024)
    return x
```

```python
import functools

import jax
import jax.numpy as jnp
import numpy as np
from jax import lax
from jax.experimental import pallas as pl
from jax.experimental.pallas import tpu as pltpu

D_MODEL = 1024
GRID_W = 64
HEAD_DIM = 64
N_HEADS = 16
N_KV_HEADS = 4
GROUP = N_HEADS // N_KV_HEADS
Q_W = N_HEADS * HEAD_DIM
KV_W = N_KV_HEADS * HEAD_DIM
WINDOW = 128
BLOCK = 128
ROPE_THETA = 10000.0
POOL_WINDOWS = (2, 4, 8, 16)
POOL_GROUPS = 4
POOL_W = D_MODEL // 2
POOL_GROUP_W = POOL_W // POOL_GROUPS
POOL_HALO = 8
REST_START = Q_W + 2 * KV_W
N_MOD = 6
EPS = 1e-6
NEG_BIG = -1e30
LOG2_E = 1.4426950408889634

LANES = 128
MXU_TILE = 256
VMEM_LIMIT = 56 * 1024 * 1024
STAGE_SLOTS = 4
STAGE_CHUNK = MXU_TILE

F32 = jnp.float32
BF16 = jnp.bfloat16


def _params(n_parallel):
    return pltpu.CompilerParams(
        dimension_semantics=("parallel",) * n_parallel,
        vmem_limit_bytes=VMEM_LIMIT,
    )


def _const_spec(shape):
    nd = len(shape)
    return pl.BlockSpec(shape, lambda *_: (0,) * nd, pipeline_mode=pl.Buffered(1))


def _dot(a, b):
    return jnp.dot(a, b, preferred_element_type=F32)


def _dot_nt(a, b):
    return lax.dot_general(a, b, (((1,), (1,)), ((), ())), preferred_element_type=F32)


def _rms_modulate(x, gs, sh):
    rs = lax.rsqrt(jnp.mean(x * x, axis=-1, keepdims=True) + EPS)
    return x * rs * gs + sh


def _head_rms(t, ones_bd):
    ms = _dot((t * t).astype(BF16), ones_bd) * (1.0 / HEAD_DIM)
    return lax.rsqrt(ms + EPS)


def _stage_weight_bf16(w_hbm, w_ref, stage_ref, sem, axis, store=None):
    slots = stage_ref.shape[0]
    chunk = stage_ref.shape[1 + axis]
    n = w_hbm.shape[axis] // chunk

    def piece(ref, i):
        return ref.at[:, pl.ds(i * chunk, chunk)] if axis == 1 else ref.at[pl.ds(i * chunk, chunk), :]

    def copy(i):
        return pltpu.make_async_copy(piece(w_hbm, i), stage_ref.at[i % slots], sem.at[i % slots])

    for i in range(min(slots - 1, n)):
        copy(i).start()
    for i in range(n):
        if i + slots - 1 < n:
            copy(i + slots - 1).start()
        copy(i).wait()
        staged = stage_ref[i % slots].astype(BF16)
        if store is None:
            piece(w_ref, i)[...] = staged
        else:
            store(i, staged)


def _sequential_params(n_axes):
    return pltpu.CompilerParams(dimension_semantics=("arbitrary",) * n_axes, vmem_limit_bytes=VMEM_LIMIT)


def _mod_kernel(c_ref, w_ref, b_ref, o_ref):
    c = c_ref[...]
    a = (c * jax.nn.sigmoid(c)).astype(BF16)
    o_ref[...] = _dot(a, w_ref[...].astype(BF16)) + b_ref[...]


def _mod_call(c8, mod_w, mod_b):
    n = mod_w.shape[1]
    bn = 2 * D_MODEL
    return pl.pallas_call(
        _mod_kernel,
        grid=(n // bn,),
        in_specs=[
            pl.BlockSpec((8, D_MODEL), lambda j: (0, 0)),
            pl.BlockSpec((D_MODEL, bn), lambda j: (0, j)),
            pl.BlockSpec((1, bn), lambda j: (0, j)),
        ],
        out_specs=pl.BlockSpec((8, bn), lambda j: (0, j)),
        out_shape=jax.ShapeDtypeStruct((8, n), F32),
        compiler_params=_params(1),
        name="mod",
    )(c8, mod_w, mod_b)


def _pool_fold_kernel(pw_ref, ps_ref, wpp_ref, o_ref):
    scaled = (pw_ref[0] * ps_ref[0]).astype(BF16)
    o_ref[...] = _dot(scaled, wpp_ref[...].astype(BF16)).astype(BF16)


def _pool_fold_call(pool_w, pool_scale3, w_pool_proj):
    n_out = w_pool_proj.shape[1]
    return pl.pallas_call(
        _pool_fold_kernel,
        grid=(POOL_GROUPS,),
        in_specs=[
            pl.BlockSpec((1, POOL_GROUP_W, POOL_GROUP_W), lambda g: (g, 0, 0)),
            pl.BlockSpec((1, 1, POOL_GROUP_W), lambda g: (g, 0, 0)),
            pl.BlockSpec((POOL_GROUP_W, n_out), lambda g: (g, 0)),
        ],
        out_specs=pl.BlockSpec((POOL_GROUP_W, n_out), lambda g: (g, 0)),
        out_shape=jax.ShapeDtypeStruct((POOL_W, n_out), BF16),
        compiler_params=_params(1),
        name="pool_fold",
    )(pool_w, pool_scale3, w_pool_proj)


def _ctx_kernel(ctx_ref, mod_ref, g1_ref, wall_hbm, kg_ref, ones_ref, kc_ref, vc_ref,
                w_ref, col_stage, col_sem, *, n_batch):
    @pl.when(pl.program_id(0) == 0)
    def _():
        _stage_weight_bf16(wall_hbm.at[:, pl.ds(Q_W, 2 * KV_W)], w_ref, col_stage, col_sem, axis=1)

    m = mod_ref[n_batch:n_batch + 1, :]
    sh = m[:, 0:D_MODEL]
    gs = g1_ref[...] * (1.0 + m[:, D_MODEL:2 * D_MODEL])
    h = _rms_modulate(ctx_ref[0], gs, sh).astype(BF16)
    kv = _dot(h, w_ref[...])
    k = kv[:, :KV_W]
    kc_ref[0] = (k * _head_rms(k, ones_ref[...]) * kg_ref[...]).astype(BF16)
    vc_ref[0] = kv[:, KV_W:].astype(BF16)


def _ctx_call(ctx, mod, norm1_g, w_all, kg256, ones_bd):
    B, C, D = ctx.shape
    rows = B * C
    kc, vc = pl.pallas_call(
        functools.partial(_ctx_kernel, n_batch=B),
        grid=(1,),
        in_specs=[
            pl.BlockSpec((1, rows, D), lambda i: (0, 0, 0)),
            _const_spec(mod.shape),
            _const_spec((1, D)),
            pl.BlockSpec(memory_space=pl.ANY),
            _const_spec((1, KV_W)),
            _const_spec((MXU_TILE, MXU_TILE)),
        ],
        out_specs=[
            pl.BlockSpec((1, rows, KV_W), lambda i: (0, 0, 0)),
            pl.BlockSpec((1, rows, KV_W), lambda i: (0, 0, 0)),
        ],
        out_shape=[
            jax.ShapeDtypeStruct((1, rows, KV_W), BF16),
            jax.ShapeDtypeStruct((1, rows, KV_W), BF16),
        ],
        scratch_shapes=[
            pltpu.VMEM((D, 2 * KV_W), BF16),
            pltpu.VMEM((STAGE_SLOTS, D, STAGE_CHUNK), F32),
            pltpu.SemaphoreType.DMA((STAGE_SLOTS,)),
        ],
        compiler_params=_sequential_params(1),
        name="ctx_kv",
    )(ctx.reshape(1, rows, D), mod, norm1_g, w_all, kg256, ones_bd)
    return kc.reshape(B, C, KV_W), vc.reshape(B, C, KV_W)


def _rope(t, cos, sin_signed, first_half):
    swap = jnp.where(first_half, pltpu.roll(t, LANES - 16, axis=1), pltpu.roll(t, 16, axis=1))
    return t * cos + swap * sin_signed


def _inproj_kernel(x_ref, mod_ref, g1_ref, wq_ref, wall_hbm, qg_ref, kg_ref, ones_ref, cos_ref, sin_ref,
                   q_ref, k_ref, v_ref, u_ref, wkvu_ref, col_stage, col_sem):
    @pl.when((pl.program_id(0) == 0) & (pl.program_id(1) == 0))
    def _():
        _stage_weight_bf16(wall_hbm.at[:, pl.ds(Q_W, Q_W)], wkvu_ref, col_stage, col_sem, axis=1)

    m = mod_ref[pl.ds(pl.program_id(0), 1), :]
    sh = m[:, 0:D_MODEL]
    gs = g1_ref[...] * (1.0 + m[:, D_MODEL:2 * D_MODEL])
    h = _rms_modulate(x_ref[0], gs, sh).astype(BF16)
    ones_bd = ones_ref[...]
    cos = cos_ref[...]
    sin = sin_ref[...]
    lane = lax.broadcasted_iota(jnp.int32, cos.shape, 1)
    first_half = (lane & 31) < 16

    def normed_rope(t, gain):
        t = t * _head_rms(t, ones_bd) * gain
        halves = [_rope(t[:, i * LANES:(i + 1) * LANES], cos, sin, first_half) for i in range(2)]
        return jnp.concatenate(halves, axis=1).astype(BF16)

    wide = 2 * MXU_TILE
    for j in range(Q_W // wide):
        t = _dot(h, wq_ref[:, j * wide:(j + 1) * wide])
        for i in range(2):
            c0 = j * wide + i * MXU_TILE
            q_ref[0, :, c0:c0 + MXU_TILE] = normed_rope(t[:, i * MXU_TILE:(i + 1) * MXU_TILE], qg_ref[...])
    kv = _dot(h, wkvu_ref[:, :2 * KV_W])
    k_ref[0] = normed_rope(kv[:, :KV_W], kg_ref[...])
    v_ref[0] = kv[:, KV_W:].astype(BF16)
    u_ref[0] = _dot(h, wkvu_ref[:, 2 * KV_W:])


def _inproj_call(x, mod, norm1_g, w_q, w_all, qg256, kg256, ones_bd, cos_t, sin_t, tile):
    B, S, D = x.shape
    row = lambda b, t: (b, t, 0)
    return pl.pallas_call(
        _inproj_kernel,
        grid=(B, S // tile),
        in_specs=[
            pl.BlockSpec((1, tile, D), row),
            _const_spec(mod.shape),
            _const_spec((1, D)),
            _const_spec((D, Q_W)),
            pl.BlockSpec(memory_space=pl.ANY),
            _const_spec((1, MXU_TILE)),
            _const_spec((1, MXU_TILE)),
            _const_spec((MXU_TILE, MXU_TILE)),
            pl.BlockSpec((tile, LANES), lambda b, t: (t, 0)),
            pl.BlockSpec((tile, LANES), lambda b, t: (t, 0)),
        ],
        out_specs=[
            pl.BlockSpec((1, tile, Q_W), row),
            pl.BlockSpec((1, tile, KV_W), row),
            pl.BlockSpec((1, tile, KV_W), row),
            pl.BlockSpec((1, tile, POOL_W), row),
        ],
        out_shape=[
            jax.ShapeDtypeStruct((B, S, Q_W), BF16),
            jax.ShapeDtypeStruct((B, S, KV_W), BF16),
            jax.ShapeDtypeStruct((B, S, KV_W), BF16),
            jax.ShapeDtypeStruct((B, S, POOL_W), F32),
        ],
        scratch_shapes=[
            pltpu.VMEM((D, Q_W), BF16),
            pltpu.VMEM((STAGE_SLOTS, D, STAGE_CHUNK), F32),
            pltpu.SemaphoreType.DMA((STAGE_SLOTS,)),
        ],
        compiler_params=_sequential_params(2),
        name="in_proj",
    )(x, mod, norm1_g, w_q, w_all, qg256, kg256, ones_bd, cos_t, sin_t)


def _attn_kernel(sink_ref, q_ref, kp_ref, kc0_ref, kn_ref, vp_ref, vc0_ref, vn_ref,
                 kctx_ref, vctx_ref, bias_ref, o_ref, *, blocks_per_tile):
    t_idx = pl.program_id(1)
    n_tiles = pl.num_programs(1)
    k_all = jnp.concatenate([kp_ref[0], kc0_ref[0], kn_ref[0]], axis=0)
    v_all = jnp.concatenate([vp_ref[0], vc0_ref[0], vn_ref[0]], axis=0)
    k_ctx = kctx_ref[0]
    v_ctx = vctx_ref[0]
    lane = lax.broadcasted_iota(jnp.int32, (BLOCK, MXU_TILE), 1)
    in_head = [(lane >= kv * HEAD_DIM) & (lane < (kv + 1) * HEAD_DIM) for kv in range(N_KV_HEADS)]
    zero = jnp.zeros((), BF16)
    one = jnp.ones((), BF16)

    def with_ones(v, half):
        vl = lax.broadcasted_iota(jnp.int32, v.shape, 1)
        return jnp.where((vl < LANES) if half == 0 else (vl >= LANES), v, one)

    v_ctx_pair = [with_ones(v_ctx, half) for half in range(2)]
    low_head = lax.broadcasted_iota(jnp.int32, (BLOCK, LANES), 1) < HEAD_DIM

    def block_operands(i):
        variant = jnp.int32(0)
        if i == blocks_per_tile - 1:
            variant = jnp.where(t_idx == n_tiles - 1, 2, variant)
        if i == 0:
            variant = jnp.where(t_idx == 0, 1, variant)
        v_win = v_all[i * BLOCK:(i + 3) * BLOCK]
        return dict(bias=bias_ref[variant], k_win=k_all[i * BLOCK:(i + 3) * BLOCK],
                    v_win_pair=[with_ones(v_win, half) for half in range(2)])

    def scores(blk, i, j):
        q_slab = q_ref[0, i * BLOCK:(i + 1) * BLOCK, j * MXU_TILE:(j + 1) * MXU_TILE]
        qz = jnp.concatenate([jnp.where(in_head[kv], q_slab, zero) for kv in range(N_KV_HEADS)], axis=0)
        return _dot_nt(qz, blk["k_win"]), _dot_nt(qz, k_ctx)

    def softmax_values(blk, i, j, s_win, s_ctx):
        p_win, p_ctx, sink_term = [], [], []
        for kv in range(N_KV_HEADS):
            rows = slice(kv * BLOCK, (kv + 1) * BLOCK)
            sw = s_win[rows] + blk["bias"]
            sc = s_ctx[rows]
            sink = sink_ref[kv * GROUP + j] * LOG2_E
            m = jnp.maximum(
                jnp.maximum(jnp.max(sw, axis=-1, keepdims=True),
                            jnp.max(sc, axis=-1, keepdims=True)), sink)
            p_win.append(jnp.exp2(sw - m).astype(BF16))
            p_ctx.append(jnp.exp2(sc - m).astype(BF16))
            sink_term.append(jnp.exp2(sink - m))
        for half in range(2):
            pw = jnp.concatenate(p_win[2 * half:2 * half + 2], axis=0)
            pc = jnp.concatenate(p_ctx[2 * half:2 * half + 2], axis=0)
            o = _dot(pw, blk["v_win_pair"][half]) + _dot(pc, v_ctx_pair[half])
            vals = o[:, half * LANES:(half + 1) * LANES]
            sums = o[:, (1 - half) * LANES:(2 - half) * LANES]
            out = []
            for r in range(2):
                rows = slice(r * BLOCK, (r + 1) * BLOCK)
                out.append(vals[rows] / (sums[rows] + sink_term[2 * half + r]))
            c0 = j * MXU_TILE + half * LANES
            o_ref[0, i * BLOCK:(i + 1) * BLOCK, c0:c0 + LANES] = jnp.where(low_head, out[0], out[1]).astype(BF16)

    blocks = [block_operands(i) for i in range(blocks_per_tile)]
    items = [(i, j) for i in range(blocks_per_tile) for j in range(GROUP)]
    s_next = scores(blocks[0], *items[0])
    for idx, (i, j) in enumerate(items):
        s_cur = s_next
        if idx + 1 < len(items):
            ni, nj = items[idx + 1]
            s_next = scores(blocks[ni], ni, nj)
        softmax_values(blocks[i], i, j, *s_cur)


def _attn_call(sink, q, k, v, kc, vc, bias, tile):
    B, S, _ = q.shape
    bpt = tile // BLOCK
    nb = S // BLOCK
    C = kc.shape[1]
    cur = lambda b, t, *_: (b, t, 0)
    prev = lambda b, t, *_: (b, jnp.maximum(t * bpt - 1, 0), 0)
    nxt = lambda b, t, *_: (b, jnp.minimum((t + 1) * bpt, nb - 1), 0)
    per_b = lambda b, t, *_: (b, 0, 0)
    halo = (1, BLOCK, KV_W)
    center = (1, tile, KV_W)
    grid_spec = pltpu.PrefetchScalarGridSpec(
        num_scalar_prefetch=1,
        grid=(B, S // tile),
        in_specs=[
            pl.BlockSpec((1, tile, Q_W), cur),
            pl.BlockSpec(halo, prev), pl.BlockSpec(center, cur), pl.BlockSpec(halo, nxt),
            pl.BlockSpec(halo, prev), pl.BlockSpec(center, cur), pl.BlockSpec(halo, nxt),
            pl.BlockSpec((1, C, KV_W), per_b),
            pl.BlockSpec((1, C, KV_W), per_b),
            pl.BlockSpec((3, BLOCK, 3 * BLOCK), lambda b, t, *_: (0, 0, 0)),
        ],
        out_specs=pl.BlockSpec((1, tile, Q_W), cur),
    )
    return pl.pallas_call(
        functools.partial(_attn_kernel, blocks_per_tile=bpt),
        grid_spec=grid_spec,
        out_shape=jax.ShapeDtypeStruct((B, S, Q_W), BF16),
        compiler_params=_params(2),
        name="attn",
    )(sink, q, k, k, k, v, v, v, kc, vc, bias)


def _merge_kernel(x_ref, mod_ref, g1_ref, attn_ref, up_ref, uc_ref, un_ref,
                  wall_hbm, gb_ref, wap_hbm, wpool_ref, wo_hbm, o_ref,
                  uext_ref, wg_ref, wap_ref, wo_ref, col_stage, row_stage, col_sem, row_sem, *, seq, tile, sub):
    @pl.when((pl.program_id(0) == 0) & (pl.program_id(1) == 0))
    def _():
        _stage_weight_bf16(wall_hbm.at[:, pl.ds(2 * D_MODEL, 2 * D_MODEL)], wg_ref, col_stage, col_sem, axis=1)
        _stage_weight_bf16(wo_hbm, wo_ref, col_stage, col_sem, axis=1)

        def store_regrouped(kv, rows_bf16):
            for j in range(GROUP):
                dst = (j * N_KV_HEADS + kv) * HEAD_DIM
                wap_ref[dst:dst + HEAD_DIM, :] = rows_bf16[j * HEAD_DIM:(j + 1) * HEAD_DIM, :]

        _stage_weight_bf16(wap_hbm, None, row_stage, row_sem, axis=0, store=store_regrouped)

    t_idx = pl.program_id(1)
    n_tiles = pl.num_programs(1)
    m = mod_ref[pl.ds(pl.program_id(0), 1), :]
    sh = m[:, 0:D_MODEL]
    gs = g1_ref[...] * (1.0 + m[:, D_MODEL:2 * D_MODEL])
    gate1 = m[:, 2 * D_MODEL:3 * D_MODEL]

    keep_prev = (t_idx > 0).astype(F32)
    keep_next = (t_idx < n_tiles - 1).astype(F32)
    uext_ref[0:POOL_HALO] = up_ref[0] * keep_prev
    uext_ref[POOL_HALO:POOL_HALO + tile] = uc_ref[0]
    uext_ref[POOL_HALO + tile:] = un_ref[0] * keep_next
    edge_rows = lax.broadcasted_iota(jnp.int32, (POOL_HALO, POOL_GROUP_W), 0)

    def attn_dot(r):
        return _dot(attn_ref[0, r * sub:(r + 1) * sub, :], wap_ref[...])

    def gate_dot(r):
        h = _rms_modulate(x_ref[0, r * sub:(r + 1) * sub, :], gs, sh).astype(BF16)
        return jax.nn.sigmoid(_dot(h, wg_ref[...]) + gb_ref[...])

    def pool_dot(r):
        r0 = r * sub
        pos_first = t_idx * tile + r0 + edge_rows
        pos_last = pos_first + (sub - POOL_HALO)
        diffs = []
        for g, w in enumerate(POOL_WINDOWS):
            cols = slice(g * POOL_GROUP_W, (g + 1) * POOL_GROUP_W)
            acc = uext_ref[pl.ds(r0, sub + 2 * POOL_HALO), cols]
            k = 1
            while k < w:
                acc = acc[:acc.shape[0] - k] + acc[k:]
                k *= 2
            win = acc[POOL_HALO - w // 2:POOL_HALO - w // 2 + sub]

            def inv_count(pos):
                return 1.0 / (jnp.minimum(pos + w // 2, seq) - jnp.maximum(pos - w // 2, 0)).astype(F32)

            pooled_avg = jnp.concatenate([
                win[:POOL_HALO] * inv_count(pos_first),
                win[POOL_HALO:sub - POOL_HALO] * (1.0 / w),
                win[sub - POOL_HALO:] * inv_count(pos_last)], axis=0)
            diffs.append((pooled_avg - uext_ref[pl.ds(r0 + POOL_HALO, sub), cols]).astype(BF16))
        return _dot(jnp.concatenate(diffs, axis=1), wpool_ref[...])

    def output(r, a, gates, p):
        rows = slice(r * sub, (r + 1) * sub)
        merged = (gates[:, :D_MODEL] * a + gates[:, D_MODEL:] * p).astype(BF16)
        o_ref[0, rows, :] = x_ref[0, rows, :] + gate1 * _dot(merged, wo_ref[...])

    n_sub = tile // sub
    a = [attn_dot(r) for r in range(n_sub)]
    p = [pool_dot(r) for r in range(n_sub)]
    gates = [gate_dot(r) for r in range(n_sub)]
    for r in range(n_sub):
        output(r, a[r], gates[r], p[r])


def _merge_call(x, mod, norm1_g, attn, u, w_all, gate_b, w_ap, w_pool, w_out, tile):
    B, S, D = x.shape
    r = tile // POOL_HALO
    n_halo = S // POOL_HALO
    row = lambda b, t: (b, t, 0)
    return pl.pallas_call(
        functools.partial(_merge_kernel, seq=S, tile=tile, sub=MXU_TILE),
        grid=(B, S // tile),
        in_specs=[
            pl.BlockSpec((1, tile, D), row),
            _const_spec(mod.shape),
            _const_spec((1, D)),
            pl.BlockSpec((1, tile, Q_W), row),
            pl.BlockSpec((1, POOL_HALO, POOL_W), lambda b, t: (b, jnp.maximum(t * r - 1, 0), 0)),
            pl.BlockSpec((1, tile, POOL_W), row),
            pl.BlockSpec((1, POOL_HALO, POOL_W), lambda b, t: (b, jnp.minimum((t + 1) * r, n_halo - 1), 0)),
            pl.BlockSpec(memory_space=pl.ANY),
            _const_spec(gate_b.shape),
            pl.BlockSpec(memory_space=pl.ANY),
            _const_spec(w_pool.shape),
            pl.BlockSpec(memory_space=pl.ANY),
        ],
        out_specs=pl.BlockSpec((1, tile, D), row),
        out_shape=jax.ShapeDtypeStruct((B, S, D), F32),
        scratch_shapes=[
            pltpu.VMEM((tile + 2 * POOL_HALO, POOL_W), F32),
            pltpu.VMEM((D, 2 * D), BF16),
            pltpu.VMEM(w_ap.shape, BF16),
            pltpu.VMEM(w_out.shape, BF16),
            pltpu.VMEM((STAGE_SLOTS, D, STAGE_CHUNK), F32),
            pltpu.VMEM((STAGE_SLOTS, STAGE_CHUNK, D), F32),
            pltpu.SemaphoreType.DMA((STAGE_SLOTS,)),
            pltpu.SemaphoreType.DMA((STAGE_SLOTS,)),
        ],
        compiler_params=_sequential_params(2),
        name="merge",
    )(x, mod, norm1_g, attn, u, u, u, w_all, gate_b, w_ap, w_pool, w_out)


def _ffn_kernel(x_ref, mod_ref, g2_ref, wup_hbm, wdn_hbm, o_ref,
                wup_ref, wdn_ref, col_stage, row_stage, col_sem, row_sem, hmid_ref, *, d_ff, tile, sub):
    @pl.when((pl.program_id(0) == 0) & (pl.program_id(1) == 0))
    def _():
        _stage_weight_bf16(wup_hbm, wup_ref, col_stage, col_sem, axis=1)
        _stage_weight_bf16(wdn_hbm, wdn_ref, row_stage, row_sem, axis=0)

    m = mod_ref[pl.ds(pl.program_id(0), 1), :]
    sh = m[:, 3 * D_MODEL:4 * D_MODEL]
    gs = g2_ref[...] * (1.0 + m[:, 4 * D_MODEL:5 * D_MODEL])
    gate2 = m[:, 5 * D_MODEL:6 * D_MODEL]

    def up(r):
        rows = slice(r * sub, (r + 1) * sub)
        h = _rms_modulate(x_ref[0, rows, :], gs, sh).astype(BF16)
        for c in range(d_ff // MXU_TILE):
            a = _dot(h, wup_ref[:, c * MXU_TILE:(c + 1) * MXU_TILE])
            b = _dot(h, wup_ref[:, d_ff + c * MXU_TILE:d_ff + (c + 1) * MXU_TILE])
            hmid_ref[rows, c * MXU_TILE:(c + 1) * MXU_TILE] = (a * jax.nn.sigmoid(a) * b).astype(BF16)

    def down(r):
        rows = slice(r * sub, (r + 1) * sub)
        o_ref[0, rows, :] = x_ref[0, rows, :] + gate2 * _dot(hmid_ref[rows, :], wdn_ref[...])

    n_sub = tile // sub
    up(0)
    for r in range(1, n_sub):
        up(r)
        down(r - 1)
    down(n_sub - 1)


def _ffn_call(x, mod, norm2_g, w_up, w_down, tile):
    B, S, D = x.shape
    d_ff = w_down.shape[0]
    row = lambda b, t: (b, t, 0)
    return pl.pallas_call(
        functools.partial(_ffn_kernel, d_ff=d_ff, tile=tile, sub=MXU_TILE),
        grid=(B, S // tile),
        in_specs=[
            pl.BlockSpec((1, tile, D), row),
            _const_spec(mod.shape),
            _const_spec((1, D)),
            pl.BlockSpec(memory_space=pl.ANY),
            pl.BlockSpec(memory_space=pl.ANY),
        ],
        out_specs=pl.BlockSpec((1, tile, D), row),
        out_shape=jax.ShapeDtypeStruct((B, S, D), F32),
        scratch_shapes=[
            pltpu.VMEM(w_up.shape, BF16),
            pltpu.VMEM(w_down.shape, BF16),
            pltpu.VMEM((STAGE_SLOTS, D, STAGE_CHUNK), F32),
            pltpu.VMEM((STAGE_SLOTS, STAGE_CHUNK, D), F32),
            pltpu.SemaphoreType.DMA((STAGE_SLOTS,)),
            pltpu.SemaphoreType.DMA((STAGE_SLOTS,)),
            pltpu.VMEM((tile, d_ff), BF16),
        ],
        compiler_params=_sequential_params(2),
        name="ffn",
    )(x, mod, norm2_g, w_up, w_down)


def _rope_tables(seq):
    t = np.arange(seq)
    row = (t // GRID_W).astype(np.float64)
    col = (t % GRID_W).astype(np.float64)
    half = HEAD_DIM // 2
    inv_freq = 1.0 / (ROPE_THETA ** (np.arange(0, half, 2, dtype=np.float64) / half))
    ang_r = row[:, None] * inv_freq
    ang_c = col[:, None] * inv_freq
    cos = np.concatenate([np.cos(ang_r)] * 2 + [np.cos(ang_c)] * 2, axis=1)
    sin = np.concatenate([-np.sin(ang_r), np.sin(ang_r), -np.sin(ang_c), np.sin(ang_c)], axis=1)
    reps = (1, LANES // HEAD_DIM)
    return jnp.asarray(np.tile(cos, reps), dtype=F32), jnp.asarray(np.tile(sin, reps), dtype=F32)


def _band_bias():
    i = np.arange(BLOCK)[:, None]
    j = np.arange(3 * BLOCK)[None, :]
    band = np.abs(j - BLOCK - i) <= WINDOW
    first = band & (j >= BLOCK)
    last = band & (j < 2 * BLOCK)
    masks = np.stack([band, first, last])
    return jnp.asarray(np.where(masks, 0.0, NEG_BIG), dtype=F32)


def kernel(x, c, ctx, c_ctx, mod_w, mod_b, norm1_g, norm2_g, w_in, gate_b, q_norm_g, k_norm_g,
           sink, pool_w, pool_scale, w_attn_proj, w_pool_proj, w_out, w_up, w_down):
    B, S, D = x.shape
    depth = mod_w.shape[0]
    cos_t, sin_t = _rope_tables(S)
    bias = _band_bias()
    ones_bd = jnp.asarray(np.kron(np.eye(MXU_TILE // HEAD_DIM), np.ones((HEAD_DIM, HEAD_DIM))), dtype=BF16)
    reps = MXU_TILE // HEAD_DIM
    assert depth == 1, "context-stream update between layers is not implemented"

    for l in range(depth):
        c8 = jnp.concatenate([c, c_ctx[None, :], jnp.zeros((8 - B - 1, D), F32)], axis=0)
        mod = _mod_call(c8, mod_w[l], mod_b[l][None, :])

        wl = w_in[l]
        w_q = (wl[:, :Q_W].reshape(D, N_KV_HEADS, GROUP, HEAD_DIM).transpose(0, 2, 1, 3)
               .reshape(D, Q_W).astype(BF16))
        g1 = norm1_g[l][None, :]
        kg256 = jnp.tile(k_norm_g[l], reps)[None, :]
        qg256 = jnp.tile(q_norm_g[l] * (HEAD_DIM ** -0.5 * LOG2_E), reps)[None, :]

        kc, vc = _ctx_call(ctx, mod, g1, wl, kg256, ones_bd)
        q, k, v, u = _inproj_call(x, mod, g1, w_q, wl, qg256, kg256, ones_bd, cos_t, sin_t, tile=1024)
        attn = _attn_call(sink[l], q, k, v, kc, vc, bias, tile=1024)
        w_pool = _pool_fold_call(pool_w[l], pool_scale[l].reshape(POOL_GROUPS, 1, POOL_GROUP_W), w_pool_proj[l])
        x = _merge_call(
            x, mod, g1, attn, u, wl, gate_b[l][None, :], w_attn_proj[l], w_pool, w_out[l], tile=1024)
        x = _ffn_call(x, mod, norm2_g[l][None, :], w_up[l], w_down[l], tile=1024)
    return x
```

```python
import functools

import jax
import jax.numpy as jnp
import numpy as np
from jax import lax
from jax.experimental import pallas as pl
from jax.experimental.pallas import tpu as pltpu

D_MODEL = 1024
GRID_W = 64
HEAD_DIM = 64
N_HEADS = 16
N_KV_HEADS = 4
GROUP = N_HEADS // N_KV_HEADS
Q_W = N_HEADS * HEAD_DIM
KV_W = N_KV_HEADS * HEAD_DIM
WINDOW = 128
BLOCK = 128
ROPE_THETA = 10000.0
POOL_WINDOWS = (2, 4, 8, 16)
POOL_GROUPS = 4
POOL_W = D_MODEL // 2
POOL_GROUP_W = POOL_W // POOL_GROUPS
POOL_HALO = 8
ROPE_PAIR = HEAD_DIM // 4
EPS = 1e-6
NEG_BIG = -1e30
LOG2_E = 1.4426950408889634

LANES = 128
F32_SUBLANES = 8
MXU_TILE = 256
VMEM_LIMIT = 56 * 1024 * 1024
STAGE_SLOTS = 4
STAGE_CHUNK = MXU_TILE
ROW_TILE = 1024
SUB_TILE = MXU_TILE

F32 = jnp.float32
BF16 = jnp.bfloat16
FP8 = jnp.float8_e4m3fn


def _params(n_parallel):
    return pltpu.CompilerParams(
        dimension_semantics=("parallel",) * n_parallel,
        vmem_limit_bytes=VMEM_LIMIT,
    )


def _const_spec(shape):
    nd = len(shape)
    return pl.BlockSpec(shape, lambda *_: (0,) * nd, pipeline_mode=pl.Buffered(1))


def _dot(a, b):
    return jnp.dot(a, b, preferred_element_type=F32)


def _dot_nt(a, b):
    return lax.dot_general(a, b, (((1,), (1,)), ((), ())), preferred_element_type=F32)


def _rms_modulate(x, gs, sh):
    rs = lax.rsqrt(jnp.mean(x * x, axis=-1, keepdims=True) + EPS)
    return x * rs * gs + sh


def _head_rms(t, ones_bd):
    ms = _dot((t * t).astype(BF16), ones_bd) * (1.0 / HEAD_DIM)
    return lax.rsqrt(ms + EPS)


def _stage_weight_bf16(w_hbm, w_ref, stage_ref, sem, axis, store=None):
    slots = stage_ref.shape[0]
    chunk = stage_ref.shape[1 + axis]
    n = w_hbm.shape[axis] // chunk

    def piece(ref, i):
        return ref.at[:, pl.ds(i * chunk, chunk)] if axis == 1 else ref.at[pl.ds(i * chunk, chunk), :]

    def copy(i):
        return pltpu.make_async_copy(piece(w_hbm, i), stage_ref.at[i % slots], sem.at[i % slots])

    for i in range(min(slots - 1, n)):
        copy(i).start()
    for i in range(n):
        if i + slots - 1 < n:
            copy(i + slots - 1).start()
        copy(i).wait()
        staged = stage_ref[i % slots].astype(BF16)
        if store is None:
            piece(w_ref, i)[...] = staged
        else:
            store(i, staged)


def _sequential_params(n_axes):
    return pltpu.CompilerParams(dimension_semantics=("arbitrary",) * n_axes, vmem_limit_bytes=VMEM_LIMIT)


def _mod_kernel(c_ref, w_ref, b_ref, o_ref):
    c = c_ref[...]
    a = (c * jax.nn.sigmoid(c)).astype(BF16)
    o_ref[...] = _dot(a, w_ref[...].astype(BF16)) + b_ref[...]


def _mod_call(c8, mod_w, mod_b):
    n = mod_w.shape[1]
    bn = 2 * D_MODEL
    return pl.pallas_call(
        _mod_kernel,
        grid=(n // bn,),
        in_specs=[
            pl.BlockSpec((F32_SUBLANES, D_MODEL), lambda j: (0, 0)),
            pl.BlockSpec((D_MODEL, bn), lambda j: (0, j)),
            pl.BlockSpec((1, bn), lambda j: (0, j)),
        ],
        out_specs=pl.BlockSpec((F32_SUBLANES, bn), lambda j: (0, j)),
        out_shape=jax.ShapeDtypeStruct((F32_SUBLANES, n), F32),
        compiler_params=_params(1),
        name="mod",
    )(c8, mod_w, mod_b)


def _pool_fold_kernel(pw_ref, ps_ref, wpp_ref, o_ref):
    scaled = (pw_ref[0] * ps_ref[0]).astype(BF16)
    o_ref[...] = _dot(scaled, wpp_ref[...].astype(BF16)).astype(BF16)


def _pool_fold_call(pool_w, pool_scale3, w_pool_proj):
    n_out = w_pool_proj.shape[1]
    return pl.pallas_call(
        _pool_fold_kernel,
        grid=(POOL_GROUPS,),
        in_specs=[
            pl.BlockSpec((1, POOL_GROUP_W, POOL_GROUP_W), lambda g: (g, 0, 0)),
            pl.BlockSpec((1, 1, POOL_GROUP_W), lambda g: (g, 0, 0)),
            pl.BlockSpec((POOL_GROUP_W, n_out), lambda g: (g, 0)),
        ],
        out_specs=pl.BlockSpec((POOL_GROUP_W, n_out), lambda g: (g, 0)),
        out_shape=jax.ShapeDtypeStruct((POOL_W, n_out), BF16),
        compiler_params=_params(1),
        name="pool_fold",
    )(pool_w, pool_scale3, w_pool_proj)


def _ctx_kernel(ctx_ref, mod_ref, g1_ref, wall_hbm, kg_ref, ones_ref, kc_ref, vc_ref,
                w_ref, col_stage, col_sem, *, n_batch):
    @pl.when(pl.program_id(0) == 0)
    def _():
        _stage_weight_bf16(wall_hbm.at[:, pl.ds(Q_W, 2 * KV_W)], w_ref, col_stage, col_sem, axis=1)

    m = mod_ref[n_batch:n_batch + 1, :]
    sh = m[:, 0:D_MODEL]
    gs = g1_ref[...] * (1.0 + m[:, D_MODEL:2 * D_MODEL])
    h = _rms_modulate(ctx_ref[0], gs, sh).astype(BF16)
    kv = _dot(h, w_ref[...])
    k = kv[:, :KV_W]
    kc_ref[0] = (k * _head_rms(k, ones_ref[...]) * kg_ref[...]).astype(BF16)
    vc_ref[0] = kv[:, KV_W:].astype(BF16)


def _ctx_call(ctx, mod, norm1_g, w_all, kg256, ones_bd):
    B, C, D = ctx.shape
    rows = B * C
    kc, vc = pl.pallas_call(
        functools.partial(_ctx_kernel, n_batch=B),
        grid=(1,),
        in_specs=[
            pl.BlockSpec((1, rows, D), lambda i: (0, 0, 0)),
            _const_spec(mod.shape),
            _const_spec((1, D)),
            pl.BlockSpec(memory_space=pl.ANY),
            _const_spec((1, KV_W)),
            _const_spec((MXU_TILE, MXU_TILE)),
        ],
        out_specs=[
            pl.BlockSpec((1, rows, KV_W), lambda i: (0, 0, 0)),
            pl.BlockSpec((1, rows, KV_W), lambda i: (0, 0, 0)),
        ],
        out_shape=[
            jax.ShapeDtypeStruct((1, rows, KV_W), BF16),
            jax.ShapeDtypeStruct((1, rows, KV_W), BF16),
        ],
        scratch_shapes=[
            pltpu.VMEM((D, 2 * KV_W), BF16),
            pltpu.VMEM((STAGE_SLOTS, D, STAGE_CHUNK), F32),
            pltpu.SemaphoreType.DMA((STAGE_SLOTS,)),
        ],
        compiler_params=_sequential_params(1),
        name="ctx_kv",
    )(ctx.reshape(1, rows, D), mod, norm1_g, w_all, kg256, ones_bd)
    return kc.reshape(B, C, KV_W), vc.reshape(B, C, KV_W)


def _rope(t, cos, sin_signed, first_half):
    swap = jnp.where(first_half, pltpu.roll(t, LANES - ROPE_PAIR, axis=1), pltpu.roll(t, ROPE_PAIR, axis=1))
    return t * cos + swap * sin_signed


def _inproj_kernel(x_ref, mod_ref, g1_ref, wq_ref, wall_hbm, qg_ref, kg_ref, ones_ref, cos_ref, sin_ref,
                   q_ref, k_ref, v_ref, u_ref, wkvu_ref, col_stage, col_sem):
    @pl.when((pl.program_id(0) == 0) & (pl.program_id(1) == 0))
    def _():
        _stage_weight_bf16(wall_hbm.at[:, pl.ds(Q_W, Q_W)], wkvu_ref, col_stage, col_sem, axis=1)

    m = mod_ref[pl.ds(pl.program_id(0), 1), :]
    sh = m[:, 0:D_MODEL]
    gs = g1_ref[...] * (1.0 + m[:, D_MODEL:2 * D_MODEL])
    h = _rms_modulate(x_ref[0], gs, sh).astype(BF16)
    ones_bd = ones_ref[...]
    cos = cos_ref[...]
    sin = sin_ref[...]
    lane = lax.broadcasted_iota(jnp.int32, cos.shape, 1)
    first_half = (lane & (2 * ROPE_PAIR - 1)) < ROPE_PAIR

    def normed_rope(t, gain):
        t = t * _head_rms(t, ones_bd) * gain
        halves = [_rope(t[:, i * LANES:(i + 1) * LANES], cos, sin, first_half) for i in range(2)]
        return jnp.concatenate(halves, axis=1).astype(BF16)

    wide = 2 * MXU_TILE
    for j in range(Q_W // wide):
        t = _dot(h, wq_ref[:, j * wide:(j + 1) * wide])
        for i in range(2):
            c0 = j * wide + i * MXU_TILE
            q_ref[0, :, c0:c0 + MXU_TILE] = normed_rope(t[:, i * MXU_TILE:(i + 1) * MXU_TILE], qg_ref[...])
    kv = _dot(h, wkvu_ref[:, :2 * KV_W])
    k_ref[0] = normed_rope(kv[:, :KV_W], kg_ref[...])
    v_ref[0] = kv[:, KV_W:].astype(BF16)
    u_ref[0] = _dot(h, wkvu_ref[:, 2 * KV_W:])


def _inproj_call(x, mod, norm1_g, w_q, w_all, qg256, kg256, ones_bd, cos_t, sin_t, tile):
    B, S, D = x.shape
    row = lambda b, t: (b, t, 0)
    return pl.pallas_call(
        _inproj_kernel,
        grid=(B, S // tile),
        in_specs=[
            pl.BlockSpec((1, tile, D), row),
            _const_spec(mod.shape),
            _const_spec((1, D)),
            _const_spec((D, Q_W)),
            pl.BlockSpec(memory_space=pl.ANY),
            _const_spec((1, MXU_TILE)),
            _const_spec((1, MXU_TILE)),
            _const_spec((MXU_TILE, MXU_TILE)),
            pl.BlockSpec((tile, LANES), lambda b, t: (t, 0)),
            pl.BlockSpec((tile, LANES), lambda b, t: (t, 0)),
        ],
        out_specs=[
            pl.BlockSpec((1, tile, Q_W), row),
            pl.BlockSpec((1, tile, KV_W), row),
            pl.BlockSpec((1, tile, KV_W), row),
            pl.BlockSpec((1, tile, POOL_W), row),
        ],
        out_shape=[
            jax.ShapeDtypeStruct((B, S, Q_W), BF16),
            jax.ShapeDtypeStruct((B, S, KV_W), BF16),
            jax.ShapeDtypeStruct((B, S, KV_W), BF16),
            jax.ShapeDtypeStruct((B, S, POOL_W), F32),
        ],
        scratch_shapes=[
            pltpu.VMEM((D, Q_W), BF16),
            pltpu.VMEM((STAGE_SLOTS, D, STAGE_CHUNK), F32),
            pltpu.SemaphoreType.DMA((STAGE_SLOTS,)),
        ],
        compiler_params=_sequential_params(2),
        name="in_proj",
    )(x, mod, norm1_g, w_q, w_all, qg256, kg256, ones_bd, cos_t, sin_t)


def _attn_kernel(sink_ref, q_ref, kp_ref, kc0_ref, kn_ref, vp_ref, vc0_ref, vn_ref,
                 kctx_ref, vctx_ref, bias_ref, o_ref, *, blocks_per_tile):
    t_idx = pl.program_id(1)
    n_tiles = pl.num_programs(1)
    k_all = jnp.concatenate([kp_ref[0], kc0_ref[0], kn_ref[0]], axis=0)
    v_all = jnp.concatenate([vp_ref[0], vc0_ref[0], vn_ref[0]], axis=0)
    k_ctx = kctx_ref[0]
    v_ctx = vctx_ref[0]
    lane = lax.broadcasted_iota(jnp.int32, (BLOCK, MXU_TILE), 1)
    in_head = [(lane >= kv * HEAD_DIM) & (lane < (kv + 1) * HEAD_DIM) for kv in range(N_KV_HEADS)]
    zero = jnp.zeros((), BF16)
    one = jnp.ones((), BF16)

    def with_ones(v, half):
        vl = lax.broadcasted_iota(jnp.int32, v.shape, 1)
        return jnp.where((vl < LANES) if half == 0 else (vl >= LANES), v, one).astype(FP8)

    v_ctx_pair = [with_ones(v_ctx, half) for half in range(2)]
    low_head = lax.broadcasted_iota(jnp.int32, (BLOCK, LANES), 1) < HEAD_DIM

    def block_operands(i):
        variant = jnp.int32(0)
        if i == blocks_per_tile - 1:
            variant = jnp.where(t_idx == n_tiles - 1, 2, variant)
        if i == 0:
            variant = jnp.where(t_idx == 0, 1, variant)
        v_win = v_all[i * BLOCK:(i + 3) * BLOCK]
        return dict(bias=bias_ref[variant], k_win=k_all[i * BLOCK:(i + 3) * BLOCK],
                    v_win_pair=[with_ones(v_win, half) for half in range(2)])

    def scores(blk, i, j):
        q_slab = q_ref[0, i * BLOCK:(i + 1) * BLOCK, j * MXU_TILE:(j + 1) * MXU_TILE]
        qz = jnp.concatenate([jnp.where(in_head[kv], q_slab, zero) for kv in range(N_KV_HEADS)], axis=0)
        return _dot_nt(qz, blk["k_win"]), _dot_nt(qz, k_ctx)

    def softmax_values(blk, i, j, s_win, s_ctx):
        p_win, p_ctx, sink_term = [], [], []
        for kv in range(N_KV_HEADS):
            rows = slice(kv * BLOCK, (kv + 1) * BLOCK)
            sw = s_win[rows] + blk["bias"]
            sc = s_ctx[rows]
            sink = sink_ref[kv * GROUP + j] * LOG2_E
            m = jnp.maximum(
                jnp.maximum(jnp.max(sw, axis=-1, keepdims=True),
                            jnp.max(sc, axis=-1, keepdims=True)), sink)
            p_win.append(jnp.exp2(sw - m).astype(FP8))
            p_ctx.append(jnp.exp2(sc - m).astype(FP8))
            sink_term.append(jnp.exp2(sink - m))
        for half in range(2):
            pw = jnp.concatenate(p_win[2 * half:2 * half + 2], axis=0)
            pc = jnp.concatenate(p_ctx[2 * half:2 * half + 2], axis=0)
            o = _dot(pw, blk["v_win_pair"][half]) + _dot(pc, v_ctx_pair[half])
            vals = o[:, half * LANES:(half + 1) * LANES]
            sums = o[:, (1 - half) * LANES:(2 - half) * LANES]
            out = []
            for r in range(2):
                rows = slice(r * BLOCK, (r + 1) * BLOCK)
                out.append(vals[rows] / (sums[rows] + sink_term[2 * half + r]))
            c0 = j * MXU_TILE + half * LANES
            o_ref[0, i * BLOCK:(i + 1) * BLOCK, c0:c0 + LANES] = jnp.where(low_head, out[0], out[1]).astype(BF16)

    blocks = [block_operands(i) for i in range(blocks_per_tile)]
    items = [(i, j) for i in range(blocks_per_tile) for j in range(GROUP)]
    s_next = scores(blocks[0], *items[0])
    for idx, (i, j) in enumerate(items):
        s_cur = s_next
        if idx + 1 < len(items):
            ni, nj = items[idx + 1]
            s_next = scores(blocks[ni], ni, nj)
        softmax_values(blocks[i], i, j, *s_cur)


def _attn_call(sink, q, k, v, kc, vc, bias, tile):
    B, S, _ = q.shape
    bpt = tile // BLOCK
    nb = S // BLOCK
    C = kc.shape[1]
    cur = lambda b, t, *_: (b, t, 0)
    prev = lambda b, t, *_: (b, jnp.maximum(t * bpt - 1, 0), 0)
    nxt = lambda b, t, *_: (b, jnp.minimum((t + 1) * bpt, nb - 1), 0)
    per_b = lambda b, t, *_: (b, 0, 0)
    halo = (1, BLOCK, KV_W)
    center = (1, tile, KV_W)
    grid_spec = pltpu.PrefetchScalarGridSpec(
        num_scalar_prefetch=1,
        grid=(B, S // tile),
        in_specs=[
            pl.BlockSpec((1, tile, Q_W), cur),
            pl.BlockSpec(halo, prev), pl.BlockSpec(center, cur), pl.BlockSpec(halo, nxt),
            pl.BlockSpec(halo, prev), pl.BlockSpec(center, cur), pl.BlockSpec(halo, nxt),
            pl.BlockSpec((1, C, KV_W), per_b),
            pl.BlockSpec((1, C, KV_W), per_b),
            pl.BlockSpec((3, BLOCK, 3 * BLOCK), lambda b, t, *_: (0, 0, 0)),
        ],
        out_specs=pl.BlockSpec((1, tile, Q_W), cur),
    )
    return pl.pallas_call(
        functools.partial(_attn_kernel, blocks_per_tile=bpt),
        grid_spec=grid_spec,
        out_shape=jax.ShapeDtypeStruct((B, S, Q_W), BF16),
        compiler_params=_params(2),
        name="attn",
    )(sink, q, k, k, k, v, v, v, kc, vc, bias)


def _merge_kernel(x_ref, mod_ref, g1_ref, attn_ref, up_ref, uc_ref, un_ref,
                  wall_hbm, gb_ref, wap_hbm, wpool_ref, wo_hbm, o_ref,
                  uext_ref, wg_ref, wap_ref, wo_ref, col_stage, row_stage, col_sem, row_sem, *, seq, tile, sub):
    @pl.when((pl.program_id(0) == 0) & (pl.program_id(1) == 0))
    def _():
        _stage_weight_bf16(wall_hbm.at[:, pl.ds(2 * D_MODEL, 2 * D_MODEL)], wg_ref, col_stage, col_sem, axis=1)
        _stage_weight_bf16(wo_hbm, wo_ref, col_stage, col_sem, axis=1)

        def store_regrouped(kv, rows_bf16):
            for j in range(GROUP):
                dst = (j * N_KV_HEADS + kv) * HEAD_DIM
                wap_ref[dst:dst + HEAD_DIM, :] = rows_bf16[j * HEAD_DIM:(j + 1) * HEAD_DIM, :]

        _stage_weight_bf16(wap_hbm, None, row_stage, row_sem, axis=0, store=store_regrouped)

    t_idx = pl.program_id(1)
    n_tiles = pl.num_programs(1)
    m = mod_ref[pl.ds(pl.program_id(0), 1), :]
    sh = m[:, 0:D_MODEL]
    gs = g1_ref[...] * (1.0 + m[:, D_MODEL:2 * D_MODEL])
    gate1 = m[:, 2 * D_MODEL:3 * D_MODEL]

    keep_prev = (t_idx > 0).astype(F32)
    keep_next = (t_idx < n_tiles - 1).astype(F32)
    uext_ref[0:POOL_HALO] = up_ref[0] * keep_prev
    uext_ref[POOL_HALO:POOL_HALO + tile] = uc_ref[0]
    uext_ref[POOL_HALO + tile:] = un_ref[0] * keep_next
    edge_rows = lax.broadcasted_iota(jnp.int32, (POOL_HALO, POOL_GROUP_W), 0)

    def attn_dot(r):
        return _dot(attn_ref[0, r * sub:(r + 1) * sub, :], wap_ref[...])

    def gate_dot(r):
        h = _rms_modulate(x_ref[0, r * sub:(r + 1) * sub, :], gs, sh).astype(BF16)
        return jax.nn.sigmoid(_dot(h, wg_ref[...]) + gb_ref[...])

    def pool_dot(r):
        r0 = r * sub
        pos_first = t_idx * tile + r0 + edge_rows
        pos_last = pos_first + (sub - POOL_HALO)
        diffs = []
        for g, w in enumerate(POOL_WINDOWS):
            cols = slice(g * POOL_GROUP_W, (g + 1) * POOL_GROUP_W)
            acc = uext_ref[pl.ds(r0, sub + 2 * POOL_HALO), cols]
            k = 1
            while k < w:
                acc = acc[:acc.shape[0] - k] + acc[k:]
                k *= 2
            win = acc[POOL_HALO - w // 2:POOL_HALO - w // 2 + sub]

            def inv_count(pos):
                return 1.0 / (jnp.minimum(pos + w // 2, seq) - jnp.maximum(pos - w // 2, 0)).astype(F32)

            pooled_avg = jnp.concatenate([
                win[:POOL_HALO] * inv_count(pos_first),
                win[POOL_HALO:sub - POOL_HALO] * (1.0 / w),
                win[sub - POOL_HALO:] * inv_count(pos_last)], axis=0)
            diffs.append((pooled_avg - uext_ref[pl.ds(r0 + POOL_HALO, sub), cols]).astype(BF16))
        return _dot(jnp.concatenate(diffs, axis=1), wpool_ref[...])

    def output(r, a, gates, p):
        rows = slice(r * sub, (r + 1) * sub)
        merged = (gates[:, :D_MODEL] * a + gates[:, D_MODEL:] * p).astype(BF16)
        o_ref[0, rows, :] = x_ref[0, rows, :] + gate1 * _dot(merged, wo_ref[...])

    n_sub = tile // sub
    a = [attn_dot(r) for r in range(n_sub)]
    p = [pool_dot(r) for r in range(n_sub)]
    gates = [gate_dot(r) for r in range(n_sub)]
    for r in range(n_sub):
        output(r, a[r], gates[r], p[r])


def _merge_call(x, mod, norm1_g, attn, u, w_all, gate_b, w_ap, w_pool, w_out, tile):
    B, S, D = x.shape
    r = tile // POOL_HALO
    n_halo = S // POOL_HALO
    row = lambda b, t: (b, t, 0)
    return pl.pallas_call(
        functools.partial(_merge_kernel, seq=S, tile=tile, sub=SUB_TILE),
        grid=(B, S // tile),
        in_specs=[
            pl.BlockSpec((1, tile, D), row),
            _const_spec(mod.shape),
            _const_spec((1, D)),
            pl.BlockSpec((1, tile, Q_W), row),
            pl.BlockSpec((1, POOL_HALO, POOL_W), lambda b, t: (b, jnp.maximum(t * r - 1, 0), 0)),
            pl.BlockSpec((1, tile, POOL_W), row),
            pl.BlockSpec((1, POOL_HALO, POOL_W), lambda b, t: (b, jnp.minimum((t + 1) * r, n_halo - 1), 0)),
            pl.BlockSpec(memory_space=pl.ANY),
            _const_spec(gate_b.shape),
            pl.BlockSpec(memory_space=pl.ANY),
            _const_spec(w_pool.shape),
            pl.BlockSpec(memory_space=pl.ANY),
        ],
        out_specs=pl.BlockSpec((1, tile, D), row),
        out_shape=jax.ShapeDtypeStruct((B, S, D), F32),
        scratch_shapes=[
            pltpu.VMEM((tile + 2 * POOL_HALO, POOL_W), F32),
            pltpu.VMEM((D, 2 * D), BF16),
            pltpu.VMEM(w_ap.shape, BF16),
            pltpu.VMEM(w_out.shape, BF16),
            pltpu.VMEM((STAGE_SLOTS, D, STAGE_CHUNK), F32),
            pltpu.VMEM((STAGE_SLOTS, STAGE_CHUNK, D), F32),
            pltpu.SemaphoreType.DMA((STAGE_SLOTS,)),
            pltpu.SemaphoreType.DMA((STAGE_SLOTS,)),
        ],
        compiler_params=_sequential_params(2),
        name="merge",
    )(x, mod, norm1_g, attn, u, u, u, w_all, gate_b, w_ap, w_pool, w_out)


def _ffn_kernel(x_ref, mod_ref, g2_ref, wup_hbm, wdn_hbm, o_ref,
                wup_ref, wdn_ref, col_stage, row_stage, col_sem, row_sem, hmid_ref, *, d_ff, tile, sub):
    @pl.when((pl.program_id(0) == 0) & (pl.program_id(1) == 0))
    def _():
        _stage_weight_bf16(wup_hbm, wup_ref, col_stage, col_sem, axis=1)
        _stage_weight_bf16(wdn_hbm, wdn_ref, row_stage, row_sem, axis=0)

    m = mod_ref[pl.ds(pl.program_id(0), 1), :]
    sh = m[:, 3 * D_MODEL:4 * D_MODEL]
    gs = g2_ref[...] * (1.0 + m[:, 4 * D_MODEL:5 * D_MODEL])
    gate2 = m[:, 5 * D_MODEL:6 * D_MODEL]

    def up(r):
        rows = slice(r * sub, (r + 1) * sub)
        h = _rms_modulate(x_ref[0, rows, :], gs, sh).astype(BF16)
        for c in range(d_ff // MXU_TILE):
            a = _dot(h, wup_ref[:, c * MXU_TILE:(c + 1) * MXU_TILE])
            b = _dot(h, wup_ref[:, d_ff + c * MXU_TILE:d_ff + (c + 1) * MXU_TILE])
            hmid_ref[rows, c * MXU_TILE:(c + 1) * MXU_TILE] = (a * jax.nn.sigmoid(a) * b).astype(BF16)

    def down(r):
        rows = slice(r * sub, (r + 1) * sub)
        o_ref[0, rows, :] = x_ref[0, rows, :] + gate2 * _dot(hmid_ref[rows, :], wdn_ref[...])

    n_sub = tile // sub
    up(0)
    for r in range(1, n_sub):
        up(r)
        down(r - 1)
    down(n_sub - 1)


def _ffn_call(x, mod, norm2_g, w_up, w_down, tile):
    B, S, D = x.shape
    d_ff = w_down.shape[0]
    row = lambda b, t: (b, t, 0)
    return pl.pallas_call(
        functools.partial(_ffn_kernel, d_ff=d_ff, tile=tile, sub=SUB_TILE),
        grid=(B, S // tile),
        in_specs=[
            pl.BlockSpec((1, tile, D), row),
            _const_spec(mod.shape),
            _const_spec((1, D)),
            pl.BlockSpec(memory_space=pl.ANY),
            pl.BlockSpec(memory_space=pl.ANY),
        ],
        out_specs=pl.BlockSpec((1, tile, D), row),
        out_shape=jax.ShapeDtypeStruct((B, S, D), F32),
        scratch_shapes=[
            pltpu.VMEM(w_up.shape, BF16),
            pltpu.VMEM(w_down.shape, BF16),
            pltpu.VMEM((STAGE_SLOTS, D, STAGE_CHUNK), F32),
            pltpu.VMEM((STAGE_SLOTS, STAGE_CHUNK, D), F32),
            pltpu.SemaphoreType.DMA((STAGE_SLOTS,)),
            pltpu.SemaphoreType.DMA((STAGE_SLOTS,)),
            pltpu.VMEM((tile, d_ff), BF16),
        ],
        compiler_params=_sequential_params(2),
        name="ffn",
    )(x, mod, norm2_g, w_up, w_down)


def _rope_tables(seq):
    t = np.arange(seq)
    row = (t // GRID_W).astype(np.float64)
    col = (t % GRID_W).astype(np.float64)
    half = HEAD_DIM // 2
    inv_freq = 1.0 / (ROPE_THETA ** (np.arange(0, half, 2, dtype=np.float64) / half))
    ang_r = row[:, None] * inv_freq
    ang_c = col[:, None] * inv_freq
    cos = np.concatenate([np.cos(ang_r)] * 2 + [np.cos(ang_c)] * 2, axis=1)
    sin = np.concatenate([-np.sin(ang_r), np.sin(ang_r), -np.sin(ang_c), np.sin(ang_c)], axis=1)
    reps = (1, LANES // HEAD_DIM)
    return jnp.asarray(np.tile(cos, reps), dtype=F32), jnp.asarray(np.tile(sin, reps), dtype=F32)


def _band_bias():
    i = np.arange(BLOCK)[:, None]
    j = np.arange(3 * BLOCK)[None, :]
    band = np.abs(j - BLOCK - i) <= WINDOW
    first = band & (j >= BLOCK)
    last = band & (j < 2 * BLOCK)
    masks = np.stack([band, first, last])
    return jnp.asarray(np.where(masks, 0.0, NEG_BIG), dtype=F32)


def kernel(x, c, ctx, c_ctx, mod_w, mod_b, norm1_g, norm2_g, w_in, gate_b, q_norm_g, k_norm_g,
           sink, pool_w, pool_scale, w_attn_proj, w_pool_proj, w_out, w_up, w_down):
    B, S, D = x.shape
    depth = mod_w.shape[0]
    cos_t, sin_t = _rope_tables(S)
    bias = _band_bias()
    ones_bd = jnp.asarray(np.kron(np.eye(MXU_TILE // HEAD_DIM), np.ones((HEAD_DIM, HEAD_DIM))), dtype=BF16)
    reps = MXU_TILE // HEAD_DIM
    assert depth == 1, "context-stream update between layers is not implemented"
    assert D == D_MODEL and S % ROW_TILE == 0 and B + 1 <= F32_SUBLANES

    for l in range(depth):
        c8 = jnp.concatenate([c, c_ctx[None, :], jnp.zeros((F32_SUBLANES - B - 1, D), F32)], axis=0)
        mod = _mod_call(c8, mod_w[l], mod_b[l][None, :])

        wl = w_in[l]
        w_q = (wl[:, :Q_W].reshape(D, N_KV_HEADS, GROUP, HEAD_DIM).transpose(0, 2, 1, 3)
               .reshape(D, Q_W).astype(BF16))
        g1 = norm1_g[l][None, :]
        kg256 = jnp.tile(k_norm_g[l], reps)[None, :]
        qg256 = jnp.tile(q_norm_g[l] * (HEAD_DIM ** -0.5 * LOG2_E), reps)[None, :]

        kc, vc = _ctx_call(ctx, mod, g1, wl, kg256, ones_bd)
        q, k, v, u = _inproj_call(x, mod, g1, w_q, wl, qg256, kg256, ones_bd, cos_t, sin_t, tile=ROW_TILE)
        attn = _attn_call(sink[l], q, k, v, kc, vc, bias, tile=ROW_TILE)
        w_pool = _pool_fold_call(pool_w[l], pool_scale[l].reshape(POOL_GROUPS, 1, POOL_GROUP_W), w_pool_proj[l])
        x = _merge_call(
            x, mod, g1, attn, u, wl, gate_b[l][None, :], w_attn_proj[l], w_pool, w_out[l], tile=ROW_TILE)
        x = _ffn_call(x, mod, norm2_g[l][None, :], w_up[l], w_down[l], tile=ROW_TILE)
    return x
```

```python
import functools

import jax
import jax.numpy as jnp
import numpy as np
from jax import lax
from jax.experimental import pallas as pl
from jax.experimental.pallas import tpu as pltpu

D_MODEL = 1024
GRID_W = 64
HEAD_DIM = 64
N_HEADS = 16
N_KV_HEADS = 4
GROUP = N_HEADS // N_KV_HEADS
Q_W = N_HEADS * HEAD_DIM
KV_W = N_KV_HEADS * HEAD_DIM
WINDOW = 128
BLOCK = 128
ROPE_THETA = 10000.0
POOL_WINDOWS = (2, 4, 8, 16)
POOL_GROUPS = 4
POOL_W = D_MODEL // 2
POOL_GROUP_W = POOL_W // POOL_GROUPS
POOL_HALO = 8
ROPE_PAIR = HEAD_DIM // 4
EPS = 1e-6
NEG_BIG = -1e30
LOG2_E = 1.4426950408889634

LANES = 128
F32_SUBLANES = 8
MXU_TILE = 256
VMEM_LIMIT = 56 * 1024 * 1024
STAGE_SLOTS = 4
STAGE_CHUNK = MXU_TILE
ROW_TILE = 1024
SUB_TILE = MXU_TILE

F32 = jnp.float32
BF16 = jnp.bfloat16
FP8 = jnp.float8_e4m3fn
FP8_MAX = 448.0


def _params(n_parallel):
    return pltpu.CompilerParams(
        dimension_semantics=("parallel",) * n_parallel,
        vmem_limit_bytes=VMEM_LIMIT,
    )


def _const_spec(shape):
    nd = len(shape)
    return pl.BlockSpec(shape, lambda *_: (0,) * nd, pipeline_mode=pl.Buffered(1))


def _dot(a, b):
    return jnp.dot(a, b, preferred_element_type=F32)


def _dot_nt(a, b):
    return lax.dot_general(a, b, (((1,), (1,)), ((), ())), preferred_element_type=F32)


def _rms_modulate(x, gs, sh):
    rs = lax.rsqrt(jnp.mean(x * x, axis=-1, keepdims=True) + EPS)
    return x * rs * gs + sh


def _head_rms(t, ones_bd):
    ms = _dot((t * t).astype(BF16), ones_bd) * (1.0 / HEAD_DIM)
    return lax.rsqrt(ms + EPS)


def _stage_weight_bf16(w_hbm, w_ref, stage_ref, sem, axis, store=None):
    slots = stage_ref.shape[0]
    chunk = stage_ref.shape[1 + axis]
    n = w_hbm.shape[axis] // chunk

    def piece(ref, i):
        return ref.at[:, pl.ds(i * chunk, chunk)] if axis == 1 else ref.at[pl.ds(i * chunk, chunk), :]

    def copy(i):
        return pltpu.make_async_copy(piece(w_hbm, i), stage_ref.at[i % slots], sem.at[i % slots])

    for i in range(min(slots - 1, n)):
        copy(i).start()
    for i in range(n):
        if i + slots - 1 < n:
            copy(i + slots - 1).start()
        copy(i).wait()
        staged = stage_ref[i % slots].astype(BF16)
        if store is None:
            piece(w_ref, i)[...] = staged
        else:
            store(i, staged)


def _sequential_params(n_axes):
    return pltpu.CompilerParams(dimension_semantics=("arbitrary",) * n_axes, vmem_limit_bytes=VMEM_LIMIT)


def _mod_kernel(c_ref, w_ref, b_ref, o_ref):
    c = c_ref[...]
    a = (c * jax.nn.sigmoid(c)).astype(BF16)
    o_ref[...] = _dot(a, w_ref[...].astype(BF16)) + b_ref[...]


def _mod_call(c8, mod_w, mod_b):
    n = mod_w.shape[1]
    bn = 2 * D_MODEL
    return pl.pallas_call(
        _mod_kernel,
        grid=(n // bn,),
        in_specs=[
            pl.BlockSpec((F32_SUBLANES, D_MODEL), lambda j: (0, 0)),
            pl.BlockSpec((D_MODEL, bn), lambda j: (0, j)),
            pl.BlockSpec((1, bn), lambda j: (0, j)),
        ],
        out_specs=pl.BlockSpec((F32_SUBLANES, bn), lambda j: (0, j)),
        out_shape=jax.ShapeDtypeStruct((F32_SUBLANES, n), F32),
        compiler_params=_params(1),
        name="mod",
    )(c8, mod_w, mod_b)


def _pool_fold_kernel(pw_ref, ps_ref, wpp_ref, o_ref):
    scaled = (pw_ref[0] * ps_ref[0]).astype(BF16)
    o_ref[...] = _dot(scaled, wpp_ref[...].astype(BF16)).astype(BF16)


def _pool_fold_call(pool_w, pool_scale3, w_pool_proj):
    n_out = w_pool_proj.shape[1]
    return pl.pallas_call(
        _pool_fold_kernel,
        grid=(POOL_GROUPS,),
        in_specs=[
            pl.BlockSpec((1, POOL_GROUP_W, POOL_GROUP_W), lambda g: (g, 0, 0)),
            pl.BlockSpec((1, 1, POOL_GROUP_W), lambda g: (g, 0, 0)),
            pl.BlockSpec((POOL_GROUP_W, n_out), lambda g: (g, 0)),
        ],
        out_specs=pl.BlockSpec((POOL_GROUP_W, n_out), lambda g: (g, 0)),
        out_shape=jax.ShapeDtypeStruct((POOL_W, n_out), BF16),
        compiler_params=_params(1),
        name="pool_fold",
    )(pool_w, pool_scale3, w_pool_proj)


def _ctx_kernel(ctx_ref, mod_ref, g1_ref, wall_hbm, kg_ref, ones_ref, kc_ref, vc_ref,
                w_ref, col_stage, col_sem, *, n_batch):
    @pl.when(pl.program_id(0) == 0)
    def _():
        _stage_weight_bf16(wall_hbm.at[:, pl.ds(Q_W, 2 * KV_W)], w_ref, col_stage, col_sem, axis=1)

    m = mod_ref[n_batch:n_batch + 1, :]
    sh = m[:, 0:D_MODEL]
    gs = g1_ref[...] * (1.0 + m[:, D_MODEL:2 * D_MODEL])
    h = _rms_modulate(ctx_ref[0], gs, sh).astype(BF16)
    kv = _dot(h, w_ref[...])
    k = kv[:, :KV_W]
    kc_ref[0] = (k * _head_rms(k, ones_ref[...]) * kg_ref[...]).astype(BF16)
    vc_ref[0] = kv[:, KV_W:].astype(BF16)


def _ctx_call(ctx, mod, norm1_g, w_all, kg256, ones_bd):
    B, C, D = ctx.shape
    rows = B * C
    kc, vc = pl.pallas_call(
        functools.partial(_ctx_kernel, n_batch=B),
        grid=(1,),
        in_specs=[
            pl.BlockSpec((1, rows, D), lambda i: (0, 0, 0)),
            _const_spec(mod.shape),
            _const_spec((1, D)),
            pl.BlockSpec(memory_space=pl.ANY),
            _const_spec((1, KV_W)),
            _const_spec((MXU_TILE, MXU_TILE)),
        ],
        out_specs=[
            pl.BlockSpec((1, rows, KV_W), lambda i: (0, 0, 0)),
            pl.BlockSpec((1, rows, KV_W), lambda i: (0, 0, 0)),
        ],
        out_shape=[
            jax.ShapeDtypeStruct((1, rows, KV_W), BF16),
            jax.ShapeDtypeStruct((1, rows, KV_W), BF16),
        ],
        scratch_shapes=[
            pltpu.VMEM((D, 2 * KV_W), BF16),
            pltpu.VMEM((STAGE_SLOTS, D, STAGE_CHUNK), F32),
            pltpu.SemaphoreType.DMA((STAGE_SLOTS,)),
        ],
        compiler_params=_sequential_params(1),
        name="ctx_kv",
    )(ctx.reshape(1, rows, D), mod, norm1_g, w_all, kg256, ones_bd)
    return kc.reshape(B, C, KV_W), vc.reshape(B, C, KV_W)


def _rope(t, cos, sin_signed, first_half):
    swap = jnp.where(first_half, pltpu.roll(t, LANES - ROPE_PAIR, axis=1), pltpu.roll(t, ROPE_PAIR, axis=1))
    return t * cos + swap * sin_signed


def _inproj_kernel(x_ref, mod_ref, g1_ref, wq_ref, wall_hbm, qg_ref, kg_ref, ones_ref, cos_ref, sin_ref,
                   q_ref, k_ref, v_ref, u_ref, wkvu_ref, col_stage, col_sem):
    @pl.when((pl.program_id(0) == 0) & (pl.program_id(1) == 0))
    def _():
        _stage_weight_bf16(wall_hbm.at[:, pl.ds(Q_W, Q_W)], wkvu_ref, col_stage, col_sem, axis=1)

    m = mod_ref[pl.ds(pl.program_id(0), 1), :]
    sh = m[:, 0:D_MODEL]
    gs = g1_ref[...] * (1.0 + m[:, D_MODEL:2 * D_MODEL])
    h = _rms_modulate(x_ref[0], gs, sh).astype(BF16)
    ones_bd = ones_ref[...]
    cos = cos_ref[...]
    sin = sin_ref[...]
    lane = lax.broadcasted_iota(jnp.int32, cos.shape, 1)
    first_half = (lane & (2 * ROPE_PAIR - 1)) < ROPE_PAIR

    def normed_rope(t, gain):
        t = t * _head_rms(t, ones_bd) * gain
        halves = [_rope(t[:, i * LANES:(i + 1) * LANES], cos, sin, first_half) for i in range(2)]
        return jnp.concatenate(halves, axis=1).astype(BF16)

    wide = 2 * MXU_TILE
    for j in range(Q_W // wide):
        t = _dot(h, wq_ref[:, j * wide:(j + 1) * wide])
        for i in range(2):
            c0 = j * wide + i * MXU_TILE
            q_ref[0, :, c0:c0 + MXU_TILE] = normed_rope(t[:, i * MXU_TILE:(i + 1) * MXU_TILE], qg_ref[...])
    kv = _dot(h, wkvu_ref[:, :2 * KV_W])
    k_ref[0] = normed_rope(kv[:, :KV_W], kg_ref[...])
    v_ref[0] = kv[:, KV_W:].astype(BF16)
    u_ref[0] = _dot(h, wkvu_ref[:, 2 * KV_W:])


def _inproj_call(x, mod, norm1_g, w_q, w_all, qg256, kg256, ones_bd, cos_t, sin_t, tile):
    B, S, D = x.shape
    row = lambda b, t: (b, t, 0)
    return pl.pallas_call(
        _inproj_kernel,
        grid=(B, S // tile),
        in_specs=[
            pl.BlockSpec((1, tile, D), row),
            _const_spec(mod.shape),
            _const_spec((1, D)),
            _const_spec((D, Q_W)),
            pl.BlockSpec(memory_space=pl.ANY),
            _const_spec((1, MXU_TILE)),
            _const_spec((1, MXU_TILE)),
            _const_spec((MXU_TILE, MXU_TILE)),
            pl.BlockSpec((tile, LANES), lambda b, t: (t, 0)),
            pl.BlockSpec((tile, LANES), lambda b, t: (t, 0)),
        ],
        out_specs=[
            pl.BlockSpec((1, tile, Q_W), row),
            pl.BlockSpec((1, tile, KV_W), row),
            pl.BlockSpec((1, tile, KV_W), row),
            pl.BlockSpec((1, tile, POOL_W), row),
        ],
        out_shape=[
            jax.ShapeDtypeStruct((B, S, Q_W), BF16),
            jax.ShapeDtypeStruct((B, S, KV_W), BF16),
            jax.ShapeDtypeStruct((B, S, KV_W), BF16),
            jax.ShapeDtypeStruct((B, S, POOL_W), F32),
        ],
        scratch_shapes=[
            pltpu.VMEM((D, Q_W), BF16),
            pltpu.VMEM((STAGE_SLOTS, D, STAGE_CHUNK), F32),
            pltpu.SemaphoreType.DMA((STAGE_SLOTS,)),
        ],
        compiler_params=_sequential_params(2),
        name="in_proj",
    )(x, mod, norm1_g, w_q, w_all, qg256, kg256, ones_bd, cos_t, sin_t)


def _attn_kernel(sink_ref, q_ref, kp_ref, kc0_ref, kn_ref, vp_ref, vc0_ref, vn_ref,
                 kctx_ref, vctx_ref, bias_ref, o_ref, *, blocks_per_tile):
    t_idx = pl.program_id(1)
    n_tiles = pl.num_programs(1)
    k_all = jnp.concatenate([kp_ref[0], kc0_ref[0], kn_ref[0]], axis=0)
    v_all = jnp.concatenate([vp_ref[0], vc0_ref[0], vn_ref[0]], axis=0)
    k_ctx = kctx_ref[0]
    v_ctx = vctx_ref[0]
    v_amax = jnp.maximum(jnp.max(jnp.abs(v_all.astype(F32)), keepdims=True),
                         jnp.max(jnp.abs(v_ctx.astype(F32)), keepdims=True))
    v_scale = jnp.exp2(jnp.floor(jnp.log2(FP8_MAX / jnp.maximum(v_amax, 1e-30))))
    lane = lax.broadcasted_iota(jnp.int32, (BLOCK, MXU_TILE), 1)
    in_head = [(lane >= kv * HEAD_DIM) & (lane < (kv + 1) * HEAD_DIM) for kv in range(N_KV_HEADS)]
    zero = jnp.zeros((), BF16)

    def with_ones(v, half):
        vl = lax.broadcasted_iota(jnp.int32, v.shape, 1)
        return jnp.where((vl < LANES) if half == 0 else (vl >= LANES), v.astype(F32) * v_scale, 1.0).astype(FP8)

    v_ctx_pair = [with_ones(v_ctx, half) for half in range(2)]
    low_head = lax.broadcasted_iota(jnp.int32, (BLOCK, LANES), 1) < HEAD_DIM

    def block_operands(i):
        variant = jnp.int32(0)
        if i == blocks_per_tile - 1:
            variant = jnp.where(t_idx == n_tiles - 1, 2, variant)
        if i == 0:
            variant = jnp.where(t_idx == 0, 1, variant)
        v_win = v_all[i * BLOCK:(i + 3) * BLOCK]
        return dict(bias=bias_ref[variant], k_win=k_all[i * BLOCK:(i + 3) * BLOCK],
                    v_win_pair=[with_ones(v_win, half) for half in range(2)])

    def scores(blk, i, j):
        q_slab = q_ref[0, i * BLOCK:(i + 1) * BLOCK, j * MXU_TILE:(j + 1) * MXU_TILE]
        qz = jnp.concatenate([jnp.where(in_head[kv], q_slab, zero) for kv in range(N_KV_HEADS)], axis=0)
        return _dot_nt(qz, blk["k_win"]), _dot_nt(qz, k_ctx)

    def softmax_values(blk, i, j, s_win, s_ctx):
        p_win, p_ctx, sink_term = [], [], []
        for kv in range(N_KV_HEADS):
            rows = slice(kv * BLOCK, (kv + 1) * BLOCK)
            sw = s_win[rows] + blk["bias"]
            sc = s_ctx[rows]
            sink = sink_ref[kv * GROUP + j] * LOG2_E
            m = jnp.maximum(
                jnp.maximum(jnp.max(sw, axis=-1, keepdims=True),
                            jnp.max(sc, axis=-1, keepdims=True)), sink)
            p_win.append(jnp.exp2(sw - m).astype(FP8))
            p_ctx.append(jnp.exp2(sc - m).astype(FP8))
            sink_term.append(jnp.exp2(sink - m))
        for half in range(2):
            pw = jnp.concatenate(p_win[2 * half:2 * half + 2], axis=0)
            pc = jnp.concatenate(p_ctx[2 * half:2 * half + 2], axis=0)
            o = _dot(pw, blk["v_win_pair"][half]) + _dot(pc, v_ctx_pair[half])
            vals = o[:, half * LANES:(half + 1) * LANES]
            sums = o[:, (1 - half) * LANES:(2 - half) * LANES]
            out = []
            for r in range(2):
                rows = slice(r * BLOCK, (r + 1) * BLOCK)
                out.append(vals[rows] / ((sums[rows] + sink_term[2 * half + r]) * v_scale))
            c0 = j * MXU_TILE + half * LANES
            o_ref[0, i * BLOCK:(i + 1) * BLOCK, c0:c0 + LANES] = jnp.where(low_head, out[0], out[1]).astype(BF16)

    blocks = [block_operands(i) for i in range(blocks_per_tile)]
    items = [(i, j) for i in range(blocks_per_tile) for j in range(GROUP)]
    s_next = scores(blocks[0], *items[0])
    for idx, (i, j) in enumerate(items):
        s_cur = s_next
        if idx + 1 < len(items):
            ni, nj = items[idx + 1]
            s_next = scores(blocks[ni], ni, nj)
        softmax_values(blocks[i], i, j, *s_cur)


def _attn_call(sink, q, k, v, kc, vc, bias, tile):
    B, S, _ = q.shape
    bpt = tile // BLOCK
    nb = S // BLOCK
    C = kc.shape[1]
    cur = lambda b, t, *_: (b, t, 0)
    prev = lambda b, t, *_: (b, jnp.maximum(t * bpt - 1, 0), 0)
    nxt = lambda b, t, *_: (b, jnp.minimum((t + 1) * bpt, nb - 1), 0)
    per_b = lambda b, t, *_: (b, 0, 0)
    halo = (1, BLOCK, KV_W)
    center = (1, tile, KV_W)
    grid_spec = pltpu.PrefetchScalarGridSpec(
        num_scalar_prefetch=1,
        grid=(B, S // tile),
        in_specs=[
            pl.BlockSpec((1, tile, Q_W), cur),
            pl.BlockSpec(halo, prev), pl.BlockSpec(center, cur), pl.BlockSpec(halo, nxt),
            pl.BlockSpec(halo, prev), pl.BlockSpec(center, cur), pl.BlockSpec(halo, nxt),
            pl.BlockSpec((1, C, KV_W), per_b),
            pl.BlockSpec((1, C, KV_W), per_b),
            pl.BlockSpec((3, BLOCK, 3 * BLOCK), lambda b, t, *_: (0, 0, 0)),
        ],
        out_specs=pl.BlockSpec((1, tile, Q_W), cur),
    )
    return pl.pallas_call(
        functools.partial(_attn_kernel, blocks_per_tile=bpt),
        grid_spec=grid_spec,
        out_shape=jax.ShapeDtypeStruct((B, S, Q_W), BF16),
        compiler_params=_params(2),
        name="attn",
    )(sink, q, k, k, k, v, v, v, kc, vc, bias)


def _merge_kernel(x_ref, mod_ref, g1_ref, attn_ref, up_ref, uc_ref, un_ref,
                  wall_hbm, gb_ref, wap_hbm, wpool_ref, wo_hbm, o_ref,
                  uext_ref, wg_ref, wap_ref, wo_ref, col_stage, row_stage, col_sem, row_sem, *, seq, tile, sub):
    @pl.when((pl.program_id(0) == 0) & (pl.program_id(1) == 0))
    def _():
        _stage_weight_bf16(wall_hbm.at[:, pl.ds(2 * D_MODEL, 2 * D_MODEL)], wg_ref, col_stage, col_sem, axis=1)
        _stage_weight_bf16(wo_hbm, wo_ref, col_stage, col_sem, axis=1)

        def store_regrouped(kv, rows_bf16):
            for j in range(GROUP):
                dst = (j * N_KV_HEADS + kv) * HEAD_DIM
                wap_ref[dst:dst + HEAD_DIM, :] = rows_bf16[j * HEAD_DIM:(j + 1) * HEAD_DIM, :]

        _stage_weight_bf16(wap_hbm, None, row_stage, row_sem, axis=0, store=store_regrouped)

    t_idx = pl.program_id(1)
    n_tiles = pl.num_programs(1)
    m = mod_ref[pl.ds(pl.program_id(0), 1), :]
    sh = m[:, 0:D_MODEL]
    gs = g1_ref[...] * (1.0 + m[:, D_MODEL:2 * D_MODEL])
    gate1 = m[:, 2 * D_MODEL:3 * D_MODEL]

    keep_prev = (t_idx > 0).astype(F32)
    keep_next = (t_idx < n_tiles - 1).astype(F32)
    uext_ref[0:POOL_HALO] = up_ref[0] * keep_prev
    uext_ref[POOL_HALO:POOL_HALO + tile] = uc_ref[0]
    uext_ref[POOL_HALO + tile:] = un_ref[0] * keep_next
    edge_rows = lax.broadcasted_iota(jnp.int32, (POOL_HALO, POOL_GROUP_W), 0)

    def attn_dot(r):
        return _dot(attn_ref[0, r * sub:(r + 1) * sub, :], wap_ref[...])

    def gate_dot(r):
        h = _rms_modulate(x_ref[0, r * sub:(r + 1) * sub, :], gs, sh).astype(BF16)
        return jax.nn.sigmoid(_dot(h, wg_ref[...]) + gb_ref[...])

    def pool_dot(r):
        r0 = r * sub
        pos_first = t_idx * tile + r0 + edge_rows
        pos_last = pos_first + (sub - POOL_HALO)
        diffs = []
        for g, w in enumerate(POOL_WINDOWS):
            cols = slice(g * POOL_GROUP_W, (g + 1) * POOL_GROUP_W)
            acc = uext_ref[pl.ds(r0, sub + 2 * POOL_HALO), cols]
            k = 1
            while k < w:
                acc = acc[:acc.shape[0] - k] + acc[k:]
                k *= 2
            win = acc[POOL_HALO - w // 2:POOL_HALO - w // 2 + sub]

            def inv_count(pos):
                return 1.0 / (jnp.minimum(pos + w // 2, seq) - jnp.maximum(pos - w // 2, 0)).astype(F32)

            pooled_avg = jnp.concatenate([
                win[:POOL_HALO] * inv_count(pos_first),
                win[POOL_HALO:sub - POOL_HALO] * (1.0 / w),
                win[sub - POOL_HALO:] * inv_count(pos_last)], axis=0)
            diffs.append((pooled_avg - uext_ref[pl.ds(r0 + POOL_HALO, sub), cols]).astype(BF16))
        return _dot(jnp.concatenate(diffs, axis=1), wpool_ref[...])

    def output(r, a, gates, p):
        rows = slice(r * sub, (r + 1) * sub)
        merged = (gates[:, :D_MODEL] * a + gates[:, D_MODEL:] * p).astype(BF16)
        o_ref[0, rows, :] = x_ref[0, rows, :] + gate1 * _dot(merged, wo_ref[...])

    n_sub = tile // sub
    a = [attn_dot(r) for r in range(n_sub)]
    p = [pool_dot(r) for r in range(n_sub)]
    gates = [gate_dot(r) for r in range(n_sub)]
    for r in range(n_sub):
        output(r, a[r], gates[r], p[r])


def _merge_call(x, mod, norm1_g, attn, u, w_all, gate_b, w_ap, w_pool, w_out, tile):
    B, S, D = x.shape
    r = tile // POOL_HALO
    n_halo = S // POOL_HALO
    row = lambda b, t: (b, t, 0)
    return pl.pallas_call(
        functools.partial(_merge_kernel, seq=S, tile=tile, sub=SUB_TILE),
        grid=(B, S // tile),
        in_specs=[
            pl.BlockSpec((1, tile, D), row),
            _const_spec(mod.shape),
            _const_spec((1, D)),
            pl.BlockSpec((1, tile, Q_W), row),
            pl.BlockSpec((1, POOL_HALO, POOL_W), lambda b, t: (b, jnp.maximum(t * r - 1, 0), 0)),
            pl.BlockSpec((1, tile, POOL_W), row),
            pl.BlockSpec((1, POOL_HALO, POOL_W), lambda b, t: (b, jnp.minimum((t + 1) * r, n_halo - 1), 0)),
            pl.BlockSpec(memory_space=pl.ANY),
            _const_spec(gate_b.shape),
            pl.BlockSpec(memory_space=pl.ANY),
            _const_spec(w_pool.shape),
            pl.BlockSpec(memory_space=pl.ANY),
        ],
        out_specs=pl.BlockSpec((1, tile, D), row),
        out_shape=jax.ShapeDtypeStruct((B, S, D), F32),
        scratch_shapes=[
            pltpu.VMEM((tile + 2 * POOL_HALO, POOL_W), F32),
            pltpu.VMEM((D, 2 * D), BF16),
            pltpu.VMEM(w_ap.shape, BF16),
            pltpu.VMEM(w_out.shape, BF16),
            pltpu.VMEM((STAGE_SLOTS, D, STAGE_CHUNK), F32),
            pltpu.VMEM((STAGE_SLOTS, STAGE_CHUNK, D), F32),
            pltpu.SemaphoreType.DMA((STAGE_SLOTS,)),
            pltpu.SemaphoreType.DMA((STAGE_SLOTS,)),
        ],
        compiler_params=_sequential_params(2),
        name="merge",
    )(x, mod, norm1_g, attn, u, u, u, w_all, gate_b, w_ap, w_pool, w_out)


def _ffn_kernel(x_ref, mod_ref, g2_ref, wup_hbm, wdn_hbm, o_ref,
                wup_ref, wdn_ref, col_stage, row_stage, col_sem, row_sem, hmid_ref, *, d_ff, tile, sub):
    @pl.when((pl.program_id(0) == 0) & (pl.program_id(1) == 0))
    def _():
        _stage_weight_bf16(wup_hbm, wup_ref, col_stage, col_sem, axis=1)
        _stage_weight_bf16(wdn_hbm, wdn_ref, row_stage, row_sem, axis=0)

    m = mod_ref[pl.ds(pl.program_id(0), 1), :]
    sh = m[:, 3 * D_MODEL:4 * D_MODEL]
    gs = g2_ref[...] * (1.0 + m[:, 4 * D_MODEL:5 * D_MODEL])
    gate2 = m[:, 5 * D_MODEL:6 * D_MODEL]

    def up(r):
        rows = slice(r * sub, (r + 1) * sub)
        h = _rms_modulate(x_ref[0, rows, :], gs, sh).astype(BF16)
        for c in range(d_ff // MXU_TILE):
            a = _dot(h, wup_ref[:, c * MXU_TILE:(c + 1) * MXU_TILE])
            b = _dot(h, wup_ref[:, d_ff + c * MXU_TILE:d_ff + (c + 1) * MXU_TILE])
            hmid_ref[rows, c * MXU_TILE:(c + 1) * MXU_TILE] = (a * jax.nn.sigmoid(a) * b).astype(BF16)

    def down(r):
        rows = slice(r * sub, (r + 1) * sub)
        o_ref[0, rows, :] = x_ref[0, rows, :] + gate2 * _dot(hmid_ref[rows, :], wdn_ref[...])

    n_sub = tile // sub
    up(0)
    for r in range(1, n_sub):
        up(r)
        down(r - 1)
    down(n_sub - 1)


def _ffn_call(x, mod, norm2_g, w_up, w_down, tile):
    B, S, D = x.shape
    d_ff = w_down.shape[0]
    row = lambda b, t: (b, t, 0)
    return pl.pallas_call(
        functools.partial(_ffn_kernel, d_ff=d_ff, tile=tile, sub=SUB_TILE),
        grid=(B, S // tile),
        in_specs=[
            pl.BlockSpec((1, tile, D), row),
            _const_spec(mod.shape),
            _const_spec((1, D)),
            pl.BlockSpec(memory_space=pl.ANY),
            pl.BlockSpec(memory_space=pl.ANY),
        ],
        out_specs=pl.BlockSpec((1, tile, D), row),
        out_shape=jax.ShapeDtypeStruct((B, S, D), F32),
        scratch_shapes=[
            pltpu.VMEM(w_up.shape, BF16),
            pltpu.VMEM(w_down.shape, BF16),
            pltpu.VMEM((STAGE_SLOTS, D, STAGE_CHUNK), F32),
            pltpu.VMEM((STAGE_SLOTS, STAGE_CHUNK, D), F32),
            pltpu.SemaphoreType.DMA((STAGE_SLOTS,)),
            pltpu.SemaphoreType.DMA((STAGE_SLOTS,)),
            pltpu.VMEM((tile, d_ff), BF16),
        ],
        compiler_params=_sequential_params(2),
        name="ffn",
    )(x, mod, norm2_g, w_up, w_down)


def _rope_tables(seq):
    t = np.arange(seq)
    row = (t // GRID_W).astype(np.float64)
    col = (t % GRID_W).astype(np.float64)
    half = HEAD_DIM // 2
    inv_freq = 1.0 / (ROPE_THETA ** (np.arange(0, half, 2, dtype=np.float64) / half))
    ang_r = row[:, None] * inv_freq
    ang_c = col[:, None] * inv_freq
    cos = np.concatenate([np.cos(ang_r)] * 2 + [np.cos(ang_c)] * 2, axis=1)
    sin = np.concatenate([-np.sin(ang_r), np.sin(ang_r), -np.sin(ang_c), np.sin(ang_c)], axis=1)
    reps = (1, LANES // HEAD_DIM)
    return jnp.asarray(np.tile(cos, reps), dtype=F32), jnp.asarray(np.tile(sin, reps), dtype=F32)


def _band_bias():
    i = np.arange(BLOCK)[:, None]
    j = np.arange(3 * BLOCK)[None, :]
    band = np.abs(j - BLOCK - i) <= WINDOW
    first = band & (j >= BLOCK)
    last = band & (j < 2 * BLOCK)
    masks = np.stack([band, first, last])
    return jnp.asarray(np.where(masks, 0.0, NEG_BIG), dtype=F32)


def kernel(x, c, ctx, c_ctx, mod_w, mod_b, norm1_g, norm2_g, w_in, gate_b, q_norm_g, k_norm_g,
           sink, pool_w, pool_scale, w_attn_proj, w_pool_proj, w_out, w_up, w_down):
    B, S, D = x.shape
    depth = mod_w.shape[0]
    cos_t, sin_t = _rope_tables(S)
    bias = _band_bias()
    ones_bd = jnp.asarray(np.kron(np.eye(MXU_TILE // HEAD_DIM), np.ones((HEAD_DIM, HEAD_DIM))), dtype=BF16)
    reps = MXU_TILE // HEAD_DIM
    assert depth == 1, "context-stream update between layers is not implemented"
    assert D == D_MODEL and S % ROW_TILE == 0 and B + 1 <= F32_SUBLANES

    for l in range(depth):
        c8 = jnp.concatenate([c, c_ctx[None, :], jnp.zeros((F32_SUBLANES - B - 1, D), F32)], axis=0)
        mod = _mod_call(c8, mod_w[l], mod_b[l][None, :])

        wl = w_in[l]
        w_q = (wl[:, :Q_W].reshape(D, N_KV_HEADS, GROUP, HEAD_DIM).transpose(0, 2, 1, 3)
               .reshape(D, Q_W).astype(BF16))
        g1 = norm1_g[l][None, :]
        kg256 = jnp.tile(k_norm_g[l], reps)[None, :]
        qg256 = jnp.tile(q_norm_g[l] * (HEAD_DIM ** -0.5 * LOG2_E), reps)[None, :]

        kc, vc = _ctx_call(ctx, mod, g1, wl, kg256, ones_bd)
        q, k, v, u = _inproj_call(x, mod, g1, w_q, wl, qg256, kg256, ones_bd, cos_t, sin_t, tile=ROW_TILE)
        attn = _attn_call(sink[l], q, k, v, kc, vc, bias, tile=ROW_TILE)
        w_pool = _pool_fold_call(pool_w[l], pool_scale[l].reshape(POOL_GROUPS, 1, POOL_GROUP_W), w_pool_proj[l])
        x = _merge_call(
            x, mod, g1, attn, u, wl, gate_b[l][None, :], w_attn_proj[l], w_pool, w_out[l], tile=ROW_TILE)
        x = _ffn_call(x, mod, norm2_g[l][None, :], w_up[l], w_down[l], tile=ROW_TILE)
    return x
```

```python
import functools

import jax
import jax.numpy as jnp
import numpy as np
from jax import lax
from jax.experimental import pallas as pl
from jax.experimental.pallas import tpu as pltpu

D_MODEL = 1024
GRID_W = 64
HEAD_DIM = 64
N_HEADS = 16
N_KV_HEADS = 4
GROUP = N_HEADS // N_KV_HEADS
Q_W = N_HEADS * HEAD_DIM
KV_W = N_KV_HEADS * HEAD_DIM
WINDOW = 128
BLOCK = 128
ROPE_THETA = 10000.0
POOL_WINDOWS = (2, 4, 8, 16)
POOL_GROUPS = 4
POOL_W = D_MODEL // 2
POOL_GROUP_W = POOL_W // POOL_GROUPS
POOL_HALO = 8
ROPE_PAIR = HEAD_DIM // 4
EPS = 1e-6
NEG_BIG = -1e30
LOG2_E = 1.4426950408889634

LANES = 128
F32_SUBLANES = 8
MXU_TILE = 256
VMEM_LIMIT = 56 * 1024 * 1024
STAGE_SLOTS = 4
STAGE_CHUNK = MXU_TILE
ROW_TILE = 1024
SUB_TILE = MXU_TILE

F32 = jnp.float32
BF16 = jnp.bfloat16
FP8 = jnp.float8_e4m3fn
FP8_MAX = 448.0


def _params(n_parallel):
    return pltpu.CompilerParams(
        dimension_semantics=("parallel",) * n_parallel,
        vmem_limit_bytes=VMEM_LIMIT,
    )


def _const_spec(shape):
    nd = len(shape)
    return pl.BlockSpec(shape, lambda *_: (0,) * nd, pipeline_mode=pl.Buffered(1))


def _dot(a, b):
    return jnp.dot(a, b, preferred_element_type=F32)


def _dot_nt(a, b):
    return lax.dot_general(a, b, (((1,), (1,)), ((), ())), preferred_element_type=F32)


def _rms_modulate(x, gs, sh):
    rs = lax.rsqrt(jnp.mean(x * x, axis=-1, keepdims=True) + EPS)
    return x * rs * gs + sh


def _head_rms(t, ones_bd, unscale=1.0):
    ms = _dot((t * t).astype(BF16), ones_bd) * (unscale * unscale * (1.0 / HEAD_DIM))
    return lax.rsqrt(ms + EPS)


def _stage_weight_bf16(w_hbm, w_ref, stage_ref, sem, axis, store=None):
    slots = stage_ref.shape[0]
    chunk = stage_ref.shape[1 + axis]
    n = w_hbm.shape[axis] // chunk

    def piece(ref, i):
        return ref.at[:, pl.ds(i * chunk, chunk)] if axis == 1 else ref.at[pl.ds(i * chunk, chunk), :]

    def copy(i):
        return pltpu.make_async_copy(piece(w_hbm, i), stage_ref.at[i % slots], sem.at[i % slots])

    for i in range(min(slots - 1, n)):
        copy(i).start()
    for i in range(n):
        if i + slots - 1 < n:
            copy(i + slots - 1).start()
        copy(i).wait()
        staged = stage_ref[i % slots]
        if store is None:
            piece(w_ref, i)[...] = staged.astype(BF16)
        else:
            store(i, staged)


def _fp8_scale(amax):
    return jnp.exp2(jnp.floor(jnp.log2(FP8_MAX / jnp.maximum(amax, 1e-30))))


def _sequential_params(n_axes):
    return pltpu.CompilerParams(dimension_semantics=("arbitrary",) * n_axes, vmem_limit_bytes=VMEM_LIMIT)


def _mod_kernel(c_ref, w_ref, b_ref, o_ref):
    c = c_ref[...]
    a = (c * jax.nn.sigmoid(c)).astype(BF16)
    o_ref[...] = _dot(a, w_ref[...].astype(BF16)) + b_ref[...]


def _mod_call(c8, mod_w, mod_b):
    n = mod_w.shape[1]
    bn = 2 * D_MODEL
    return pl.pallas_call(
        _mod_kernel,
        grid=(n // bn,),
        in_specs=[
            pl.BlockSpec((F32_SUBLANES, D_MODEL), lambda j: (0, 0)),
            pl.BlockSpec((D_MODEL, bn), lambda j: (0, j)),
            pl.BlockSpec((1, bn), lambda j: (0, j)),
        ],
        out_specs=pl.BlockSpec((F32_SUBLANES, bn), lambda j: (0, j)),
        out_shape=jax.ShapeDtypeStruct((F32_SUBLANES, n), F32),
        compiler_params=_params(1),
        name="mod",
    )(c8, mod_w, mod_b)


def _pool_fold_kernel(pw_ref, ps_ref, wpp_ref, o_ref):
    scaled = (pw_ref[0] * ps_ref[0]).astype(BF16)
    o_ref[...] = _dot(scaled, wpp_ref[...].astype(BF16)).astype(BF16)


def _pool_fold_call(pool_w, pool_scale3, w_pool_proj):
    n_out = w_pool_proj.shape[1]
    return pl.pallas_call(
        _pool_fold_kernel,
        grid=(POOL_GROUPS,),
        in_specs=[
            pl.BlockSpec((1, POOL_GROUP_W, POOL_GROUP_W), lambda g: (g, 0, 0)),
            pl.BlockSpec((1, 1, POOL_GROUP_W), lambda g: (g, 0, 0)),
            pl.BlockSpec((POOL_GROUP_W, n_out), lambda g: (g, 0)),
        ],
        out_specs=pl.BlockSpec((POOL_GROUP_W, n_out), lambda g: (g, 0)),
        out_shape=jax.ShapeDtypeStruct((POOL_W, n_out), BF16),
        compiler_params=_params(1),
        name="pool_fold",
    )(pool_w, pool_scale3, w_pool_proj)


def _ctx_kernel(ctx_ref, mod_ref, g1_ref, wall_hbm, kg_ref, ones_ref, kc_ref, vc_ref,
                w_ref, col_stage, col_sem, *, n_batch):
    @pl.when(pl.program_id(0) == 0)
    def _():
        _stage_weight_bf16(wall_hbm.at[:, pl.ds(Q_W, 2 * KV_W)], w_ref, col_stage, col_sem, axis=1)

    m = mod_ref[n_batch:n_batch + 1, :]
    sh = m[:, 0:D_MODEL]
    gs = g1_ref[...] * (1.0 + m[:, D_MODEL:2 * D_MODEL])
    h = _rms_modulate(ctx_ref[0], gs, sh).astype(BF16)
    kv = _dot(h, w_ref[...])
    k = kv[:, :KV_W]
    kc_ref[0] = (k * _head_rms(k, ones_ref[...]) * kg_ref[...]).astype(BF16)
    vc_ref[0] = kv[:, KV_W:].astype(BF16)


def _ctx_call(ctx, mod, norm1_g, w_all, kg256, ones_bd):
    B, C, D = ctx.shape
    rows = B * C
    kc, vc = pl.pallas_call(
        functools.partial(_ctx_kernel, n_batch=B),
        grid=(1,),
        in_specs=[
            pl.BlockSpec((1, rows, D), lambda i: (0, 0, 0)),
            _const_spec(mod.shape),
            _const_spec((1, D)),
            pl.BlockSpec(memory_space=pl.ANY),
            _const_spec((1, KV_W)),
            _const_spec((MXU_TILE, MXU_TILE)),
        ],
        out_specs=[
            pl.BlockSpec((1, rows, KV_W), lambda i: (0, 0, 0)),
            pl.BlockSpec((1, rows, KV_W), lambda i: (0, 0, 0)),
        ],
        out_shape=[
            jax.ShapeDtypeStruct((1, rows, KV_W), BF16),
            jax.ShapeDtypeStruct((1, rows, KV_W), BF16),
        ],
        scratch_shapes=[
            pltpu.VMEM((D, 2 * KV_W), BF16),
            pltpu.VMEM((STAGE_SLOTS, D, STAGE_CHUNK), F32),
            pltpu.SemaphoreType.DMA((STAGE_SLOTS,)),
        ],
        compiler_params=_sequential_params(1),
        name="ctx_kv",
    )(ctx.reshape(1, rows, D), mod, norm1_g, w_all, kg256, ones_bd)
    return kc.reshape(B, C, KV_W), vc.reshape(B, C, KV_W)


def _rope(t, cos, sin_signed, first_half):
    swap = jnp.where(first_half, pltpu.roll(t, LANES - ROPE_PAIR, axis=1), pltpu.roll(t, ROPE_PAIR, axis=1))
    return t * cos + swap * sin_signed


def _inproj_kernel(x_ref, mod_ref, g1_ref, wq_ref, wall_hbm, wscale_ref, qg_ref, kg_ref, ones_ref, cos_ref,
                   sin_ref, q_ref, k_ref, v_ref, u_ref, wkv_ref, wu_ref, col_stage, col_sem):
    w_scale = wscale_ref[...]

    @pl.when((pl.program_id(0) == 0) & (pl.program_id(1) == 0))
    def _():
        def store_kv_fp8(i, cols_f32):
            wkv_ref[:, i * STAGE_CHUNK:(i + 1) * STAGE_CHUNK] = (cols_f32 * w_scale).astype(FP8)

        _stage_weight_bf16(wall_hbm.at[:, pl.ds(Q_W, 2 * KV_W)], None, col_stage, col_sem, axis=1,
                           store=store_kv_fp8)
        _stage_weight_bf16(wall_hbm.at[:, pl.ds(Q_W + 2 * KV_W, POOL_W)], wu_ref, col_stage, col_sem, axis=1)

    m = mod_ref[pl.ds(pl.program_id(0), 1), :]
    sh = m[:, 0:D_MODEL]
    gs = g1_ref[...] * (1.0 + m[:, D_MODEL:2 * D_MODEL])
    h32 = _rms_modulate(x_ref[0], gs, sh)
    h = h32.astype(BF16)
    h_scale = _fp8_scale(jnp.max(jnp.abs(h32), keepdims=True))
    h8 = (h32 * h_scale).astype(FP8)
    unscale = 1.0 / (h_scale * w_scale)
    ones_bd = ones_ref[...]
    cos = cos_ref[...]
    sin = sin_ref[...]
    lane = lax.broadcasted_iota(jnp.int32, cos.shape, 1)
    first_half = (lane & (2 * ROPE_PAIR - 1)) < ROPE_PAIR

    def normed_rope(t, gain):
        t = t * _head_rms(t, ones_bd, unscale) * (gain * unscale)
        halves = [_rope(t[:, i * LANES:(i + 1) * LANES], cos, sin, first_half) for i in range(2)]
        return jnp.concatenate(halves, axis=1).astype(BF16)

    wide = 2 * MXU_TILE
    for j in range(Q_W // wide):
        t = _dot(h8, wq_ref[:, j * wide:(j + 1) * wide])
        for i in range(2):
            c0 = j * wide + i * MXU_TILE
            q_ref[0, :, c0:c0 + MXU_TILE] = normed_rope(t[:, i * MXU_TILE:(i + 1) * MXU_TILE], qg_ref[...])
    kv = _dot(h8, wkv_ref[...])
    k_ref[0] = normed_rope(kv[:, :KV_W], kg_ref[...])
    v_ref[0] = (kv[:, KV_W:] * unscale).astype(BF16)
    u_ref[0] = _dot(h, wu_ref[...])


def _inproj_call(x, mod, norm1_g, w_q, w_all, w_scale, qg256, kg256, ones_bd, cos_t, sin_t, tile):
    B, S, D = x.shape
    row = lambda b, t: (b, t, 0)
    return pl.pallas_call(
        _inproj_kernel,
        grid=(B, S // tile),
        in_specs=[
            pl.BlockSpec((1, tile, D), row),
            _const_spec(mod.shape),
            _const_spec((1, D)),
            _const_spec((D, Q_W)),
            pl.BlockSpec(memory_space=pl.ANY),
            _const_spec((1, 1)),
            _const_spec((1, MXU_TILE)),
            _const_spec((1, MXU_TILE)),
            _const_spec((MXU_TILE, MXU_TILE)),
            pl.BlockSpec((tile, LANES), lambda b, t: (t, 0)),
            pl.BlockSpec((tile, LANES), lambda b, t: (t, 0)),
        ],
        out_specs=[
            pl.BlockSpec((1, tile, Q_W), row),
            pl.BlockSpec((1, tile, KV_W), row),
            pl.BlockSpec((1, tile, KV_W), row),
            pl.BlockSpec((1, tile, POOL_W), row),
        ],
        out_shape=[
            jax.ShapeDtypeStruct((B, S, Q_W), BF16),
            jax.ShapeDtypeStruct((B, S, KV_W), BF16),
            jax.ShapeDtypeStruct((B, S, KV_W), BF16),
            jax.ShapeDtypeStruct((B, S, POOL_W), F32),
        ],
        scratch_shapes=[
            pltpu.VMEM((D, 2 * KV_W), FP8),
            pltpu.VMEM((D, POOL_W), BF16),
            pltpu.VMEM((STAGE_SLOTS, D, STAGE_CHUNK), F32),
            pltpu.SemaphoreType.DMA((STAGE_SLOTS,)),
        ],
        compiler_params=_sequential_params(2),
        name="in_proj",
    )(x, mod, norm1_g, w_q, w_all, w_scale, qg256, kg256, ones_bd, cos_t, sin_t)


def _attn_kernel(sink_ref, q_ref, kp_ref, kc0_ref, kn_ref, vp_ref, vc0_ref, vn_ref,
                 kctx_ref, vctx_ref, bias_ref, o_ref, *, blocks_per_tile):
    t_idx = pl.program_id(1)
    n_tiles = pl.num_programs(1)
    k_all = jnp.concatenate([kp_ref[0], kc0_ref[0], kn_ref[0]], axis=0)
    v_all = jnp.concatenate([vp_ref[0], vc0_ref[0], vn_ref[0]], axis=0)
    k_ctx = kctx_ref[0]
    v_ctx = vctx_ref[0]
    col_amax = jnp.maximum(jnp.max(jnp.abs(v_all), axis=0, keepdims=True),
                           jnp.max(jnp.abs(v_ctx), axis=0, keepdims=True)).astype(F32)
    v_scale = _fp8_scale(jnp.max(col_amax, axis=1, keepdims=True))
    scale_row = jnp.broadcast_to(v_scale, (1, KV_W)).astype(BF16)
    v_all = v_all * scale_row
    v_ctx = v_ctx * scale_row
    one = jnp.ones((), BF16)
    lane = lax.broadcasted_iota(jnp.int32, (BLOCK, MXU_TILE), 1)
    in_head = [(lane >= kv * HEAD_DIM) & (lane < (kv + 1) * HEAD_DIM) for kv in range(N_KV_HEADS)]
    zero = jnp.zeros((), BF16)

    def with_ones(v, half):
        vl = lax.broadcasted_iota(jnp.int32, v.shape, 1)
        return jnp.where((vl < LANES) if half == 0 else (vl >= LANES), v, one).astype(FP8)

    v_ctx_pair = [with_ones(v_ctx, half) for half in range(2)]
    low_head = lax.broadcasted_iota(jnp.int32, (BLOCK, LANES), 1) < HEAD_DIM

    def block_operands(i):
        variant = jnp.int32(0)
        if i == blocks_per_tile - 1:
            variant = jnp.where(t_idx == n_tiles - 1, 2, variant)
        if i == 0:
            variant = jnp.where(t_idx == 0, 1, variant)
        v_win = v_all[i * BLOCK:(i + 3) * BLOCK]
        return dict(bias=bias_ref[variant], k_win=k_all[i * BLOCK:(i + 3) * BLOCK],
                    v_win_pair=[with_ones(v_win, half) for half in range(2)])

    def scores(blk, i, j):
        q_slab = q_ref[0, i * BLOCK:(i + 1) * BLOCK, j * MXU_TILE:(j + 1) * MXU_TILE]
        qz = jnp.concatenate([jnp.where(in_head[kv], q_slab, zero) for kv in range(N_KV_HEADS)], axis=0)
        return _dot_nt(qz, blk["k_win"]), _dot_nt(qz, k_ctx)

    def softmax_values(blk, i, j, s_win, s_ctx):
        p_win, p_ctx, sink_term = [], [], []
        for kv in range(N_KV_HEADS):
            rows = slice(kv * BLOCK, (kv + 1) * BLOCK)
            sw = s_win[rows] + blk["bias"]
            sc = s_ctx[rows]
            sink = sink_ref[kv * GROUP + j] * LOG2_E
            m = jnp.maximum(
                jnp.maximum(jnp.max(sw, axis=-1, keepdims=True),
                            jnp.max(sc, axis=-1, keepdims=True)), sink)
            p_win.append(jnp.exp2(sw - m).astype(FP8))
            p_ctx.append(jnp.exp2(sc - m).astype(FP8))
            sink_term.append(jnp.exp2(sink - m))
        for half in range(2):
            pw = jnp.concatenate(p_win[2 * half:2 * half + 2], axis=0)
            pc = jnp.concatenate(p_ctx[2 * half:2 * half + 2], axis=0)
            o = _dot(pw, blk["v_win_pair"][half]) + _dot(pc, v_ctx_pair[half])
            vals = o[:, half * LANES:(half + 1) * LANES]
            sums = o[:, (1 - half) * LANES:(2 - half) * LANES]
            out = []
            for r in range(2):
                rows = slice(r * BLOCK, (r + 1) * BLOCK)
                out.append(vals[rows] / ((sums[rows] + sink_term[2 * half + r]) * v_scale))
            c0 = j * MXU_TILE + half * LANES
            o_ref[0, i * BLOCK:(i + 1) * BLOCK, c0:c0 + LANES] = jnp.where(low_head, out[0], out[1]).astype(BF16)

    blocks = [block_operands(i) for i in range(blocks_per_tile)]
    items = [(i, j) for i in range(blocks_per_tile) for j in range(GROUP)]
    s_next = scores(blocks[0], *items[0])
    for idx, (i, j) in enumerate(items):
        s_cur = s_next
        if idx + 1 < len(items):
            ni, nj = items[idx + 1]
            s_next = scores(blocks[ni], ni, nj)
        softmax_values(blocks[i], i, j, *s_cur)


def _attn_call(sink, q, k, v, kc, vc, bias, tile):
    B, S, _ = q.shape
    bpt = tile // BLOCK
    nb = S // BLOCK
    C = kc.shape[1]
    cur = lambda b, t, *_: (b, t, 0)
    prev = lambda b, t, *_: (b, jnp.maximum(t * bpt - 1, 0), 0)
    nxt = lambda b, t, *_: (b, jnp.minimum((t + 1) * bpt, nb - 1), 0)
    per_b = lambda b, t, *_: (b, 0, 0)
    halo = (1, BLOCK, KV_W)
    center = (1, tile, KV_W)
    grid_spec = pltpu.PrefetchScalarGridSpec(
        num_scalar_prefetch=1,
        grid=(B, S // tile),
        in_specs=[
            pl.BlockSpec((1, tile, Q_W), cur),
            pl.BlockSpec(halo, prev), pl.BlockSpec(center, cur), pl.BlockSpec(halo, nxt),
            pl.BlockSpec(halo, prev), pl.BlockSpec(center, cur), pl.BlockSpec(halo, nxt),
            pl.BlockSpec((1, C, KV_W), per_b),
            pl.BlockSpec((1, C, KV_W), per_b),
            pl.BlockSpec((3, BLOCK, 3 * BLOCK), lambda b, t, *_: (0, 0, 0)),
        ],
        out_specs=pl.BlockSpec((1, tile, Q_W), cur),
    )
    return pl.pallas_call(
        functools.partial(_attn_kernel, blocks_per_tile=bpt),
        grid_spec=grid_spec,
        out_shape=jax.ShapeDtypeStruct((B, S, Q_W), BF16),
        compiler_params=_params(2),
        name="attn",
    )(sink, q, k, k, k, v, v, v, kc, vc, bias)


def _merge_kernel(x_ref, mod_ref, g1_ref, attn_ref, up_ref, uc_ref, un_ref,
                  wall_hbm, gb_ref, wap_hbm, wpool_ref, wo_hbm, o_ref,
                  uext_ref, wg_ref, wap_ref, wo_ref, col_stage, row_stage, col_sem, row_sem, *, seq, tile, sub):
    @pl.when((pl.program_id(0) == 0) & (pl.program_id(1) == 0))
    def _():
        _stage_weight_bf16(wall_hbm.at[:, pl.ds(2 * D_MODEL, 2 * D_MODEL)], wg_ref, col_stage, col_sem, axis=1)
        _stage_weight_bf16(wo_hbm, wo_ref, col_stage, col_sem, axis=1)

        def store_regrouped(kv, rows_f32):
            for j in range(GROUP):
                dst = (j * N_KV_HEADS + kv) * HEAD_DIM
                wap_ref[dst:dst + HEAD_DIM, :] = rows_f32[j * HEAD_DIM:(j + 1) * HEAD_DIM, :].astype(BF16)

        _stage_weight_bf16(wap_hbm, None, row_stage, row_sem, axis=0, store=store_regrouped)

    t_idx = pl.program_id(1)
    n_tiles = pl.num_programs(1)
    m = mod_ref[pl.ds(pl.program_id(0), 1), :]
    sh = m[:, 0:D_MODEL]
    gs = g1_ref[...] * (1.0 + m[:, D_MODEL:2 * D_MODEL])
    gate1 = m[:, 2 * D_MODEL:3 * D_MODEL]

    keep_prev = (t_idx > 0).astype(F32)
    keep_next = (t_idx < n_tiles - 1).astype(F32)
    uext_ref[0:POOL_HALO] = up_ref[0] * keep_prev
    uext_ref[POOL_HALO:POOL_HALO + tile] = uc_ref[0]
    uext_ref[POOL_HALO + tile:] = un_ref[0] * keep_next
    edge_rows = lax.broadcasted_iota(jnp.int32, (POOL_HALO, POOL_GROUP_W), 0)

    def attn_dot(r):
        return _dot(attn_ref[0, r * sub:(r + 1) * sub, :], wap_ref[...])

    def gate_dot(r):
        h = _rms_modulate(x_ref[0, r * sub:(r + 1) * sub, :], gs, sh).astype(BF16)
        return jax.nn.sigmoid(_dot(h, wg_ref[...]) + gb_ref[...])

    def pool_dot(r):
        r0 = r * sub
        pos_first = t_idx * tile + r0 + edge_rows
        pos_last = pos_first + (sub - POOL_HALO)
        diffs = []
        for g, w in enumerate(POOL_WINDOWS):
            cols = slice(g * POOL_GROUP_W, (g + 1) * POOL_GROUP_W)
            acc = uext_ref[pl.ds(r0, sub + 2 * POOL_HALO), cols]
            k = 1
            while k < w:
                acc = acc[:acc.shape[0] - k] + acc[k:]
                k *= 2
            win = acc[POOL_HALO - w // 2:POOL_HALO - w // 2 + sub]

            def inv_count(pos):
                return 1.0 / (jnp.minimum(pos + w // 2, seq) - jnp.maximum(pos - w // 2, 0)).astype(F32)

            pooled_avg = jnp.concatenate([
                win[:POOL_HALO] * inv_count(pos_first),
                win[POOL_HALO:sub - POOL_HALO] * (1.0 / w),
                win[sub - POOL_HALO:] * inv_count(pos_last)], axis=0)
            diffs.append((pooled_avg - uext_ref[pl.ds(r0 + POOL_HALO, sub), cols]).astype(BF16))
        return _dot(jnp.concatenate(diffs, axis=1), wpool_ref[...])

    def output(r, a, gates, p):
        rows = slice(r * sub, (r + 1) * sub)
        merged = (gates[:, :D_MODEL] * a + gates[:, D_MODEL:] * p).astype(BF16)
        o_ref[0, rows, :] = x_ref[0, rows, :] + gate1 * _dot(merged, wo_ref[...])

    n_sub = tile // sub
    a = [attn_dot(r) for r in range(n_sub)]
    p = [pool_dot(r) for r in range(n_sub)]
    gates = [gate_dot(r) for r in range(n_sub)]
    for r in range(n_sub):
        output(r, a[r], gates[r], p[r])


def _merge_call(x, mod, norm1_g, attn, u, w_all, gate_b, w_ap, w_pool, w_out, tile):
    B, S, D = x.shape
    r = tile // POOL_HALO
    n_halo = S // POOL_HALO
    row = lambda b, t: (b, t, 0)
    return pl.pallas_call(
        functools.partial(_merge_kernel, seq=S, tile=tile, sub=SUB_TILE),
        grid=(B, S // tile),
        in_specs=[
            pl.BlockSpec((1, tile, D), row),
            _const_spec(mod.shape),
            _const_spec((1, D)),
            pl.BlockSpec((1, tile, Q_W), row),
            pl.BlockSpec((1, POOL_HALO, POOL_W), lambda b, t: (b, jnp.maximum(t * r - 1, 0), 0)),
            pl.BlockSpec((1, tile, POOL_W), row),
            pl.BlockSpec((1, POOL_HALO, POOL_W), lambda b, t: (b, jnp.minimum((t + 1) * r, n_halo - 1), 0)),
            pl.BlockSpec(memory_space=pl.ANY),
            _const_spec(gate_b.shape),
            pl.BlockSpec(memory_space=pl.ANY),
            _const_spec(w_pool.shape),
            pl.BlockSpec(memory_space=pl.ANY),
        ],
        out_specs=pl.BlockSpec((1, tile, D), row),
        out_shape=jax.ShapeDtypeStruct((B, S, D), F32),
        scratch_shapes=[
            pltpu.VMEM((tile + 2 * POOL_HALO, POOL_W), F32),
            pltpu.VMEM((D, 2 * D), BF16),
            pltpu.VMEM(w_ap.shape, BF16),
            pltpu.VMEM(w_out.shape, BF16),
            pltpu.VMEM((STAGE_SLOTS, D, STAGE_CHUNK), F32),
            pltpu.VMEM((STAGE_SLOTS, STAGE_CHUNK, D), F32),
            pltpu.SemaphoreType.DMA((STAGE_SLOTS,)),
            pltpu.SemaphoreType.DMA((STAGE_SLOTS,)),
        ],
        compiler_params=_sequential_params(2),
        name="merge",
    )(x, mod, norm1_g, attn, u, u, u, w_all, gate_b, w_ap, w_pool, w_out)


def _ffn_kernel(x_ref, mod_ref, g2_ref, wup_hbm, wdn_hbm, o_ref,
                wup_ref, wdn_ref, col_stage, row_stage, col_sem, row_sem, hmid_ref, *, d_ff, tile, sub):
    @pl.when((pl.program_id(0) == 0) & (pl.program_id(1) == 0))
    def _():
        _stage_weight_bf16(wup_hbm, wup_ref, col_stage, col_sem, axis=1)
        _stage_weight_bf16(wdn_hbm, wdn_ref, row_stage, row_sem, axis=0)

    m = mod_ref[pl.ds(pl.program_id(0), 1), :]
    sh = m[:, 3 * D_MODEL:4 * D_MODEL]
    gs = g2_ref[...] * (1.0 + m[:, 4 * D_MODEL:5 * D_MODEL])
    gate2 = m[:, 5 * D_MODEL:6 * D_MODEL]

    def up(r):
        rows = slice(r * sub, (r + 1) * sub)
        h = _rms_modulate(x_ref[0, rows, :], gs, sh).astype(BF16)
        for c in range(d_ff // MXU_TILE):
            a = _dot(h, wup_ref[:, c * MXU_TILE:(c + 1) * MXU_TILE])
            b = _dot(h, wup_ref[:, d_ff + c * MXU_TILE:d_ff + (c + 1) * MXU_TILE])
            hmid_ref[rows, c * MXU_TILE:(c + 1) * MXU_TILE] = (a * jax.nn.sigmoid(a) * b).astype(BF16)

    def down(r):
        rows = slice(r * sub, (r + 1) * sub)
        o_ref[0, rows, :] = x_ref[0, rows, :] + gate2 * _dot(hmid_ref[rows, :], wdn_ref[...])

    n_sub = tile // sub
    up(0)
    for r in range(1, n_sub):
        up(r)
        down(r - 1)
    down(n_sub - 1)


def _ffn_call(x, mod, norm2_g, w_up, w_down, tile):
    B, S, D = x.shape
    d_ff = w_down.shape[0]
    row = lambda b, t: (b, t, 0)
    return pl.pallas_call(
        functools.partial(_ffn_kernel, d_ff=d_ff, tile=tile, sub=SUB_TILE),
        grid=(B, S // tile),
        in_specs=[
            pl.BlockSpec((1, tile, D), row),
            _const_spec(mod.shape),
            _const_spec((1, D)),
            pl.BlockSpec(memory_space=pl.ANY),
            pl.BlockSpec(memory_space=pl.ANY),
        ],
        out_specs=pl.BlockSpec((1, tile, D), row),
        out_shape=jax.ShapeDtypeStruct((B, S, D), F32),
        scratch_shapes=[
            pltpu.VMEM(w_up.shape, BF16),
            pltpu.VMEM(w_down.shape, BF16),
            pltpu.VMEM((STAGE_SLOTS, D, STAGE_CHUNK), F32),
            pltpu.VMEM((STAGE_SLOTS, STAGE_CHUNK, D), F32),
            pltpu.SemaphoreType.DMA((STAGE_SLOTS,)),
            pltpu.SemaphoreType.DMA((STAGE_SLOTS,)),
            pltpu.VMEM((tile, d_ff), BF16),
        ],
        compiler_params=_sequential_params(2),
        name="ffn",
    )(x, mod, norm2_g, w_up, w_down)


def _rope_tables(seq):
    t = np.arange(seq)
    row = (t // GRID_W).astype(np.float64)
    col = (t % GRID_W).astype(np.float64)
    half = HEAD_DIM // 2
    inv_freq = 1.0 / (ROPE_THETA ** (np.arange(0, half, 2, dtype=np.float64) / half))
    ang_r = row[:, None] * inv_freq
    ang_c = col[:, None] * inv_freq
    cos = np.concatenate([np.cos(ang_r)] * 2 + [np.cos(ang_c)] * 2, axis=1)
    sin = np.concatenate([-np.sin(ang_r), np.sin(ang_r), -np.sin(ang_c), np.sin(ang_c)], axis=1)
    reps = (1, LANES // HEAD_DIM)
    return jnp.asarray(np.tile(cos, reps), dtype=F32), jnp.asarray(np.tile(sin, reps), dtype=F32)


def _band_bias():
    i = np.arange(BLOCK)[:, None]
    j = np.arange(3 * BLOCK)[None, :]
    band = np.abs(j - BLOCK - i) <= WINDOW
    first = band & (j >= BLOCK)
    last = band & (j < 2 * BLOCK)
    masks = np.stack([band, first, last])
    return jnp.asarray(np.where(masks, 0.0, NEG_BIG), dtype=F32)


def kernel(x, c, ctx, c_ctx, mod_w, mod_b, norm1_g, norm2_g, w_in, gate_b, q_norm_g, k_norm_g,
           sink, pool_w, pool_scale, w_attn_proj, w_pool_proj, w_out, w_up, w_down):
    B, S, D = x.shape
    depth = mod_w.shape[0]
    cos_t, sin_t = _rope_tables(S)
    bias = _band_bias()
    ones_bd = jnp.asarray(np.kron(np.eye(MXU_TILE // HEAD_DIM), np.ones((HEAD_DIM, HEAD_DIM))), dtype=BF16)
    reps = MXU_TILE // HEAD_DIM
    assert depth == 1, "context-stream update between layers is not implemented"
    assert D == D_MODEL and S % ROW_TILE == 0 and B + 1 <= F32_SUBLANES

    for l in range(depth):
        c8 = jnp.concatenate([c, c_ctx[None, :], jnp.zeros((F32_SUBLANES - B - 1, D), F32)], axis=0)
        mod = _mod_call(c8, mod_w[l], mod_b[l][None, :])

        wl = w_in[l]
        w_scale = _fp8_scale(jnp.max(jnp.abs(wl[:, :Q_W + 2 * KV_W]))).reshape(1, 1)
        w_q = ((wl[:, :Q_W] * w_scale).reshape(D, N_KV_HEADS, GROUP, HEAD_DIM).transpose(0, 2, 1, 3)
               .reshape(D, Q_W).astype(FP8))
        g1 = norm1_g[l][None, :]
        kg256 = jnp.tile(k_norm_g[l], reps)[None, :]
        qg256 = jnp.tile(q_norm_g[l] * (HEAD_DIM ** -0.5 * LOG2_E), reps)[None, :]

        kc, vc = _ctx_call(ctx, mod, g1, wl, kg256, ones_bd)
        q, k, v, u = _inproj_call(x, mod, g1, w_q, wl, w_scale, qg256, kg256, ones_bd, cos_t, sin_t, tile=ROW_TILE)
        attn = _attn_call(sink[l], q, k, v, kc, vc, bias, tile=ROW_TILE)
        w_pool = _pool_fold_call(pool_w[l], pool_scale[l].reshape(POOL_GROUPS, 1, POOL_GROUP_W), w_pool_proj[l])
        x = _merge_call(
            x, mod, g1, attn, u, wl, gate_b[l][None, :], w_attn_proj[l], w_pool, w_out[l], tile=ROW_TILE)
        x = _ffn_call(x, mod, norm2_g[l][None, :], w_up[l], w_down[l], tile=ROW_TILE)
    return x
```

```python
import functools

import jax
import jax.numpy as jnp
import numpy as np
from jax import lax
from jax.experimental import pallas as pl
from jax.experimental.pallas import tpu as pltpu

D_MODEL = 1024
GRID_W = 64
HEAD_DIM = 64
N_HEADS = 16
N_KV_HEADS = 4
GROUP = N_HEADS // N_KV_HEADS
Q_W = N_HEADS * HEAD_DIM
KV_W = N_KV_HEADS * HEAD_DIM
WINDOW = 128
BLOCK = 128
ROPE_THETA = 10000.0
POOL_WINDOWS = (2, 4, 8, 16)
POOL_GROUPS = 4
POOL_W = D_MODEL // 2
POOL_GROUP_W = POOL_W // POOL_GROUPS
POOL_HALO = 8
ROPE_PAIR = HEAD_DIM // 4
EPS = 1e-6
NEG_BIG = -1e30
LOG2_E = 1.4426950408889634

LANES = 128
F32_SUBLANES = 8
MXU_TILE = 256
VMEM_LIMIT = 56 * 1024 * 1024
STAGE_SLOTS = 4
STAGE_CHUNK = MXU_TILE
ROW_TILE = 1024
SUB_TILE = MXU_TILE

F32 = jnp.float32
BF16 = jnp.bfloat16
FP8 = jnp.float8_e4m3fn
FP8_MAX = 448.0


def _params(n_parallel):
    return pltpu.CompilerParams(
        dimension_semantics=("parallel",) * n_parallel,
        vmem_limit_bytes=VMEM_LIMIT,
    )


def _const_spec(shape):
    nd = len(shape)
    return pl.BlockSpec(shape, lambda *_: (0,) * nd, pipeline_mode=pl.Buffered(1))


def _dot(a, b):
    return jnp.dot(a, b, preferred_element_type=F32)


def _dot_nt(a, b):
    return lax.dot_general(a, b, (((1,), (1,)), ((), ())), preferred_element_type=F32)


def _rms_modulate(x, gs, sh):
    rs = lax.rsqrt(jnp.mean(x * x, axis=-1, keepdims=True) + EPS)
    return x * rs * gs + sh


def _head_rms(t, ones_bd, unscale=1.0):
    ms = _dot((t * t).astype(BF16), ones_bd) * (unscale * unscale * (1.0 / HEAD_DIM))
    return lax.rsqrt(ms + EPS)


def _stage_weight_bf16(w_hbm, w_ref, stage_ref, sem, axis, store=None):
    slots = stage_ref.shape[0]
    chunk = stage_ref.shape[1 + axis]
    n = w_hbm.shape[axis] // chunk

    def piece(ref, i):
        return ref.at[:, pl.ds(i * chunk, chunk)] if axis == 1 else ref.at[pl.ds(i * chunk, chunk), :]

    def copy(i):
        return pltpu.make_async_copy(piece(w_hbm, i), stage_ref.at[i % slots], sem.at[i % slots])

    for i in range(min(slots - 1, n)):
        copy(i).start()
    for i in range(n):
        if i + slots - 1 < n:
            copy(i + slots - 1).start()
        copy(i).wait()
        staged = stage_ref[i % slots]
        if store is None:
            piece(w_ref, i)[...] = staged.astype(BF16)
        else:
            store(i, staged)


def _fp8_scale(amax):
    return jnp.exp2(jnp.floor(jnp.log2(FP8_MAX / jnp.maximum(amax, 1e-30))))


def _sequential_params(n_axes):
    return pltpu.CompilerParams(dimension_semantics=("arbitrary",) * n_axes, vmem_limit_bytes=VMEM_LIMIT)


def _mod_kernel(c_ref, w_ref, b_ref, o_ref):
    c = c_ref[...]
    a = (c * jax.nn.sigmoid(c)).astype(BF16)
    o_ref[...] = _dot(a, w_ref[...].astype(BF16)) + b_ref[...]


def _mod_call(c8, mod_w, mod_b):
    n = mod_w.shape[1]
    bn = 2 * D_MODEL
    return pl.pallas_call(
        _mod_kernel,
        grid=(n // bn,),
        in_specs=[
            pl.BlockSpec((F32_SUBLANES, D_MODEL), lambda j: (0, 0)),
            pl.BlockSpec((D_MODEL, bn), lambda j: (0, j)),
            pl.BlockSpec((1, bn), lambda j: (0, j)),
        ],
        out_specs=pl.BlockSpec((F32_SUBLANES, bn), lambda j: (0, j)),
        out_shape=jax.ShapeDtypeStruct((F32_SUBLANES, n), F32),
        compiler_params=_params(1),
        name="mod",
    )(c8, mod_w, mod_b)


def _pool_fold_kernel(pw_ref, ps_ref, wpp_ref, o_ref):
    scaled = (pw_ref[0] * ps_ref[0]).astype(BF16)
    o_ref[...] = _dot(scaled, wpp_ref[...].astype(BF16)).astype(BF16)


def _pool_fold_call(pool_w, pool_scale3, w_pool_proj):
    n_out = w_pool_proj.shape[1]
    return pl.pallas_call(
        _pool_fold_kernel,
        grid=(POOL_GROUPS,),
        in_specs=[
            pl.BlockSpec((1, POOL_GROUP_W, POOL_GROUP_W), lambda g: (g, 0, 0)),
            pl.BlockSpec((1, 1, POOL_GROUP_W), lambda g: (g, 0, 0)),
            pl.BlockSpec((POOL_GROUP_W, n_out), lambda g: (g, 0)),
        ],
        out_specs=pl.BlockSpec((POOL_GROUP_W, n_out), lambda g: (g, 0)),
        out_shape=jax.ShapeDtypeStruct((POOL_W, n_out), BF16),
        compiler_params=_params(1),
        name="pool_fold",
    )(pool_w, pool_scale3, w_pool_proj)


def _ctx_kernel(ctx_ref, mod_ref, g1_ref, wall_hbm, kg_ref, ones_ref, kc_ref, vc_ref,
                w_ref, col_stage, col_sem, *, n_batch):
    @pl.when(pl.program_id(0) == 0)
    def _():
        _stage_weight_bf16(wall_hbm.at[:, pl.ds(Q_W, 2 * KV_W)], w_ref, col_stage, col_sem, axis=1)

    m = mod_ref[n_batch:n_batch + 1, :]
    sh = m[:, 0:D_MODEL]
    gs = g1_ref[...] * (1.0 + m[:, D_MODEL:2 * D_MODEL])
    h = _rms_modulate(ctx_ref[0], gs, sh).astype(BF16)
    kv = _dot(h, w_ref[...])
    k = kv[:, :KV_W]
    kc_ref[0] = (k * _head_rms(k, ones_ref[...]) * kg_ref[...]).astype(BF16)
    vc_ref[0] = kv[:, KV_W:].astype(BF16)


def _ctx_call(ctx, mod, norm1_g, w_all, kg256, ones_bd):
    B, C, D = ctx.shape
    rows = B * C
    kc, vc = pl.pallas_call(
        functools.partial(_ctx_kernel, n_batch=B),
        grid=(1,),
        in_specs=[
            pl.BlockSpec((1, rows, D), lambda i: (0, 0, 0)),
            _const_spec(mod.shape),
            _const_spec((1, D)),
            pl.BlockSpec(memory_space=pl.ANY),
            _const_spec((1, KV_W)),
            _const_spec((MXU_TILE, MXU_TILE)),
        ],
        out_specs=[
            pl.BlockSpec((1, rows, KV_W), lambda i: (0, 0, 0)),
            pl.BlockSpec((1, rows, KV_W), lambda i: (0, 0, 0)),
        ],
        out_shape=[
            jax.ShapeDtypeStruct((1, rows, KV_W), BF16),
            jax.ShapeDtypeStruct((1, rows, KV_W), BF16),
        ],
        scratch_shapes=[
            pltpu.VMEM((D, 2 * KV_W), BF16),
            pltpu.VMEM((STAGE_SLOTS, D, STAGE_CHUNK), F32),
            pltpu.SemaphoreType.DMA((STAGE_SLOTS,)),
        ],
        compiler_params=_sequential_params(1),
        name="ctx_kv",
    )(ctx.reshape(1, rows, D), mod, norm1_g, w_all, kg256, ones_bd)
    return kc.reshape(B, C, KV_W), vc.reshape(B, C, KV_W)


def _rope(t, cos, sin_signed, first_half):
    swap = jnp.where(first_half, pltpu.roll(t, LANES - ROPE_PAIR, axis=1), pltpu.roll(t, ROPE_PAIR, axis=1))
    return t * cos + swap * sin_signed


def _inproj_kernel(x_ref, mod_ref, g1_ref, wq_ref, wall_hbm, wscale_ref, qg_ref, kg_ref, ones_ref, cos_ref,
                   sin_ref, q_ref, k_ref, v_ref, u_ref, wkv_ref, wu_ref, col_stage, col_sem):
    w_scale = wscale_ref[...]

    @pl.when((pl.program_id(0) == 0) & (pl.program_id(1) == 0))
    def _():
        def store_kv_fp8(i, cols_f32):
            wkv_ref[:, i * STAGE_CHUNK:(i + 1) * STAGE_CHUNK] = (cols_f32 * w_scale).astype(FP8)

        _stage_weight_bf16(wall_hbm.at[:, pl.ds(Q_W, 2 * KV_W)], None, col_stage, col_sem, axis=1,
                           store=store_kv_fp8)
        _stage_weight_bf16(wall_hbm.at[:, pl.ds(Q_W + 2 * KV_W, POOL_W)], wu_ref, col_stage, col_sem, axis=1)

    m = mod_ref[pl.ds(pl.program_id(0), 1), :]
    sh = m[:, 0:D_MODEL]
    gs = g1_ref[...] * (1.0 + m[:, D_MODEL:2 * D_MODEL])
    h32 = _rms_modulate(x_ref[0], gs, sh)
    h = h32.astype(BF16)
    h_scale = _fp8_scale(jnp.max(jnp.abs(h32), keepdims=True))
    h8 = (h32 * h_scale).astype(FP8)
    unscale = 1.0 / (h_scale * w_scale)
    ones_bd = ones_ref[...]
    cos = cos_ref[...]
    sin = sin_ref[...]
    lane = lax.broadcasted_iota(jnp.int32, cos.shape, 1)
    first_half = (lane & (2 * ROPE_PAIR - 1)) < ROPE_PAIR

    def normed_rope(t, gain):
        t = t * _head_rms(t, ones_bd, unscale) * (gain * unscale)
        halves = [_rope(t[:, i * LANES:(i + 1) * LANES], cos, sin, first_half) for i in range(2)]
        return jnp.concatenate(halves, axis=1).astype(BF16)

    wide = 2 * MXU_TILE
    for j in range(Q_W // wide):
        t = _dot(h8, wq_ref[:, j * wide:(j + 1) * wide])
        for i in range(2):
            c0 = j * wide + i * MXU_TILE
            q_ref[0, :, c0:c0 + MXU_TILE] = normed_rope(t[:, i * MXU_TILE:(i + 1) * MXU_TILE], qg_ref[...])
    kv = _dot(h8, wkv_ref[...])
    k_ref[0] = normed_rope(kv[:, :KV_W], kg_ref[...])
    v_ref[0] = (kv[:, KV_W:] * unscale).astype(BF16)
    u_ref[0] = _dot(h, wu_ref[...])


def _inproj_call(x, mod, norm1_g, w_q, w_all, w_scale, qg256, kg256, ones_bd, cos_t, sin_t, tile):
    B, S, D = x.shape
    row = lambda b, t: (b, t, 0)
    return pl.pallas_call(
        _inproj_kernel,
        grid=(B, S // tile),
        in_specs=[
            pl.BlockSpec((1, tile, D), row),
            _const_spec(mod.shape),
            _const_spec((1, D)),
            _const_spec((D, Q_W)),
            pl.BlockSpec(memory_space=pl.ANY),
            _const_spec((1, 1)),
            _const_spec((1, MXU_TILE)),
            _const_spec((1, MXU_TILE)),
            _const_spec((MXU_TILE, MXU_TILE)),
            pl.BlockSpec((tile, LANES), lambda b, t: (t, 0)),
            pl.BlockSpec((tile, LANES), lambda b, t: (t, 0)),
        ],
        out_specs=[
            pl.BlockSpec((1, tile, Q_W), row),
            pl.BlockSpec((1, tile, KV_W), row),
            pl.BlockSpec((1, tile, KV_W), row),
            pl.BlockSpec((1, tile, POOL_W), row),
        ],
        out_shape=[
            jax.ShapeDtypeStruct((B, S, Q_W), BF16),
            jax.ShapeDtypeStruct((B, S, KV_W), BF16),
            jax.ShapeDtypeStruct((B, S, KV_W), BF16),
            jax.ShapeDtypeStruct((B, S, POOL_W), F32),
        ],
        scratch_shapes=[
            pltpu.VMEM((D, 2 * KV_W), FP8),
            pltpu.VMEM((D, POOL_W), BF16),
            pltpu.VMEM((STAGE_SLOTS, D, STAGE_CHUNK), F32),
            pltpu.SemaphoreType.DMA((STAGE_SLOTS,)),
        ],
        compiler_params=_sequential_params(2),
        name="in_proj",
    )(x, mod, norm1_g, w_q, w_all, w_scale, qg256, kg256, ones_bd, cos_t, sin_t)


def _attn_kernel(sink_ref, q_ref, kp_ref, kc0_ref, kn_ref, vp_ref, vc0_ref, vn_ref,
                 kctx_ref, vctx_ref, bias_ref, o_ref, *, blocks_per_tile):
    t_idx = pl.program_id(1)
    n_tiles = pl.num_programs(1)
    k_all = jnp.concatenate([kp_ref[0], kc0_ref[0], kn_ref[0]], axis=0)
    v_all = jnp.concatenate([vp_ref[0], vc0_ref[0], vn_ref[0]], axis=0)
    k_ctx = kctx_ref[0]
    v_ctx = vctx_ref[0]
    col_amax = jnp.maximum(jnp.max(jnp.abs(v_all), axis=0, keepdims=True),
                           jnp.max(jnp.abs(v_ctx), axis=0, keepdims=True)).astype(F32)
    v_scale = _fp8_scale(jnp.max(col_amax, axis=1, keepdims=True))
    scale_row = jnp.broadcast_to(v_scale, (1, KV_W)).astype(BF16)
    v_all = v_all * scale_row
    v_ctx = v_ctx * scale_row
    one = jnp.ones((), BF16)
    lane = lax.broadcasted_iota(jnp.int32, (BLOCK, MXU_TILE), 1)
    in_head = [(lane >= kv * HEAD_DIM) & (lane < (kv + 1) * HEAD_DIM) for kv in range(N_KV_HEADS)]
    zero = jnp.zeros((), BF16)

    def with_ones(v, half):
        vl = lax.broadcasted_iota(jnp.int32, v.shape, 1)
        return jnp.where((vl < LANES) if half == 0 else (vl >= LANES), v, one).astype(FP8)

    v_ctx_pair = [with_ones(v_ctx, half) for half in range(2)]
    low_head = lax.broadcasted_iota(jnp.int32, (BLOCK, LANES), 1) < HEAD_DIM

    def block_operands(i):
        variant = jnp.int32(0)
        if i == blocks_per_tile - 1:
            variant = jnp.where(t_idx == n_tiles - 1, 2, variant)
        if i == 0:
            variant = jnp.where(t_idx == 0, 1, variant)
        v_win = v_all[i * BLOCK:(i + 3) * BLOCK]
        return dict(bias=bias_ref[variant], k_win=k_all[i * BLOCK:(i + 3) * BLOCK],
                    v_win_pair=[with_ones(v_win, half) for half in range(2)])

    def scores(blk, i, j):
        q_slab = q_ref[0, i * BLOCK:(i + 1) * BLOCK, j * MXU_TILE:(j + 1) * MXU_TILE]
        qz = jnp.concatenate([jnp.where(in_head[kv], q_slab, zero) for kv in range(N_KV_HEADS)], axis=0)
        return _dot_nt(qz, blk["k_win"]), _dot_nt(qz, k_ctx)

    def softmax_values(blk, i, j, s_win, s_ctx):
        p_win, p_ctx, sink_term = [], [], []
        for kv in range(N_KV_HEADS):
            rows = slice(kv * BLOCK, (kv + 1) * BLOCK)
            sw = s_win[rows] + blk["bias"]
            sc = s_ctx[rows]
            sink = sink_ref[kv * GROUP + j] * LOG2_E
            m = jnp.maximum(
                jnp.maximum(jnp.max(sw, axis=-1, keepdims=True),
                            jnp.max(sc, axis=-1, keepdims=True)), sink)
            p_win.append(jnp.exp2(sw - m).astype(FP8))
            p_ctx.append(jnp.exp2(sc - m).astype(FP8))
            sink_term.append(jnp.exp2(sink - m))
        for half in range(2):
            pw = jnp.concatenate(p_win[2 * half:2 * half + 2], axis=0)
            pc = jnp.concatenate(p_ctx[2 * half:2 * half + 2], axis=0)
            o = _dot(pw, blk["v_win_pair"][half]) + _dot(pc, v_ctx_pair[half])
            vals = o[:, half * LANES:(half + 1) * LANES]
            sums = o[:, (1 - half) * LANES:(2 - half) * LANES]
            out = []
            for r in range(2):
                rows = slice(r * BLOCK, (r + 1) * BLOCK)
                out.append(vals[rows] / ((sums[rows] + sink_term[2 * half + r]) * v_scale))
            c0 = j * MXU_TILE + half * LANES
            o_ref[0, i * BLOCK:(i + 1) * BLOCK, c0:c0 + LANES] = jnp.where(low_head, out[0], out[1]).astype(BF16)

    blocks = [block_operands(i) for i in range(blocks_per_tile)]
    items = [(i, j) for i in range(blocks_per_tile) for j in range(GROUP)]
    s_next = scores(blocks[0], *items[0])
    for idx, (i, j) in enumerate(items):
        s_cur = s_next
        if idx + 1 < len(items):
            ni, nj = items[idx + 1]
            s_next = scores(blocks[ni], ni, nj)
        softmax_values(blocks[i], i, j, *s_cur)


def _attn_call(sink, q, k, v, kc, vc, bias, tile):
    B, S, _ = q.shape
    bpt = tile // BLOCK
    nb = S // BLOCK
    C = kc.shape[1]
    cur = lambda b, t, *_: (b, t, 0)
    prev = lambda b, t, *_: (b, jnp.maximum(t * bpt - 1, 0), 0)
    nxt = lambda b, t, *_: (b, jnp.minimum((t + 1) * bpt, nb - 1), 0)
    per_b = lambda b, t, *_: (b, 0, 0)
    halo = (1, BLOCK, KV_W)
    center = (1, tile, KV_W)
    grid_spec = pltpu.PrefetchScalarGridSpec(
        num_scalar_prefetch=1,
        grid=(B, S // tile),
        in_specs=[
            pl.BlockSpec((1, tile, Q_W), cur),
            pl.BlockSpec(halo, prev), pl.BlockSpec(center, cur), pl.BlockSpec(halo, nxt),
            pl.BlockSpec(halo, prev), pl.BlockSpec(center, cur), pl.BlockSpec(halo, nxt),
            pl.BlockSpec((1, C, KV_W), per_b),
            pl.BlockSpec((1, C, KV_W), per_b),
            pl.BlockSpec((3, BLOCK, 3 * BLOCK), lambda b, t, *_: (0, 0, 0)),
        ],
        out_specs=pl.BlockSpec((1, tile, Q_W), cur),
    )
    return pl.pallas_call(
        functools.partial(_attn_kernel, blocks_per_tile=bpt),
        grid_spec=grid_spec,
        out_shape=jax.ShapeDtypeStruct((B, S, Q_W), BF16),
        compiler_params=_params(2),
        name="attn",
    )(sink, q, k, k, k, v, v, v, kc, vc, bias)


def _merge_kernel(x_ref, mod_ref, g1_ref, attn_ref, up_ref, uc_ref, un_ref,
                  wall_hbm, wgscale_ref, gb_ref, wap_hbm, wpool_ref, wo_hbm, o_ref,
                  uext_ref, wg_ref, wap_ref, wo_ref, col_stage, row_stage, col_sem, row_sem, *, seq, tile, sub):
    wg_scale = wgscale_ref[...]

    @pl.when((pl.program_id(0) == 0) & (pl.program_id(1) == 0))
    def _():
        def store_gate_fp8(i, cols_f32):
            wg_ref[:, i * STAGE_CHUNK:(i + 1) * STAGE_CHUNK] = (cols_f32 * wg_scale).astype(FP8)

        _stage_weight_bf16(wall_hbm.at[:, pl.ds(2 * D_MODEL, 2 * D_MODEL)], None, col_stage, col_sem, axis=1,
                           store=store_gate_fp8)
        _stage_weight_bf16(wo_hbm, wo_ref, col_stage, col_sem, axis=1)

        def store_regrouped(kv, rows_f32):
            for j in range(GROUP):
                dst = (j * N_KV_HEADS + kv) * HEAD_DIM
                wap_ref[dst:dst + HEAD_DIM, :] = rows_f32[j * HEAD_DIM:(j + 1) * HEAD_DIM, :].astype(BF16)

        _stage_weight_bf16(wap_hbm, None, row_stage, row_sem, axis=0, store=store_regrouped)

    t_idx = pl.program_id(1)
    n_tiles = pl.num_programs(1)
    m = mod_ref[pl.ds(pl.program_id(0), 1), :]
    sh = m[:, 0:D_MODEL]
    gs = g1_ref[...] * (1.0 + m[:, D_MODEL:2 * D_MODEL])
    gate1 = m[:, 2 * D_MODEL:3 * D_MODEL]

    keep_prev = (t_idx > 0).astype(F32)
    keep_next = (t_idx < n_tiles - 1).astype(F32)
    uext_ref[0:POOL_HALO] = up_ref[0] * keep_prev
    uext_ref[POOL_HALO:POOL_HALO + tile] = uc_ref[0]
    uext_ref[POOL_HALO + tile:] = un_ref[0] * keep_next
    edge_rows = lax.broadcasted_iota(jnp.int32, (POOL_HALO, POOL_GROUP_W), 0)

    def attn_dot(r):
        return _dot(attn_ref[0, r * sub:(r + 1) * sub, :], wap_ref[...])

    def gate_dot(r):
        h32 = _rms_modulate(x_ref[0, r * sub:(r + 1) * sub, :], gs, sh)
        h_scale = _fp8_scale(jnp.max(jnp.abs(h32), keepdims=True))
        logits = _dot((h32 * h_scale).astype(FP8), wg_ref[...]) * (1.0 / (h_scale * wg_scale))
        return jax.nn.sigmoid(logits + gb_ref[...])

    def pool_dot(r):
        r0 = r * sub
        pos_first = t_idx * tile + r0 + edge_rows
        pos_last = pos_first + (sub - POOL_HALO)
        diffs = []
        for g, w in enumerate(POOL_WINDOWS):
            cols = slice(g * POOL_GROUP_W, (g + 1) * POOL_GROUP_W)
            acc = uext_ref[pl.ds(r0, sub + 2 * POOL_HALO), cols]
            k = 1
            while k < w:
                acc = acc[:acc.shape[0] - k] + acc[k:]
                k *= 2
            win = acc[POOL_HALO - w // 2:POOL_HALO - w // 2 + sub]

            def inv_count(pos):
                return 1.0 / (jnp.minimum(pos + w // 2, seq) - jnp.maximum(pos - w // 2, 0)).astype(F32)

            pooled_avg = jnp.concatenate([
                win[:POOL_HALO] * inv_count(pos_first),
                win[POOL_HALO:sub - POOL_HALO] * (1.0 / w),
                win[sub - POOL_HALO:] * inv_count(pos_last)], axis=0)
            diffs.append((pooled_avg - uext_ref[pl.ds(r0 + POOL_HALO, sub), cols]).astype(BF16))
        return _dot(jnp.concatenate(diffs, axis=1), wpool_ref[...])

    def output(r, a, gates, p):
        rows = slice(r * sub, (r + 1) * sub)
        merged = (gates[:, :D_MODEL] * a + gates[:, D_MODEL:] * p).astype(BF16)
        o_ref[0, rows, :] = x_ref[0, rows, :] + gate1 * _dot(merged, wo_ref[...])

    n_sub = tile // sub
    a = [attn_dot(r) for r in range(n_sub)]
    p = [pool_dot(r) for r in range(n_sub)]
    gates = [gate_dot(r) for r in range(n_sub)]
    for r in range(n_sub):
        output(r, a[r], gates[r], p[r])


def _merge_call(x, mod, norm1_g, attn, u, w_all, wg_scale, gate_b, w_ap, w_pool, w_out, tile):
    B, S, D = x.shape
    r = tile // POOL_HALO
    n_halo = S // POOL_HALO
    row = lambda b, t: (b, t, 0)
    return pl.pallas_call(
        functools.partial(_merge_kernel, seq=S, tile=tile, sub=SUB_TILE),
        grid=(B, S // tile),
        in_specs=[
            pl.BlockSpec((1, tile, D), row),
            _const_spec(mod.shape),
            _const_spec((1, D)),
            pl.BlockSpec((1, tile, Q_W), row),
            pl.BlockSpec((1, POOL_HALO, POOL_W), lambda b, t: (b, jnp.maximum(t * r - 1, 0), 0)),
            pl.BlockSpec((1, tile, POOL_W), row),
            pl.BlockSpec((1, POOL_HALO, POOL_W), lambda b, t: (b, jnp.minimum((t + 1) * r, n_halo - 1), 0)),
            pl.BlockSpec(memory_space=pl.ANY),
            _const_spec((1, 1)),
            _const_spec(gate_b.shape),
            pl.BlockSpec(memory_space=pl.ANY),
            _const_spec(w_pool.shape),
            pl.BlockSpec(memory_space=pl.ANY),
        ],
        out_specs=pl.BlockSpec((1, tile, D), row),
        out_shape=jax.ShapeDtypeStruct((B, S, D), F32),
        scratch_shapes=[
            pltpu.VMEM((tile + 2 * POOL_HALO, POOL_W), F32),
            pltpu.VMEM((D, 2 * D), FP8),
            pltpu.VMEM(w_ap.shape, BF16),
            pltpu.VMEM(w_out.shape, BF16),
            pltpu.VMEM((STAGE_SLOTS, D, STAGE_CHUNK), F32),
            pltpu.VMEM((STAGE_SLOTS, STAGE_CHUNK, D), F32),
            pltpu.SemaphoreType.DMA((STAGE_SLOTS,)),
            pltpu.SemaphoreType.DMA((STAGE_SLOTS,)),
        ],
        compiler_params=_sequential_params(2),
        name="merge",
    )(x, mod, norm1_g, attn, u, u, u, w_all, wg_scale, gate_b, w_ap, w_pool, w_out)


def _ffn_kernel(x_ref, mod_ref, g2_ref, wup_hbm, wdn_hbm, o_ref,
                wup_ref, wdn_ref, col_stage, row_stage, col_sem, row_sem, hmid_ref, *, d_ff, tile, sub):
    @pl.when((pl.program_id(0) == 0) & (pl.program_id(1) == 0))
    def _():
        _stage_weight_bf16(wup_hbm, wup_ref, col_stage, col_sem, axis=1)
        _stage_weight_bf16(wdn_hbm, wdn_ref, row_stage, row_sem, axis=0)

    m = mod_ref[pl.ds(pl.program_id(0), 1), :]
    sh = m[:, 3 * D_MODEL:4 * D_MODEL]
    gs = g2_ref[...] * (1.0 + m[:, 4 * D_MODEL:5 * D_MODEL])
    gate2 = m[:, 5 * D_MODEL:6 * D_MODEL]

    def up(r):
        rows = slice(r * sub, (r + 1) * sub)
        h = _rms_modulate(x_ref[0, rows, :], gs, sh).astype(BF16)
        for c in range(d_ff // MXU_TILE):
            a = _dot(h, wup_ref[:, c * MXU_TILE:(c + 1) * MXU_TILE])
            b = _dot(h, wup_ref[:, d_ff + c * MXU_TILE:d_ff + (c + 1) * MXU_TILE])
            hmid_ref[rows, c * MXU_TILE:(c + 1) * MXU_TILE] = (a * jax.nn.sigmoid(a) * b).astype(BF16)

    def down(r):
        rows = slice(r * sub, (r + 1) * sub)
        o_ref[0, rows, :] = x_ref[0, rows, :] + gate2 * _dot(hmid_ref[rows, :], wdn_ref[...])

    n_sub = tile // sub
    up(0)
    for r in range(1, n_sub):
        up(r)
        down(r - 1)
    down(n_sub - 1)


def _ffn_call(x, mod, norm2_g, w_up, w_down, tile):
    B, S, D = x.shape
    d_ff = w_down.shape[0]
    row = lambda b, t: (b, t, 0)
    return pl.pallas_call(
        functools.partial(_ffn_kernel, d_ff=d_ff, tile=tile, sub=SUB_TILE),
        grid=(B, S // tile),
        in_specs=[
            pl.BlockSpec((1, tile, D), row),
            _const_spec(mod.shape),
            _const_spec((1, D)),
            pl.BlockSpec(memory_space=pl.ANY),
            pl.BlockSpec(memory_space=pl.ANY),
        ],
        out_specs=pl.BlockSpec((1, tile, D), row),
        out_shape=jax.ShapeDtypeStruct((B, S, D), F32),
        scratch_shapes=[
            pltpu.VMEM(w_up.shape, BF16),
            pltpu.VMEM(w_down.shape, BF16),
            pltpu.VMEM((STAGE_SLOTS, D, STAGE_CHUNK), F32),
            pltpu.VMEM((STAGE_SLOTS, STAGE_CHUNK, D), F32),
            pltpu.SemaphoreType.DMA((STAGE_SLOTS,)),
            pltpu.SemaphoreType.DMA((STAGE_SLOTS,)),
            pltpu.VMEM((tile, d_ff), BF16),
        ],
        compiler_params=_sequential_params(2),
        name="ffn",
    )(x, mod, norm2_g, w_up, w_down)


def _rope_tables(seq):
    t = np.arange(seq)
    row = (t // GRID_W).astype(np.float64)
    col = (t % GRID_W).astype(np.float64)
    half = HEAD_DIM // 2
    inv_freq = 1.0 / (ROPE_THETA ** (np.arange(0, half, 2, dtype=np.float64) / half))
    ang_r = row[:, None] * inv_freq
    ang_c = col[:, None] * inv_freq
    cos = np.concatenate([np.cos(ang_r)] * 2 + [np.cos(ang_c)] * 2, axis=1)
    sin = np.concatenate([-np.sin(ang_r), np.sin(ang_r), -np.sin(ang_c), np.sin(ang_c)], axis=1)
    reps = (1, LANES // HEAD_DIM)
    return jnp.asarray(np.tile(cos, reps), dtype=F32), jnp.asarray(np.tile(sin, reps), dtype=F32)


def _band_bias():
    i = np.arange(BLOCK)[:, None]
    j = np.arange(3 * BLOCK)[None, :]
    band = np.abs(j - BLOCK - i) <= WINDOW
    first = band & (j >= BLOCK)
    last = band & (j < 2 * BLOCK)
    masks = np.stack([band, first, last])
    return jnp.asarray(np.where(masks, 0.0, NEG_BIG), dtype=F32)


def kernel(x, c, ctx, c_ctx, mod_w, mod_b, norm1_g, norm2_g, w_in, gate_b, q_norm_g, k_norm_g,
           sink, pool_w, pool_scale, w_attn_proj, w_pool_proj, w_out, w_up, w_down):
    B, S, D = x.shape
    depth = mod_w.shape[0]
    cos_t, sin_t = _rope_tables(S)
    bias = _band_bias()
    ones_bd = jnp.asarray(np.kron(np.eye(MXU_TILE // HEAD_DIM), np.ones((HEAD_DIM, HEAD_DIM))), dtype=BF16)
    reps = MXU_TILE // HEAD_DIM
    assert depth == 1, "context-stream update between layers is not implemented"
    assert D == D_MODEL and S % ROW_TILE == 0 and B + 1 <= F32_SUBLANES

    for l in range(depth):
        c8 = jnp.concatenate([c, c_ctx[None, :], jnp.zeros((F32_SUBLANES - B - 1, D), F32)], axis=0)
        mod = _mod_call(c8, mod_w[l], mod_b[l][None, :])

        wl = w_in[l]
        w_scale = _fp8_scale(jnp.max(jnp.abs(wl[:, :Q_W + 2 * KV_W]))).reshape(1, 1)
        w_q = ((wl[:, :Q_W] * w_scale).reshape(D, N_KV_HEADS, GROUP, HEAD_DIM).transpose(0, 2, 1, 3)
               .reshape(D, Q_W).astype(FP8))
        g1 = norm1_g[l][None, :]
        kg256 = jnp.tile(k_norm_g[l], reps)[None, :]
        qg256 = jnp.tile(q_norm_g[l] * (HEAD_DIM ** -0.5 * LOG2_E), reps)[None, :]

        kc, vc = _ctx_call(ctx, mod, g1, wl, kg256, ones_bd)
        q, k, v, u = _inproj_call(x, mod, g1, w_q, wl, w_scale, qg256, kg256, ones_bd, cos_t, sin_t, tile=ROW_TILE)
        attn = _attn_call(sink[l], q, k, v, kc, vc, bias, tile=ROW_TILE)
        w_pool = _pool_fold_call(pool_w[l], pool_scale[l].reshape(POOL_GROUPS, 1, POOL_GROUP_W), w_pool_proj[l])
        wg_scale = _fp8_scale(jnp.max(jnp.abs(wl[:, 2 * D:]))).reshape(1, 1)
        x = _merge_call(
            x, mod, g1, attn, u, wl, wg_scale, gate_b[l][None, :], w_attn_proj[l], w_pool, w_out[l], tile=ROW_TILE)
        x = _ffn_call(x, mod, norm2_g[l][None, :], w_up[l], w_down[l], tile=ROW_TILE)
    return x
```

```python
import functools

import jax
import jax.numpy as jnp
import numpy as np
from jax import lax
from jax.experimental import pallas as pl
from jax.experimental.pallas import tpu as pltpu

D_MODEL = 1024
GRID_W = 64
HEAD_DIM = 64
N_HEADS = 16
N_KV_HEADS = 4
GROUP = N_HEADS // N_KV_HEADS
Q_W = N_HEADS * HEAD_DIM
KV_W = N_KV_HEADS * HEAD_DIM
WINDOW = 128
BLOCK = 128
ROPE_THETA = 10000.0
POOL_WINDOWS = (2, 4, 8, 16)
POOL_GROUPS = 4
POOL_W = D_MODEL // 2
POOL_GROUP_W = POOL_W // POOL_GROUPS
POOL_HALO = 8
ROPE_PAIR = HEAD_DIM // 4
EPS = 1e-6
NEG_BIG = -1e30
LOG2_E = 1.4426950408889634

LANES = 128
F32_SUBLANES = 8
MXU_TILE = 256
VMEM_LIMIT = 56 * 1024 * 1024
STAGE_SLOTS = 4
STAGE_CHUNK = MXU_TILE
ROW_TILE = 1024
SUB_TILE = MXU_TILE

F32 = jnp.float32
BF16 = jnp.bfloat16
FP8 = jnp.float8_e4m3fn
FP8_MAX = 448.0


def _params(n_parallel):
    return pltpu.CompilerParams(
        dimension_semantics=("parallel",) * n_parallel,
        vmem_limit_bytes=VMEM_LIMIT,
    )


def _const_spec(shape):
    nd = len(shape)
    return pl.BlockSpec(shape, lambda *_: (0,) * nd, pipeline_mode=pl.Buffered(1))


def _dot(a, b):
    return jnp.dot(a, b, preferred_element_type=F32)


def _dot_nt(a, b):
    return lax.dot_general(a, b, (((1,), (1,)), ((), ())), preferred_element_type=F32)


def _rms_modulate(x, gs, sh):
    rs = lax.rsqrt(jnp.mean(x * x, axis=-1, keepdims=True) + EPS)
    return x * rs * gs + sh


def _head_rms(t, ones_bd, unscale=1.0):
    ms = _dot((t * t).astype(BF16), ones_bd) * (unscale * unscale * (1.0 / HEAD_DIM))
    return lax.rsqrt(ms + EPS)


def _stage_weight_bf16(w_hbm, w_ref, stage_ref, sem, axis, store=None):
    slots = stage_ref.shape[0]
    chunk = stage_ref.shape[1 + axis]
    n = w_hbm.shape[axis] // chunk

    def piece(ref, i):
        return ref.at[:, pl.ds(i * chunk, chunk)] if axis == 1 else ref.at[pl.ds(i * chunk, chunk), :]

    def copy(i):
        return pltpu.make_async_copy(piece(w_hbm, i), stage_ref.at[i % slots], sem.at[i % slots])

    for i in range(min(slots - 1, n)):
        copy(i).start()
    for i in range(n):
        if i + slots - 1 < n:
            copy(i + slots - 1).start()
        copy(i).wait()
        staged = stage_ref[i % slots]
        if store is None:
            piece(w_ref, i)[...] = staged.astype(BF16)
        else:
            store(i, staged)


def _fp8_scale(amax):
    return jnp.exp2(jnp.floor(jnp.log2(FP8_MAX / jnp.maximum(amax, 1e-30))))


def _sequential_params(n_axes):
    return pltpu.CompilerParams(dimension_semantics=("arbitrary",) * n_axes, vmem_limit_bytes=VMEM_LIMIT)


def _mod_kernel(c_ref, w_ref, b_ref, o_ref):
    c = c_ref[...]
    a = (c * jax.nn.sigmoid(c)).astype(BF16)
    o_ref[...] = _dot(a, w_ref[...].astype(BF16)) + b_ref[...]


def _mod_call(c8, mod_w, mod_b):
    n = mod_w.shape[1]
    bn = 2 * D_MODEL
    return pl.pallas_call(
        _mod_kernel,
        grid=(n // bn,),
        in_specs=[
            pl.BlockSpec((F32_SUBLANES, D_MODEL), lambda j: (0, 0)),
            pl.BlockSpec((D_MODEL, bn), lambda j: (0, j)),
            pl.BlockSpec((1, bn), lambda j: (0, j)),
        ],
        out_specs=pl.BlockSpec((F32_SUBLANES, bn), lambda j: (0, j)),
        out_shape=jax.ShapeDtypeStruct((F32_SUBLANES, n), F32),
        compiler_params=_params(1),
        name="mod",
    )(c8, mod_w, mod_b)


def _pool_fold_kernel(pw_ref, ps_ref, wpp_ref, o_ref):
    scaled = (pw_ref[0] * ps_ref[0]).astype(BF16)
    o_ref[...] = _dot(scaled, wpp_ref[...].astype(BF16)).astype(BF16)


def _pool_fold_call(pool_w, pool_scale3, w_pool_proj):
    n_out = w_pool_proj.shape[1]
    return pl.pallas_call(
        _pool_fold_kernel,
        grid=(POOL_GROUPS,),
        in_specs=[
            pl.BlockSpec((1, POOL_GROUP_W, POOL_GROUP_W), lambda g: (g, 0, 0)),
            pl.BlockSpec((1, 1, POOL_GROUP_W), lambda g: (g, 0, 0)),
            pl.BlockSpec((POOL_GROUP_W, n_out), lambda g: (g, 0)),
        ],
        out_specs=pl.BlockSpec((POOL_GROUP_W, n_out), lambda g: (g, 0)),
        out_shape=jax.ShapeDtypeStruct((POOL_W, n_out), BF16),
        compiler_params=_params(1),
        name="pool_fold",
    )(pool_w, pool_scale3, w_pool_proj)


def _ctx_kernel(ctx_ref, mod_ref, g1_ref, wall_hbm, kg_ref, ones_ref, kc_ref, vc_ref,
                w_ref, col_stage, col_sem, *, n_batch):
    @pl.when(pl.program_id(0) == 0)
    def _():
        _stage_weight_bf16(wall_hbm.at[:, pl.ds(Q_W, 2 * KV_W)], w_ref, col_stage, col_sem, axis=1)

    m = mod_ref[n_batch:n_batch + 1, :]
    sh = m[:, 0:D_MODEL]
    gs = g1_ref[...] * (1.0 + m[:, D_MODEL:2 * D_MODEL])
    h = _rms_modulate(ctx_ref[0], gs, sh).astype(BF16)
    kv = _dot(h, w_ref[...])
    k = kv[:, :KV_W]
    kc_ref[0] = (k * _head_rms(k, ones_ref[...]) * kg_ref[...]).astype(BF16)
    vc_ref[0] = kv[:, KV_W:].astype(BF16)


def _ctx_call(ctx, mod, norm1_g, w_all, kg256, ones_bd):
    B, C, D = ctx.shape
    rows = B * C
    kc, vc = pl.pallas_call(
        functools.partial(_ctx_kernel, n_batch=B),
        grid=(1,),
        in_specs=[
            pl.BlockSpec((1, rows, D), lambda i: (0, 0, 0)),
            _const_spec(mod.shape),
            _const_spec((1, D)),
            pl.BlockSpec(memory_space=pl.ANY),
            _const_spec((1, KV_W)),
            _const_spec((MXU_TILE, MXU_TILE)),
        ],
        out_specs=[
            pl.BlockSpec((1, rows, KV_W), lambda i: (0, 0, 0)),
            pl.BlockSpec((1, rows, KV_W), lambda i: (0, 0, 0)),
        ],
        out_shape=[
            jax.ShapeDtypeStruct((1, rows, KV_W), BF16),
            jax.ShapeDtypeStruct((1, rows, KV_W), BF16),
        ],
        scratch_shapes=[
            pltpu.VMEM((D, 2 * KV_W), BF16),
            pltpu.VMEM((STAGE_SLOTS, D, STAGE_CHUNK), F32),
            pltpu.SemaphoreType.DMA((STAGE_SLOTS,)),
        ],
        compiler_params=_sequential_params(1),
        name="ctx_kv",
    )(ctx.reshape(1, rows, D), mod, norm1_g, w_all, kg256, ones_bd)
    return kc.reshape(B, C, KV_W), vc.reshape(B, C, KV_W)


def _rope(t, cos, sin_signed, first_half):
    swap = jnp.where(first_half, pltpu.roll(t, LANES - ROPE_PAIR, axis=1), pltpu.roll(t, ROPE_PAIR, axis=1))
    return t * cos + swap * sin_signed


def _inproj_kernel(x_ref, mod_ref, g1_ref, wq_ref, wall_hbm, wscale_ref, qg_ref, kg_ref, ones_ref, cos_ref,
                   sin_ref, q_ref, k_ref, v_ref, u_ref, wkv_ref, wu_ref, col_stage, col_sem):
    w_scale = wscale_ref[...]

    @pl.when((pl.program_id(0) == 0) & (pl.program_id(1) == 0))
    def _():
        def store_kv_fp8(i, cols_f32):
            wkv_ref[:, i * STAGE_CHUNK:(i + 1) * STAGE_CHUNK] = (cols_f32 * w_scale).astype(FP8)

        _stage_weight_bf16(wall_hbm.at[:, pl.ds(Q_W, 2 * KV_W)], None, col_stage, col_sem, axis=1,
                           store=store_kv_fp8)
        _stage_weight_bf16(wall_hbm.at[:, pl.ds(Q_W + 2 * KV_W, POOL_W)], wu_ref, col_stage, col_sem, axis=1)

    m = mod_ref[pl.ds(pl.program_id(0), 1), :]
    sh = m[:, 0:D_MODEL]
    gs = g1_ref[...] * (1.0 + m[:, D_MODEL:2 * D_MODEL])
    h32 = _rms_modulate(x_ref[0], gs, sh)
    h = h32.astype(BF16)
    h_scale = _fp8_scale(jnp.max(jnp.abs(h32), keepdims=True))
    h8 = (h32 * h_scale).astype(FP8)
    unscale = 1.0 / (h_scale * w_scale)
    ones_bd = ones_ref[...]
    cos = cos_ref[...]
    sin = sin_ref[...]
    lane = lax.broadcasted_iota(jnp.int32, cos.shape, 1)
    first_half = (lane & (2 * ROPE_PAIR - 1)) < ROPE_PAIR

    def normed_rope(t, gain):
        t = t * _head_rms(t, ones_bd, unscale) * (gain * unscale)
        halves = [_rope(t[:, i * LANES:(i + 1) * LANES], cos, sin, first_half) for i in range(2)]
        return jnp.concatenate(halves, axis=1).astype(BF16)

    wide = 2 * MXU_TILE
    for j in range(Q_W // wide):
        t = _dot(h8, wq_ref[:, j * wide:(j + 1) * wide])
        for i in range(2):
            c0 = j * wide + i * MXU_TILE
            q_ref[0, :, c0:c0 + MXU_TILE] = normed_rope(t[:, i * MXU_TILE:(i + 1) * MXU_TILE], qg_ref[...])
    kv = _dot(h8, wkv_ref[...])
    k_ref[0] = normed_rope(kv[:, :KV_W], kg_ref[...])
    v_ref[0] = (kv[:, KV_W:] * unscale).astype(BF16)
    u_ref[0] = _dot(h, wu_ref[...])


def _inproj_call(x, mod, norm1_g, w_q, w_all, w_scale, qg256, kg256, ones_bd, cos_t, sin_t, tile):
    B, S, D = x.shape
    row = lambda b, t: (b, t, 0)
    return pl.pallas_call(
        _inproj_kernel,
        grid=(B, S // tile),
        in_specs=[
            pl.BlockSpec((1, tile, D), row),
            _const_spec(mod.shape),
            _const_spec((1, D)),
            _const_spec((D, Q_W)),
            pl.BlockSpec(memory_space=pl.ANY),
            _const_spec((1, 1)),
            _const_spec((1, MXU_TILE)),
            _const_spec((1, MXU_TILE)),
            _const_spec((MXU_TILE, MXU_TILE)),
            pl.BlockSpec((tile, LANES), lambda b, t: (t, 0)),
            pl.BlockSpec((tile, LANES), lambda b, t: (t, 0)),
        ],
        out_specs=[
            pl.BlockSpec((1, tile, Q_W), row),
            pl.BlockSpec((1, tile, KV_W), row),
            pl.BlockSpec((1, tile, KV_W), row),
            pl.BlockSpec((1, tile, POOL_W), row),
        ],
        out_shape=[
            jax.ShapeDtypeStruct((B, S, Q_W), BF16),
            jax.ShapeDtypeStruct((B, S, KV_W), BF16),
            jax.ShapeDtypeStruct((B, S, KV_W), BF16),
            jax.ShapeDtypeStruct((B, S, POOL_W), F32),
        ],
        scratch_shapes=[
            pltpu.VMEM((D, 2 * KV_W), FP8),
            pltpu.VMEM((D, POOL_W), BF16),
            pltpu.VMEM((STAGE_SLOTS, D, STAGE_CHUNK), F32),
            pltpu.SemaphoreType.DMA((STAGE_SLOTS,)),
        ],
        compiler_params=_sequential_params(2),
        name="in_proj",
    )(x, mod, norm1_g, w_q, w_all, w_scale, qg256, kg256, ones_bd, cos_t, sin_t)


def _attn_kernel(sink_ref, q_ref, kp_ref, kc0_ref, kn_ref, vp_ref, vc0_ref, vn_ref,
                 kctx_ref, vctx_ref, bias_ref, o_ref, *, blocks_per_tile):
    t_idx = pl.program_id(1)
    n_tiles = pl.num_programs(1)
    k_all = jnp.concatenate([kp_ref[0], kc0_ref[0], kn_ref[0]], axis=0)
    v_all = jnp.concatenate([vp_ref[0], vc0_ref[0], vn_ref[0]], axis=0)
    k_ctx = kctx_ref[0]
    v_ctx = vctx_ref[0]
    col_amax = jnp.maximum(jnp.max(jnp.abs(v_all), axis=0, keepdims=True),
                           jnp.max(jnp.abs(v_ctx), axis=0, keepdims=True)).astype(F32)
    v_scale = _fp8_scale(jnp.max(col_amax, axis=1, keepdims=True))
    scale_row = jnp.broadcast_to(v_scale, (1, KV_W)).astype(BF16)
    v_all = v_all * scale_row
    v_ctx = v_ctx * scale_row
    one = jnp.ones((), BF16)
    lane = lax.broadcasted_iota(jnp.int32, (BLOCK, MXU_TILE), 1)
    in_head = [(lane >= kv * HEAD_DIM) & (lane < (kv + 1) * HEAD_DIM) for kv in range(N_KV_HEADS)]
    zero = jnp.zeros((), BF16)

    def with_ones(v, half):
        vl = lax.broadcasted_iota(jnp.int32, v.shape, 1)
        return jnp.where((vl < LANES) if half == 0 else (vl >= LANES), v, one).astype(FP8)

    v_ctx_pair = [with_ones(v_ctx, half) for half in range(2)]
    low_head = lax.broadcasted_iota(jnp.int32, (BLOCK, LANES), 1) < HEAD_DIM

    def block_operands(i):
        variant = jnp.int32(0)
        if i == blocks_per_tile - 1:
            variant = jnp.where(t_idx == n_tiles - 1, 2, variant)
        if i == 0:
            variant = jnp.where(t_idx == 0, 1, variant)
        v_win = v_all[i * BLOCK:(i + 3) * BLOCK]
        return dict(bias=bias_ref[variant], k_win=k_all[i * BLOCK:(i + 3) * BLOCK],
                    v_win_pair=[with_ones(v_win, half) for half in range(2)])

    def scores(blk, i, j):
        q_slab = q_ref[0, i * BLOCK:(i + 1) * BLOCK, j * MXU_TILE:(j + 1) * MXU_TILE]
        qz = jnp.concatenate([jnp.where(in_head[kv], q_slab, zero) for kv in range(N_KV_HEADS)], axis=0)
        return _dot_nt(qz, blk["k_win"]), _dot_nt(qz, k_ctx)

    def softmax_values(blk, i, j, s_win, s_ctx):
        p_win, p_ctx, sink_term = [], [], []
        for kv in range(N_KV_HEADS):
            rows = slice(kv * BLOCK, (kv + 1) * BLOCK)
            sw = jnp.concatenate([
                s_win[rows, :BLOCK] + blk["bias"][:, :BLOCK],
                s_win[rows, BLOCK:2 * BLOCK],
                s_win[rows, 2 * BLOCK:] + blk["bias"][:, 2 * BLOCK:]], axis=1)
            sc = s_ctx[rows]
            sink = sink_ref[kv * GROUP + j] * LOG2_E
            groups = [sw[:, g * LANES:(g + 1) * LANES] for g in range(sw.shape[1] // LANES)]
            groups += [sc[:, g * LANES:(g + 1) * LANES] for g in range(sc.shape[1] // LANES)]
            m = jnp.maximum(jnp.max(functools.reduce(jnp.maximum, groups), axis=-1, keepdims=True), sink)
            p_win.append(jnp.exp2(sw - m).astype(FP8))
            p_ctx.append(jnp.exp2(sc - m).astype(FP8))
            sink_term.append(jnp.exp2(sink - m))
        for half in range(2):
            pw = jnp.concatenate(p_win[2 * half:2 * half + 2], axis=0)
            pc = jnp.concatenate(p_ctx[2 * half:2 * half + 2], axis=0)
            o = _dot(pw, blk["v_win_pair"][half]) + _dot(pc, v_ctx_pair[half])
            vals = o[:, half * LANES:(half + 1) * LANES]
            sums = o[:, (1 - half) * LANES:(2 - half) * LANES]
            out = []
            for r in range(2):
                rows = slice(r * BLOCK, (r + 1) * BLOCK)
                out.append(vals[rows] / ((sums[rows] + sink_term[2 * half + r]) * v_scale))
            c0 = j * MXU_TILE + half * LANES
            o_ref[0, i * BLOCK:(i + 1) * BLOCK, c0:c0 + LANES] = jnp.where(low_head, out[0], out[1]).astype(BF16)

    blocks = [block_operands(i) for i in range(blocks_per_tile)]
    items = [(i, j) for i in range(blocks_per_tile) for j in range(GROUP)]
    s_next = scores(blocks[0], *items[0])
    for idx, (i, j) in enumerate(items):
        s_cur = s_next
        if idx + 1 < len(items):
            ni, nj = items[idx + 1]
            s_next = scores(blocks[ni], ni, nj)
        softmax_values(blocks[i], i, j, *s_cur)


def _attn_call(sink, q, k, v, kc, vc, bias, tile):
    B, S, _ = q.shape
    bpt = tile // BLOCK
    nb = S // BLOCK
    C = kc.shape[1]
    cur = lambda b, t, *_: (b, t, 0)
    prev = lambda b, t, *_: (b, jnp.maximum(t * bpt - 1, 0), 0)
    nxt = lambda b, t, *_: (b, jnp.minimum((t + 1) * bpt, nb - 1), 0)
    per_b = lambda b, t, *_: (b, 0, 0)
    halo = (1, BLOCK, KV_W)
    center = (1, tile, KV_W)
    grid_spec = pltpu.PrefetchScalarGridSpec(
        num_scalar_prefetch=1,
        grid=(B, S // tile),
        in_specs=[
            pl.BlockSpec((1, tile, Q_W), cur),
            pl.BlockSpec(halo, prev), pl.BlockSpec(center, cur), pl.BlockSpec(halo, nxt),
            pl.BlockSpec(halo, prev), pl.BlockSpec(center, cur), pl.BlockSpec(halo, nxt),
            pl.BlockSpec((1, C, KV_W), per_b),
            pl.BlockSpec((1, C, KV_W), per_b),
            pl.BlockSpec((3, BLOCK, 3 * BLOCK), lambda b, t, *_: (0, 0, 0)),
        ],
        out_specs=pl.BlockSpec((1, tile, Q_W), cur),
    )
    return pl.pallas_call(
        functools.partial(_attn_kernel, blocks_per_tile=bpt),
        grid_spec=grid_spec,
        out_shape=jax.ShapeDtypeStruct((B, S, Q_W), BF16),
        compiler_params=_params(2),
        name="attn",
    )(sink, q, k, k, k, v, v, v, kc, vc, bias)


def _merge_kernel(x_ref, mod_ref, g1_ref, attn_ref, up_ref, uc_ref, un_ref,
                  wall_hbm, gb_ref, wap_hbm, wpool_ref, wo_hbm, o_ref,
                  uext_ref, wg_ref, wap_ref, wo_ref, col_stage, row_stage, col_sem, row_sem, *, seq, tile, sub):
    @pl.when((pl.program_id(0) == 0) & (pl.program_id(1) == 0))
    def _():
        _stage_weight_bf16(wall_hbm.at[:, pl.ds(2 * D_MODEL, 2 * D_MODEL)], wg_ref, col_stage, col_sem, axis=1)
        _stage_weight_bf16(wo_hbm, wo_ref, col_stage, col_sem, axis=1)

        def store_regrouped(kv, rows_f32):
            for j in range(GROUP):
                dst = (j * N_KV_HEADS + kv) * HEAD_DIM
                wap_ref[dst:dst + HEAD_DIM, :] = rows_f32[j * HEAD_DIM:(j + 1) * HEAD_DIM, :].astype(BF16)

        _stage_weight_bf16(wap_hbm, None, row_stage, row_sem, axis=0, store=store_regrouped)

    t_idx = pl.program_id(1)
    n_tiles = pl.num_programs(1)
    m = mod_ref[pl.ds(pl.program_id(0), 1), :]
    sh = m[:, 0:D_MODEL]
    gs = g1_ref[...] * (1.0 + m[:, D_MODEL:2 * D_MODEL])
    gate1 = m[:, 2 * D_MODEL:3 * D_MODEL]

    keep_prev = (t_idx > 0).astype(F32)
    keep_next = (t_idx < n_tiles - 1).astype(F32)
    uext_ref[0:POOL_HALO] = up_ref[0] * keep_prev
    uext_ref[POOL_HALO:POOL_HALO + tile] = uc_ref[0]
    uext_ref[POOL_HALO + tile:] = un_ref[0] * keep_next
    edge_rows = lax.broadcasted_iota(jnp.int32, (POOL_HALO, POOL_GROUP_W), 0)

    def attn_dot(r):
        return _dot(attn_ref[0, r * sub:(r + 1) * sub, :], wap_ref[...])

    def gate_dot(r):
        h = _rms_modulate(x_ref[0, r * sub:(r + 1) * sub, :], gs, sh).astype(BF16)
        return jax.nn.sigmoid(_dot(h, wg_ref[...]) + gb_ref[...])

    def pool_dot(r):
        r0 = r * sub
        pos_first = t_idx * tile + r0 + edge_rows
        pos_last = pos_first + (sub - POOL_HALO)
        diffs = []
        for g, w in enumerate(POOL_WINDOWS):
            cols = slice(g * POOL_GROUP_W, (g + 1) * POOL_GROUP_W)
            acc = uext_ref[pl.ds(r0, sub + 2 * POOL_HALO), cols]
            k = 1
            while k < w:
                acc = acc[:acc.shape[0] - k] + acc[k:]
                k *= 2
            win = acc[POOL_HALO - w // 2:POOL_HALO - w // 2 + sub]

            def inv_count(pos):
                return 1.0 / (jnp.minimum(pos + w // 2, seq) - jnp.maximum(pos - w // 2, 0)).astype(F32)

            pooled_avg = jnp.concatenate([
                win[:POOL_HALO] * inv_count(pos_first),
                win[POOL_HALO:sub - POOL_HALO] * (1.0 / w),
                win[sub - POOL_HALO:] * inv_count(pos_last)], axis=0)
            diffs.append((pooled_avg - uext_ref[pl.ds(r0 + POOL_HALO, sub), cols]).astype(BF16))
        return _dot(jnp.concatenate(diffs, axis=1), wpool_ref[...])

    def output(r, a, gates, p):
        rows = slice(r * sub, (r + 1) * sub)
        merged = (gates[:, :D_MODEL] * a + gates[:, D_MODEL:] * p).astype(BF16)
        o_ref[0, rows, :] = x_ref[0, rows, :] + gate1 * _dot(merged, wo_ref[...])

    n_sub = tile // sub
    a = [attn_dot(r) for r in range(n_sub)]
    p = [pool_dot(r) for r in range(n_sub)]
    gates = [gate_dot(r) for r in range(n_sub)]
    for r in range(n_sub):
        output(r, a[r], gates[r], p[r])


def _merge_call(x, mod, norm1_g, attn, u, w_all, gate_b, w_ap, w_pool, w_out, tile):
    B, S, D = x.shape
    r = tile // POOL_HALO
    n_halo = S // POOL_HALO
    row = lambda b, t: (b, t, 0)
    return pl.pallas_call(
        functools.partial(_merge_kernel, seq=S, tile=tile, sub=SUB_TILE),
        grid=(B, S // tile),
        in_specs=[
            pl.BlockSpec((1, tile, D), row),
            _const_spec(mod.shape),
            _const_spec((1, D)),
            pl.BlockSpec((1, tile, Q_W), row),
            pl.BlockSpec((1, POOL_HALO, POOL_W), lambda b, t: (b, jnp.maximum(t * r - 1, 0), 0)),
            pl.BlockSpec((1, tile, POOL_W), row),
            pl.BlockSpec((1, POOL_HALO, POOL_W), lambda b, t: (b, jnp.minimum((t + 1) * r, n_halo - 1), 0)),
            pl.BlockSpec(memory_space=pl.ANY),
            _const_spec(gate_b.shape),
            pl.BlockSpec(memory_space=pl.ANY),
            _const_spec(w_pool.shape),
            pl.BlockSpec(memory_space=pl.ANY),
        ],
        out_specs=pl.BlockSpec((1, tile, D), row),
        out_shape=jax.ShapeDtypeStruct((B, S, D), F32),
        scratch_shapes=[
            pltpu.VMEM((tile + 2 * POOL_HALO, POOL_W), F32),
            pltpu.VMEM((D, 2 * D), BF16),
            pltpu.VMEM(w_ap.shape, BF16),
            pltpu.VMEM(w_out.shape, BF16),
            pltpu.VMEM((STAGE_SLOTS, D, STAGE_CHUNK), F32),
            pltpu.VMEM((STAGE_SLOTS, STAGE_CHUNK, D), F32),
            pltpu.SemaphoreType.DMA((STAGE_SLOTS,)),
            pltpu.SemaphoreType.DMA((STAGE_SLOTS,)),
        ],
        compiler_params=_sequential_params(2),
        name="merge",
    )(x, mod, norm1_g, attn, u, u, u, w_all, gate_b, w_ap, w_pool, w_out)


def _ffn_kernel(x_ref, mod_ref, g2_ref, wup_hbm, wdn_hbm, o_ref,
                wup_ref, wdn_ref, col_stage, row_stage, col_sem, row_sem, hmid_ref, *, d_ff, tile, sub):
    @pl.when((pl.program_id(0) == 0) & (pl.program_id(1) == 0))
    def _():
        _stage_weight_bf16(wup_hbm, wup_ref, col_stage, col_sem, axis=1)
        _stage_weight_bf16(wdn_hbm, wdn_ref, row_stage, row_sem, axis=0)

    m = mod_ref[pl.ds(pl.program_id(0), 1), :]
    sh = m[:, 3 * D_MODEL:4 * D_MODEL]
    gs = g2_ref[...] * (1.0 + m[:, 4 * D_MODEL:5 * D_MODEL])
    gate2 = m[:, 5 * D_MODEL:6 * D_MODEL]

    def up(r):
        rows = slice(r * sub, (r + 1) * sub)
        h = _rms_modulate(x_ref[0, rows, :], gs, sh).astype(BF16)
        for c in range(d_ff // MXU_TILE):
            a = _dot(h, wup_ref[:, c * MXU_TILE:(c + 1) * MXU_TILE])
            b = _dot(h, wup_ref[:, d_ff + c * MXU_TILE:d_ff + (c + 1) * MXU_TILE])
            hmid_ref[rows, c * MXU_TILE:(c + 1) * MXU_TILE] = (a * jax.nn.sigmoid(a) * b).astype(BF16)

    def down(r):
        rows = slice(r * sub, (r + 1) * sub)
        o_ref[0, rows, :] = x_ref[0, rows, :] + gate2 * _dot(hmid_ref[rows, :], wdn_ref[...])

    n_sub = tile // sub
    up(0)
    for r in range(1, n_sub):
        up(r)
        down(r - 1)
    down(n_sub - 1)


def _ffn_call(x, mod, norm2_g, w_up, w_down, tile):
    B, S, D = x.shape
    d_ff = w_down.shape[0]
    row = lambda b, t: (b, t, 0)
    return pl.pallas_call(
        functools.partial(_ffn_kernel, d_ff=d_ff, tile=tile, sub=SUB_TILE),
        grid=(B, S // tile),
        in_specs=[
            pl.BlockSpec((1, tile, D), row),
            _const_spec(mod.shape),
            _const_spec((1, D)),
            pl.BlockSpec(memory_space=pl.ANY),
            pl.BlockSpec(memory_space=pl.ANY),
        ],
        out_specs=pl.BlockSpec((1, tile, D), row),
        out_shape=jax.ShapeDtypeStruct((B, S, D), F32),
        scratch_shapes=[
            pltpu.VMEM(w_up.shape, BF16),
            pltpu.VMEM(w_down.shape, BF16),
            pltpu.VMEM((STAGE_SLOTS, D, STAGE_CHUNK), F32),
            pltpu.VMEM((STAGE_SLOTS, STAGE_CHUNK, D), F32),
            pltpu.SemaphoreType.DMA((STAGE_SLOTS,)),
            pltpu.SemaphoreType.DMA((STAGE_SLOTS,)),
            pltpu.VMEM((tile, d_ff), BF16),
        ],
        compiler_params=_sequential_params(2),
        name="ffn",
    )(x, mod, norm2_g, w_up, w_down)


def _rope_tables(seq):
    t = np.arange(seq)
    row = (t // GRID_W).astype(np.float64)
    col = (t % GRID_W).astype(np.float64)
    half = HEAD_DIM // 2
    inv_freq = 1.0 / (ROPE_THETA ** (np.arange(0, half, 2, dtype=np.float64) / half))
    ang_r = row[:, None] * inv_freq
    ang_c = col[:, None] * inv_freq
    cos = np.concatenate([np.cos(ang_r)] * 2 + [np.cos(ang_c)] * 2, axis=1)
    sin = np.concatenate([-np.sin(ang_r), np.sin(ang_r), -np.sin(ang_c), np.sin(ang_c)], axis=1)
    reps = (1, LANES // HEAD_DIM)
    return jnp.asarray(np.tile(cos, reps), dtype=F32), jnp.asarray(np.tile(sin, reps), dtype=F32)


def _band_bias():
    i = np.arange(BLOCK)[:, None]
    j = np.arange(3 * BLOCK)[None, :]
    band = np.abs(j - BLOCK - i) <= WINDOW
    first = band & (j >= BLOCK)
    last = band & (j < 2 * BLOCK)
    masks = np.stack([band, first, last])
    return jnp.asarray(np.where(masks, 0.0, NEG_BIG), dtype=F32)


def kernel(x, c, ctx, c_ctx, mod_w, mod_b, norm1_g, norm2_g, w_in, gate_b, q_norm_g, k_norm_g,
           sink, pool_w, pool_scale, w_attn_proj, w_pool_proj, w_out, w_up, w_down):
    B, S, D = x.shape
    depth = mod_w.shape[0]
    cos_t, sin_t = _rope_tables(S)
    bias = _band_bias()
    ones_bd = jnp.asarray(np.kron(np.eye(MXU_TILE // HEAD_DIM), np.ones((HEAD_DIM, HEAD_DIM))), dtype=BF16)
    reps = MXU_TILE // HEAD_DIM
    assert depth == 1, "context-stream update between layers is not implemented"
    assert D == D_MODEL and S % ROW_TILE == 0 and B + 1 <= F32_SUBLANES

    for l in range(depth):
        c8 = jnp.concatenate([c, c_ctx[None, :], jnp.zeros((F32_SUBLANES - B - 1, D), F32)], axis=0)
        mod = _mod_call(c8, mod_w[l], mod_b[l][None, :])

        wl = w_in[l]
        w_scale = _fp8_scale(jnp.max(jnp.abs(wl[:, :Q_W + 2 * KV_W]))).reshape(1, 1)
        w_q = ((wl[:, :Q_W] * w_scale).reshape(D, N_KV_HEADS, GROUP, HEAD_DIM).transpose(0, 2, 1, 3)
               .reshape(D, Q_W).astype(FP8))
        g1 = norm1_g[l][None, :]
        kg256 = jnp.tile(k_norm_g[l], reps)[None, :]
        qg256 = jnp.tile(q_norm_g[l] * (HEAD_DIM ** -0.5 * LOG2_E), reps)[None, :]

        kc, vc = _ctx_call(ctx, mod, g1, wl, kg256, ones_bd)
        q, k, v, u = _inproj_call(x, mod, g1, w_q, wl, w_scale, qg256, kg256, ones_bd, cos_t, sin_t, tile=ROW_TILE)
        attn = _attn_call(sink[l], q, k, v, kc, vc, bias, tile=ROW_TILE)
        w_pool = _pool_fold_call(pool_w[l], pool_scale[l].reshape(POOL_GROUPS, 1, POOL_GROUP_W), w_pool_proj[l])
        x = _merge_call(
            x, mod, g1, attn, u, wl, gate_b[l][None, :], w_attn_proj[l], w_pool, w_out[l], tile=ROW_TILE)
        x = _ffn_call(x, mod, norm2_g[l][None, :], w_up[l], w_down[l], tile=ROW_TILE)
    return x
```

```python
import functools

import jax
import jax.numpy as jnp
import numpy as np
from jax import lax
from jax.experimental import pallas as pl
from jax.experimental.pallas import tpu as pltpu

D_MODEL = 1024
GRID_W = 64
HEAD_DIM = 64
N_HEADS = 16
N_KV_HEADS = 4
GROUP = N_HEADS // N_KV_HEADS
Q_W = N_HEADS * HEAD_DIM
KV_W = N_KV_HEADS * HEAD_DIM
WINDOW = 128
BLOCK = 128
ROPE_THETA = 10000.0
POOL_WINDOWS = (2, 4, 8, 16)
POOL_GROUPS = 4
POOL_W = D_MODEL // 2
POOL_GROUP_W = POOL_W // POOL_GROUPS
POOL_HALO = 8
ROPE_PAIR = HEAD_DIM // 4
EPS = 1e-6
NEG_BIG = -1e30
LOG2_E = 1.4426950408889634

LANES = 128
F32_SUBLANES = 8
MXU_TILE = 256
VMEM_LIMIT = 56 * 1024 * 1024
STAGE_SLOTS = 4
STAGE_CHUNK = MXU_TILE
ROW_TILE = 1024
SUB_TILE = MXU_TILE

F32 = jnp.float32
BF16 = jnp.bfloat16
FP8 = jnp.float8_e4m3fn
FP8_MAX = 448.0


def _params(n_parallel):
    return pltpu.CompilerParams(
        dimension_semantics=("parallel",) * n_parallel,
        vmem_limit_bytes=VMEM_LIMIT,
    )


def _const_spec(shape):
    nd = len(shape)
    return pl.BlockSpec(shape, lambda *_: (0,) * nd, pipeline_mode=pl.Buffered(1))


def _dot(a, b):
    return jnp.dot(a, b, preferred_element_type=F32)


def _dot_nt(a, b):
    return lax.dot_general(a, b, (((1,), (1,)), ((), ())), preferred_element_type=F32)


def _rms_modulate(x, gs, sh):
    rs = lax.rsqrt(jnp.mean(x * x, axis=-1, keepdims=True) + EPS)
    return x * rs * gs + sh


def _head_rms(t, ones_bd, unscale=1.0):
    ms = _dot((t * t).astype(BF16), ones_bd) * (unscale * unscale * (1.0 / HEAD_DIM))
    return lax.rsqrt(ms + EPS)


def _stage_weight_bf16(w_hbm, w_ref, stage_ref, sem, axis, store=None):
    slots = stage_ref.shape[0]
    chunk = stage_ref.shape[1 + axis]
    n = w_hbm.shape[axis] // chunk

    def piece(ref, i):
        return ref.at[:, pl.ds(i * chunk, chunk)] if axis == 1 else ref.at[pl.ds(i * chunk, chunk), :]

    def copy(i):
        return pltpu.make_async_copy(piece(w_hbm, i), stage_ref.at[i % slots], sem.at[i % slots])

    for i in range(min(slots - 1, n)):
        copy(i).start()
    for i in range(n):
        if i + slots - 1 < n:
            copy(i + slots - 1).start()
        copy(i).wait()
        staged = stage_ref[i % slots]
        if store is None:
            piece(w_ref, i)[...] = staged.astype(BF16)
        else:
            store(i, staged)


def _fp8_scale(amax):
    return jnp.exp2(jnp.floor(jnp.log2(FP8_MAX / jnp.maximum(amax, 1e-30))))


def _sequential_params(n_axes):
    return pltpu.CompilerParams(dimension_semantics=("arbitrary",) * n_axes, vmem_limit_bytes=VMEM_LIMIT)


def _mod_kernel(c_ref, w_ref, b_ref, o_ref):
    c = c_ref[...]
    a = (c * jax.nn.sigmoid(c)).astype(BF16)
    o_ref[...] = _dot(a, w_ref[...].astype(BF16)) + b_ref[...]


def _mod_call(c8, mod_w, mod_b):
    n = mod_w.shape[1]
    bn = 2 * D_MODEL
    return pl.pallas_call(
        _mod_kernel,
        grid=(n // bn,),
        in_specs=[
            pl.BlockSpec((F32_SUBLANES, D_MODEL), lambda j: (0, 0)),
            pl.BlockSpec((D_MODEL, bn), lambda j: (0, j)),
            pl.BlockSpec((1, bn), lambda j: (0, j)),
        ],
        out_specs=pl.BlockSpec((F32_SUBLANES, bn), lambda j: (0, j)),
        out_shape=jax.ShapeDtypeStruct((F32_SUBLANES, n), F32),
        compiler_params=_params(1),
        name="mod",
    )(c8, mod_w, mod_b)


def _ctx_kernel(ctx_ref, mod_ref, g1_ref, wall_hbm, kg_ref, ones_ref, kc_ref, vc_ref,
                w_ref, col_stage, col_sem, *, n_batch):
    @pl.when(pl.program_id(0) == 0)
    def _():
        _stage_weight_bf16(wall_hbm.at[:, pl.ds(Q_W, 2 * KV_W)], w_ref, col_stage, col_sem, axis=1)

    m = mod_ref[n_batch:n_batch + 1, :]
    sh = m[:, 0:D_MODEL]
    gs = g1_ref[...] * (1.0 + m[:, D_MODEL:2 * D_MODEL])
    h = _rms_modulate(ctx_ref[0], gs, sh).astype(BF16)
    kv = _dot(h, w_ref[...])
    k = kv[:, :KV_W]
    kc_ref[0] = (k * _head_rms(k, ones_ref[...]) * kg_ref[...]).astype(BF16)
    vc_ref[0] = kv[:, KV_W:].astype(BF16)


def _ctx_call(ctx, mod, norm1_g, w_all, kg256, ones_bd):
    B, C, D = ctx.shape
    rows = B * C
    kc, vc = pl.pallas_call(
        functools.partial(_ctx_kernel, n_batch=B),
        grid=(1,),
        in_specs=[
            pl.BlockSpec((1, rows, D), lambda i: (0, 0, 0)),
            _const_spec(mod.shape),
            _const_spec((1, D)),
            pl.BlockSpec(memory_space=pl.ANY),
            _const_spec((1, KV_W)),
            _const_spec((MXU_TILE, MXU_TILE)),
        ],
        out_specs=[
            pl.BlockSpec((1, rows, KV_W), lambda i: (0, 0, 0)),
            pl.BlockSpec((1, rows, KV_W), lambda i: (0, 0, 0)),
        ],
        out_shape=[
            jax.ShapeDtypeStruct((1, rows, KV_W), BF16),
            jax.ShapeDtypeStruct((1, rows, KV_W), BF16),
        ],
        scratch_shapes=[
            pltpu.VMEM((D, 2 * KV_W), BF16),
            pltpu.VMEM((STAGE_SLOTS, D, STAGE_CHUNK), F32),
            pltpu.SemaphoreType.DMA((STAGE_SLOTS,)),
        ],
        compiler_params=_sequential_params(1),
        name="ctx_kv",
    )(ctx.reshape(1, rows, D), mod, norm1_g, w_all, kg256, ones_bd)
    return kc.reshape(B, C, KV_W), vc.reshape(B, C, KV_W)


def _rope(t, cos, sin_signed, first_half):
    swap = jnp.where(first_half, pltpu.roll(t, LANES - ROPE_PAIR, axis=1), pltpu.roll(t, ROPE_PAIR, axis=1))
    return t * cos + swap * sin_signed


def _inproj_kernel(x_ref, mod_ref, g1_ref, wq_ref, wall_hbm, wscale_ref, qg_ref, kg_ref, ones_ref, cos_ref,
                   sin_ref, q_ref, k_ref, v_ref, u_ref, wkv_ref, wu_ref, col_stage, col_sem):
    w_scale = wscale_ref[...]

    @pl.when((pl.program_id(0) == 0) & (pl.program_id(1) == 0))
    def _():
        def store_kv_fp8(i, cols_f32):
            wkv_ref[:, i * STAGE_CHUNK:(i + 1) * STAGE_CHUNK] = (cols_f32 * w_scale).astype(FP8)

        _stage_weight_bf16(wall_hbm.at[:, pl.ds(Q_W, 2 * KV_W)], None, col_stage, col_sem, axis=1,
                           store=store_kv_fp8)
        _stage_weight_bf16(wall_hbm.at[:, pl.ds(Q_W + 2 * KV_W, POOL_W)], wu_ref, col_stage, col_sem, axis=1)

    m = mod_ref[pl.ds(pl.program_id(0), 1), :]
    sh = m[:, 0:D_MODEL]
    gs = g1_ref[...] * (1.0 + m[:, D_MODEL:2 * D_MODEL])
    h32 = _rms_modulate(x_ref[0], gs, sh)
    h = h32.astype(BF16)
    h_scale = _fp8_scale(jnp.max(jnp.abs(h32), keepdims=True))
    h8 = (h32 * h_scale).astype(FP8)
    unscale = 1.0 / (h_scale * w_scale)
    ones_bd = ones_ref[...]
    cos = cos_ref[...]
    sin = sin_ref[...]
    lane = lax.broadcasted_iota(jnp.int32, cos.shape, 1)
    first_half = (lane & (2 * ROPE_PAIR - 1)) < ROPE_PAIR

    def normed_rope(t, gain):
        t = t * _head_rms(t, ones_bd, unscale) * (gain * unscale)
        halves = [_rope(t[:, i * LANES:(i + 1) * LANES], cos, sin, first_half) for i in range(2)]
        return jnp.concatenate(halves, axis=1).astype(BF16)

    wide = 2 * MXU_TILE
    for j in range(Q_W // wide):
        t = _dot(h8, wq_ref[:, j * wide:(j + 1) * wide])
        for i in range(2):
            c0 = j * wide + i * MXU_TILE
            q_ref[0, :, c0:c0 + MXU_TILE] = normed_rope(t[:, i * MXU_TILE:(i + 1) * MXU_TILE], qg_ref[...])
    kv = _dot(h8, wkv_ref[...])
    k_ref[0] = normed_rope(kv[:, :KV_W], kg_ref[...])
    v_ref[0] = (kv[:, KV_W:] * unscale).astype(BF16)
    u_ref[0] = _dot(h, wu_ref[...])


def _inproj_call(x, mod, norm1_g, w_q, w_all, w_scale, qg256, kg256, ones_bd, cos_t, sin_t, tile):
    B, S, D = x.shape
    row = lambda b, t: (b, t, 0)
    return pl.pallas_call(
        _inproj_kernel,
        grid=(B, S // tile),
        in_specs=[
            pl.BlockSpec((1, tile, D), row),
            _const_spec(mod.shape),
            _const_spec((1, D)),
            _const_spec((D, Q_W)),
            pl.BlockSpec(memory_space=pl.ANY),
            _const_spec((1, 1)),
            _const_spec((1, MXU_TILE)),
            _const_spec((1, MXU_TILE)),
            _const_spec((MXU_TILE, MXU_TILE)),
            pl.BlockSpec((tile, LANES), lambda b, t: (t, 0)),
            pl.BlockSpec((tile, LANES), lambda b, t: (t, 0)),
        ],
        out_specs=[
            pl.BlockSpec((1, tile, Q_W), row),
            pl.BlockSpec((1, tile, KV_W), row),
            pl.BlockSpec((1, tile, KV_W), row),
            pl.BlockSpec((1, tile, POOL_W), row),
        ],
        out_shape=[
            jax.ShapeDtypeStruct((B, S, Q_W), BF16),
            jax.ShapeDtypeStruct((B, S, KV_W), BF16),
            jax.ShapeDtypeStruct((B, S, KV_W), BF16),
            jax.ShapeDtypeStruct((B, S, POOL_W), F32),
        ],
        scratch_shapes=[
            pltpu.VMEM((D, 2 * KV_W), FP8),
            pltpu.VMEM((D, POOL_W), BF16),
            pltpu.VMEM((STAGE_SLOTS, D, STAGE_CHUNK), F32),
            pltpu.SemaphoreType.DMA((STAGE_SLOTS,)),
        ],
        compiler_params=_sequential_params(2),
        name="in_proj",
    )(x, mod, norm1_g, w_q, w_all, w_scale, qg256, kg256, ones_bd, cos_t, sin_t)


def _attn_kernel(sink_ref, q_ref, kp_ref, kc0_ref, kn_ref, vp_ref, vc0_ref, vn_ref,
                 kctx_ref, vctx_ref, bias_ref, o_ref, *, blocks_per_tile):
    t_idx = pl.program_id(1)
    n_tiles = pl.num_programs(1)
    k_all = jnp.concatenate([kp_ref[0], kc0_ref[0], kn_ref[0]], axis=0)
    v_all = jnp.concatenate([vp_ref[0], vc0_ref[0], vn_ref[0]], axis=0)
    k_ctx = kctx_ref[0]
    v_ctx = vctx_ref[0]
    col_amax = jnp.maximum(jnp.max(jnp.abs(v_all), axis=0, keepdims=True),
                           jnp.max(jnp.abs(v_ctx), axis=0, keepdims=True)).astype(F32)
    v_scale = _fp8_scale(jnp.max(col_amax, axis=1, keepdims=True))
    scale_row = jnp.broadcast_to(v_scale, (1, KV_W)).astype(BF16)
    v_all = v_all * scale_row
    v_ctx = v_ctx * scale_row
    one = jnp.ones((), BF16)
    lane = lax.broadcasted_iota(jnp.int32, (BLOCK, MXU_TILE), 1)
    in_head = [(lane >= kv * HEAD_DIM) & (lane < (kv + 1) * HEAD_DIM) for kv in range(N_KV_HEADS)]
    zero = jnp.zeros((), BF16)

    def with_ones(v, half):
        vl = lax.broadcasted_iota(jnp.int32, v.shape, 1)
        return jnp.where((vl < LANES) if half == 0 else (vl >= LANES), v, one).astype(FP8)

    v_ctx_pair = [with_ones(v_ctx, half) for half in range(2)]
    low_head = lax.broadcasted_iota(jnp.int32, (BLOCK, LANES), 1) < HEAD_DIM

    def block_operands(i):
        variant = jnp.int32(0)
        if i == blocks_per_tile - 1:
            variant = jnp.where(t_idx == n_tiles - 1, 2, variant)
        if i == 0:
            variant = jnp.where(t_idx == 0, 1, variant)
        v_win = v_all[i * BLOCK:(i + 3) * BLOCK]
        return dict(bias=bias_ref[variant], k_win=k_all[i * BLOCK:(i + 3) * BLOCK],
                    v_win_pair=[with_ones(v_win, half) for half in range(2)])

    def scores(blk, i, j):
        q_slab = q_ref[0, i * BLOCK:(i + 1) * BLOCK, j * MXU_TILE:(j + 1) * MXU_TILE]
        qz = jnp.concatenate([jnp.where(in_head[kv], q_slab, zero) for kv in range(N_KV_HEADS)], axis=0)
        return _dot_nt(qz, blk["k_win"]), _dot_nt(qz, k_ctx)

    def softmax_values(blk, i, j, s_win, s_ctx):
        p_win, p_ctx, sink_term = [], [], []
        for kv in range(N_KV_HEADS):
            rows = slice(kv * BLOCK, (kv + 1) * BLOCK)
            sw = jnp.concatenate([
                s_win[rows, :BLOCK] + blk["bias"][:, :BLOCK],
                s_win[rows, BLOCK:2 * BLOCK],
                s_win[rows, 2 * BLOCK:] + blk["bias"][:, 2 * BLOCK:]], axis=1)
            sc = s_ctx[rows]
            sink = sink_ref[kv * GROUP + j] * LOG2_E
            groups = [sw[:, g * LANES:(g + 1) * LANES] for g in range(sw.shape[1] // LANES)]
            groups += [sc[:, g * LANES:(g + 1) * LANES] for g in range(sc.shape[1] // LANES)]
            m = jnp.maximum(jnp.max(functools.reduce(jnp.maximum, groups), axis=-1, keepdims=True), sink)
            p_win.append(jnp.exp2(sw - m).astype(FP8))
            p_ctx.append(jnp.exp2(sc - m).astype(FP8))
            sink_term.append(jnp.exp2(sink - m))
        for half in range(2):
            pw = jnp.concatenate(p_win[2 * half:2 * half + 2], axis=0)
            pc = jnp.concatenate(p_ctx[2 * half:2 * half + 2], axis=0)
            o = _dot(pw, blk["v_win_pair"][half]) + _dot(pc, v_ctx_pair[half])
            vals = o[:, half * LANES:(half + 1) * LANES]
            sums = o[:, (1 - half) * LANES:(2 - half) * LANES]
            out = []
            for r in range(2):
                rows = slice(r * BLOCK, (r + 1) * BLOCK)
                out.append(vals[rows] / ((sums[rows] + sink_term[2 * half + r]) * v_scale))
            c0 = j * MXU_TILE + half * LANES
            o_ref[0, i * BLOCK:(i + 1) * BLOCK, c0:c0 + LANES] = jnp.where(low_head, out[0], out[1]).astype(BF16)

    blocks = [block_operands(i) for i in range(blocks_per_tile)]
    items = [(i, j) for i in range(blocks_per_tile) for j in range(GROUP)]
    s_next = scores(blocks[0], *items[0])
    for idx, (i, j) in enumerate(items):
        s_cur = s_next
        if idx + 1 < len(items):
            ni, nj = items[idx + 1]
            s_next = scores(blocks[ni], ni, nj)
        softmax_values(blocks[i], i, j, *s_cur)


def _attn_call(sink, q, k, v, kc, vc, bias, tile):
    B, S, _ = q.shape
    bpt = tile // BLOCK
    nb = S // BLOCK
    C = kc.shape[1]
    cur = lambda b, t, *_: (b, t, 0)
    prev = lambda b, t, *_: (b, jnp.maximum(t * bpt - 1, 0), 0)
    nxt = lambda b, t, *_: (b, jnp.minimum((t + 1) * bpt, nb - 1), 0)
    per_b = lambda b, t, *_: (b, 0, 0)
    halo = (1, BLOCK, KV_W)
    center = (1, tile, KV_W)
    grid_spec = pltpu.PrefetchScalarGridSpec(
        num_scalar_prefetch=1,
        grid=(B, S // tile),
        in_specs=[
            pl.BlockSpec((1, tile, Q_W), cur),
            pl.BlockSpec(halo, prev), pl.BlockSpec(center, cur), pl.BlockSpec(halo, nxt),
            pl.BlockSpec(halo, prev), pl.BlockSpec(center, cur), pl.BlockSpec(halo, nxt),
            pl.BlockSpec((1, C, KV_W), per_b),
            pl.BlockSpec((1, C, KV_W), per_b),
            pl.BlockSpec((3, BLOCK, 3 * BLOCK), lambda b, t, *_: (0, 0, 0)),
        ],
        out_specs=pl.BlockSpec((1, tile, Q_W), cur),
    )
    return pl.pallas_call(
        functools.partial(_attn_kernel, blocks_per_tile=bpt),
        grid_spec=grid_spec,
        out_shape=jax.ShapeDtypeStruct((B, S, Q_W), BF16),
        compiler_params=_params(2),
        name="attn",
    )(sink, q, k, k, k, v, v, v, kc, vc, bias)


def _merge_kernel(x_ref, mod_ref, g1_ref, attn_ref, up_ref, uc_ref, un_ref,
                  wall_hbm, gb_ref, wap_hbm, pw_ref, ps_ref, wpp_hbm, wo_hbm, o_ref,
                  uext_ref, wg_ref, wap_ref, wo_ref, wpool_ref, col_stage, row_stage, col_sem, row_sem,
                  *, seq, tile, sub):
    @pl.when((pl.program_id(0) == 0) & (pl.program_id(1) == 0))
    def _():
        _stage_weight_bf16(wpp_hbm, wpool_ref, row_stage, row_sem, axis=0)
        for g in range(POOL_GROUPS):
            rows = slice(g * POOL_GROUP_W, (g + 1) * POOL_GROUP_W)
            scaled = (pw_ref[g] * ps_ref[g]).astype(BF16)
            wpool_ref[rows, :] = _dot(scaled, wpool_ref[rows, :]).astype(BF16)

        _stage_weight_bf16(wall_hbm.at[:, pl.ds(2 * D_MODEL, 2 * D_MODEL)], wg_ref, col_stage, col_sem, axis=1)
        _stage_weight_bf16(wo_hbm, wo_ref, col_stage, col_sem, axis=1)

        def store_regrouped(kv, rows_f32):
            for j in range(GROUP):
                dst = (j * N_KV_HEADS + kv) * HEAD_DIM
                wap_ref[dst:dst + HEAD_DIM, :] = rows_f32[j * HEAD_DIM:(j + 1) * HEAD_DIM, :].astype(BF16)

        _stage_weight_bf16(wap_hbm, None, row_stage, row_sem, axis=0, store=store_regrouped)

    t_idx = pl.program_id(1)
    n_tiles = pl.num_programs(1)
    m = mod_ref[pl.ds(pl.program_id(0), 1), :]
    sh = m[:, 0:D_MODEL]
    gs = g1_ref[...] * (1.0 + m[:, D_MODEL:2 * D_MODEL])
    gate1 = m[:, 2 * D_MODEL:3 * D_MODEL]

    keep_prev = (t_idx > 0).astype(F32)
    keep_next = (t_idx < n_tiles - 1).astype(F32)
    uext_ref[0:POOL_HALO] = up_ref[0] * keep_prev
    uext_ref[POOL_HALO:POOL_HALO + tile] = uc_ref[0]
    uext_ref[POOL_HALO + tile:] = un_ref[0] * keep_next
    edge_rows = lax.broadcasted_iota(jnp.int32, (POOL_HALO, POOL_GROUP_W), 0)

    def attn_dot(r):
        return _dot(attn_ref[0, r * sub:(r + 1) * sub, :], wap_ref[...])

    def gate_dot(r):
        h = _rms_modulate(x_ref[0, r * sub:(r + 1) * sub, :], gs, sh).astype(BF16)
        return jax.nn.sigmoid(_dot(h, wg_ref[...]) + gb_ref[...])

    def pool_dot(r):
        r0 = r * sub
        pos_first = t_idx * tile + r0 + edge_rows
        pos_last = pos_first + (sub - POOL_HALO)
        diffs = []
        for g, w in enumerate(POOL_WINDOWS):
            cols = slice(g * POOL_GROUP_W, (g + 1) * POOL_GROUP_W)
            acc = uext_ref[pl.ds(r0, sub + 2 * POOL_HALO), cols]
            k = 1
            while k < w:
                acc = acc[:acc.shape[0] - k] + acc[k:]
                k *= 2
            win = acc[POOL_HALO - w // 2:POOL_HALO - w // 2 + sub]

            def inv_count(pos):
                return 1.0 / (jnp.minimum(pos + w // 2, seq) - jnp.maximum(pos - w // 2, 0)).astype(F32)

            pooled_avg = jnp.concatenate([
                win[:POOL_HALO] * inv_count(pos_first),
                win[POOL_HALO:sub - POOL_HALO] * (1.0 / w),
                win[sub - POOL_HALO:] * inv_count(pos_last)], axis=0)
            diffs.append((pooled_avg - uext_ref[pl.ds(r0 + POOL_HALO, sub), cols]).astype(BF16))
        return _dot(jnp.concatenate(diffs, axis=1), wpool_ref[...])

    def output(r, a, gates, p):
        rows = slice(r * sub, (r + 1) * sub)
        merged = (gates[:, :D_MODEL] * a + gates[:, D_MODEL:] * p).astype(BF16)
        o_ref[0, rows, :] = x_ref[0, rows, :] + gate1 * _dot(merged, wo_ref[...])

    n_sub = tile // sub
    a = [attn_dot(r) for r in range(n_sub)]
    p = [pool_dot(r) for r in range(n_sub)]
    gates = [gate_dot(r) for r in range(n_sub)]
    for r in range(n_sub):
        output(r, a[r], gates[r], p[r])


def _merge_call(x, mod, norm1_g, attn, u, w_all, gate_b, w_ap, pool_w, pool_scale3, w_pool_proj, w_out, tile):
    B, S, D = x.shape
    r = tile // POOL_HALO
    n_halo = S // POOL_HALO
    row = lambda b, t: (b, t, 0)
    return pl.pallas_call(
        functools.partial(_merge_kernel, seq=S, tile=tile, sub=SUB_TILE),
        grid=(B, S // tile),
        in_specs=[
            pl.BlockSpec((1, tile, D), row),
            _const_spec(mod.shape),
            _const_spec((1, D)),
            pl.BlockSpec((1, tile, Q_W), row),
            pl.BlockSpec((1, POOL_HALO, POOL_W), lambda b, t: (b, jnp.maximum(t * r - 1, 0), 0)),
            pl.BlockSpec((1, tile, POOL_W), row),
            pl.BlockSpec((1, POOL_HALO, POOL_W), lambda b, t: (b, jnp.minimum((t + 1) * r, n_halo - 1), 0)),
            pl.BlockSpec(memory_space=pl.ANY),
            _const_spec(gate_b.shape),
            pl.BlockSpec(memory_space=pl.ANY),
            _const_spec(pool_w.shape),
            _const_spec(pool_scale3.shape),
            pl.BlockSpec(memory_space=pl.ANY),
            pl.BlockSpec(memory_space=pl.ANY),
        ],
        out_specs=pl.BlockSpec((1, tile, D), row),
        out_shape=jax.ShapeDtypeStruct((B, S, D), F32),
        scratch_shapes=[
            pltpu.VMEM((tile + 2 * POOL_HALO, POOL_W), F32),
            pltpu.VMEM((D, 2 * D), BF16),
            pltpu.VMEM(w_ap.shape, BF16),
            pltpu.VMEM(w_out.shape, BF16),
            pltpu.VMEM(w_pool_proj.shape, BF16),
            pltpu.VMEM((STAGE_SLOTS, D, STAGE_CHUNK), F32),
            pltpu.VMEM((STAGE_SLOTS, STAGE_CHUNK, D), F32),
            pltpu.SemaphoreType.DMA((STAGE_SLOTS,)),
            pltpu.SemaphoreType.DMA((STAGE_SLOTS,)),
        ],
        compiler_params=_sequential_params(2),
        name="merge",
    )(x, mod, norm1_g, attn, u, u, u, w_all, gate_b, w_ap, pool_w, pool_scale3, w_pool_proj, w_out)


def _ffn_kernel(x_ref, mod_ref, g2_ref, wup_hbm, wdn_hbm, o_ref,
                wup_ref, wdn_ref, col_stage, row_stage, col_sem, row_sem, hmid_ref, *, d_ff, tile, sub):
    @pl.when((pl.program_id(0) == 0) & (pl.program_id(1) == 0))
    def _():
        _stage_weight_bf16(wup_hbm, wup_ref, col_stage, col_sem, axis=1)
        _stage_weight_bf16(wdn_hbm, wdn_ref, row_stage, row_sem, axis=0)

    m = mod_ref[pl.ds(pl.program_id(0), 1), :]
    sh = m[:, 3 * D_MODEL:4 * D_MODEL]
    gs = g2_ref[...] * (1.0 + m[:, 4 * D_MODEL:5 * D_MODEL])
    gate2 = m[:, 5 * D_MODEL:6 * D_MODEL]

    def up(r):
        rows = slice(r * sub, (r + 1) * sub)
        h = _rms_modulate(x_ref[0, rows, :], gs, sh).astype(BF16)
        for c in range(d_ff // MXU_TILE):
            a = _dot(h, wup_ref[:, c * MXU_TILE:(c + 1) * MXU_TILE])
            b = _dot(h, wup_ref[:, d_ff + c * MXU_TILE:d_ff + (c + 1) * MXU_TILE])
            hmid_ref[rows, c * MXU_TILE:(c + 1) * MXU_TILE] = (a * jax.nn.sigmoid(a) * b).astype(BF16)

    def down(r):
        rows = slice(r * sub, (r + 1) * sub)
        o_ref[0, rows, :] = x_ref[0, rows, :] + gate2 * _dot(hmid_ref[rows, :], wdn_ref[...])

    n_sub = tile // sub
    up(0)
    for r in range(1, n_sub):
        up(r)
        down(r - 1)
    down(n_sub - 1)


def _ffn_call(x, mod, norm2_g, w_up, w_down, tile):
    B, S, D = x.shape
    d_ff = w_down.shape[0]
    row = lambda b, t: (b, t, 0)
    return pl.pallas_call(
        functools.partial(_ffn_kernel, d_ff=d_ff, tile=tile, sub=SUB_TILE),
        grid=(B, S // tile),
        in_specs=[
            pl.BlockSpec((1, tile, D), row),
            _const_spec(mod.shape),
            _const_spec((1, D)),
            pl.BlockSpec(memory_space=pl.ANY),
            pl.BlockSpec(memory_space=pl.ANY),
        ],
        out_specs=pl.BlockSpec((1, tile, D), row),
        out_shape=jax.ShapeDtypeStruct((B, S, D), F32),
        scratch_shapes=[
            pltpu.VMEM(w_up.shape, BF16),
            pltpu.VMEM(w_down.shape, BF16),
            pltpu.VMEM((STAGE_SLOTS, D, STAGE_CHUNK), F32),
            pltpu.VMEM((STAGE_SLOTS, STAGE_CHUNK, D), F32),
            pltpu.SemaphoreType.DMA((STAGE_SLOTS,)),
            pltpu.SemaphoreType.DMA((STAGE_SLOTS,)),
            pltpu.VMEM((tile, d_ff), BF16),
        ],
        compiler_params=_sequential_params(2),
        name="ffn",
    )(x, mod, norm2_g, w_up, w_down)


def _rope_tables(seq):
    t = np.arange(seq)
    row = (t // GRID_W).astype(np.float64)
    col = (t % GRID_W).astype(np.float64)
    half = HEAD_DIM // 2
    inv_freq = 1.0 / (ROPE_THETA ** (np.arange(0, half, 2, dtype=np.float64) / half))
    ang_r = row[:, None] * inv_freq
    ang_c = col[:, None] * inv_freq
    cos = np.concatenate([np.cos(ang_r)] * 2 + [np.cos(ang_c)] * 2, axis=1)
    sin = np.concatenate([-np.sin(ang_r), np.sin(ang_r), -np.sin(ang_c), np.sin(ang_c)], axis=1)
    reps = (1, LANES // HEAD_DIM)
    return jnp.asarray(np.tile(cos, reps), dtype=F32), jnp.asarray(np.tile(sin, reps), dtype=F32)


def _band_bias():
    i = np.arange(BLOCK)[:, None]
    j = np.arange(3 * BLOCK)[None, :]
    band = np.abs(j - BLOCK - i) <= WINDOW
    first = band & (j >= BLOCK)
    last = band & (j < 2 * BLOCK)
    masks = np.stack([band, first, last])
    return jnp.asarray(np.where(masks, 0.0, NEG_BIG), dtype=F32)


def kernel(x, c, ctx, c_ctx, mod_w, mod_b, norm1_g, norm2_g, w_in, gate_b, q_norm_g, k_norm_g,
           sink, pool_w, pool_scale, w_attn_proj, w_pool_proj, w_out, w_up, w_down):
    B, S, D = x.shape
    depth = mod_w.shape[0]
    cos_t, sin_t = _rope_tables(S)
    bias = _band_bias()
    ones_bd = jnp.asarray(np.kron(np.eye(MXU_TILE // HEAD_DIM), np.ones((HEAD_DIM, HEAD_DIM))), dtype=BF16)
    reps = MXU_TILE // HEAD_DIM
    assert depth == 1, "context-stream update between layers is not implemented"
    assert D == D_MODEL and S % ROW_TILE == 0 and B + 1 <= F32_SUBLANES

    for l in range(depth):
        c8 = jnp.concatenate([c, c_ctx[None, :], jnp.zeros((F32_SUBLANES - B - 1, D), F32)], axis=0)
        mod = _mod_call(c8, mod_w[l], mod_b[l][None, :])

        wl = w_in[l]
        w_scale = _fp8_scale(jnp.max(jnp.abs(wl[:, :Q_W + 2 * KV_W]))).reshape(1, 1)
        w_q = ((wl[:, :Q_W] * w_scale).reshape(D, N_KV_HEADS, GROUP, HEAD_DIM).transpose(0, 2, 1, 3)
               .reshape(D, Q_W).astype(FP8))
        g1 = norm1_g[l][None, :]
        kg256 = jnp.tile(k_norm_g[l], reps)[None, :]
        qg256 = jnp.tile(q_norm_g[l] * (HEAD_DIM ** -0.5 * LOG2_E), reps)[None, :]

        kc, vc = _ctx_call(ctx, mod, g1, wl, kg256, ones_bd)
        q, k, v, u = _inproj_call(x, mod, g1, w_q, wl, w_scale, qg256, kg256, ones_bd, cos_t, sin_t, tile=ROW_TILE)
        attn = _attn_call(sink[l], q, k, v, kc, vc, bias, tile=ROW_TILE)
        x = _merge_call(
            x, mod, g1, attn, u, wl, gate_b[l][None, :], w_attn_proj[l], pool_w[l],
            pool_scale[l].reshape(POOL_GROUPS, 1, POOL_GROUP_W), w_pool_proj[l], w_out[l], tile=ROW_TILE)
        x = _ffn_call(x, mod, norm2_g[l][None, :], w_up[l], w_down[l], tile=ROW_TILE)
    return x
```

```python
import functools

import jax
import jax.numpy as jnp
import numpy as np
from jax import lax
from jax.experimental import pallas as pl
from jax.experimental.pallas import tpu as pltpu

D_MODEL = 1024
GRID_W = 64
HEAD_DIM = 64
N_HEADS = 16
N_KV_HEADS = 4
GROUP = N_HEADS // N_KV_HEADS
Q_W = N_HEADS * HEAD_DIM
KV_W = N_KV_HEADS * HEAD_DIM
WINDOW = 128
BLOCK = 128
ROPE_THETA = 10000.0
POOL_WINDOWS = (2, 4, 8, 16)
POOL_GROUPS = 4
POOL_W = D_MODEL // 2
POOL_GROUP_W = POOL_W // POOL_GROUPS
POOL_HALO = 8
ROPE_PAIR = HEAD_DIM // 4
EPS = 1e-6
NEG_BIG = -1e30
LOG2_E = 1.4426950408889634

LANES = 128
F32_SUBLANES = 8
MXU_TILE = 256
VMEM_LIMIT = 56 * 1024 * 1024
STAGE_SLOTS = 4
STAGE_CHUNK = MXU_TILE
ROW_TILE = 1024
SUB_TILE = MXU_TILE

F32 = jnp.float32
BF16 = jnp.bfloat16
FP8 = jnp.float8_e4m3fn
FP8_MAX = 448.0


def _params(n_parallel):
    return pltpu.CompilerParams(
        dimension_semantics=("parallel",) * n_parallel,
        vmem_limit_bytes=VMEM_LIMIT,
    )


def _const_spec(shape):
    nd = len(shape)
    return pl.BlockSpec(shape, lambda *_: (0,) * nd, pipeline_mode=pl.Buffered(1))


def _dot(a, b):
    return jnp.dot(a, b, preferred_element_type=F32)


def _dot_nt(a, b):
    return lax.dot_general(a, b, (((1,), (1,)), ((), ())), preferred_element_type=F32)


def _rms_modulate(x, gs, sh):
    rs = lax.rsqrt(jnp.mean(x * x, axis=-1, keepdims=True) + EPS)
    return x * rs * gs + sh


def _head_rms(t, ones_bd, unscale=1.0):
    ms = _dot((t * t).astype(BF16), ones_bd) * (unscale * unscale * (1.0 / HEAD_DIM))
    return lax.rsqrt(ms + EPS)


def _stage_weight_bf16(w_hbm, w_ref, stage_ref, sem, axis, store=None):
    slots = stage_ref.shape[0]
    chunk = stage_ref.shape[1 + axis]
    n = w_hbm.shape[axis] // chunk

    def piece(ref, i):
        return ref.at[:, pl.ds(i * chunk, chunk)] if axis == 1 else ref.at[pl.ds(i * chunk, chunk), :]

    def copy(i):
        return pltpu.make_async_copy(piece(w_hbm, i), stage_ref.at[i % slots], sem.at[i % slots])

    for i in range(min(slots - 1, n)):
        copy(i).start()
    for i in range(n):
        if i + slots - 1 < n:
            copy(i + slots - 1).start()
        copy(i).wait()
        staged = stage_ref[i % slots]
        if store is None:
            piece(w_ref, i)[...] = staged.astype(BF16)
        else:
            store(i, staged)


def _fp8_scale(amax):
    return jnp.exp2(jnp.floor(jnp.log2(FP8_MAX / jnp.maximum(amax, 1e-30))))


def _sequential_params(n_axes):
    return pltpu.CompilerParams(dimension_semantics=("arbitrary",) * n_axes, vmem_limit_bytes=VMEM_LIMIT)


def _mod_kernel(c_ref, w_ref, b_ref, o_ref):
    c = c_ref[...]
    a = (c * jax.nn.sigmoid(c)).astype(BF16)
    o_ref[...] = _dot(a, w_ref[...].astype(BF16)) + b_ref[...]


def _mod_call(c8, mod_w, mod_b):
    n = mod_w.shape[1]
    bn = 2 * D_MODEL
    return pl.pallas_call(
        _mod_kernel,
        grid=(n // bn,),
        in_specs=[
            pl.BlockSpec((F32_SUBLANES, D_MODEL), lambda j: (0, 0)),
            pl.BlockSpec((D_MODEL, bn), lambda j: (0, j)),
            pl.BlockSpec((1, bn), lambda j: (0, j)),
        ],
        out_specs=pl.BlockSpec((F32_SUBLANES, bn), lambda j: (0, j)),
        out_shape=jax.ShapeDtypeStruct((F32_SUBLANES, n), F32),
        compiler_params=_params(1),
        name="mod",
    )(c8, mod_w, mod_b)


def _ctx_kernel(ctx_ref, mod_ref, g1_ref, wall_hbm, kg_ref, ones_ref, kc_ref, vc_ref,
                w_ref, col_stage, col_sem, *, n_batch):
    @pl.when(pl.program_id(0) == 0)
    def _():
        _stage_weight_bf16(wall_hbm.at[:, pl.ds(Q_W, 2 * KV_W)], w_ref, col_stage, col_sem, axis=1)

    m = mod_ref[n_batch:n_batch + 1, :]
    sh = m[:, 0:D_MODEL]
    gs = g1_ref[...] * (1.0 + m[:, D_MODEL:2 * D_MODEL])
    h = _rms_modulate(ctx_ref[0], gs, sh).astype(BF16)
    kv = _dot(h, w_ref[...])
    k = kv[:, :KV_W]
    kc_ref[0] = (k * _head_rms(k, ones_ref[...]) * kg_ref[...]).astype(BF16)
    vc_ref[0] = kv[:, KV_W:].astype(BF16)


def _ctx_call(ctx, mod, norm1_g, w_all, kg256, ones_bd):
    B, C, D = ctx.shape
    rows = B * C
    kc, vc = pl.pallas_call(
        functools.partial(_ctx_kernel, n_batch=B),
        grid=(1,),
        in_specs=[
            pl.BlockSpec((1, rows, D), lambda i: (0, 0, 0)),
            _const_spec(mod.shape),
            _const_spec((1, D)),
            pl.BlockSpec(memory_space=pl.ANY),
            _const_spec((1, KV_W)),
            _const_spec((MXU_TILE, MXU_TILE)),
        ],
        out_specs=[
            pl.BlockSpec((1, rows, KV_W), lambda i: (0, 0, 0)),
            pl.BlockSpec((1, rows, KV_W), lambda i: (0, 0, 0)),
        ],
        out_shape=[
            jax.ShapeDtypeStruct((1, rows, KV_W), BF16),
            jax.ShapeDtypeStruct((1, rows, KV_W), BF16),
        ],
        scratch_shapes=[
            pltpu.VMEM((D, 2 * KV_W), BF16),
            pltpu.VMEM((STAGE_SLOTS, D, STAGE_CHUNK), F32),
            pltpu.SemaphoreType.DMA((STAGE_SLOTS,)),
        ],
        compiler_params=_sequential_params(1),
        name="ctx_kv",
    )(ctx.reshape(1, rows, D), mod, norm1_g, w_all, kg256, ones_bd)
    return kc.reshape(B, C, KV_W), vc.reshape(B, C, KV_W)


def _rope(t, cos, sin_signed, first_half):
    swap = jnp.where(first_half, pltpu.roll(t, LANES - ROPE_PAIR, axis=1), pltpu.roll(t, ROPE_PAIR, axis=1))
    return t * cos + swap * sin_signed


def _inproj_kernel(x_ref, mod_ref, g1_ref, wall_hbm, wscale_ref, qg_ref, kg_ref, ones_ref, cos_ref,
                   sin_ref, q_ref, k_ref, v_ref, u_ref, wq_ref, wkv_ref, wu_ref, col_stage, col_sem):
    w_scale = wscale_ref[...]

    @pl.when((pl.program_id(0) == 0) & (pl.program_id(1) == 0))
    def _():
        def store_q_fp8(kv, cols_f32):
            cols8 = (cols_f32 * w_scale).astype(FP8)
            for j in range(GROUP):
                c0 = j * KV_W + kv * HEAD_DIM
                wq_ref[:, c0:c0 + HEAD_DIM] = cols8[:, j * HEAD_DIM:(j + 1) * HEAD_DIM]

        def store_kv_fp8(i, cols_f32):
            wkv_ref[:, i * STAGE_CHUNK:(i + 1) * STAGE_CHUNK] = (cols_f32 * w_scale).astype(FP8)

        _stage_weight_bf16(wall_hbm.at[:, pl.ds(0, Q_W)], None, col_stage, col_sem, axis=1, store=store_q_fp8)
        _stage_weight_bf16(wall_hbm.at[:, pl.ds(Q_W, 2 * KV_W)], None, col_stage, col_sem, axis=1,
                           store=store_kv_fp8)
        _stage_weight_bf16(wall_hbm.at[:, pl.ds(Q_W + 2 * KV_W, POOL_W)], wu_ref, col_stage, col_sem, axis=1)

    m = mod_ref[pl.ds(pl.program_id(0), 1), :]
    sh = m[:, 0:D_MODEL]
    gs = g1_ref[...] * (1.0 + m[:, D_MODEL:2 * D_MODEL])
    h32 = _rms_modulate(x_ref[0], gs, sh)
    h = h32.astype(BF16)
    h_scale = _fp8_scale(jnp.max(jnp.abs(h32), keepdims=True))
    h8 = (h32 * h_scale).astype(FP8)
    unscale = 1.0 / (h_scale * w_scale)
    ones_bd = ones_ref[...]
    cos = cos_ref[...]
    sin = sin_ref[...]
    lane = lax.broadcasted_iota(jnp.int32, cos.shape, 1)
    first_half = (lane & (2 * ROPE_PAIR - 1)) < ROPE_PAIR

    def normed_rope(t, gain):
        t = t * _head_rms(t, ones_bd, unscale) * (gain * unscale)
        halves = [_rope(t[:, i * LANES:(i + 1) * LANES], cos, sin, first_half) for i in range(2)]
        return jnp.concatenate(halves, axis=1).astype(BF16)

    wide = 2 * MXU_TILE
    for j in range(Q_W // wide):
        t = _dot(h8, wq_ref[:, j * wide:(j + 1) * wide])
        for i in range(2):
            c0 = j * wide + i * MXU_TILE
            q_ref[0, :, c0:c0 + MXU_TILE] = normed_rope(t[:, i * MXU_TILE:(i + 1) * MXU_TILE], qg_ref[...])
    kv = _dot(h8, wkv_ref[...])
    k_ref[0] = normed_rope(kv[:, :KV_W], kg_ref[...])
    v_ref[0] = (kv[:, KV_W:] * unscale).astype(BF16)
    u_ref[0] = _dot(h, wu_ref[...])


def _inproj_call(x, mod, norm1_g, w_all, w_scale, qg256, kg256, ones_bd, cos_t, sin_t, tile):
    B, S, D = x.shape
    assert STAGE_CHUNK == GROUP * HEAD_DIM
    row = lambda b, t: (b, t, 0)
    return pl.pallas_call(
        _inproj_kernel,
        grid=(B, S // tile),
        in_specs=[
            pl.BlockSpec((1, tile, D), row),
            _const_spec(mod.shape),
            _const_spec((1, D)),
            pl.BlockSpec(memory_space=pl.ANY),
            _const_spec((1, 1)),
            _const_spec((1, MXU_TILE)),
            _const_spec((1, MXU_TILE)),
            _const_spec((MXU_TILE, MXU_TILE)),
            pl.BlockSpec((tile, LANES), lambda b, t: (t, 0)),
            pl.BlockSpec((tile, LANES), lambda b, t: (t, 0)),
        ],
        out_specs=[
            pl.BlockSpec((1, tile, Q_W), row),
            pl.BlockSpec((1, tile, KV_W), row),
            pl.BlockSpec((1, tile, KV_W), row),
            pl.BlockSpec((1, tile, POOL_W), row),
        ],
        out_shape=[
            jax.ShapeDtypeStruct((B, S, Q_W), BF16),
            jax.ShapeDtypeStruct((B, S, KV_W), BF16),
            jax.ShapeDtypeStruct((B, S, KV_W), BF16),
            jax.ShapeDtypeStruct((B, S, POOL_W), F32),
        ],
        scratch_shapes=[
            pltpu.VMEM((D, Q_W), FP8),
            pltpu.VMEM((D, 2 * KV_W), FP8),
            pltpu.VMEM((D, POOL_W), BF16),
            pltpu.VMEM((STAGE_SLOTS, D, STAGE_CHUNK), F32),
            pltpu.SemaphoreType.DMA((STAGE_SLOTS,)),
        ],
        compiler_params=_sequential_params(2),
        name="in_proj",
    )(x, mod, norm1_g, w_all, w_scale, qg256, kg256, ones_bd, cos_t, sin_t)


def _attn_kernel(sink_ref, q_ref, kp_ref, kc0_ref, kn_ref, vp_ref, vc0_ref, vn_ref,
                 kctx_ref, vctx_ref, bias_ref, o_ref, *, blocks_per_tile):
    t_idx = pl.program_id(1)
    n_tiles = pl.num_programs(1)
    k_all = jnp.concatenate([kp_ref[0], kc0_ref[0], kn_ref[0]], axis=0)
    v_all = jnp.concatenate([vp_ref[0], vc0_ref[0], vn_ref[0]], axis=0)
    k_ctx = kctx_ref[0]
    v_ctx = vctx_ref[0]
    col_amax = jnp.maximum(jnp.max(jnp.abs(v_all), axis=0, keepdims=True),
                           jnp.max(jnp.abs(v_ctx), axis=0, keepdims=True)).astype(F32)
    v_scale = _fp8_scale(jnp.max(col_amax, axis=1, keepdims=True))
    scale_row = jnp.broadcast_to(v_scale, (1, KV_W)).astype(BF16)
    v_all = v_all * scale_row
    v_ctx = v_ctx * scale_row
    one = jnp.ones((), BF16)
    lane = lax.broadcasted_iota(jnp.int32, (BLOCK, MXU_TILE), 1)
    in_head = [(lane >= kv * HEAD_DIM) & (lane < (kv + 1) * HEAD_DIM) for kv in range(N_KV_HEADS)]
    zero = jnp.zeros((), BF16)

    def with_ones(v, half):
        vl = lax.broadcasted_iota(jnp.int32, v.shape, 1)
        return jnp.where((vl < LANES) if half == 0 else (vl >= LANES), v, one).astype(FP8)

    v_ctx_pair = [with_ones(v_ctx, half) for half in range(2)]
    low_head = lax.broadcasted_iota(jnp.int32, (BLOCK, LANES), 1) < HEAD_DIM

    def block_operands(i):
        variant = jnp.int32(0)
        if i == blocks_per_tile - 1:
            variant = jnp.where(t_idx == n_tiles - 1, 2, variant)
        if i == 0:
            variant = jnp.where(t_idx == 0, 1, variant)
        v_win = v_all[i * BLOCK:(i + 3) * BLOCK]
        return dict(bias=bias_ref[variant], k_win=k_all[i * BLOCK:(i + 3) * BLOCK],
                    v_win_pair=[with_ones(v_win, half) for half in range(2)])

    def scores(blk, i, j):
        q_slab = q_ref[0, i * BLOCK:(i + 1) * BLOCK, j * MXU_TILE:(j + 1) * MXU_TILE]
        qz = jnp.concatenate([jnp.where(in_head[kv], q_slab, zero) for kv in range(N_KV_HEADS)], axis=0)
        return _dot_nt(qz, blk["k_win"]), _dot_nt(qz, k_ctx)

    def softmax_values(blk, i, j, s_win, s_ctx):
        p_win, p_ctx, sink_term = [], [], []
        for kv in range(N_KV_HEADS):
            rows = slice(kv * BLOCK, (kv + 1) * BLOCK)
            sw = jnp.concatenate([
                s_win[rows, :BLOCK] + blk["bias"][:, :BLOCK],
                s_win[rows, BLOCK:2 * BLOCK],
                s_win[rows, 2 * BLOCK:] + blk["bias"][:, 2 * BLOCK:]], axis=1)
            sc = s_ctx[rows]
            sink = sink_ref[kv * GROUP + j] * LOG2_E
            groups = [sw[:, g * LANES:(g + 1) * LANES] for g in range(sw.shape[1] // LANES)]
            groups += [sc[:, g * LANES:(g + 1) * LANES] for g in range(sc.shape[1] // LANES)]
            m = jnp.maximum(jnp.max(functools.reduce(jnp.maximum, groups), axis=-1, keepdims=True), sink)
            p_win.append(jnp.exp2(sw - m).astype(FP8))
            p_ctx.append(jnp.exp2(sc - m).astype(FP8))
            sink_term.append(jnp.exp2(sink - m))
        for half in range(2):
            pw = jnp.concatenate(p_win[2 * half:2 * half + 2], axis=0)
            pc = jnp.concatenate(p_ctx[2 * half:2 * half + 2], axis=0)
            o = _dot(pw, blk["v_win_pair"][half]) + _dot(pc, v_ctx_pair[half])
            vals = o[:, half * LANES:(half + 1) * LANES]
            sums = o[:, (1 - half) * LANES:(2 - half) * LANES]
            out = []
            for r in range(2):
                rows = slice(r * BLOCK, (r + 1) * BLOCK)
                out.append(vals[rows] / ((sums[rows] + sink_term[2 * half + r]) * v_scale))
            c0 = j * MXU_TILE + half * LANES
            o_ref[0, i * BLOCK:(i + 1) * BLOCK, c0:c0 + LANES] = jnp.where(low_head, out[0], out[1]).astype(BF16)

    blocks = [block_operands(i) for i in range(blocks_per_tile)]
    items = [(i, j) for i in range(blocks_per_tile) for j in range(GROUP)]
    s_next = scores(blocks[0], *items[0])
    for idx, (i, j) in enumerate(items):
        s_cur = s_next
        if idx + 1 < len(items):
            ni, nj = items[idx + 1]
            s_next = scores(blocks[ni], ni, nj)
        softmax_values(blocks[i], i, j, *s_cur)


def _attn_call(sink, q, k, v, kc, vc, bias, tile):
    B, S, _ = q.shape
    bpt = tile // BLOCK
    nb = S // BLOCK
    C = kc.shape[1]
    cur = lambda b, t, *_: (b, t, 0)
    prev = lambda b, t, *_: (b, jnp.maximum(t * bpt - 1, 0), 0)
    nxt = lambda b, t, *_: (b, jnp.minimum((t + 1) * bpt, nb - 1), 0)
    per_b = lambda b, t, *_: (b, 0, 0)
    halo = (1, BLOCK, KV_W)
    center = (1, tile, KV_W)
    grid_spec = pltpu.PrefetchScalarGridSpec(
        num_scalar_prefetch=1,
        grid=(B, S // tile),
        in_specs=[
            pl.BlockSpec((1, tile, Q_W), cur),
            pl.BlockSpec(halo, prev), pl.BlockSpec(center, cur), pl.BlockSpec(halo, nxt),
            pl.BlockSpec(halo, prev), pl.BlockSpec(center, cur), pl.BlockSpec(halo, nxt),
            pl.BlockSpec((1, C, KV_W), per_b),
            pl.BlockSpec((1, C, KV_W), per_b),
            pl.BlockSpec((3, BLOCK, 3 * BLOCK), lambda b, t, *_: (0, 0, 0)),
        ],
        out_specs=pl.BlockSpec((1, tile, Q_W), cur),
    )
    return pl.pallas_call(
        functools.partial(_attn_kernel, blocks_per_tile=bpt),
        grid_spec=grid_spec,
        out_shape=jax.ShapeDtypeStruct((B, S, Q_W), BF16),
        compiler_params=_params(2),
        name="attn",
    )(sink, q, k, k, k, v, v, v, kc, vc, bias)


def _merge_kernel(x_ref, mod_ref, g1_ref, attn_ref, up_ref, uc_ref, un_ref,
                  wall_hbm, gb_ref, wap_hbm, pw_ref, ps_ref, wpp_hbm, wo_hbm, o_ref,
                  uext_ref, wg_ref, wap_ref, wo_ref, wpool_ref, col_stage, row_stage, col_sem, row_sem,
                  *, seq, tile, sub):
    @pl.when((pl.program_id(0) == 0) & (pl.program_id(1) == 0))
    def _():
        _stage_weight_bf16(wpp_hbm, wpool_ref, row_stage, row_sem, axis=0)
        for g in range(POOL_GROUPS):
            rows = slice(g * POOL_GROUP_W, (g + 1) * POOL_GROUP_W)
            scaled = (pw_ref[g] * ps_ref[g]).astype(BF16)
            wpool_ref[rows, :] = _dot(scaled, wpool_ref[rows, :]).astype(BF16)

        _stage_weight_bf16(wall_hbm.at[:, pl.ds(2 * D_MODEL, 2 * D_MODEL)], wg_ref, col_stage, col_sem, axis=1)
        _stage_weight_bf16(wo_hbm, wo_ref, col_stage, col_sem, axis=1)

        def store_regrouped(kv, rows_f32):
            for j in range(GROUP):
                dst = (j * N_KV_HEADS + kv) * HEAD_DIM
                wap_ref[dst:dst + HEAD_DIM, :] = rows_f32[j * HEAD_DIM:(j + 1) * HEAD_DIM, :].astype(BF16)

        _stage_weight_bf16(wap_hbm, None, row_stage, row_sem, axis=0, store=store_regrouped)

    t_idx = pl.program_id(1)
    n_tiles = pl.num_programs(1)
    m = mod_ref[pl.ds(pl.program_id(0), 1), :]
    sh = m[:, 0:D_MODEL]
    gs = g1_ref[...] * (1.0 + m[:, D_MODEL:2 * D_MODEL])
    gate1 = m[:, 2 * D_MODEL:3 * D_MODEL]

    keep_prev = (t_idx > 0).astype(F32)
    keep_next = (t_idx < n_tiles - 1).astype(F32)
    uext_ref[0:POOL_HALO] = up_ref[0] * keep_prev
    uext_ref[POOL_HALO:POOL_HALO + tile] = uc_ref[0]
    uext_ref[POOL_HALO + tile:] = un_ref[0] * keep_next
    edge_rows = lax.broadcasted_iota(jnp.int32, (POOL_HALO, POOL_GROUP_W), 0)

    def attn_dot(r):
        return _dot(attn_ref[0, r * sub:(r + 1) * sub, :], wap_ref[...])

    def gate_dot(r):
        h = _rms_modulate(x_ref[0, r * sub:(r + 1) * sub, :], gs, sh).astype(BF16)
        return jax.nn.sigmoid(_dot(h, wg_ref[...]) + gb_ref[...])

    def pool_dot(r):
        r0 = r * sub
        pos_first = t_idx * tile + r0 + edge_rows
        pos_last = pos_first + (sub - POOL_HALO)
        diffs = []
        for g, w in enumerate(POOL_WINDOWS):
            cols = slice(g * POOL_GROUP_W, (g + 1) * POOL_GROUP_W)
            acc = uext_ref[pl.ds(r0, sub + 2 * POOL_HALO), cols]
            k = 1
            while k < w:
                acc = acc[:acc.shape[0] - k] + acc[k:]
                k *= 2
            win = acc[POOL_HALO - w // 2:POOL_HALO - w // 2 + sub]

            def inv_count(pos):
                return 1.0 / (jnp.minimum(pos + w // 2, seq) - jnp.maximum(pos - w // 2, 0)).astype(F32)

            pooled_avg = jnp.concatenate([
                win[:POOL_HALO] * inv_count(pos_first),
                win[POOL_HALO:sub - POOL_HALO] * (1.0 / w),
                win[sub - POOL_HALO:] * inv_count(pos_last)], axis=0)
            diffs.append((pooled_avg - uext_ref[pl.ds(r0 + POOL_HALO, sub), cols]).astype(BF16))
        return _dot(jnp.concatenate(diffs, axis=1), wpool_ref[...])

    def output(r, a, gates, p):
        rows = slice(r * sub, (r + 1) * sub)
        merged = (gates[:, :D_MODEL] * a + gates[:, D_MODEL:] * p).astype(BF16)
        o_ref[0, rows, :] = x_ref[0, rows, :] + gate1 * _dot(merged, wo_ref[...])

    n_sub = tile // sub
    a = [attn_dot(r) for r in range(n_sub)]
    p = [pool_dot(r) for r in range(n_sub)]
    gates = [gate_dot(r) for r in range(n_sub)]
    for r in range(n_sub):
        output(r, a[r], gates[r], p[r])


def _merge_call(x, mod, norm1_g, attn, u, w_all, gate_b, w_ap, pool_w, pool_scale3, w_pool_proj, w_out, tile):
    B, S, D = x.shape
    r = tile // POOL_HALO
    n_halo = S // POOL_HALO
    row = lambda b, t: (b, t, 0)
    return pl.pallas_call(
        functools.partial(_merge_kernel, seq=S, tile=tile, sub=SUB_TILE),
        grid=(B, S // tile),
        in_specs=[
            pl.BlockSpec((1, tile, D), row),
            _const_spec(mod.shape),
            _const_spec((1, D)),
            pl.BlockSpec((1, tile, Q_W), row),
            pl.BlockSpec((1, POOL_HALO, POOL_W), lambda b, t: (b, jnp.maximum(t * r - 1, 0), 0)),
            pl.BlockSpec((1, tile, POOL_W), row),
            pl.BlockSpec((1, POOL_HALO, POOL_W), lambda b, t: (b, jnp.minimum((t + 1) * r, n_halo - 1), 0)),
            pl.BlockSpec(memory_space=pl.ANY),
            _const_spec(gate_b.shape),
            pl.BlockSpec(memory_space=pl.ANY),
            _const_spec(pool_w.shape),
            _const_spec(pool_scale3.shape),
            pl.BlockSpec(memory_space=pl.ANY),
            pl.BlockSpec(memory_space=pl.ANY),
        ],
        out_specs=pl.BlockSpec((1, tile, D), row),
        out_shape=jax.ShapeDtypeStruct((B, S, D), F32),
        scratch_shapes=[
            pltpu.VMEM((tile + 2 * POOL_HALO, POOL_W), F32),
            pltpu.VMEM((D, 2 * D), BF16),
            pltpu.VMEM(w_ap.shape, BF16),
            pltpu.VMEM(w_out.shape, BF16),
            pltpu.VMEM(w_pool_proj.shape, BF16),
            pltpu.VMEM((STAGE_SLOTS, D, STAGE_CHUNK), F32),
            pltpu.VMEM((STAGE_SLOTS, STAGE_CHUNK, D), F32),
            pltpu.SemaphoreType.DMA((STAGE_SLOTS,)),
            pltpu.SemaphoreType.DMA((STAGE_SLOTS,)),
        ],
        compiler_params=_sequential_params(2),
        name="merge",
    )(x, mod, norm1_g, attn, u, u, u, w_all, gate_b, w_ap, pool_w, pool_scale3, w_pool_proj, w_out)


def _ffn_kernel(x_ref, mod_ref, g2_ref, wup_hbm, wdn_hbm, o_ref,
                wup_ref, wdn_ref, col_stage, row_stage, col_sem, row_sem, hmid_ref, *, d_ff, tile, sub):
    @pl.when((pl.program_id(0) == 0) & (pl.program_id(1) == 0))
    def _():
        _stage_weight_bf16(wup_hbm, wup_ref, col_stage, col_sem, axis=1)
        _stage_weight_bf16(wdn_hbm, wdn_ref, row_stage, row_sem, axis=0)

    m = mod_ref[pl.ds(pl.program_id(0), 1), :]
    sh = m[:, 3 * D_MODEL:4 * D_MODEL]
    gs = g2_ref[...] * (1.0 + m[:, 4 * D_MODEL:5 * D_MODEL])
    gate2 = m[:, 5 * D_MODEL:6 * D_MODEL]

    def up(r):
        rows = slice(r * sub, (r + 1) * sub)
        h = _rms_modulate(x_ref[0, rows, :], gs, sh).astype(BF16)
        for c in range(d_ff // MXU_TILE):
            a = _dot(h, wup_ref[:, c * MXU_TILE:(c + 1) * MXU_TILE])
            b = _dot(h, wup_ref[:, d_ff + c * MXU_TILE:d_ff + (c + 1) * MXU_TILE])
            hmid_ref[rows, c * MXU_TILE:(c + 1) * MXU_TILE] = (a * jax.nn.sigmoid(a) * b).astype(BF16)

    def down(r):
        rows = slice(r * sub, (r + 1) * sub)
        o_ref[0, rows, :] = x_ref[0, rows, :] + gate2 * _dot(hmid_ref[rows, :], wdn_ref[...])

    n_sub = tile // sub
    up(0)
    for r in range(1, n_sub):
        up(r)
        down(r - 1)
    down(n_sub - 1)


def _ffn_call(x, mod, norm2_g, w_up, w_down, tile):
    B, S, D = x.shape
    d_ff = w_down.shape[0]
    row = lambda b, t: (b, t, 0)
    return pl.pallas_call(
        functools.partial(_ffn_kernel, d_ff=d_ff, tile=tile, sub=SUB_TILE),
        grid=(B, S // tile),
        in_specs=[
            pl.BlockSpec((1, tile, D), row),
            _const_spec(mod.shape),
            _const_spec((1, D)),
            pl.BlockSpec(memory_space=pl.ANY),
            pl.BlockSpec(memory_space=pl.ANY),
        ],
        out_specs=pl.BlockSpec((1, tile, D), row),
        out_shape=jax.ShapeDtypeStruct((B, S, D), F32),
        scratch_shapes=[
            pltpu.VMEM(w_up.shape, BF16),
            pltpu.VMEM(w_down.shape, BF16),
            pltpu.VMEM((STAGE_SLOTS, D, STAGE_CHUNK), F32),
            pltpu.VMEM((STAGE_SLOTS, STAGE_CHUNK, D), F32),
            pltpu.SemaphoreType.DMA((STAGE_SLOTS,)),
            pltpu.SemaphoreType.DMA((STAGE_SLOTS,)),
            pltpu.VMEM((tile, d_ff), BF16),
        ],
        compiler_params=_sequential_params(2),
        name="ffn",
    )(x, mod, norm2_g, w_up, w_down)


def _rope_tables(seq):
    t = np.arange(seq)
    row = (t // GRID_W).astype(np.float64)
    col = (t % GRID_W).astype(np.float64)
    half = HEAD_DIM // 2
    inv_freq = 1.0 / (ROPE_THETA ** (np.arange(0, half, 2, dtype=np.float64) / half))
    ang_r = row[:, None] * inv_freq
    ang_c = col[:, None] * inv_freq
    cos = np.concatenate([np.cos(ang_r)] * 2 + [np.cos(ang_c)] * 2, axis=1)
    sin = np.concatenate([-np.sin(ang_r), np.sin(ang_r), -np.sin(ang_c), np.sin(ang_c)], axis=1)
    reps = (1, LANES // HEAD_DIM)
    return jnp.asarray(np.tile(cos, reps), dtype=F32), jnp.asarray(np.tile(sin, reps), dtype=F32)


def _band_bias():
    i = np.arange(BLOCK)[:, None]
    j = np.arange(3 * BLOCK)[None, :]
    band = np.abs(j - BLOCK - i) <= WINDOW
    first = band & (j >= BLOCK)
    last = band & (j < 2 * BLOCK)
    masks = np.stack([band, first, last])
    return jnp.asarray(np.where(masks, 0.0, NEG_BIG), dtype=F32)


def kernel(x, c, ctx, c_ctx, mod_w, mod_b, norm1_g, norm2_g, w_in, gate_b, q_norm_g, k_norm_g,
           sink, pool_w, pool_scale, w_attn_proj, w_pool_proj, w_out, w_up, w_down):
    B, S, D = x.shape
    depth = mod_w.shape[0]
    cos_t, sin_t = _rope_tables(S)
    bias = _band_bias()
    ones_bd = jnp.asarray(np.kron(np.eye(MXU_TILE // HEAD_DIM), np.ones((HEAD_DIM, HEAD_DIM))), dtype=BF16)
    reps = MXU_TILE // HEAD_DIM
    assert depth == 1, "context-stream update between layers is not implemented"
    assert D == D_MODEL and S % ROW_TILE == 0 and B + 1 <= F32_SUBLANES

    for l in range(depth):
        c8 = jnp.concatenate([c, c_ctx[None, :], jnp.zeros((F32_SUBLANES - B - 1, D), F32)], axis=0)
        mod = _mod_call(c8, mod_w[l], mod_b[l][None, :])

        wl = w_in[l]
        w_scale = _fp8_scale(jnp.max(jnp.abs(wl[:, :Q_W + 2 * KV_W]))).reshape(1, 1)
        g1 = norm1_g[l][None, :]
        kg256 = jnp.tile(k_norm_g[l], reps)[None, :]
        qg256 = jnp.tile(q_norm_g[l] * (HEAD_DIM ** -0.5 * LOG2_E), reps)[None, :]

        kc, vc = _ctx_call(ctx, mod, g1, wl, kg256, ones_bd)
        q, k, v, u = _inproj_call(x, mod, g1, wl, w_scale, qg256, kg256, ones_bd, cos_t, sin_t, tile=ROW_TILE)
        attn = _attn_call(sink[l], q, k, v, kc, vc, bias, tile=ROW_TILE)
        x = _merge_call(
            x, mod, g1, attn, u, wl, gate_b[l][None, :], w_attn_proj[l], pool_w[l],
            pool_scale[l].reshape(POOL_GROUPS, 1, POOL_GROUP_W), w_pool_proj[l], w_out[l], tile=ROW_TILE)
        x = _ffn_call(x, mod, norm2_g[l][None, :], w_up[l], w_down[l], tile=ROW_TILE)
    return x
```

```python
import functools

import jax
import jax.numpy as jnp
import numpy as np
from jax import lax
from jax.experimental import pallas as pl
from jax.experimental.pallas import tpu as pltpu

D_MODEL = 1024
GRID_W = 64
HEAD_DIM = 64
N_HEADS = 16
N_KV_HEADS = 4
GROUP = N_HEADS // N_KV_HEADS
Q_W = N_HEADS * HEAD_DIM
KV_W = N_KV_HEADS * HEAD_DIM
WINDOW = 128
BLOCK = 128
ROPE_THETA = 10000.0
POOL_WINDOWS = (2, 4, 8, 16)
POOL_GROUPS = 4
POOL_W = D_MODEL // 2
POOL_GROUP_W = POOL_W // POOL_GROUPS
POOL_HALO = 8
ROPE_PAIR = HEAD_DIM // 4
EPS = 1e-6
NEG_BIG = -1e30
LOG2_E = 1.4426950408889634

LANES = 128
F32_SUBLANES = 8
MXU_TILE = 256
VMEM_LIMIT = 56 * 1024 * 1024
STAGE_SLOTS = 4
STAGE_CHUNK = MXU_TILE
ROW_TILE = 1024
SUB_TILE = MXU_TILE

F32 = jnp.float32
BF16 = jnp.bfloat16
FP8 = jnp.float8_e4m3fn
FP8_MAX = 448.0


def _params(n_parallel):
    return pltpu.CompilerParams(
        dimension_semantics=("parallel",) * n_parallel,
        vmem_limit_bytes=VMEM_LIMIT,
    )


def _const_spec(shape):
    nd = len(shape)
    return pl.BlockSpec(shape, lambda *_: (0,) * nd, pipeline_mode=pl.Buffered(1))


def _dot(a, b):
    return jnp.dot(a, b, preferred_element_type=F32)


def _dot_nt(a, b):
    return lax.dot_general(a, b, (((1,), (1,)), ((), ())), preferred_element_type=F32)


def _rms_modulate(x, gs, sh):
    rs = lax.rsqrt(jnp.mean(x * x, axis=-1, keepdims=True) + EPS)
    return x * rs * gs + sh


def _head_rms(t, ones_bd, unscale=1.0):
    ms = _dot((t * t).astype(BF16), ones_bd) * (unscale * unscale * (1.0 / HEAD_DIM))
    return lax.rsqrt(ms + EPS)


def _tile_heads(g):
    return jnp.concatenate([g] * N_KV_HEADS, axis=1)


def _stage_weight_bf16(w_hbm, w_ref, stage_ref, sem, axis, store=None):
    slots = stage_ref.shape[0]
    chunk = stage_ref.shape[1 + axis]
    n = w_hbm.shape[axis] // chunk

    def piece(ref, i):
        return ref.at[:, pl.ds(i * chunk, chunk)] if axis == 1 else ref.at[pl.ds(i * chunk, chunk), :]

    def copy(i):
        return pltpu.make_async_copy(piece(w_hbm, i), stage_ref.at[i % slots], sem.at[i % slots])

    for i in range(min(slots - 1, n)):
        copy(i).start()
    for i in range(n):
        if i + slots - 1 < n:
            copy(i + slots - 1).start()
        copy(i).wait()
        staged = stage_ref[i % slots]
        if store is None:
            piece(w_ref, i)[...] = staged.astype(BF16)
        else:
            store(i, staged)


def _fp8_scale(amax):
    return jnp.exp2(jnp.floor(jnp.log2(FP8_MAX / jnp.maximum(amax, 1e-30))))


def _sequential_params(n_axes):
    return pltpu.CompilerParams(dimension_semantics=("arbitrary",) * n_axes, vmem_limit_bytes=VMEM_LIMIT)


def _mod_kernel(c_ref, cctx_ref, w_ref, b_ref, o_ref, rows_ref):
    n_batch = c_ref.shape[0]
    rows_ref[...] = jnp.zeros_like(rows_ref)
    rows_ref[0:n_batch, :] = c_ref[...]
    rows_ref[n_batch:n_batch + 1, :] = cctx_ref[...]
    c = rows_ref[...]
    a = (c * jax.nn.sigmoid(c)).astype(BF16)
    o_ref[...] = _dot(a, w_ref[...].astype(BF16)) + b_ref[...]


def _mod_call(c, c_ctx, mod_w, mod_b):
    n = mod_w.shape[1]
    bn = 2 * D_MODEL
    return pl.pallas_call(
        _mod_kernel,
        grid=(n // bn,),
        in_specs=[
            pl.BlockSpec(c.shape, lambda j: (0, 0)),
            pl.BlockSpec((1, D_MODEL), lambda j: (0, 0)),
            pl.BlockSpec((D_MODEL, bn), lambda j: (0, j)),
            pl.BlockSpec((1, bn), lambda j: (0, j)),
        ],
        out_specs=pl.BlockSpec((F32_SUBLANES, bn), lambda j: (0, j)),
        out_shape=jax.ShapeDtypeStruct((F32_SUBLANES, n), F32),
        scratch_shapes=[pltpu.VMEM((F32_SUBLANES, D_MODEL), F32)],
        compiler_params=_params(1),
        name="mod",
    )(c, c_ctx, mod_w, mod_b)


def _ctx_kernel(ctx_ref, mod_ref, g1_ref, wall_hbm, kg_ref, ones_ref, kc_ref, vc_ref,
                w_ref, col_stage, col_sem, *, n_batch):
    @pl.when(pl.program_id(0) == 0)
    def _():
        _stage_weight_bf16(wall_hbm.at[:, pl.ds(Q_W, 2 * KV_W)], w_ref, col_stage, col_sem, axis=1)

    m = mod_ref[n_batch:n_batch + 1, :]
    sh = m[:, 0:D_MODEL]
    gs = g1_ref[...] * (1.0 + m[:, D_MODEL:2 * D_MODEL])
    h = _rms_modulate(ctx_ref[0], gs, sh).astype(BF16)
    kv = _dot(h, w_ref[...])
    k = kv[:, :KV_W]
    kc_ref[0] = (k * _head_rms(k, ones_ref[...]) * _tile_heads(kg_ref[...])).astype(BF16)
    vc_ref[0] = kv[:, KV_W:].astype(BF16)


def _ctx_call(ctx, mod, norm1_g, w_all, k_gain, ones_bd):
    B, C, D = ctx.shape
    rows = B * C
    kc, vc = pl.pallas_call(
        functools.partial(_ctx_kernel, n_batch=B),
        grid=(1,),
        in_specs=[
            pl.BlockSpec((1, rows, D), lambda i: (0, 0, 0)),
            _const_spec(mod.shape),
            _const_spec((1, D)),
            pl.BlockSpec(memory_space=pl.ANY),
            _const_spec((1, HEAD_DIM)),
            _const_spec((MXU_TILE, MXU_TILE)),
        ],
        out_specs=[
            pl.BlockSpec((1, rows, KV_W), lambda i: (0, 0, 0)),
            pl.BlockSpec((1, rows, KV_W), lambda i: (0, 0, 0)),
        ],
        out_shape=[
            jax.ShapeDtypeStruct((1, rows, KV_W), BF16),
            jax.ShapeDtypeStruct((1, rows, KV_W), BF16),
        ],
        scratch_shapes=[
            pltpu.VMEM((D, 2 * KV_W), BF16),
            pltpu.VMEM((STAGE_SLOTS, D, STAGE_CHUNK), F32),
            pltpu.SemaphoreType.DMA((STAGE_SLOTS,)),
        ],
        compiler_params=_sequential_params(1),
        name="ctx_kv",
    )(ctx.reshape(1, rows, D), mod, norm1_g, w_all, k_gain, ones_bd)
    return kc.reshape(B, C, KV_W), vc.reshape(B, C, KV_W)


def _rope(t, cos, sin_signed, first_half):
    swap = jnp.where(first_half, pltpu.roll(t, LANES - ROPE_PAIR, axis=1), pltpu.roll(t, ROPE_PAIR, axis=1))
    return t * cos + swap * sin_signed


def _inproj_kernel(x_ref, mod_ref, g1_ref, wall_hbm, wscale_ref, qg_ref, kg_ref, ones_ref, cos_ref,
                   sin_ref, q_ref, k_ref, v_ref, u_ref, wq_ref, wkv_ref, wu_ref, col_stage, col_sem):
    w_scale = wscale_ref[...]

    @pl.when((pl.program_id(0) == 0) & (pl.program_id(1) == 0))
    def _():
        def store_q_fp8(kv, cols_f32):
            cols8 = (cols_f32 * w_scale).astype(FP8)
            for j in range(GROUP):
                c0 = j * KV_W + kv * HEAD_DIM
                wq_ref[:, c0:c0 + HEAD_DIM] = cols8[:, j * HEAD_DIM:(j + 1) * HEAD_DIM]

        def store_kv_fp8(i, cols_f32):
            wkv_ref[:, i * STAGE_CHUNK:(i + 1) * STAGE_CHUNK] = (cols_f32 * w_scale).astype(FP8)

        _stage_weight_bf16(wall_hbm.at[:, pl.ds(0, Q_W)], None, col_stage, col_sem, axis=1, store=store_q_fp8)
        _stage_weight_bf16(wall_hbm.at[:, pl.ds(Q_W, 2 * KV_W)], None, col_stage, col_sem, axis=1,
                           store=store_kv_fp8)
        _stage_weight_bf16(wall_hbm.at[:, pl.ds(Q_W + 2 * KV_W, POOL_W)], wu_ref, col_stage, col_sem, axis=1)

    m = mod_ref[pl.ds(pl.program_id(0), 1), :]
    sh = m[:, 0:D_MODEL]
    gs = g1_ref[...] * (1.0 + m[:, D_MODEL:2 * D_MODEL])
    h32 = _rms_modulate(x_ref[0], gs, sh)
    h = h32.astype(BF16)
    h_scale = _fp8_scale(jnp.max(jnp.abs(h32), keepdims=True))
    h8 = (h32 * h_scale).astype(FP8)
    unscale = 1.0 / (h_scale * w_scale)
    ones_bd = ones_ref[...]
    q_gain = _tile_heads(qg_ref[...]) * (HEAD_DIM ** -0.5 * LOG2_E)
    k_gain = _tile_heads(kg_ref[...])
    cos = cos_ref[...]
    sin = sin_ref[...]
    lane = lax.broadcasted_iota(jnp.int32, cos.shape, 1)
    first_half = (lane & (2 * ROPE_PAIR - 1)) < ROPE_PAIR

    def normed_rope(t, gain):
        t = t * _head_rms(t, ones_bd, unscale) * (gain * unscale)
        halves = [_rope(t[:, i * LANES:(i + 1) * LANES], cos, sin, first_half) for i in range(2)]
        return jnp.concatenate(halves, axis=1).astype(BF16)

    wide = 2 * MXU_TILE
    for j in range(Q_W // wide):
        t = _dot(h8, wq_ref[:, j * wide:(j + 1) * wide])
        for i in range(2):
            c0 = j * wide + i * MXU_TILE
            q_ref[0, :, c0:c0 + MXU_TILE] = normed_rope(t[:, i * MXU_TILE:(i + 1) * MXU_TILE], q_gain)
    kv = _dot(h8, wkv_ref[...])
    k_ref[0] = normed_rope(kv[:, :KV_W], k_gain)
    v_ref[0] = (kv[:, KV_W:] * unscale).astype(BF16)
    u_ref[0] = _dot(h, wu_ref[...])


def _inproj_call(x, mod, norm1_g, w_all, w_scale, q_gain, k_gain, ones_bd, cos_t, sin_t, tile):
    B, S, D = x.shape
    assert STAGE_CHUNK == GROUP * HEAD_DIM
    row = lambda b, t: (b, t, 0)
    return pl.pallas_call(
        _inproj_kernel,
        grid=(B, S // tile),
        in_specs=[
            pl.BlockSpec((1, tile, D), row),
            _const_spec(mod.shape),
            _const_spec((1, D)),
            pl.BlockSpec(memory_space=pl.ANY),
            _const_spec((1, 1)),
            _const_spec((1, HEAD_DIM)),
            _const_spec((1, HEAD_DIM)),
            _const_spec((MXU_TILE, MXU_TILE)),
            pl.BlockSpec((tile, LANES), lambda b, t: (t, 0)),
            pl.BlockSpec((tile, LANES), lambda b, t: (t, 0)),
        ],
        out_specs=[
            pl.BlockSpec((1, tile, Q_W), row),
            pl.BlockSpec((1, tile, KV_W), row),
            pl.BlockSpec((1, tile, KV_W), row),
            pl.BlockSpec((1, tile, POOL_W), row),
        ],
        out_shape=[
            jax.ShapeDtypeStruct((B, S, Q_W), BF16),
            jax.ShapeDtypeStruct((B, S, KV_W), BF16),
            jax.ShapeDtypeStruct((B, S, KV_W), BF16),
            jax.ShapeDtypeStruct((B, S, POOL_W), F32),
        ],
        scratch_shapes=[
            pltpu.VMEM((D, Q_W), FP8),
            pltpu.VMEM((D, 2 * KV_W), FP8),
            pltpu.VMEM((D, POOL_W), BF16),
            pltpu.VMEM((STAGE_SLOTS, D, STAGE_CHUNK), F32),
            pltpu.SemaphoreType.DMA((STAGE_SLOTS,)),
        ],
        compiler_params=_sequential_params(2),
        name="in_proj",
    )(x, mod, norm1_g, w_all, w_scale, q_gain, k_gain, ones_bd, cos_t, sin_t)


def _attn_kernel(sink_ref, q_ref, kp_ref, kc0_ref, kn_ref, vp_ref, vc0_ref, vn_ref,
                 kctx_ref, vctx_ref, bias_ref, o_ref, *, blocks_per_tile):
    t_idx = pl.program_id(1)
    n_tiles = pl.num_programs(1)
    k_all = jnp.concatenate([kp_ref[0], kc0_ref[0], kn_ref[0]], axis=0)
    v_all = jnp.concatenate([vp_ref[0], vc0_ref[0], vn_ref[0]], axis=0)
    k_ctx = kctx_ref[0]
    v_ctx = vctx_ref[0]
    col_amax = jnp.maximum(jnp.max(jnp.abs(v_all), axis=0, keepdims=True),
                           jnp.max(jnp.abs(v_ctx), axis=0, keepdims=True)).astype(F32)
    v_scale = _fp8_scale(jnp.max(col_amax, axis=1, keepdims=True))
    scale_row = jnp.broadcast_to(v_scale, (1, KV_W)).astype(BF16)
    v_all = v_all * scale_row
    v_ctx = v_ctx * scale_row
    one = jnp.ones((), BF16)
    lane = lax.broadcasted_iota(jnp.int32, (BLOCK, MXU_TILE), 1)
    in_head = [(lane >= kv * HEAD_DIM) & (lane < (kv + 1) * HEAD_DIM) for kv in range(N_KV_HEADS)]
    zero = jnp.zeros((), BF16)

    def with_ones(v, half):
        vl = lax.broadcasted_iota(jnp.int32, v.shape, 1)
        return jnp.where((vl < LANES) if half == 0 else (vl >= LANES), v, one).astype(FP8)

    v_ctx_pair = [with_ones(v_ctx, half) for half in range(2)]
    low_head = lax.broadcasted_iota(jnp.int32, (BLOCK, LANES), 1) < HEAD_DIM

    def block_operands(i):
        variant = jnp.int32(0)
        if i == blocks_per_tile - 1:
            variant = jnp.where(t_idx == n_tiles - 1, 2, variant)
        if i == 0:
            variant = jnp.where(t_idx == 0, 1, variant)
        v_win = v_all[i * BLOCK:(i + 3) * BLOCK]
        return dict(bias=bias_ref[variant], k_win=k_all[i * BLOCK:(i + 3) * BLOCK],
                    v_win_pair=[with_ones(v_win, half) for half in range(2)])

    def scores(blk, i, j):
        q_slab = q_ref[0, i * BLOCK:(i + 1) * BLOCK, j * MXU_TILE:(j + 1) * MXU_TILE]
        qz = jnp.concatenate([jnp.where(in_head[kv], q_slab, zero) for kv in range(N_KV_HEADS)], axis=0)
        return _dot_nt(qz, blk["k_win"]), _dot_nt(qz, k_ctx)

    def softmax_values(blk, i, j, s_win, s_ctx):
        p_win, p_ctx, sink_term = [], [], []
        for kv in range(N_KV_HEADS):
            rows = slice(kv * BLOCK, (kv + 1) * BLOCK)
            sw = jnp.concatenate([
                s_win[rows, :BLOCK] + blk["bias"][:, :BLOCK],
                s_win[rows, BLOCK:2 * BLOCK],
                s_win[rows, 2 * BLOCK:] + blk["bias"][:, 2 * BLOCK:]], axis=1)
            sc = s_ctx[rows]
            sink = sink_ref[kv * GROUP + j] * LOG2_E
            groups = [sw[:, g * LANES:(g + 1) * LANES] for g in range(sw.shape[1] // LANES)]
            groups += [sc[:, g * LANES:(g + 1) * LANES] for g in range(sc.shape[1] // LANES)]
            m = jnp.maximum(jnp.max(functools.reduce(jnp.maximum, groups), axis=-1, keepdims=True), sink)
            p_win.append(jnp.exp2(sw - m).astype(FP8))
            p_ctx.append(jnp.exp2(sc - m).astype(FP8))
            sink_term.append(jnp.exp2(sink - m))
        for half in range(2):
            pw = jnp.concatenate(p_win[2 * half:2 * half + 2], axis=0)
            pc = jnp.concatenate(p_ctx[2 * half:2 * half + 2], axis=0)
            o = _dot(pw, blk["v_win_pair"][half]) + _dot(pc, v_ctx_pair[half])
            vals = o[:, half * LANES:(half + 1) * LANES]
            sums = o[:, (1 - half) * LANES:(2 - half) * LANES]
            out = []
            for r in range(2):
                rows = slice(r * BLOCK, (r + 1) * BLOCK)
                out.append(vals[rows] / ((sums[rows] + sink_term[2 * half + r]) * v_scale))
            c0 = j * MXU_TILE + half * LANES
            o_ref[0, i * BLOCK:(i + 1) * BLOCK, c0:c0 + LANES] = jnp.where(low_head, out[0], out[1]).astype(BF16)

    blocks = [block_operands(i) for i in range(blocks_per_tile)]
    items = [(i, j) for i in range(blocks_per_tile) for j in range(GROUP)]
    s_next = scores(blocks[0], *items[0])
    for idx, (i, j) in enumerate(items):
        s_cur = s_next
        if idx + 1 < len(items):
            ni, nj = items[idx + 1]
            s_next = scores(blocks[ni], ni, nj)
        softmax_values(blocks[i], i, j, *s_cur)


def _attn_call(sink, q, k, v, kc, vc, bias, tile):
    B, S, _ = q.shape
    bpt = tile // BLOCK
    nb = S // BLOCK
    C = kc.shape[1]
    cur = lambda b, t, *_: (b, t, 0)
    prev = lambda b, t, *_: (b, jnp.maximum(t * bpt - 1, 0), 0)
    nxt = lambda b, t, *_: (b, jnp.minimum((t + 1) * bpt, nb - 1), 0)
    per_b = lambda b, t, *_: (b, 0, 0)
    halo = (1, BLOCK, KV_W)
    center = (1, tile, KV_W)
    grid_spec = pltpu.PrefetchScalarGridSpec(
        num_scalar_prefetch=1,
        grid=(B, S // tile),
        in_specs=[
            pl.BlockSpec((1, tile, Q_W), cur),
            pl.BlockSpec(halo, prev), pl.BlockSpec(center, cur), pl.BlockSpec(halo, nxt),
            pl.BlockSpec(halo, prev), pl.BlockSpec(center, cur), pl.BlockSpec(halo, nxt),
            pl.BlockSpec((1, C, KV_W), per_b),
            pl.BlockSpec((1, C, KV_W), per_b),
            pl.BlockSpec((3, BLOCK, 3 * BLOCK), lambda b, t, *_: (0, 0, 0)),
        ],
        out_specs=pl.BlockSpec((1, tile, Q_W), cur),
    )
    return pl.pallas_call(
        functools.partial(_attn_kernel, blocks_per_tile=bpt),
        grid_spec=grid_spec,
        out_shape=jax.ShapeDtypeStruct((B, S, Q_W), BF16),
        compiler_params=_params(2),
        name="attn",
    )(sink, q, k, k, k, v, v, v, kc, vc, bias)


def _merge_kernel(x_ref, mod_ref, g1_ref, attn_ref, up_ref, uc_ref, un_ref,
                  wall_hbm, gb_ref, wap_hbm, pw_ref, ps_ref, wpp_hbm, wo_hbm, o_ref,
                  uext_ref, wg_ref, wap_ref, wo_ref, wpool_ref, col_stage, row_stage, col_sem, row_sem,
                  *, seq, tile, sub):
    @pl.when((pl.program_id(0) == 0) & (pl.program_id(1) == 0))
    def _():
        _stage_weight_bf16(wpp_hbm, wpool_ref, row_stage, row_sem, axis=0)
        for g in range(POOL_GROUPS):
            rows = slice(g * POOL_GROUP_W, (g + 1) * POOL_GROUP_W)
            scaled = (pw_ref[g] * ps_ref[g]).astype(BF16)
            wpool_ref[rows, :] = _dot(scaled, wpool_ref[rows, :]).astype(BF16)

        _stage_weight_bf16(wall_hbm.at[:, pl.ds(2 * D_MODEL, 2 * D_MODEL)], wg_ref, col_stage, col_sem, axis=1)
        _stage_weight_bf16(wo_hbm, wo_ref, col_stage, col_sem, axis=1)

        def store_regrouped(kv, rows_f32):
            for j in range(GROUP):
                dst = (j * N_KV_HEADS + kv) * HEAD_DIM
                wap_ref[dst:dst + HEAD_DIM, :] = rows_f32[j * HEAD_DIM:(j + 1) * HEAD_DIM, :].astype(BF16)

        _stage_weight_bf16(wap_hbm, None, row_stage, row_sem, axis=0, store=store_regrouped)

    t_idx = pl.program_id(1)
    n_tiles = pl.num_programs(1)
    m = mod_ref[pl.ds(pl.program_id(0), 1), :]
    sh = m[:, 0:D_MODEL]
    gs = g1_ref[...] * (1.0 + m[:, D_MODEL:2 * D_MODEL])
    gate1 = m[:, 2 * D_MODEL:3 * D_MODEL]

    keep_prev = (t_idx > 0).astype(F32)
    keep_next = (t_idx < n_tiles - 1).astype(F32)
    uext_ref[0:POOL_HALO] = up_ref[0] * keep_prev
    uext_ref[POOL_HALO:POOL_HALO + tile] = uc_ref[0]
    uext_ref[POOL_HALO + tile:] = un_ref[0] * keep_next
    edge_rows = lax.broadcasted_iota(jnp.int32, (POOL_HALO, POOL_GROUP_W), 0)

    def attn_dot(r):
        return _dot(attn_ref[0, r * sub:(r + 1) * sub, :], wap_ref[...])

    def gate_dot(r):
        h = _rms_modulate(x_ref[0, r * sub:(r + 1) * sub, :], gs, sh).astype(BF16)
        return jax.nn.sigmoid(_dot(h, wg_ref[...]) + gb_ref[...])

    def pool_dot(r):
        r0 = r * sub
        pos_first = t_idx * tile + r0 + edge_rows
        pos_last = pos_first + (sub - POOL_HALO)
        diffs = []
        for g, w in enumerate(POOL_WINDOWS):
            cols = slice(g * POOL_GROUP_W, (g + 1) * POOL_GROUP_W)
            acc = uext_ref[pl.ds(r0, sub + 2 * POOL_HALO), cols]
            k = 1
            while k < w:
                acc = acc[:acc.shape[0] - k] + acc[k:]
                k *= 2
            win = acc[POOL_HALO - w // 2:POOL_HALO - w // 2 + sub]

            def inv_count(pos):
                return 1.0 / (jnp.minimum(pos + w // 2, seq) - jnp.maximum(pos - w // 2, 0)).astype(F32)

            pooled_avg = jnp.concatenate([
                win[:POOL_HALO] * inv_count(pos_first),
                win[POOL_HALO:sub - POOL_HALO] * (1.0 / w),
                win[sub - POOL_HALO:] * inv_count(pos_last)], axis=0)
            diffs.append((pooled_avg - uext_ref[pl.ds(r0 + POOL_HALO, sub), cols]).astype(BF16))
        return _dot(jnp.concatenate(diffs, axis=1), wpool_ref[...])

    def output(r, a, gates, p):
        rows = slice(r * sub, (r + 1) * sub)
        merged = (gates[:, :D_MODEL] * a + gates[:, D_MODEL:] * p).astype(BF16)
        o_ref[0, rows, :] = x_ref[0, rows, :] + gate1 * _dot(merged, wo_ref[...])

    n_sub = tile // sub
    a = [attn_dot(r) for r in range(n_sub)]
    p = [pool_dot(r) for r in range(n_sub)]
    gates = [gate_dot(r) for r in range(n_sub)]
    for r in range(n_sub):
        output(r, a[r], gates[r], p[r])


def _merge_call(x, mod, norm1_g, attn, u, w_all, gate_b, w_ap, pool_w, pool_scale3, w_pool_proj, w_out, tile):
    B, S, D = x.shape
    r = tile // POOL_HALO
    n_halo = S // POOL_HALO
    row = lambda b, t: (b, t, 0)
    return pl.pallas_call(
        functools.partial(_merge_kernel, seq=S, tile=tile, sub=SUB_TILE),
        grid=(B, S // tile),
        in_specs=[
            pl.BlockSpec((1, tile, D), row),
            _const_spec(mod.shape),
            _const_spec((1, D)),
            pl.BlockSpec((1, tile, Q_W), row),
            pl.BlockSpec((1, POOL_HALO, POOL_W), lambda b, t: (b, jnp.maximum(t * r - 1, 0), 0)),
            pl.BlockSpec((1, tile, POOL_W), row),
            pl.BlockSpec((1, POOL_HALO, POOL_W), lambda b, t: (b, jnp.minimum((t + 1) * r, n_halo - 1), 0)),
            pl.BlockSpec(memory_space=pl.ANY),
            _const_spec(gate_b.shape),
            pl.BlockSpec(memory_space=pl.ANY),
            _const_spec(pool_w.shape),
            _const_spec(pool_scale3.shape),
            pl.BlockSpec(memory_space=pl.ANY),
            pl.BlockSpec(memory_space=pl.ANY),
        ],
        out_specs=pl.BlockSpec((1, tile, D), row),
        out_shape=jax.ShapeDtypeStruct((B, S, D), F32),
        scratch_shapes=[
            pltpu.VMEM((tile + 2 * POOL_HALO, POOL_W), F32),
            pltpu.VMEM((D, 2 * D), BF16),
            pltpu.VMEM(w_ap.shape, BF16),
            pltpu.VMEM(w_out.shape, BF16),
            pltpu.VMEM(w_pool_proj.shape, BF16),
            pltpu.VMEM((STAGE_SLOTS, D, STAGE_CHUNK), F32),
            pltpu.VMEM((STAGE_SLOTS, STAGE_CHUNK, D), F32),
            pltpu.SemaphoreType.DMA((STAGE_SLOTS,)),
            pltpu.SemaphoreType.DMA((STAGE_SLOTS,)),
        ],
        compiler_params=_sequential_params(2),
        name="merge",
    )(x, mod, norm1_g, attn, u, u, u, w_all, gate_b, w_ap, pool_w, pool_scale3, w_pool_proj, w_out)


def _ffn_kernel(x_ref, mod_ref, g2_ref, wup_hbm, wdn_hbm, o_ref,
                wup_ref, wdn_ref, col_stage, row_stage, col_sem, row_sem, hmid_ref, *, d_ff, tile, sub):
    @pl.when((pl.program_id(0) == 0) & (pl.program_id(1) == 0))
    def _():
        _stage_weight_bf16(wup_hbm, wup_ref, col_stage, col_sem, axis=1)
        _stage_weight_bf16(wdn_hbm, wdn_ref, row_stage, row_sem, axis=0)

    m = mod_ref[pl.ds(pl.program_id(0), 1), :]
    sh = m[:, 3 * D_MODEL:4 * D_MODEL]
    gs = g2_ref[...] * (1.0 + m[:, 4 * D_MODEL:5 * D_MODEL])
    gate2 = m[:, 5 * D_MODEL:6 * D_MODEL]

    def up(r):
        rows = slice(r * sub, (r + 1) * sub)
        h = _rms_modulate(x_ref[0, rows, :], gs, sh).astype(BF16)
        for c in range(d_ff // MXU_TILE):
            a = _dot(h, wup_ref[:, c * MXU_TILE:(c + 1) * MXU_TILE])
            b = _dot(h, wup_ref[:, d_ff + c * MXU_TILE:d_ff + (c + 1) * MXU_TILE])
            hmid_ref[rows, c * MXU_TILE:(c + 1) * MXU_TILE] = (a * jax.nn.sigmoid(a) * b).astype(BF16)

    def down(r):
        rows = slice(r * sub, (r + 1) * sub)
        o_ref[0, rows, :] = x_ref[0, rows, :] + gate2 * _dot(hmid_ref[rows, :], wdn_ref[...])

    n_sub = tile // sub
    up(0)
    for r in range(1, n_sub):
        up(r)
        down(r - 1)
    down(n_sub - 1)


def _ffn_call(x, mod, norm2_g, w_up, w_down, tile):
    B, S, D = x.shape
    d_ff = w_down.shape[0]
    row = lambda b, t: (b, t, 0)
    return pl.pallas_call(
        functools.partial(_ffn_kernel, d_ff=d_ff, tile=tile, sub=SUB_TILE),
        grid=(B, S // tile),
        in_specs=[
            pl.BlockSpec((1, tile, D), row),
            _const_spec(mod.shape),
            _const_spec((1, D)),
            pl.BlockSpec(memory_space=pl.ANY),
            pl.BlockSpec(memory_space=pl.ANY),
        ],
        out_specs=pl.BlockSpec((1, tile, D), row),
        out_shape=jax.ShapeDtypeStruct((B, S, D), F32),
        scratch_shapes=[
            pltpu.VMEM(w_up.shape, BF16),
            pltpu.VMEM(w_down.shape, BF16),
            pltpu.VMEM((STAGE_SLOTS, D, STAGE_CHUNK), F32),
            pltpu.VMEM((STAGE_SLOTS, STAGE_CHUNK, D), F32),
            pltpu.SemaphoreType.DMA((STAGE_SLOTS,)),
            pltpu.SemaphoreType.DMA((STAGE_SLOTS,)),
            pltpu.VMEM((tile, d_ff), BF16),
        ],
        compiler_params=_sequential_params(2),
        name="ffn",
    )(x, mod, norm2_g, w_up, w_down)


def _rope_tables(seq):
    t = np.arange(seq)
    row = (t // GRID_W).astype(np.float64)
    col = (t % GRID_W).astype(np.float64)
    half = HEAD_DIM // 2
    inv_freq = 1.0 / (ROPE_THETA ** (np.arange(0, half, 2, dtype=np.float64) / half))
    ang_r = row[:, None] * inv_freq
    ang_c = col[:, None] * inv_freq
    cos = np.concatenate([np.cos(ang_r)] * 2 + [np.cos(ang_c)] * 2, axis=1)
    sin = np.concatenate([-np.sin(ang_r), np.sin(ang_r), -np.sin(ang_c), np.sin(ang_c)], axis=1)
    reps = (1, LANES // HEAD_DIM)
    return jnp.asarray(np.tile(cos, reps), dtype=F32), jnp.asarray(np.tile(sin, reps), dtype=F32)


def _band_bias():
    i = np.arange(BLOCK)[:, None]
    j = np.arange(3 * BLOCK)[None, :]
    band = np.abs(j - BLOCK - i) <= WINDOW
    first = band & (j >= BLOCK)
    last = band & (j < 2 * BLOCK)
    masks = np.stack([band, first, last])
    return jnp.asarray(np.where(masks, 0.0, NEG_BIG), dtype=F32)


def kernel(x, c, ctx, c_ctx, mod_w, mod_b, norm1_g, norm2_g, w_in, gate_b, q_norm_g, k_norm_g,
           sink, pool_w, pool_scale, w_attn_proj, w_pool_proj, w_out, w_up, w_down):
    B, S, D = x.shape
    depth = mod_w.shape[0]
    cos_t, sin_t = _rope_tables(S)
    bias = _band_bias()
    ones_bd = jnp.asarray(np.kron(np.eye(MXU_TILE // HEAD_DIM), np.ones((HEAD_DIM, HEAD_DIM))), dtype=BF16)
    assert depth == 1, "context-stream update between layers is not implemented"
    assert D == D_MODEL and S % ROW_TILE == 0 and B + 1 <= F32_SUBLANES

    for l in range(depth):
        mod = _mod_call(c, c_ctx[None, :], mod_w[l], mod_b[l][None, :])

        wl = w_in[l]
        w_scale = _fp8_scale(jnp.max(jnp.abs(wl[:, :Q_W + 2 * KV_W]))).reshape(1, 1)
        g1 = norm1_g[l][None, :]
        k_gain = k_norm_g[l][None, :]
        q_gain = q_norm_g[l][None, :]

        kc, vc = _ctx_call(ctx, mod, g1, wl, k_gain, ones_bd)
        q, k, v, u = _inproj_call(x, mod, g1, wl, w_scale, q_gain, k_gain, ones_bd, cos_t, sin_t, tile=ROW_TILE)
        attn = _attn_call(sink[l], q, k, v, kc, vc, bias, tile=ROW_TILE)
        x = _merge_call(
            x, mod, g1, attn, u, wl, gate_b[l][None, :], w_attn_proj[l], pool_w[l],
            pool_scale[l].reshape(POOL_GROUPS, 1, POOL_GROUP_W), w_pool_proj[l], w_out[l], tile=ROW_TILE)
        x = _ffn_call(x, mod, norm2_g[l][None, :], w_up[l], w_down[l], tile=ROW_TILE)
    return x
```

```python
import functools

import jax
import jax.numpy as jnp
import numpy as np
from jax import lax
from jax.experimental import pallas as pl
from jax.experimental.pallas import tpu as pltpu

D_MODEL = 1024
GRID_W = 64
HEAD_DIM = 64
N_HEADS = 16
N_KV_HEADS = 4
GROUP = N_HEADS // N_KV_HEADS
Q_W = N_HEADS * HEAD_DIM
KV_W = N_KV_HEADS * HEAD_DIM
WINDOW = 128
BLOCK = 128
ROPE_THETA = 10000.0
POOL_WINDOWS = (2, 4, 8, 16)
POOL_GROUPS = 4
POOL_W = D_MODEL // 2
POOL_GROUP_W = POOL_W // POOL_GROUPS
POOL_HALO = 8
ROPE_PAIR = HEAD_DIM // 4
EPS = 1e-6
NEG_BIG = -1e30
LOG2_E = 1.4426950408889634

LANES = 128
F32_SUBLANES = 8
MXU_TILE = 256
VMEM_LIMIT = 56 * 1024 * 1024
STAGE_SLOTS = 4
STAGE_CHUNK = MXU_TILE
ROW_TILE = 1024
SUB_TILE = MXU_TILE

F32 = jnp.float32
BF16 = jnp.bfloat16
FP8 = jnp.float8_e4m3fn
FP8_MAX = 448.0


def _params(n_parallel):
    return pltpu.CompilerParams(
        dimension_semantics=("parallel",) * n_parallel,
        vmem_limit_bytes=VMEM_LIMIT,
    )


def _const_spec(shape):
    nd = len(shape)
    return pl.BlockSpec(shape, lambda *_: (0,) * nd, pipeline_mode=pl.Buffered(1))


def _dot(a, b):
    return jnp.dot(a, b, preferred_element_type=F32)


def _dot_nt(a, b):
    return lax.dot_general(a, b, (((1,), (1,)), ((), ())), preferred_element_type=F32)


def _rms_modulate(x, gs, sh):
    rs = lax.rsqrt(jnp.mean(x * x, axis=-1, keepdims=True) + EPS)
    return x * rs * gs + sh


def _head_rms(t, ones_bd, unscale=1.0):
    ms = _dot((t * t).astype(BF16), ones_bd) * (unscale * unscale * (1.0 / HEAD_DIM))
    return lax.rsqrt(ms + EPS)


def _tile_heads(g):
    return jnp.concatenate([g] * N_KV_HEADS, axis=1)


def _stage_weight_bf16(w_hbm, w_ref, stage_ref, sem, axis, store=None):
    slots = stage_ref.shape[0]
    chunk = stage_ref.shape[1 + axis]
    n = w_hbm.shape[axis] // chunk

    def piece(ref, i):
        return ref.at[:, pl.ds(i * chunk, chunk)] if axis == 1 else ref.at[pl.ds(i * chunk, chunk), :]

    def copy(i):
        return pltpu.make_async_copy(piece(w_hbm, i), stage_ref.at[i % slots], sem.at[i % slots])

    for i in range(min(slots - 1, n)):
        copy(i).start()
    for i in range(n):
        if i + slots - 1 < n:
            copy(i + slots - 1).start()
        copy(i).wait()
        staged = stage_ref[i % slots]
        if store is None:
            piece(w_ref, i)[...] = staged.astype(BF16)
        else:
            store(i, staged)


def _fp8_scale(amax):
    return jnp.exp2(jnp.floor(jnp.log2(FP8_MAX / jnp.maximum(amax, 1e-30))))


def _sequential_params(n_axes):
    return pltpu.CompilerParams(dimension_semantics=("arbitrary",) * n_axes, vmem_limit_bytes=VMEM_LIMIT)


def _mod_kernel(c_ref, cctx_ref, w_ref, b_ref, o_ref, rows_ref):
    n_batch = c_ref.shape[0]
    rows_ref[...] = jnp.zeros_like(rows_ref)
    rows_ref[0:n_batch, :] = c_ref[...]
    rows_ref[n_batch:n_batch + 1, :] = cctx_ref[...]
    c = rows_ref[...]
    a = (c * jax.nn.sigmoid(c)).astype(BF16)
    o_ref[...] = _dot(a, w_ref[...].astype(BF16)) + b_ref[...]


def _mod_call(c, c_ctx, mod_w, mod_b):
    n = mod_w.shape[1]
    bn = 2 * D_MODEL
    return pl.pallas_call(
        _mod_kernel,
        grid=(n // bn,),
        in_specs=[
            pl.BlockSpec(c.shape, lambda j: (0, 0)),
            pl.BlockSpec((1, D_MODEL), lambda j: (0, 0)),
            pl.BlockSpec((D_MODEL, bn), lambda j: (0, j)),
            pl.BlockSpec((1, bn), lambda j: (0, j)),
        ],
        out_specs=pl.BlockSpec((F32_SUBLANES, bn), lambda j: (0, j)),
        out_shape=jax.ShapeDtypeStruct((F32_SUBLANES, n), F32),
        scratch_shapes=[pltpu.VMEM((F32_SUBLANES, D_MODEL), F32)],
        compiler_params=_params(1),
        name="mod",
    )(c, c_ctx, mod_w, mod_b)


def _ctx_kernel(ctx_ref, mod_ref, g1_ref, wall_hbm, kg_ref, ones_ref, kc_ref, vc_ref,
                w_ref, col_stage, col_sem, *, n_batch):
    @pl.when(pl.program_id(0) == 0)
    def _():
        _stage_weight_bf16(wall_hbm.at[:, pl.ds(Q_W, 2 * KV_W)], w_ref, col_stage, col_sem, axis=1)

    m = mod_ref[n_batch:n_batch + 1, :]
    sh = m[:, 0:D_MODEL]
    gs = g1_ref[...] * (1.0 + m[:, D_MODEL:2 * D_MODEL])
    h = _rms_modulate(ctx_ref[0], gs, sh).astype(BF16)
    kv = _dot(h, w_ref[...])
    k = kv[:, :KV_W]
    kc_ref[0] = (k * _head_rms(k, ones_ref[...]) * _tile_heads(kg_ref[...])).astype(BF16)
    vc_ref[0] = kv[:, KV_W:].astype(BF16)


def _ctx_call(ctx, mod, norm1_g, w_all, k_gain, ones_bd):
    B, C, D = ctx.shape
    rows = B * C
    kc, vc = pl.pallas_call(
        functools.partial(_ctx_kernel, n_batch=B),
        grid=(1,),
        in_specs=[
            pl.BlockSpec((1, rows, D), lambda i: (0, 0, 0)),
            _const_spec(mod.shape),
            _const_spec((1, D)),
            pl.BlockSpec(memory_space=pl.ANY),
            _const_spec((1, HEAD_DIM)),
            _const_spec((MXU_TILE, MXU_TILE)),
        ],
        out_specs=[
            pl.BlockSpec((1, rows, KV_W), lambda i: (0, 0, 0)),
            pl.BlockSpec((1, rows, KV_W), lambda i: (0, 0, 0)),
        ],
        out_shape=[
            jax.ShapeDtypeStruct((1, rows, KV_W), BF16),
            jax.ShapeDtypeStruct((1, rows, KV_W), BF16),
        ],
        scratch_shapes=[
            pltpu.VMEM((D, 2 * KV_W), BF16),
            pltpu.VMEM((STAGE_SLOTS, D, STAGE_CHUNK), F32),
            pltpu.SemaphoreType.DMA((STAGE_SLOTS,)),
        ],
        compiler_params=_sequential_params(1),
        name="ctx_kv",
    )(ctx.reshape(1, rows, D), mod, norm1_g, w_all, k_gain, ones_bd)
    return kc.reshape(B, C, KV_W), vc.reshape(B, C, KV_W)


def _rope(t, cos, sin_signed, first_half):
    swap = jnp.where(first_half, pltpu.roll(t, LANES - ROPE_PAIR, axis=1), pltpu.roll(t, ROPE_PAIR, axis=1))
    return t * cos + swap * sin_signed


def _inproj_kernel(x_ref, mod_ref, g1_ref, wall_hbm, qg_ref, kg_ref, ones_ref, cos_ref, sin_ref,
                   q_ref, k_ref, v_ref, u_ref, wq_ref, wkv_ref, wu_ref, wscale_ref, qkv_stage, qkv_sem,
                   col_stage, col_sem):
    @pl.when((pl.program_id(0) == 0) & (pl.program_id(1) == 0))
    def _():
        n_pieces = qkv_stage.shape[0]

        def copy(i):
            return pltpu.make_async_copy(wall_hbm.at[:, pl.ds(i * STAGE_CHUNK, STAGE_CHUNK)], qkv_stage.at[i],
                                         qkv_sem.at[i])

        for i in range(n_pieces):
            copy(i).start()
        amax = jnp.zeros((1, 1), F32)
        for i in range(n_pieces):
            copy(i).wait()
            amax = jnp.maximum(amax, jnp.max(jnp.abs(qkv_stage[i]), keepdims=True))
        scale = _fp8_scale(amax)
        wscale_ref[...] = scale
        for i in range(n_pieces):
            cols8 = (qkv_stage[i] * scale).astype(FP8)
            if i < N_KV_HEADS:
                for j in range(GROUP):
                    c0 = j * KV_W + i * HEAD_DIM
                    wq_ref[:, c0:c0 + HEAD_DIM] = cols8[:, j * HEAD_DIM:(j + 1) * HEAD_DIM]
            else:
                c0 = (i - N_KV_HEADS) * STAGE_CHUNK
                wkv_ref[:, c0:c0 + STAGE_CHUNK] = cols8
        _stage_weight_bf16(wall_hbm.at[:, pl.ds(Q_W + 2 * KV_W, POOL_W)], wu_ref, col_stage, col_sem, axis=1)

    w_scale = wscale_ref[...]

    m = mod_ref[pl.ds(pl.program_id(0), 1), :]
    sh = m[:, 0:D_MODEL]
    gs = g1_ref[...] * (1.0 + m[:, D_MODEL:2 * D_MODEL])
    h32 = _rms_modulate(x_ref[0], gs, sh)
    h = h32.astype(BF16)
    h_scale = _fp8_scale(jnp.max(jnp.abs(h32), keepdims=True))
    h8 = (h32 * h_scale).astype(FP8)
    unscale = 1.0 / (h_scale * w_scale)
    ones_bd = ones_ref[...]
    q_gain = _tile_heads(qg_ref[...]) * (HEAD_DIM ** -0.5 * LOG2_E)
    k_gain = _tile_heads(kg_ref[...])
    cos = cos_ref[...]
    sin = sin_ref[...]
    lane = lax.broadcasted_iota(jnp.int32, cos.shape, 1)
    first_half = (lane & (2 * ROPE_PAIR - 1)) < ROPE_PAIR

    def normed_rope(t, gain):
        t = t * _head_rms(t, ones_bd, unscale) * (gain * unscale)
        halves = [_rope(t[:, i * LANES:(i + 1) * LANES], cos, sin, first_half) for i in range(2)]
        return jnp.concatenate(halves, axis=1).astype(BF16)

    wide = 2 * MXU_TILE
    for j in range(Q_W // wide):
        t = _dot(h8, wq_ref[:, j * wide:(j + 1) * wide])
        for i in range(2):
            c0 = j * wide + i * MXU_TILE
            q_ref[0, :, c0:c0 + MXU_TILE] = normed_rope(t[:, i * MXU_TILE:(i + 1) * MXU_TILE], q_gain)
    kv = _dot(h8, wkv_ref[...])
    k_ref[0] = normed_rope(kv[:, :KV_W], k_gain)
    v_ref[0] = (kv[:, KV_W:] * unscale).astype(BF16)
    u_ref[0] = _dot(h, wu_ref[...])


def _inproj_call(x, mod, norm1_g, w_all, q_gain, k_gain, ones_bd, cos_t, sin_t, tile):
    B, S, D = x.shape
    assert STAGE_CHUNK == GROUP * HEAD_DIM
    qkv_pieces = (Q_W + 2 * KV_W) // STAGE_CHUNK
    row = lambda b, t: (b, t, 0)
    return pl.pallas_call(
        _inproj_kernel,
        grid=(B, S // tile),
        in_specs=[
            pl.BlockSpec((1, tile, D), row),
            _const_spec(mod.shape),
            _const_spec((1, D)),
            pl.BlockSpec(memory_space=pl.ANY),
            _const_spec((1, HEAD_DIM)),
            _const_spec((1, HEAD_DIM)),
            _const_spec((MXU_TILE, MXU_TILE)),
            pl.BlockSpec((tile, LANES), lambda b, t: (t, 0)),
            pl.BlockSpec((tile, LANES), lambda b, t: (t, 0)),
        ],
        out_specs=[
            pl.BlockSpec((1, tile, Q_W), row),
            pl.BlockSpec((1, tile, KV_W), row),
            pl.BlockSpec((1, tile, KV_W), row),
            pl.BlockSpec((1, tile, POOL_W), row),
        ],
        out_shape=[
            jax.ShapeDtypeStruct((B, S, Q_W), BF16),
            jax.ShapeDtypeStruct((B, S, KV_W), BF16),
            jax.ShapeDtypeStruct((B, S, KV_W), BF16),
            jax.ShapeDtypeStruct((B, S, POOL_W), F32),
        ],
        scratch_shapes=[
            pltpu.VMEM((D, Q_W), FP8),
            pltpu.VMEM((D, 2 * KV_W), FP8),
            pltpu.VMEM((D, POOL_W), BF16),
            pltpu.VMEM((1, 1), F32),
            pltpu.VMEM((qkv_pieces, D, STAGE_CHUNK), F32),
            pltpu.SemaphoreType.DMA((qkv_pieces,)),
            pltpu.VMEM((STAGE_SLOTS, D, STAGE_CHUNK), F32),
            pltpu.SemaphoreType.DMA((STAGE_SLOTS,)),
        ],
        compiler_params=_sequential_params(2),
        name="in_proj",
    )(x, mod, norm1_g, w_all, q_gain, k_gain, ones_bd, cos_t, sin_t)


def _attn_kernel(sink_ref, q_ref, kp_ref, kc0_ref, kn_ref, vp_ref, vc0_ref, vn_ref,
                 kctx_ref, vctx_ref, bias_ref, o_ref, *, blocks_per_tile):
    t_idx = pl.program_id(1)
    n_tiles = pl.num_programs(1)
    k_all = jnp.concatenate([kp_ref[0], kc0_ref[0], kn_ref[0]], axis=0)
    v_all = jnp.concatenate([vp_ref[0], vc0_ref[0], vn_ref[0]], axis=0)
    k_ctx = kctx_ref[0]
    v_ctx = vctx_ref[0]
    col_amax = jnp.maximum(jnp.max(jnp.abs(v_all), axis=0, keepdims=True),
                           jnp.max(jnp.abs(v_ctx), axis=0, keepdims=True)).astype(F32)
    v_scale = _fp8_scale(jnp.max(col_amax, axis=1, keepdims=True))
    scale_row = jnp.broadcast_to(v_scale, (1, KV_W)).astype(BF16)
    v_all = v_all * scale_row
    v_ctx = v_ctx * scale_row
    one = jnp.ones((), BF16)
    lane = lax.broadcasted_iota(jnp.int32, (BLOCK, MXU_TILE), 1)
    in_head = [(lane >= kv * HEAD_DIM) & (lane < (kv + 1) * HEAD_DIM) for kv in range(N_KV_HEADS)]
    zero = jnp.zeros((), BF16)

    def with_ones(v, half):
        vl = lax.broadcasted_iota(jnp.int32, v.shape, 1)
        return jnp.where((vl < LANES) if half == 0 else (vl >= LANES), v, one).astype(FP8)

    v_ctx_pair = [with_ones(v_ctx, half) for half in range(2)]
    low_head = lax.broadcasted_iota(jnp.int32, (BLOCK, LANES), 1) < HEAD_DIM

    def block_operands(i):
        variant = jnp.int32(0)
        if i == blocks_per_tile - 1:
            variant = jnp.where(t_idx == n_tiles - 1, 2, variant)
        if i == 0:
            variant = jnp.where(t_idx == 0, 1, variant)
        v_win = v_all[i * BLOCK:(i + 3) * BLOCK]
        return dict(bias=bias_ref[variant], k_win=k_all[i * BLOCK:(i + 3) * BLOCK],
                    v_win_pair=[with_ones(v_win, half) for half in range(2)])

    def scores(blk, i, j):
        q_slab = q_ref[0, i * BLOCK:(i + 1) * BLOCK, j * MXU_TILE:(j + 1) * MXU_TILE]
        qz = jnp.concatenate([jnp.where(in_head[kv], q_slab, zero) for kv in range(N_KV_HEADS)], axis=0)
        return _dot_nt(qz, blk["k_win"]), _dot_nt(qz, k_ctx)

    def softmax_values(blk, i, j, s_win, s_ctx):
        p_win, p_ctx, sink_term = [], [], []
        for kv in range(N_KV_HEADS):
            rows = slice(kv * BLOCK, (kv + 1) * BLOCK)
            sw = jnp.concatenate([
                s_win[rows, :BLOCK] + blk["bias"][:, :BLOCK],
                s_win[rows, BLOCK:2 * BLOCK],
                s_win[rows, 2 * BLOCK:] + blk["bias"][:, 2 * BLOCK:]], axis=1)
            sc = s_ctx[rows]
            sink = sink_ref[kv * GROUP + j] * LOG2_E
            groups = [sw[:, g * LANES:(g + 1) * LANES] for g in range(sw.shape[1] // LANES)]
            groups += [sc[:, g * LANES:(g + 1) * LANES] for g in range(sc.shape[1] // LANES)]
            m = jnp.maximum(jnp.max(functools.reduce(jnp.maximum, groups), axis=-1, keepdims=True), sink)
            p_win.append(jnp.exp2(sw - m).astype(FP8))
            p_ctx.append(jnp.exp2(sc - m).astype(FP8))
            sink_term.append(jnp.exp2(sink - m))
        for half in range(2):
            pw = jnp.concatenate(p_win[2 * half:2 * half + 2], axis=0)
            pc = jnp.concatenate(p_ctx[2 * half:2 * half + 2], axis=0)
            o = _dot(pw, blk["v_win_pair"][half]) + _dot(pc, v_ctx_pair[half])
            vals = o[:, half * LANES:(half + 1) * LANES]
            sums = o[:, (1 - half) * LANES:(2 - half) * LANES]
            out = []
            for r in range(2):
                rows = slice(r * BLOCK, (r + 1) * BLOCK)
                out.append(vals[rows] / ((sums[rows] + sink_term[2 * half + r]) * v_scale))
            c0 = j * MXU_TILE + half * LANES
            o_ref[0, i * BLOCK:(i + 1) * BLOCK, c0:c0 + LANES] = jnp.where(low_head, out[0], out[1]).astype(BF16)

    blocks = [block_operands(i) for i in range(blocks_per_tile)]
    items = [(i, j) for i in range(blocks_per_tile) for j in range(GROUP)]
    s_next = scores(blocks[0], *items[0])
    for idx, (i, j) in enumerate(items):
        s_cur = s_next
        if idx + 1 < len(items):
            ni, nj = items[idx + 1]
            s_next = scores(blocks[ni], ni, nj)
        softmax_values(blocks[i], i, j, *s_cur)


def _attn_call(sink, q, k, v, kc, vc, bias, tile):
    B, S, _ = q.shape
    bpt = tile // BLOCK
    nb = S // BLOCK
    C = kc.shape[1]
    cur = lambda b, t, *_: (b, t, 0)
    prev = lambda b, t, *_: (b, jnp.maximum(t * bpt - 1, 0), 0)
    nxt = lambda b, t, *_: (b, jnp.minimum((t + 1) * bpt, nb - 1), 0)
    per_b = lambda b, t, *_: (b, 0, 0)
    halo = (1, BLOCK, KV_W)
    center = (1, tile, KV_W)
    grid_spec = pltpu.PrefetchScalarGridSpec(
        num_scalar_prefetch=1,
        grid=(B, S // tile),
        in_specs=[
            pl.BlockSpec((1, tile, Q_W), cur),
            pl.BlockSpec(halo, prev), pl.BlockSpec(center, cur), pl.BlockSpec(halo, nxt),
            pl.BlockSpec(halo, prev), pl.BlockSpec(center, cur), pl.BlockSpec(halo, nxt),
            pl.BlockSpec((1, C, KV_W), per_b),
            pl.BlockSpec((1, C, KV_W), per_b),
            pl.BlockSpec((3, BLOCK, 3 * BLOCK), lambda b, t, *_: (0, 0, 0)),
        ],
        out_specs=pl.BlockSpec((1, tile, Q_W), cur),
    )
    return pl.pallas_call(
        functools.partial(_attn_kernel, blocks_per_tile=bpt),
        grid_spec=grid_spec,
        out_shape=jax.ShapeDtypeStruct((B, S, Q_W), BF16),
        compiler_params=_params(2),
        name="attn",
    )(sink, q, k, k, k, v, v, v, kc, vc, bias)


def _merge_kernel(x_ref, mod_ref, g1_ref, attn_ref, up_ref, uc_ref, un_ref,
                  wall_hbm, gb_ref, wap_hbm, pw_ref, ps_ref, wpp_hbm, wo_hbm, o_ref,
                  uext_ref, wg_ref, wap_ref, wo_ref, wpool_ref, col_stage, row_stage, col_sem, row_sem,
                  *, seq, tile, sub):
    @pl.when((pl.program_id(0) == 0) & (pl.program_id(1) == 0))
    def _():
        _stage_weight_bf16(wpp_hbm, wpool_ref, row_stage, row_sem, axis=0)
        for g in range(POOL_GROUPS):
            rows = slice(g * POOL_GROUP_W, (g + 1) * POOL_GROUP_W)
            scaled = (pw_ref[g] * ps_ref[g]).astype(BF16)
            wpool_ref[rows, :] = _dot(scaled, wpool_ref[rows, :]).astype(BF16)

        _stage_weight_bf16(wall_hbm.at[:, pl.ds(2 * D_MODEL, 2 * D_MODEL)], wg_ref, col_stage, col_sem, axis=1)
        _stage_weight_bf16(wo_hbm, wo_ref, col_stage, col_sem, axis=1)

        def store_regrouped(kv, rows_f32):
            for j in range(GROUP):
                dst = (j * N_KV_HEADS + kv) * HEAD_DIM
                wap_ref[dst:dst + HEAD_DIM, :] = rows_f32[j * HEAD_DIM:(j + 1) * HEAD_DIM, :].astype(BF16)

        _stage_weight_bf16(wap_hbm, None, row_stage, row_sem, axis=0, store=store_regrouped)

    t_idx = pl.program_id(1)
    n_tiles = pl.num_programs(1)
    m = mod_ref[pl.ds(pl.program_id(0), 1), :]
    sh = m[:, 0:D_MODEL]
    gs = g1_ref[...] * (1.0 + m[:, D_MODEL:2 * D_MODEL])
    gate1 = m[:, 2 * D_MODEL:3 * D_MODEL]

    keep_prev = (t_idx > 0).astype(F32)
    keep_next = (t_idx < n_tiles - 1).astype(F32)
    uext_ref[0:POOL_HALO] = up_ref[0] * keep_prev
    uext_ref[POOL_HALO:POOL_HALO + tile] = uc_ref[0]
    uext_ref[POOL_HALO + tile:] = un_ref[0] * keep_next
    edge_rows = lax.broadcasted_iota(jnp.int32, (POOL_HALO, POOL_GROUP_W), 0)

    def attn_dot(r):
        return _dot(attn_ref[0, r * sub:(r + 1) * sub, :], wap_ref[...])

    def gate_dot(r):
        h = _rms_modulate(x_ref[0, r * sub:(r + 1) * sub, :], gs, sh).astype(BF16)
        return jax.nn.sigmoid(_dot(h, wg_ref[...]) + gb_ref[...])

    def pool_dot(r):
        r0 = r * sub
        pos_first = t_idx * tile + r0 + edge_rows
        pos_last = pos_first + (sub - POOL_HALO)
        diffs = []
        for g, w in enumerate(POOL_WINDOWS):
            cols = slice(g * POOL_GROUP_W, (g + 1) * POOL_GROUP_W)
            acc = uext_ref[pl.ds(r0, sub + 2 * POOL_HALO), cols]
            k = 1
            while k < w:
                acc = acc[:acc.shape[0] - k] + acc[k:]
                k *= 2
            win = acc[POOL_HALO - w // 2:POOL_HALO - w // 2 + sub]

            def inv_count(pos):
                return 1.0 / (jnp.minimum(pos + w // 2, seq) - jnp.maximum(pos - w // 2, 0)).astype(F32)

            pooled_avg = jnp.concatenate([
                win[:POOL_HALO] * inv_count(pos_first),
                win[POOL_HALO:sub - POOL_HALO] * (1.0 / w),
                win[sub - POOL_HALO:] * inv_count(pos_last)], axis=0)
            diffs.append((pooled_avg - uext_ref[pl.ds(r0 + POOL_HALO, sub), cols]).astype(BF16))
        return _dot(jnp.concatenate(diffs, axis=1), wpool_ref[...])

    def output(r, a, gates, p):
        rows = slice(r * sub, (r + 1) * sub)
        merged = (gates[:, :D_MODEL] * a + gates[:, D_MODEL:] * p).astype(BF16)
        o_ref[0, rows, :] = x_ref[0, rows, :] + gate1 * _dot(merged, wo_ref[...])

    n_sub = tile // sub
    a = [attn_dot(r) for r in range(n_sub)]
    p = [pool_dot(r) for r in range(n_sub)]
    gates = [gate_dot(r) for r in range(n_sub)]
    for r in range(n_sub):
        output(r, a[r], gates[r], p[r])


def _merge_call(x, mod, norm1_g, attn, u, w_all, gate_b, w_ap, pool_w, pool_scale3, w_pool_proj, w_out, tile):
    B, S, D = x.shape
    r = tile // POOL_HALO
    n_halo = S // POOL_HALO
    row = lambda b, t: (b, t, 0)
    return pl.pallas_call(
        functools.partial(_merge_kernel, seq=S, tile=tile, sub=SUB_TILE),
        grid=(B, S // tile),
        in_specs=[
            pl.BlockSpec((1, tile, D), row),
            _const_spec(mod.shape),
            _const_spec((1, D)),
            pl.BlockSpec((1, tile, Q_W), row),
            pl.BlockSpec((1, POOL_HALO, POOL_W), lambda b, t: (b, jnp.maximum(t * r - 1, 0), 0)),
            pl.BlockSpec((1, tile, POOL_W), row),
            pl.BlockSpec((1, POOL_HALO, POOL_W), lambda b, t: (b, jnp.minimum((t + 1) * r, n_halo - 1), 0)),
            pl.BlockSpec(memory_space=pl.ANY),
            _const_spec(gate_b.shape),
            pl.BlockSpec(memory_space=pl.ANY),
            _const_spec(pool_w.shape),
            _const_spec(pool_scale3.shape),
            pl.BlockSpec(memory_space=pl.ANY),
            pl.BlockSpec(memory_space=pl.ANY),
        ],
        out_specs=pl.BlockSpec((1, tile, D), row),
        out_shape=jax.ShapeDtypeStruct((B, S, D), F32),
        scratch_shapes=[
            pltpu.VMEM((tile + 2 * POOL_HALO, POOL_W), F32),
            pltpu.VMEM((D, 2 * D), BF16),
            pltpu.VMEM(w_ap.shape, BF16),
            pltpu.VMEM(w_out.shape, BF16),
            pltpu.VMEM(w_pool_proj.shape, BF16),
            pltpu.VMEM((STAGE_SLOTS, D, STAGE_CHUNK), F32),
            pltpu.VMEM((STAGE_SLOTS, STAGE_CHUNK, D), F32),
            pltpu.SemaphoreType.DMA((STAGE_SLOTS,)),
            pltpu.SemaphoreType.DMA((STAGE_SLOTS,)),
        ],
        compiler_params=_sequential_params(2),
        name="merge",
    )(x, mod, norm1_g, attn, u, u, u, w_all, gate_b, w_ap, pool_w, pool_scale3, w_pool_proj, w_out)


def _ffn_kernel(x_ref, mod_ref, g2_ref, wup_hbm, wdn_hbm, o_ref,
                wup_ref, wdn_ref, col_stage, row_stage, col_sem, row_sem, hmid_ref, *, d_ff, tile, sub):
    @pl.when((pl.program_id(0) == 0) & (pl.program_id(1) == 0))
    def _():
        _stage_weight_bf16(wup_hbm, wup_ref, col_stage, col_sem, axis=1)
        _stage_weight_bf16(wdn_hbm, wdn_ref, row_stage, row_sem, axis=0)

    m = mod_ref[pl.ds(pl.program_id(0), 1), :]
    sh = m[:, 3 * D_MODEL:4 * D_MODEL]
    gs = g2_ref[...] * (1.0 + m[:, 4 * D_MODEL:5 * D_MODEL])
    gate2 = m[:, 5 * D_MODEL:6 * D_MODEL]

    def up(r):
        rows = slice(r * sub, (r + 1) * sub)
        h = _rms_modulate(x_ref[0, rows, :], gs, sh).astype(BF16)
        for c in range(d_ff // MXU_TILE):
            a = _dot(h, wup_ref[:, c * MXU_TILE:(c + 1) * MXU_TILE])
            b = _dot(h, wup_ref[:, d_ff + c * MXU_TILE:d_ff + (c + 1) * MXU_TILE])
            hmid_ref[rows, c * MXU_TILE:(c + 1) * MXU_TILE] = (a * jax.nn.sigmoid(a) * b).astype(BF16)

    def down(r):
        rows = slice(r * sub, (r + 1) * sub)
        o_ref[0, rows, :] = x_ref[0, rows, :] + gate2 * _dot(hmid_ref[rows, :], wdn_ref[...])

    n_sub = tile // sub
    up(0)
    for r in range(1, n_sub):
        up(r)
        down(r - 1)
    down(n_sub - 1)


def _ffn_call(x, mod, norm2_g, w_up, w_down, tile):
    B, S, D = x.shape
    d_ff = w_down.shape[0]
    row = lambda b, t: (b, t, 0)
    return pl.pallas_call(
        functools.partial(_ffn_kernel, d_ff=d_ff, tile=tile, sub=SUB_TILE),
        grid=(B, S // tile),
        in_specs=[
            pl.BlockSpec((1, tile, D), row),
            _const_spec(mod.shape),
            _const_spec((1, D)),
            pl.BlockSpec(memory_space=pl.ANY),
            pl.BlockSpec(memory_space=pl.ANY),
        ],
        out_specs=pl.BlockSpec((1, tile, D), row),
        out_shape=jax.ShapeDtypeStruct((B, S, D), F32),
        scratch_shapes=[
            pltpu.VMEM(w_up.shape, BF16),
            pltpu.VMEM(w_down.shape, BF16),
            pltpu.VMEM((STAGE_SLOTS, D, STAGE_CHUNK), F32),
            pltpu.VMEM((STAGE_SLOTS, STAGE_CHUNK, D), F32),
            pltpu.SemaphoreType.DMA((STAGE_SLOTS,)),
            pltpu.SemaphoreType.DMA((STAGE_SLOTS,)),
            pltpu.VMEM((tile, d_ff), BF16),
        ],
        compiler_params=_sequential_params(2),
        name="ffn",
    )(x, mod, norm2_g, w_up, w_down)


def _rope_tables(seq):
    t = np.arange(seq)
    row = (t // GRID_W).astype(np.float64)
    col = (t % GRID_W).astype(np.float64)
    half = HEAD_DIM // 2
    inv_freq = 1.0 / (ROPE_THETA ** (np.arange(0, half, 2, dtype=np.float64) / half))
    ang_r = row[:, None] * inv_freq
    ang_c = col[:, None] * inv_freq
    cos = np.concatenate([np.cos(ang_r)] * 2 + [np.cos(ang_c)] * 2, axis=1)
    sin = np.concatenate([-np.sin(ang_r), np.sin(ang_r), -np.sin(ang_c), np.sin(ang_c)], axis=1)
    reps = (1, LANES // HEAD_DIM)
    return jnp.asarray(np.tile(cos, reps), dtype=F32), jnp.asarray(np.tile(sin, reps), dtype=F32)


def _band_bias():
    i = np.arange(BLOCK)[:, None]
    j = np.arange(3 * BLOCK)[None, :]
    band = np.abs(j - BLOCK - i) <= WINDOW
    first = band & (j >= BLOCK)
    last = band & (j < 2 * BLOCK)
    masks = np.stack([band, first, last])
    return jnp.asarray(np.where(masks, 0.0, NEG_BIG), dtype=F32)


def kernel(x, c, ctx, c_ctx, mod_w, mod_b, norm1_g, norm2_g, w_in, gate_b, q_norm_g, k_norm_g,
           sink, pool_w, pool_scale, w_attn_proj, w_pool_proj, w_out, w_up, w_down):
    B, S, D = x.shape
    depth = mod_w.shape[0]
    cos_t, sin_t = _rope_tables(S)
    bias = _band_bias()
    ones_bd = jnp.asarray(np.kron(np.eye(MXU_TILE // HEAD_DIM), np.ones((HEAD_DIM, HEAD_DIM))), dtype=BF16)
    assert depth == 1, "context-stream update between layers is not implemented"
    assert D == D_MODEL and S % ROW_TILE == 0 and B + 1 <= F32_SUBLANES

    for l in range(depth):
        mod = _mod_call(c, c_ctx[None, :], mod_w[l], mod_b[l][None, :])

        wl = w_in[l]
        g1 = norm1_g[l][None, :]
        k_gain = k_norm_g[l][None, :]
        q_gain = q_norm_g[l][None, :]

        kc, vc = _ctx_call(ctx, mod, g1, wl, k_gain, ones_bd)
        q, k, v, u = _inproj_call(x, mod, g1, wl, q_gain, k_gain, ones_bd, cos_t, sin_t, tile=ROW_TILE)
        attn = _attn_call(sink[l], q, k, v, kc, vc, bias, tile=ROW_TILE)
        x = _merge_call(
            x, mod, g1, attn, u, wl, gate_b[l][None, :], w_attn_proj[l], pool_w[l],
            pool_scale[l].reshape(POOL_GROUPS, 1, POOL_GROUP_W), w_pool_proj[l], w_out[l], tile=ROW_TILE)
        x = _ffn_call(x, mod, norm2_g[l][None, :], w_up[l], w_down[l], tile=ROW_TILE)
    return x
```

```python
import functools

import jax
import jax.numpy as jnp
import numpy as np
from jax import lax
from jax.experimental import pallas as pl
from jax.experimental.pallas import tpu as pltpu

D_MODEL = 1024
GRID_W = 64
HEAD_DIM = 64
N_HEADS = 16
N_KV_HEADS = 4
GROUP = N_HEADS // N_KV_HEADS
Q_W = N_HEADS * HEAD_DIM
KV_W = N_KV_HEADS * HEAD_DIM
WINDOW = 128
BLOCK = 128
ROPE_THETA = 10000.0
POOL_WINDOWS = (2, 4, 8, 16)
POOL_GROUPS = 4
POOL_W = D_MODEL // 2
POOL_GROUP_W = POOL_W // POOL_GROUPS
POOL_HALO = 8
ROPE_PAIR = HEAD_DIM // 4
EPS = 1e-6
NEG_BIG = -1e30
LOG2_E = 1.4426950408889634

LANES = 128
F32_SUBLANES = 8
MXU_TILE = 256
VMEM_LIMIT = 56 * 1024 * 1024
STAGE_SLOTS = 4
STAGE_CHUNK = MXU_TILE
ROW_TILE = 1024
SUB_TILE = MXU_TILE

F32 = jnp.float32
BF16 = jnp.bfloat16
FP8 = jnp.float8_e4m3fn
FP8_MAX = 448.0


def _params(n_parallel):
    return pltpu.CompilerParams(
        dimension_semantics=("parallel",) * n_parallel,
        vmem_limit_bytes=VMEM_LIMIT,
    )


def _const_spec(shape):
    nd = len(shape)
    return pl.BlockSpec(shape, lambda *_: (0,) * nd, pipeline_mode=pl.Buffered(1))


def _dot(a, b):
    return jnp.dot(a, b, preferred_element_type=F32)


def _dot_nt(a, b):
    return lax.dot_general(a, b, (((1,), (1,)), ((), ())), preferred_element_type=F32)


def _rms_modulate(x, gs, sh):
    rs = lax.rsqrt(jnp.mean(x * x, axis=-1, keepdims=True) + EPS)
    return x * rs * gs + sh


def _head_rms(t, ones_bd, unscale=1.0):
    ms = _dot((t * t).astype(BF16), ones_bd) * (unscale * unscale * (1.0 / HEAD_DIM))
    return lax.rsqrt(ms + EPS)


def _tile_heads(g):
    return jnp.concatenate([g] * N_KV_HEADS, axis=1)


def _stage_weight_bf16(w_hbm, w_ref, stage_ref, sem, axis, store=None):
    slots = stage_ref.shape[0]
    chunk = stage_ref.shape[1 + axis]
    n = w_hbm.shape[axis] // chunk

    def piece(ref, i):
        return ref.at[:, pl.ds(i * chunk, chunk)] if axis == 1 else ref.at[pl.ds(i * chunk, chunk), :]

    def copy(i):
        return pltpu.make_async_copy(piece(w_hbm, i), stage_ref.at[i % slots], sem.at[i % slots])

    for i in range(min(slots - 1, n)):
        copy(i).start()
    for i in range(n):
        if i + slots - 1 < n:
            copy(i + slots - 1).start()
        copy(i).wait()
        staged = stage_ref[i % slots]
        if store is None:
            piece(w_ref, i)[...] = staged.astype(BF16)
        else:
            store(i, staged)


def _fp8_scale(amax):
    return jnp.exp2(jnp.floor(jnp.log2(FP8_MAX / jnp.maximum(amax, 1e-30))))


def _sequential_params(n_axes):
    return pltpu.CompilerParams(dimension_semantics=("arbitrary",) * n_axes, vmem_limit_bytes=VMEM_LIMIT)


def _mod_ctx_kernel(c_ref, cctx_ref, w_ref, b_ref, ctx_ref, g1_ref, wall_hbm, kg_ref, ones_ref,
                    o_ref, kc_ref, vc_ref, rows_ref, ctx_mod_ref, wkv_ref, col_stage, col_sem):
    n_batch = c_ref.shape[0]
    rows_ref[...] = jnp.zeros_like(rows_ref)
    rows_ref[0:n_batch, :] = c_ref[...]
    rows_ref[n_batch:n_batch + 1, :] = cctx_ref[...]
    c = rows_ref[...]
    a = (c * jax.nn.sigmoid(c)).astype(BF16)
    o_ref[...] = _dot(a, w_ref[...].astype(BF16)) + b_ref[...]

    @pl.when(pl.program_id(0) == 0)
    def _():
        ctx_mod_ref[...] = o_ref[n_batch:n_batch + 1, :]
        _stage_weight_bf16(wall_hbm.at[:, pl.ds(Q_W, 2 * KV_W)], wkv_ref, col_stage, col_sem, axis=1)

    @pl.when(pl.program_id(0) == 1)
    def _():
        m = ctx_mod_ref[...]
        sh = m[:, 0:D_MODEL]
        gs = g1_ref[...] * (1.0 + m[:, D_MODEL:2 * D_MODEL])
        h = _rms_modulate(ctx_ref[0], gs, sh).astype(BF16)
        kv = _dot(h, wkv_ref[...])
        k = kv[:, :KV_W]
        kc_ref[0] = (k * _head_rms(k, ones_ref[...]) * _tile_heads(kg_ref[...])).astype(BF16)
        vc_ref[0] = kv[:, KV_W:].astype(BF16)


def _mod_ctx_call(c, c_ctx, mod_w, mod_b, ctx, norm1_g, w_all, k_gain, ones_bd):
    n = mod_w.shape[1]
    bn = 2 * D_MODEL
    B, C, D = ctx.shape
    rows = B * C
    assert n // bn >= 2
    whole = lambda j: (0, 0, 0)
    mod, kc, vc = pl.pallas_call(
        _mod_ctx_kernel,
        grid=(n // bn,),
        in_specs=[
            pl.BlockSpec(c.shape, lambda j: (0, 0)),
            pl.BlockSpec((1, D_MODEL), lambda j: (0, 0)),
            pl.BlockSpec((D_MODEL, bn), lambda j: (0, j)),
            pl.BlockSpec((1, bn), lambda j: (0, j)),
            _const_spec((1, rows, D)),
            _const_spec((1, D)),
            pl.BlockSpec(memory_space=pl.ANY),
            _const_spec((1, HEAD_DIM)),
            _const_spec((MXU_TILE, MXU_TILE)),
        ],
        out_specs=[
            pl.BlockSpec((F32_SUBLANES, bn), lambda j: (0, j)),
            pl.BlockSpec((1, rows, KV_W), whole),
            pl.BlockSpec((1, rows, KV_W), whole),
        ],
        out_shape=[
            jax.ShapeDtypeStruct((F32_SUBLANES, n), F32),
            jax.ShapeDtypeStruct((1, rows, KV_W), BF16),
            jax.ShapeDtypeStruct((1, rows, KV_W), BF16),
        ],
        scratch_shapes=[
            pltpu.VMEM((F32_SUBLANES, D_MODEL), F32),
            pltpu.VMEM((1, bn), F32),
            pltpu.VMEM((D, 2 * KV_W), BF16),
            pltpu.VMEM((STAGE_SLOTS, D, STAGE_CHUNK), F32),
            pltpu.SemaphoreType.DMA((STAGE_SLOTS,)),
        ],
        compiler_params=_sequential_params(1),
        name="mod_ctx",
    )(c, c_ctx, mod_w, mod_b, ctx.reshape(1, rows, D), norm1_g, w_all, k_gain, ones_bd)
    return mod, kc.reshape(B, C, KV_W), vc.reshape(B, C, KV_W)


def _rope(t, cos, sin_signed, first_half):
    swap = jnp.where(first_half, pltpu.roll(t, LANES - ROPE_PAIR, axis=1), pltpu.roll(t, ROPE_PAIR, axis=1))
    return t * cos + swap * sin_signed


def _inproj_kernel(x_ref, mod_ref, g1_ref, wall_hbm, qg_ref, kg_ref, ones_ref, cos_ref, sin_ref,
                   q_ref, k_ref, v_ref, u_ref, wq_ref, wkv_ref, wu_ref, wscale_ref, qkv_stage, qkv_sem,
                   col_stage, col_sem):
    @pl.when((pl.program_id(0) == 0) & (pl.program_id(1) == 0))
    def _():
        n_pieces = qkv_stage.shape[0]

        def copy(i):
            return pltpu.make_async_copy(wall_hbm.at[:, pl.ds(i * STAGE_CHUNK, STAGE_CHUNK)], qkv_stage.at[i],
                                         qkv_sem.at[i])

        for i in range(n_pieces):
            copy(i).start()
        amax = jnp.zeros((1, 1), F32)
        for i in range(n_pieces):
            copy(i).wait()
            amax = jnp.maximum(amax, jnp.max(jnp.abs(qkv_stage[i]), keepdims=True))
        scale = _fp8_scale(amax)
        wscale_ref[...] = scale
        for i in range(n_pieces):
            cols8 = (qkv_stage[i] * scale).astype(FP8)
            if i < N_KV_HEADS:
                for j in range(GROUP):
                    c0 = j * KV_W + i * HEAD_DIM
                    wq_ref[:, c0:c0 + HEAD_DIM] = cols8[:, j * HEAD_DIM:(j + 1) * HEAD_DIM]
            else:
                c0 = (i - N_KV_HEADS) * STAGE_CHUNK
                wkv_ref[:, c0:c0 + STAGE_CHUNK] = cols8
        _stage_weight_bf16(wall_hbm.at[:, pl.ds(Q_W + 2 * KV_W, POOL_W)], wu_ref, col_stage, col_sem, axis=1)

    w_scale = wscale_ref[...]

    m = mod_ref[pl.ds(pl.program_id(0), 1), :]
    sh = m[:, 0:D_MODEL]
    gs = g1_ref[...] * (1.0 + m[:, D_MODEL:2 * D_MODEL])
    h32 = _rms_modulate(x_ref[0], gs, sh)
    h = h32.astype(BF16)
    h_scale = _fp8_scale(jnp.max(jnp.abs(h32), keepdims=True))
    h8 = (h32 * h_scale).astype(FP8)
    unscale = 1.0 / (h_scale * w_scale)
    ones_bd = ones_ref[...]
    q_gain = _tile_heads(qg_ref[...]) * (HEAD_DIM ** -0.5 * LOG2_E)
    k_gain = _tile_heads(kg_ref[...])
    cos = cos_ref[...]
    sin = sin_ref[...]
    lane = lax.broadcasted_iota(jnp.int32, cos.shape, 1)
    first_half = (lane & (2 * ROPE_PAIR - 1)) < ROPE_PAIR

    def normed_rope(t, gain):
        t = t * _head_rms(t, ones_bd, unscale) * (gain * unscale)
        halves = [_rope(t[:, i * LANES:(i + 1) * LANES], cos, sin, first_half) for i in range(2)]
        return jnp.concatenate(halves, axis=1).astype(BF16)

    wide = 2 * MXU_TILE
    for j in range(Q_W // wide):
        t = _dot(h8, wq_ref[:, j * wide:(j + 1) * wide])
        for i in range(2):
            c0 = j * wide + i * MXU_TILE
            q_ref[0, :, c0:c0 + MXU_TILE] = normed_rope(t[:, i * MXU_TILE:(i + 1) * MXU_TILE], q_gain)
    kv = _dot(h8, wkv_ref[...])
    k_ref[0] = normed_rope(kv[:, :KV_W], k_gain)
    v_ref[0] = (kv[:, KV_W:] * unscale).astype(BF16)
    u_ref[0] = _dot(h, wu_ref[...])


def _inproj_call(x, mod, norm1_g, w_all, q_gain, k_gain, ones_bd, cos_t, sin_t, tile):
    B, S, D = x.shape
    assert STAGE_CHUNK == GROUP * HEAD_DIM
    qkv_pieces = (Q_W + 2 * KV_W) // STAGE_CHUNK
    row = lambda b, t: (b, t, 0)
    return pl.pallas_call(
        _inproj_kernel,
        grid=(B, S // tile),
        in_specs=[
            pl.BlockSpec((1, tile, D), row),
            _const_spec(mod.shape),
            _const_spec((1, D)),
            pl.BlockSpec(memory_space=pl.ANY),
            _const_spec((1, HEAD_DIM)),
            _const_spec((1, HEAD_DIM)),
            _const_spec((MXU_TILE, MXU_TILE)),
            pl.BlockSpec((tile, LANES), lambda b, t: (t, 0)),
            pl.BlockSpec((tile, LANES), lambda b, t: (t, 0)),
        ],
        out_specs=[
            pl.BlockSpec((1, tile, Q_W), row),
            pl.BlockSpec((1, tile, KV_W), row),
            pl.BlockSpec((1, tile, KV_W), row),
            pl.BlockSpec((1, tile, POOL_W), row),
        ],
        out_shape=[
            jax.ShapeDtypeStruct((B, S, Q_W), BF16),
            jax.ShapeDtypeStruct((B, S, KV_W), BF16),
            jax.ShapeDtypeStruct((B, S, KV_W), BF16),
            jax.ShapeDtypeStruct((B, S, POOL_W), F32),
        ],
        scratch_shapes=[
            pltpu.VMEM((D, Q_W), FP8),
            pltpu.VMEM((D, 2 * KV_W), FP8),
            pltpu.VMEM((D, POOL_W), BF16),
            pltpu.VMEM((1, 1), F32),
            pltpu.VMEM((qkv_pieces, D, STAGE_CHUNK), F32),
            pltpu.SemaphoreType.DMA((qkv_pieces,)),
            pltpu.VMEM((STAGE_SLOTS, D, STAGE_CHUNK), F32),
            pltpu.SemaphoreType.DMA((STAGE_SLOTS,)),
        ],
        compiler_params=_sequential_params(2),
        name="in_proj",
    )(x, mod, norm1_g, w_all, q_gain, k_gain, ones_bd, cos_t, sin_t)


def _attn_kernel(sink_ref, q_ref, kp_ref, kc0_ref, kn_ref, vp_ref, vc0_ref, vn_ref,
                 kctx_ref, vctx_ref, bias_ref, o_ref, *, blocks_per_tile):
    t_idx = pl.program_id(1)
    n_tiles = pl.num_programs(1)
    k_all = jnp.concatenate([kp_ref[0], kc0_ref[0], kn_ref[0]], axis=0)
    v_all = jnp.concatenate([vp_ref[0], vc0_ref[0], vn_ref[0]], axis=0)
    k_ctx = kctx_ref[0]
    v_ctx = vctx_ref[0]
    col_amax = jnp.maximum(jnp.max(jnp.abs(v_all), axis=0, keepdims=True),
                           jnp.max(jnp.abs(v_ctx), axis=0, keepdims=True)).astype(F32)
    v_scale = _fp8_scale(jnp.max(col_amax, axis=1, keepdims=True))
    scale_row = jnp.broadcast_to(v_scale, (1, KV_W)).astype(BF16)
    v_all = v_all * scale_row
    v_ctx = v_ctx * scale_row
    one = jnp.ones((), BF16)
    lane = lax.broadcasted_iota(jnp.int32, (BLOCK, MXU_TILE), 1)
    in_head = [(lane >= kv * HEAD_DIM) & (lane < (kv + 1) * HEAD_DIM) for kv in range(N_KV_HEADS)]
    zero = jnp.zeros((), BF16)

    def with_ones(v, half):
        vl = lax.broadcasted_iota(jnp.int32, v.shape, 1)
        return jnp.where((vl < LANES) if half == 0 else (vl >= LANES), v, one).astype(FP8)

    v_ctx_pair = [with_ones(v_ctx, half) for half in range(2)]
    low_head = lax.broadcasted_iota(jnp.int32, (BLOCK, LANES), 1) < HEAD_DIM

    def block_operands(i):
        variant = jnp.int32(0)
        if i == blocks_per_tile - 1:
            variant = jnp.where(t_idx == n_tiles - 1, 2, variant)
        if i == 0:
            variant = jnp.where(t_idx == 0, 1, variant)
        v_win = v_all[i * BLOCK:(i + 3) * BLOCK]
        return dict(bias=bias_ref[variant], k_win=k_all[i * BLOCK:(i + 3) * BLOCK],
                    v_win_pair=[with_ones(v_win, half) for half in range(2)])

    def scores(blk, i, j):
        q_slab = q_ref[0, i * BLOCK:(i + 1) * BLOCK, j * MXU_TILE:(j + 1) * MXU_TILE]
        qz = jnp.concatenate([jnp.where(in_head[kv], q_slab, zero) for kv in range(N_KV_HEADS)], axis=0)
        return _dot_nt(qz, blk["k_win"]), _dot_nt(qz, k_ctx)

    def softmax_values(blk, i, j, s_win, s_ctx):
        p_win, p_ctx, sink_term = [], [], []
        for kv in range(N_KV_HEADS):
            rows = slice(kv * BLOCK, (kv + 1) * BLOCK)
            sw = jnp.concatenate([
                s_win[rows, :BLOCK] + blk["bias"][:, :BLOCK],
                s_win[rows, BLOCK:2 * BLOCK],
                s_win[rows, 2 * BLOCK:] + blk["bias"][:, 2 * BLOCK:]], axis=1)
            sc = s_ctx[rows]
            sink = sink_ref[kv * GROUP + j] * LOG2_E
            groups = [sw[:, g * LANES:(g + 1) * LANES] for g in range(sw.shape[1] // LANES)]
            groups += [sc[:, g * LANES:(g + 1) * LANES] for g in range(sc.shape[1] // LANES)]
            m = jnp.maximum(jnp.max(functools.reduce(jnp.maximum, groups), axis=-1, keepdims=True), sink)
            p_win.append(jnp.exp2(sw - m).astype(FP8))
            p_ctx.append(jnp.exp2(sc - m).astype(FP8))
            sink_term.append(jnp.exp2(sink - m))
        for half in range(2):
            pw = jnp.concatenate(p_win[2 * half:2 * half + 2], axis=0)
            pc = jnp.concatenate(p_ctx[2 * half:2 * half + 2], axis=0)
            o = _dot(pw, blk["v_win_pair"][half]) + _dot(pc, v_ctx_pair[half])
            vals = o[:, half * LANES:(half + 1) * LANES]
            sums = o[:, (1 - half) * LANES:(2 - half) * LANES]
            out = []
            for r in range(2):
                rows = slice(r * BLOCK, (r + 1) * BLOCK)
                out.append(vals[rows] / ((sums[rows] + sink_term[2 * half + r]) * v_scale))
            c0 = j * MXU_TILE + half * LANES
            o_ref[0, i * BLOCK:(i + 1) * BLOCK, c0:c0 + LANES] = jnp.where(low_head, out[0], out[1]).astype(BF16)

    blocks = [block_operands(i) for i in range(blocks_per_tile)]
    items = [(i, j) for i in range(blocks_per_tile) for j in range(GROUP)]
    s_next = scores(blocks[0], *items[0])
    for idx, (i, j) in enumerate(items):
        s_cur = s_next
        if idx + 1 < len(items):
            ni, nj = items[idx + 1]
            s_next = scores(blocks[ni], ni, nj)
        softmax_values(blocks[i], i, j, *s_cur)


def _attn_call(sink, q, k, v, kc, vc, bias, tile):
    B, S, _ = q.shape
    bpt = tile // BLOCK
    nb = S // BLOCK
    C = kc.shape[1]
    cur = lambda b, t, *_: (b, t, 0)
    prev = lambda b, t, *_: (b, jnp.maximum(t * bpt - 1, 0), 0)
    nxt = lambda b, t, *_: (b, jnp.minimum((t + 1) * bpt, nb - 1), 0)
    per_b = lambda b, t, *_: (b, 0, 0)
    halo = (1, BLOCK, KV_W)
    center = (1, tile, KV_W)
    grid_spec = pltpu.PrefetchScalarGridSpec(
        num_scalar_prefetch=1,
        grid=(B, S // tile),
        in_specs=[
            pl.BlockSpec((1, tile, Q_W), cur),
            pl.BlockSpec(halo, prev), pl.BlockSpec(center, cur), pl.BlockSpec(halo, nxt),
            pl.BlockSpec(halo, prev), pl.BlockSpec(center, cur), pl.BlockSpec(halo, nxt),
            pl.BlockSpec((1, C, KV_W), per_b),
            pl.BlockSpec((1, C, KV_W), per_b),
            pl.BlockSpec((3, BLOCK, 3 * BLOCK), lambda b, t, *_: (0, 0, 0)),
        ],
        out_specs=pl.BlockSpec((1, tile, Q_W), cur),
    )
    return pl.pallas_call(
        functools.partial(_attn_kernel, blocks_per_tile=bpt),
        grid_spec=grid_spec,
        out_shape=jax.ShapeDtypeStruct((B, S, Q_W), BF16),
        compiler_params=_params(2),
        name="attn",
    )(sink, q, k, k, k, v, v, v, kc, vc, bias)


def _merge_kernel(x_ref, mod_ref, g1_ref, attn_ref, up_ref, uc_ref, un_ref,
                  wall_hbm, gb_ref, wap_hbm, pw_ref, ps_ref, wpp_hbm, wo_hbm, o_ref,
                  uext_ref, wg_ref, wap_ref, wo_ref, wpool_ref, col_stage, row_stage, col_sem, row_sem,
                  *, seq, tile, sub):
    @pl.when((pl.program_id(0) == 0) & (pl.program_id(1) == 0))
    def _():
        _stage_weight_bf16(wpp_hbm, wpool_ref, row_stage, row_sem, axis=0)
        for g in range(POOL_GROUPS):
            rows = slice(g * POOL_GROUP_W, (g + 1) * POOL_GROUP_W)
            scaled = (pw_ref[g] * ps_ref[g]).astype(BF16)
            wpool_ref[rows, :] = _dot(scaled, wpool_ref[rows, :]).astype(BF16)

        _stage_weight_bf16(wall_hbm.at[:, pl.ds(2 * D_MODEL, 2 * D_MODEL)], wg_ref, col_stage, col_sem, axis=1)
        _stage_weight_bf16(wo_hbm, wo_ref, col_stage, col_sem, axis=1)

        def store_regrouped(kv, rows_f32):
            for j in range(GROUP):
                dst = (j * N_KV_HEADS + kv) * HEAD_DIM
                wap_ref[dst:dst + HEAD_DIM, :] = rows_f32[j * HEAD_DIM:(j + 1) * HEAD_DIM, :].astype(BF16)

        _stage_weight_bf16(wap_hbm, None, row_stage, row_sem, axis=0, store=store_regrouped)

    t_idx = pl.program_id(1)
    n_tiles = pl.num_programs(1)
    m = mod_ref[pl.ds(pl.program_id(0), 1), :]
    sh = m[:, 0:D_MODEL]
    gs = g1_ref[...] * (1.0 + m[:, D_MODEL:2 * D_MODEL])
    gate1 = m[:, 2 * D_MODEL:3 * D_MODEL]

    keep_prev = (t_idx > 0).astype(F32)
    keep_next = (t_idx < n_tiles - 1).astype(F32)
    uext_ref[0:POOL_HALO] = up_ref[0] * keep_prev
    uext_ref[POOL_HALO:POOL_HALO + tile] = uc_ref[0]
    uext_ref[POOL_HALO + tile:] = un_ref[0] * keep_next
    edge_rows = lax.broadcasted_iota(jnp.int32, (POOL_HALO, POOL_GROUP_W), 0)

    def attn_dot(r):
        return _dot(attn_ref[0, r * sub:(r + 1) * sub, :], wap_ref[...])

    def gate_dot(r):
        h = _rms_modulate(x_ref[0, r * sub:(r + 1) * sub, :], gs, sh).astype(BF16)
        return jax.nn.sigmoid(_dot(h, wg_ref[...]) + gb_ref[...])

    def pool_dot(r):
        r0 = r * sub
        pos_first = t_idx * tile + r0 + edge_rows
        pos_last = pos_first + (sub - POOL_HALO)
        diffs = []
        for g, w in enumerate(POOL_WINDOWS):
            cols = slice(g * POOL_GROUP_W, (g + 1) * POOL_GROUP_W)
            acc = uext_ref[pl.ds(r0, sub + 2 * POOL_HALO), cols]
            k = 1
            while k < w:
                acc = acc[:acc.shape[0] - k] + acc[k:]
                k *= 2
            win = acc[POOL_HALO - w // 2:POOL_HALO - w // 2 + sub]

            def inv_count(pos):
                return 1.0 / (jnp.minimum(pos + w // 2, seq) - jnp.maximum(pos - w // 2, 0)).astype(F32)

            pooled_avg = jnp.concatenate([
                win[:POOL_HALO] * inv_count(pos_first),
                win[POOL_HALO:sub - POOL_HALO] * (1.0 / w),
                win[sub - POOL_HALO:] * inv_count(pos_last)], axis=0)
            diffs.append((pooled_avg - uext_ref[pl.ds(r0 + POOL_HALO, sub), cols]).astype(BF16))
        return _dot(jnp.concatenate(diffs, axis=1), wpool_ref[...])

    def output(r, a, gates, p):
        rows = slice(r * sub, (r + 1) * sub)
        merged = (gates[:, :D_MODEL] * a + gates[:, D_MODEL:] * p).astype(BF16)
        o_ref[0, rows, :] = x_ref[0, rows, :] + gate1 * _dot(merged, wo_ref[...])

    n_sub = tile // sub
    a = [attn_dot(r) for r in range(n_sub)]
    p = [pool_dot(r) for r in range(n_sub)]
    gates = [gate_dot(r) for r in range(n_sub)]
    for r in range(n_sub):
        output(r, a[r], gates[r], p[r])


def _merge_call(x, mod, norm1_g, attn, u, w_all, gate_b, w_ap, pool_w, pool_scale3, w_pool_proj, w_out, tile):
    B, S, D = x.shape
    r = tile // POOL_HALO
    n_halo = S // POOL_HALO
    row = lambda b, t: (b, t, 0)
    return pl.pallas_call(
        functools.partial(_merge_kernel, seq=S, tile=tile, sub=SUB_TILE),
        grid=(B, S // tile),
        in_specs=[
            pl.BlockSpec((1, tile, D), row),
            _const_spec(mod.shape),
            _const_spec((1, D)),
            pl.BlockSpec((1, tile, Q_W), row),
            pl.BlockSpec((1, POOL_HALO, POOL_W), lambda b, t: (b, jnp.maximum(t * r - 1, 0), 0)),
            pl.BlockSpec((1, tile, POOL_W), row),
            pl.BlockSpec((1, POOL_HALO, POOL_W), lambda b, t: (b, jnp.minimum((t + 1) * r, n_halo - 1), 0)),
            pl.BlockSpec(memory_space=pl.ANY),
            _const_spec(gate_b.shape),
            pl.BlockSpec(memory_space=pl.ANY),
            _const_spec(pool_w.shape),
            _const_spec(pool_scale3.shape),
            pl.BlockSpec(memory_space=pl.ANY),
            pl.BlockSpec(memory_space=pl.ANY),
        ],
        out_specs=pl.BlockSpec((1, tile, D), row),
        out_shape=jax.ShapeDtypeStruct((B, S, D), F32),
        scratch_shapes=[
            pltpu.VMEM((tile + 2 * POOL_HALO, POOL_W), F32),
            pltpu.VMEM((D, 2 * D), BF16),
            pltpu.VMEM(w_ap.shape, BF16),
            pltpu.VMEM(w_out.shape, BF16),
            pltpu.VMEM(w_pool_proj.shape, BF16),
            pltpu.VMEM((STAGE_SLOTS, D, STAGE_CHUNK), F32),
            pltpu.VMEM((STAGE_SLOTS, STAGE_CHUNK, D), F32),
            pltpu.SemaphoreType.DMA((STAGE_SLOTS,)),
            pltpu.SemaphoreType.DMA((STAGE_SLOTS,)),
        ],
        compiler_params=_sequential_params(2),
        name="merge",
    )(x, mod, norm1_g, attn, u, u, u, w_all, gate_b, w_ap, pool_w, pool_scale3, w_pool_proj, w_out)


def _ffn_kernel(x_ref, mod_ref, g2_ref, wup_hbm, wdn_hbm, o_ref,
                wup_ref, wdn_ref, col_stage, row_stage, col_sem, row_sem, hmid_ref, *, d_ff, tile, sub):
    @pl.when((pl.program_id(0) == 0) & (pl.program_id(1) == 0))
    def _():
        _stage_weight_bf16(wup_hbm, wup_ref, col_stage, col_sem, axis=1)
        _stage_weight_bf16(wdn_hbm, wdn_ref, row_stage, row_sem, axis=0)

    m = mod_ref[pl.ds(pl.program_id(0), 1), :]
    sh = m[:, 3 * D_MODEL:4 * D_MODEL]
    gs = g2_ref[...] * (1.0 + m[:, 4 * D_MODEL:5 * D_MODEL])
    gate2 = m[:, 5 * D_MODEL:6 * D_MODEL]

    def up(r):
        rows = slice(r * sub, (r + 1) * sub)
        h = _rms_modulate(x_ref[0, rows, :], gs, sh).astype(BF16)
        for c in range(d_ff // MXU_TILE):
            a = _dot(h, wup_ref[:, c * MXU_TILE:(c + 1) * MXU_TILE])
            b = _dot(h, wup_ref[:, d_ff + c * MXU_TILE:d_ff + (c + 1) * MXU_TILE])
            hmid_ref[rows, c * MXU_TILE:(c + 1) * MXU_TILE] = (a * jax.nn.sigmoid(a) * b).astype(BF16)

    def down(r):
        rows = slice(r * sub, (r + 1) * sub)
        o_ref[0, rows, :] = x_ref[0, rows, :] + gate2 * _dot(hmid_ref[rows, :], wdn_ref[...])

    n_sub = tile // sub
    up(0)
    for r in range(1, n_sub):
        up(r)
        down(r - 1)
    down(n_sub - 1)


def _ffn_call(x, mod, norm2_g, w_up, w_down, tile):
    B, S, D = x.shape
    d_ff = w_down.shape[0]
    row = lambda b, t: (b, t, 0)
    return pl.pallas_call(
        functools.partial(_ffn_kernel, d_ff=d_ff, tile=tile, sub=SUB_TILE),
        grid=(B, S // tile),
        in_specs=[
            pl.BlockSpec((1, tile, D), row),
            _const_spec(mod.shape),
            _const_spec((1, D)),
            pl.BlockSpec(memory_space=pl.ANY),
            pl.BlockSpec(memory_space=pl.ANY),
        ],
        out_specs=pl.BlockSpec((1, tile, D), row),
        out_shape=jax.ShapeDtypeStruct((B, S, D), F32),
        scratch_shapes=[
            pltpu.VMEM(w_up.shape, BF16),
            pltpu.VMEM(w_down.shape, BF16),
            pltpu.VMEM((STAGE_SLOTS, D, STAGE_CHUNK), F32),
            pltpu.VMEM((STAGE_SLOTS, STAGE_CHUNK, D), F32),
            pltpu.SemaphoreType.DMA((STAGE_SLOTS,)),
            pltpu.SemaphoreType.DMA((STAGE_SLOTS,)),
            pltpu.VMEM((tile, d_ff), BF16),
        ],
        compiler_params=_sequential_params(2),
        name="ffn",
    )(x, mod, norm2_g, w_up, w_down)


def _rope_tables(seq):
    t = np.arange(seq)
    row = (t // GRID_W).astype(np.float64)
    col = (t % GRID_W).astype(np.float64)
    half = HEAD_DIM // 2
    inv_freq = 1.0 / (ROPE_THETA ** (np.arange(0, half, 2, dtype=np.float64) / half))
    ang_r = row[:, None] * inv_freq
    ang_c = col[:, None] * inv_freq
    cos = np.concatenate([np.cos(ang_r)] * 2 + [np.cos(ang_c)] * 2, axis=1)
    sin = np.concatenate([-np.sin(ang_r), np.sin(ang_r), -np.sin(ang_c), np.sin(ang_c)], axis=1)
    reps = (1, LANES // HEAD_DIM)
    return jnp.asarray(np.tile(cos, reps), dtype=F32), jnp.asarray(np.tile(sin, reps), dtype=F32)


def _band_bias():
    i = np.arange(BLOCK)[:, None]
    j = np.arange(3 * BLOCK)[None, :]
    band = np.abs(j - BLOCK - i) <= WINDOW
    first = band & (j >= BLOCK)
    last = band & (j < 2 * BLOCK)
    masks = np.stack([band, first, last])
    return jnp.asarray(np.where(masks, 0.0, NEG_BIG), dtype=F32)


def kernel(x, c, ctx, c_ctx, mod_w, mod_b, norm1_g, norm2_g, w_in, gate_b, q_norm_g, k_norm_g,
           sink, pool_w, pool_scale, w_attn_proj, w_pool_proj, w_out, w_up, w_down):
    B, S, D = x.shape
    depth = mod_w.shape[0]
    cos_t, sin_t = _rope_tables(S)
    bias = _band_bias()
    ones_bd = jnp.asarray(np.kron(np.eye(MXU_TILE // HEAD_DIM), np.ones((HEAD_DIM, HEAD_DIM))), dtype=BF16)
    assert depth == 1, "context-stream update between layers is not implemented"
    assert D == D_MODEL and S % ROW_TILE == 0 and B + 1 <= F32_SUBLANES

    for l in range(depth):
        wl = w_in[l]
        g1 = norm1_g[l][None, :]
        k_gain = k_norm_g[l][None, :]
        q_gain = q_norm_g[l][None, :]

        mod, kc, vc = _mod_ctx_call(c, c_ctx[None, :], mod_w[l], mod_b[l][None, :], ctx, g1, wl, k_gain, ones_bd)
        q, k, v, u = _inproj_call(x, mod, g1, wl, q_gain, k_gain, ones_bd, cos_t, sin_t, tile=ROW_TILE)
        attn = _attn_call(sink[l], q, k, v, kc, vc, bias, tile=ROW_TILE)
        x = _merge_call(
            x, mod, g1, attn, u, wl, gate_b[l][None, :], w_attn_proj[l], pool_w[l],
            pool_scale[l].reshape(POOL_GROUPS, 1, POOL_GROUP_W), w_pool_proj[l], w_out[l], tile=ROW_TILE)
        x = _ffn_call(x, mod, norm2_g[l][None, :], w_up[l], w_down[l], tile=ROW_TILE)
    return x
```

```python
import functools

import jax
import jax.numpy as jnp
import numpy as np
from jax import lax
from jax.experimental import pallas as pl
from jax.experimental.pallas import tpu as pltpu

D_MODEL = 1024
GRID_W = 64
HEAD_DIM = 64
N_HEADS = 16
N_KV_HEADS = 4
GROUP = N_HEADS // N_KV_HEADS
Q_W = N_HEADS * HEAD_DIM
KV_W = N_KV_HEADS * HEAD_DIM
WINDOW = 128
BLOCK = 128
ROPE_THETA = 10000.0
POOL_WINDOWS = (2, 4, 8, 16)
POOL_GROUPS = 4
POOL_W = D_MODEL // 2
POOL_GROUP_W = POOL_W // POOL_GROUPS
POOL_HALO = 8
ROPE_PAIR = HEAD_DIM // 4
EPS = 1e-6
NEG_BIG = -1e30
LOG2_E = 1.4426950408889634

LANES = 128
F32_SUBLANES = 8
MXU_TILE = 256
VMEM_LIMIT = 56 * 1024 * 1024
STAGE_SLOTS = 4
STAGE_CHUNK = MXU_TILE
ROW_TILE = 1024
SUB_TILE = MXU_TILE

F32 = jnp.float32
BF16 = jnp.bfloat16
FP8 = jnp.float8_e4m3fn
FP8_MAX = 448.0


def _params(n_parallel):
    return pltpu.CompilerParams(
        dimension_semantics=("parallel",) * n_parallel,
        vmem_limit_bytes=VMEM_LIMIT,
    )


def _const_spec(shape):
    nd = len(shape)
    return pl.BlockSpec(shape, lambda *_: (0,) * nd, pipeline_mode=pl.Buffered(1))


def _dot(a, b):
    return jnp.dot(a, b, preferred_element_type=F32)


def _dot_nt(a, b):
    return lax.dot_general(a, b, (((1,), (1,)), ((), ())), preferred_element_type=F32)


def _rms_modulate(x, gs, sh):
    rs = lax.rsqrt(jnp.mean(x * x, axis=-1, keepdims=True) + EPS)
    return x * rs * gs + sh


def _head_rms(t, ones_bd, unscale=1.0):
    ms = _dot((t * t).astype(BF16), ones_bd) * (unscale * unscale * (1.0 / HEAD_DIM))
    return lax.rsqrt(ms + EPS)


def _tile_heads(g):
    return jnp.concatenate([g] * N_KV_HEADS, axis=1)


def _stage_weight_bf16(w_hbm, w_ref, stage_ref, sem, axis, store=None):
    slots = stage_ref.shape[0]
    chunk = stage_ref.shape[1 + axis]
    n = w_hbm.shape[axis] // chunk

    def piece(ref, i):
        return ref.at[:, pl.ds(i * chunk, chunk)] if axis == 1 else ref.at[pl.ds(i * chunk, chunk), :]

    def copy(i):
        return pltpu.make_async_copy(piece(w_hbm, i), stage_ref.at[i % slots], sem.at[i % slots])

    for i in range(min(slots - 1, n)):
        copy(i).start()
    for i in range(n):
        if i + slots - 1 < n:
            copy(i + slots - 1).start()
        copy(i).wait()
        staged = stage_ref[i % slots]
        if store is None:
            piece(w_ref, i)[...] = staged.astype(BF16)
        else:
            store(i, staged)


def _fp8_scale(amax):
    return jnp.exp2(jnp.floor(jnp.log2(FP8_MAX / jnp.maximum(amax, 1e-30))))


def _sequential_params(n_axes):
    return pltpu.CompilerParams(dimension_semantics=("arbitrary",) * n_axes, vmem_limit_bytes=VMEM_LIMIT)


def _mod_ctx_kernel(c_ref, cctx_ref, w_ref, b_ref, ctx_hbm, g1_ref, wall_hbm, kg_ref, ones_ref,
                    o_ref, kc_ref, vc_ref, rows_ref, ctx_mod_ref, wkv_ref, col_stage, col_sem, ctx_ref, ctx_sem):
    ctx_copy = pltpu.make_async_copy(ctx_hbm, ctx_ref, ctx_sem.at[0])

    n_batch = c_ref.shape[0]
    rows_ref[...] = jnp.zeros_like(rows_ref)
    rows_ref[0:n_batch, :] = c_ref[...]
    rows_ref[n_batch:n_batch + 1, :] = cctx_ref[...]
    c = rows_ref[...]
    a = (c * jax.nn.sigmoid(c)).astype(BF16)
    o_ref[...] = _dot(a, w_ref[...].astype(BF16)) + b_ref[...]

    @pl.when(pl.program_id(0) == 0)
    def _():
        ctx_copy.start()
        ctx_mod_ref[...] = o_ref[n_batch:n_batch + 1, :]
        _stage_weight_bf16(wall_hbm.at[:, pl.ds(Q_W, 2 * KV_W)], wkv_ref, col_stage, col_sem, axis=1)

    @pl.when(pl.program_id(0) == 1)
    def _():
        ctx_copy.wait()
        m = ctx_mod_ref[...]
        sh = m[:, 0:D_MODEL]
        gs = g1_ref[...] * (1.0 + m[:, D_MODEL:2 * D_MODEL])
        h = _rms_modulate(ctx_ref[0], gs, sh).astype(BF16)
        kv = _dot(h, wkv_ref[...])
        k = kv[:, :KV_W]
        kc_ref[0] = (k * _head_rms(k, ones_ref[...]) * _tile_heads(kg_ref[...])).astype(BF16)
        vc_ref[0] = kv[:, KV_W:].astype(BF16)


def _mod_ctx_call(c, c_ctx, mod_w, mod_b, ctx, norm1_g, w_all, k_gain, ones_bd):
    n = mod_w.shape[1]
    bn = 2 * D_MODEL
    B, C, D = ctx.shape
    rows = B * C
    assert n // bn >= 2
    whole = lambda j: (0, 0, 0)
    mod, kc, vc = pl.pallas_call(
        _mod_ctx_kernel,
        grid=(n // bn,),
        in_specs=[
            pl.BlockSpec(c.shape, lambda j: (0, 0)),
            pl.BlockSpec((1, D_MODEL), lambda j: (0, 0)),
            pl.BlockSpec((D_MODEL, bn), lambda j: (0, j)),
            pl.BlockSpec((1, bn), lambda j: (0, j)),
            pl.BlockSpec(memory_space=pl.ANY),
            _const_spec((1, D)),
            pl.BlockSpec(memory_space=pl.ANY),
            _const_spec((1, HEAD_DIM)),
            _const_spec((MXU_TILE, MXU_TILE)),
        ],
        out_specs=[
            pl.BlockSpec((F32_SUBLANES, bn), lambda j: (0, j)),
            pl.BlockSpec((1, rows, KV_W), whole),
            pl.BlockSpec((1, rows, KV_W), whole),
        ],
        out_shape=[
            jax.ShapeDtypeStruct((F32_SUBLANES, n), F32),
            jax.ShapeDtypeStruct((1, rows, KV_W), BF16),
            jax.ShapeDtypeStruct((1, rows, KV_W), BF16),
        ],
        scratch_shapes=[
            pltpu.VMEM((F32_SUBLANES, D_MODEL), F32),
            pltpu.VMEM((1, bn), F32),
            pltpu.VMEM((D, 2 * KV_W), BF16),
            pltpu.VMEM((STAGE_SLOTS, D, STAGE_CHUNK), F32),
            pltpu.SemaphoreType.DMA((STAGE_SLOTS,)),
            pltpu.VMEM((1, rows, D), F32),
            pltpu.SemaphoreType.DMA((1,)),
        ],
        compiler_params=_sequential_params(1),
        name="mod_ctx",
    )(c, c_ctx, mod_w, mod_b, ctx.reshape(1, rows, D), norm1_g, w_all, k_gain, ones_bd)
    return mod, kc.reshape(B, C, KV_W), vc.reshape(B, C, KV_W)


def _rope(t, cos, sin_signed, first_half):
    swap = jnp.where(first_half, pltpu.roll(t, LANES - ROPE_PAIR, axis=1), pltpu.roll(t, ROPE_PAIR, axis=1))
    return t * cos + swap * sin_signed


def _inproj_kernel(x_ref, mod_ref, g1_ref, wall_hbm, qg_ref, kg_ref, ones_ref, cos_ref, sin_ref,
                   q_ref, k_ref, v_ref, u_ref, wq_ref, wkv_ref, wu_ref, wscale_ref, qkv_stage, qkv_sem,
                   col_stage, col_sem):
    @pl.when((pl.program_id(0) == 0) & (pl.program_id(1) == 0))
    def _():
        n_pieces = qkv_stage.shape[0]

        def copy(i):
            return pltpu.make_async_copy(wall_hbm.at[:, pl.ds(i * STAGE_CHUNK, STAGE_CHUNK)], qkv_stage.at[i],
                                         qkv_sem.at[i])

        for i in range(n_pieces):
            copy(i).start()
        amax = jnp.zeros((1, 1), F32)
        for i in range(n_pieces):
            copy(i).wait()
            amax = jnp.maximum(amax, jnp.max(jnp.abs(qkv_stage[i]), keepdims=True))
        scale = _fp8_scale(amax)
        wscale_ref[...] = scale
        for i in range(n_pieces):
            cols8 = (qkv_stage[i] * scale).astype(FP8)
            if i < N_KV_HEADS:
                for j in range(GROUP):
                    c0 = j * KV_W + i * HEAD_DIM
                    wq_ref[:, c0:c0 + HEAD_DIM] = cols8[:, j * HEAD_DIM:(j + 1) * HEAD_DIM]
            else:
                c0 = (i - N_KV_HEADS) * STAGE_CHUNK
                wkv_ref[:, c0:c0 + STAGE_CHUNK] = cols8
        _stage_weight_bf16(wall_hbm.at[:, pl.ds(Q_W + 2 * KV_W, POOL_W)], wu_ref, col_stage, col_sem, axis=1)

    w_scale = wscale_ref[...]

    m = mod_ref[pl.ds(pl.program_id(0), 1), :]
    sh = m[:, 0:D_MODEL]
    gs = g1_ref[...] * (1.0 + m[:, D_MODEL:2 * D_MODEL])
    h32 = _rms_modulate(x_ref[0], gs, sh)
    h = h32.astype(BF16)
    h_scale = _fp8_scale(jnp.max(jnp.abs(h32), keepdims=True))
    h8 = (h32 * h_scale).astype(FP8)
    unscale = 1.0 / (h_scale * w_scale)
    ones_bd = ones_ref[...]
    q_gain = _tile_heads(qg_ref[...]) * (HEAD_DIM ** -0.5 * LOG2_E)
    k_gain = _tile_heads(kg_ref[...])
    cos = cos_ref[...]
    sin = sin_ref[...]
    lane = lax.broadcasted_iota(jnp.int32, cos.shape, 1)
    first_half = (lane & (2 * ROPE_PAIR - 1)) < ROPE_PAIR

    def normed_rope(t, gain):
        t = t * _head_rms(t, ones_bd, unscale) * (gain * unscale)
        halves = [_rope(t[:, i * LANES:(i + 1) * LANES], cos, sin, first_half) for i in range(2)]
        return jnp.concatenate(halves, axis=1).astype(BF16)

    wide = 2 * MXU_TILE
    for j in range(Q_W // wide):
        t = _dot(h8, wq_ref[:, j * wide:(j + 1) * wide])
        for i in range(2):
            c0 = j * wide + i * MXU_TILE
            q_ref[0, :, c0:c0 + MXU_TILE] = normed_rope(t[:, i * MXU_TILE:(i + 1) * MXU_TILE], q_gain)
    kv = _dot(h8, wkv_ref[...])
    k_ref[0] = normed_rope(kv[:, :KV_W], k_gain)
    v_ref[0] = (kv[:, KV_W:] * unscale).astype(BF16)
    u_ref[0] = _dot(h, wu_ref[...])


def _inproj_call(x, mod, norm1_g, w_all, q_gain, k_gain, ones_bd, cos_t, sin_t, tile):
    B, S, D = x.shape
    assert STAGE_CHUNK == GROUP * HEAD_DIM
    qkv_pieces = (Q_W + 2 * KV_W) // STAGE_CHUNK
    row = lambda b, t: (b, t, 0)
    return pl.pallas_call(
        _inproj_kernel,
        grid=(B, S // tile),
        in_specs=[
            pl.BlockSpec((1, tile, D), row),
            _const_spec(mod.shape),
            _const_spec((1, D)),
            pl.BlockSpec(memory_space=pl.ANY),
            _const_spec((1, HEAD_DIM)),
            _const_spec((1, HEAD_DIM)),
            _const_spec((MXU_TILE, MXU_TILE)),
            pl.BlockSpec((tile, LANES), lambda b, t: (t, 0)),
            pl.BlockSpec((tile, LANES), lambda b, t: (t, 0)),
        ],
        out_specs=[
            pl.BlockSpec((1, tile, Q_W), row),
            pl.BlockSpec((1, tile, KV_W), row),
            pl.BlockSpec((1, tile, KV_W), row),
            pl.BlockSpec((1, tile, POOL_W), row),
        ],
        out_shape=[
            jax.ShapeDtypeStruct((B, S, Q_W), BF16),
            jax.ShapeDtypeStruct((B, S, KV_W), BF16),
            jax.ShapeDtypeStruct((B, S, KV_W), BF16),
            jax.ShapeDtypeStruct((B, S, POOL_W), F32),
        ],
        scratch_shapes=[
            pltpu.VMEM((D, Q_W), FP8),
            pltpu.VMEM((D, 2 * KV_W), FP8),
            pltpu.VMEM((D, POOL_W), BF16),
            pltpu.VMEM((1, 1), F32),
            pltpu.VMEM((qkv_pieces, D, STAGE_CHUNK), F32),
            pltpu.SemaphoreType.DMA((qkv_pieces,)),
            pltpu.VMEM((STAGE_SLOTS, D, STAGE_CHUNK), F32),
            pltpu.SemaphoreType.DMA((STAGE_SLOTS,)),
        ],
        compiler_params=_sequential_params(2),
        name="in_proj",
    )(x, mod, norm1_g, w_all, q_gain, k_gain, ones_bd, cos_t, sin_t)


def _attn_kernel(sink_ref, q_ref, kp_ref, kc0_ref, kn_ref, vp_ref, vc0_ref, vn_ref,
                 kctx_ref, vctx_ref, bias_ref, o_ref, *, blocks_per_tile):
    t_idx = pl.program_id(1)
    n_tiles = pl.num_programs(1)
    k_all = jnp.concatenate([kp_ref[0], kc0_ref[0], kn_ref[0]], axis=0)
    v_all = jnp.concatenate([vp_ref[0], vc0_ref[0], vn_ref[0]], axis=0)
    k_ctx = kctx_ref[0]
    v_ctx = vctx_ref[0]
    col_amax = jnp.maximum(jnp.max(jnp.abs(v_all), axis=0, keepdims=True),
                           jnp.max(jnp.abs(v_ctx), axis=0, keepdims=True)).astype(F32)
    v_scale = _fp8_scale(jnp.max(col_amax, axis=1, keepdims=True))
    scale_row = jnp.broadcast_to(v_scale, (1, KV_W)).astype(BF16)
    v_all = v_all * scale_row
    v_ctx = v_ctx * scale_row
    one = jnp.ones((), BF16)
    lane = lax.broadcasted_iota(jnp.int32, (BLOCK, MXU_TILE), 1)
    in_head = [(lane >= kv * HEAD_DIM) & (lane < (kv + 1) * HEAD_DIM) for kv in range(N_KV_HEADS)]
    zero = jnp.zeros((), BF16)

    def with_ones(v, half):
        vl = lax.broadcasted_iota(jnp.int32, v.shape, 1)
        return jnp.where((vl < LANES) if half == 0 else (vl >= LANES), v, one).astype(FP8)

    v_ctx_pair = [with_ones(v_ctx, half) for half in range(2)]
    low_head = lax.broadcasted_iota(jnp.int32, (BLOCK, LANES), 1) < HEAD_DIM

    def block_operands(i):
        variant = jnp.int32(0)
        if i == blocks_per_tile - 1:
            variant = jnp.where(t_idx == n_tiles - 1, 2, variant)
        if i == 0:
            variant = jnp.where(t_idx == 0, 1, variant)
        v_win = v_all[i * BLOCK:(i + 3) * BLOCK]
        return dict(bias=bias_ref[variant], k_win=k_all[i * BLOCK:(i + 3) * BLOCK],
                    v_win_pair=[with_ones(v_win, half) for half in range(2)])

    def scores(blk, i, j):
        q_slab = q_ref[0, i * BLOCK:(i + 1) * BLOCK, j * MXU_TILE:(j + 1) * MXU_TILE]
        qz = jnp.concatenate([jnp.where(in_head[kv], q_slab, zero) for kv in range(N_KV_HEADS)], axis=0)
        return _dot_nt(qz, blk["k_win"]), _dot_nt(qz, k_ctx)

    def softmax_values(blk, i, j, s_win, s_ctx):
        p_win, p_ctx, sink_term = [], [], []
        for kv in range(N_KV_HEADS):
            rows = slice(kv * BLOCK, (kv + 1) * BLOCK)
            sw = jnp.concatenate([
                s_win[rows, :BLOCK] + blk["bias"][:, :BLOCK],
                s_win[rows, BLOCK:2 * BLOCK],
                s_win[rows, 2 * BLOCK:] + blk["bias"][:, 2 * BLOCK:]], axis=1)
            sc = s_ctx[rows]
            sink = sink_ref[kv * GROUP + j] * LOG2_E
            groups = [sw[:, g * LANES:(g + 1) * LANES] for g in range(sw.shape[1] // LANES)]
            groups += [sc[:, g * LANES:(g + 1) * LANES] for g in range(sc.shape[1] // LANES)]
            m = jnp.maximum(jnp.max(functools.reduce(jnp.maximum, groups), axis=-1, keepdims=True), sink)
            p_win.append(jnp.exp2(sw - m).astype(FP8))
            p_ctx.append(jnp.exp2(sc - m).astype(FP8))
            sink_term.append(jnp.exp2(sink - m))
        for half in range(2):
            pw = jnp.concatenate(p_win[2 * half:2 * half + 2], axis=0)
            pc = jnp.concatenate(p_ctx[2 * half:2 * half + 2], axis=0)
            o = _dot(pw, blk["v_win_pair"][half]) + _dot(pc, v_ctx_pair[half])
            vals = o[:, half * LANES:(half + 1) * LANES]
            sums = o[:, (1 - half) * LANES:(2 - half) * LANES]
            out = []
            for r in range(2):
                rows = slice(r * BLOCK, (r + 1) * BLOCK)
                out.append(vals[rows] / ((sums[rows] + sink_term[2 * half + r]) * v_scale))
            c0 = j * MXU_TILE + half * LANES
            o_ref[0, i * BLOCK:(i + 1) * BLOCK, c0:c0 + LANES] = jnp.where(low_head, out[0], out[1]).astype(BF16)

    blocks = [block_operands(i) for i in range(blocks_per_tile)]
    items = [(i, j) for i in range(blocks_per_tile) for j in range(GROUP)]
    s_next = scores(blocks[0], *items[0])
    for idx, (i, j) in enumerate(items):
        s_cur = s_next
        if idx + 1 < len(items):
            ni, nj = items[idx + 1]
            s_next = scores(blocks[ni], ni, nj)
        softmax_values(blocks[i], i, j, *s_cur)


def _attn_call(sink, q, k, v, kc, vc, bias, tile):
    B, S, _ = q.shape
    bpt = tile // BLOCK
    nb = S // BLOCK
    C = kc.shape[1]
    cur = lambda b, t, *_: (b, t, 0)
    prev = lambda b, t, *_: (b, jnp.maximum(t * bpt - 1, 0), 0)
    nxt = lambda b, t, *_: (b, jnp.minimum((t + 1) * bpt, nb - 1), 0)
    per_b = lambda b, t, *_: (b, 0, 0)
    halo = (1, BLOCK, KV_W)
    center = (1, tile, KV_W)
    grid_spec = pltpu.PrefetchScalarGridSpec(
        num_scalar_prefetch=1,
        grid=(B, S // tile),
        in_specs=[
            pl.BlockSpec((1, tile, Q_W), cur),
            pl.BlockSpec(halo, prev), pl.BlockSpec(center, cur), pl.BlockSpec(halo, nxt),
            pl.BlockSpec(halo, prev), pl.BlockSpec(center, cur), pl.BlockSpec(halo, nxt),
            pl.BlockSpec((1, C, KV_W), per_b),
            pl.BlockSpec((1, C, KV_W), per_b),
            pl.BlockSpec((3, BLOCK, 3 * BLOCK), lambda b, t, *_: (0, 0, 0)),
        ],
        out_specs=pl.BlockSpec((1, tile, Q_W), cur),
    )
    return pl.pallas_call(
        functools.partial(_attn_kernel, blocks_per_tile=bpt),
        grid_spec=grid_spec,
        out_shape=jax.ShapeDtypeStruct((B, S, Q_W), BF16),
        compiler_params=_params(2),
        name="attn",
    )(sink, q, k, k, k, v, v, v, kc, vc, bias)


def _merge_kernel(x_ref, mod_ref, g1_ref, attn_ref, up_ref, uc_ref, un_ref,
                  wall_hbm, gb_ref, wap_hbm, pw_ref, ps_ref, wpp_hbm, wo_hbm, o_ref,
                  uext_ref, wg_ref, wap_ref, wo_ref, wpool_ref, col_stage, row_stage, col_sem, row_sem,
                  *, seq, tile, sub):
    @pl.when((pl.program_id(0) == 0) & (pl.program_id(1) == 0))
    def _():
        _stage_weight_bf16(wpp_hbm, wpool_ref, row_stage, row_sem, axis=0)
        for g in range(POOL_GROUPS):
            rows = slice(g * POOL_GROUP_W, (g + 1) * POOL_GROUP_W)
            scaled = (pw_ref[g] * ps_ref[g]).astype(BF16)
            wpool_ref[rows, :] = _dot(scaled, wpool_ref[rows, :]).astype(BF16)

        _stage_weight_bf16(wall_hbm.at[:, pl.ds(2 * D_MODEL, 2 * D_MODEL)], wg_ref, col_stage, col_sem, axis=1)
        _stage_weight_bf16(wo_hbm, wo_ref, col_stage, col_sem, axis=1)

        def store_regrouped(kv, rows_f32):
            for j in range(GROUP):
                dst = (j * N_KV_HEADS + kv) * HEAD_DIM
                wap_ref[dst:dst + HEAD_DIM, :] = rows_f32[j * HEAD_DIM:(j + 1) * HEAD_DIM, :].astype(BF16)

        _stage_weight_bf16(wap_hbm, None, row_stage, row_sem, axis=0, store=store_regrouped)

    t_idx = pl.program_id(1)
    n_tiles = pl.num_programs(1)
    m = mod_ref[pl.ds(pl.program_id(0), 1), :]
    sh = m[:, 0:D_MODEL]
    gs = g1_ref[...] * (1.0 + m[:, D_MODEL:2 * D_MODEL])
    gate1 = m[:, 2 * D_MODEL:3 * D_MODEL]

    keep_prev = (t_idx > 0).astype(F32)
    keep_next = (t_idx < n_tiles - 1).astype(F32)
    uext_ref[0:POOL_HALO] = up_ref[0] * keep_prev
    uext_ref[POOL_HALO:POOL_HALO + tile] = uc_ref[0]
    uext_ref[POOL_HALO + tile:] = un_ref[0] * keep_next
    edge_rows = lax.broadcasted_iota(jnp.int32, (POOL_HALO, POOL_GROUP_W), 0)

    def attn_dot(r):
        return _dot(attn_ref[0, r * sub:(r + 1) * sub, :], wap_ref[...])

    def gate_dot(r):
        h = _rms_modulate(x_ref[0, r * sub:(r + 1) * sub, :], gs, sh).astype(BF16)
        return jax.nn.sigmoid(_dot(h, wg_ref[...]) + gb_ref[...])

    def pool_dot(r):
        r0 = r * sub
        pos_first = t_idx * tile + r0 + edge_rows
        pos_last = pos_first + (sub - POOL_HALO)
        diffs = []
        for g, w in enumerate(POOL_WINDOWS):
            cols = slice(g * POOL_GROUP_W, (g + 1) * POOL_GROUP_W)
            acc = uext_ref[pl.ds(r0, sub + 2 * POOL_HALO), cols]
            k = 1
            while k < w:
                acc = acc[:acc.shape[0] - k] + acc[k:]
                k *= 2
            win = acc[POOL_HALO - w // 2:POOL_HALO - w // 2 + sub]

            def inv_count(pos):
                return 1.0 / (jnp.minimum(pos + w // 2, seq) - jnp.maximum(pos - w // 2, 0)).astype(F32)

            pooled_avg = jnp.concatenate([
                win[:POOL_HALO] * inv_count(pos_first),
                win[POOL_HALO:sub - POOL_HALO] * (1.0 / w),
                win[sub - POOL_HALO:] * inv_count(pos_last)], axis=0)
            diffs.append((pooled_avg - uext_ref[pl.ds(r0 + POOL_HALO, sub), cols]).astype(BF16))
        return _dot(jnp.concatenate(diffs, axis=1), wpool_ref[...])

    def output(r, a, gates, p):
        rows = slice(r * sub, (r + 1) * sub)
        merged = (gates[:, :D_MODEL] * a + gates[:, D_MODEL:] * p).astype(BF16)
        o_ref[0, rows, :] = x_ref[0, rows, :] + gate1 * _dot(merged, wo_ref[...])

    n_sub = tile // sub
    a = [attn_dot(r) for r in range(n_sub)]
    p = [pool_dot(r) for r in range(n_sub)]
    gates = [gate_dot(r) for r in range(n_sub)]
    for r in range(n_sub):
        output(r, a[r], gates[r], p[r])


def _merge_call(x, mod, norm1_g, attn, u, w_all, gate_b, w_ap, pool_w, pool_scale3, w_pool_proj, w_out, tile):
    B, S, D = x.shape
    r = tile // POOL_HALO
    n_halo = S // POOL_HALO
    row = lambda b, t: (b, t, 0)
    return pl.pallas_call(
        functools.partial(_merge_kernel, seq=S, tile=tile, sub=SUB_TILE),
        grid=(B, S // tile),
        in_specs=[
            pl.BlockSpec((1, tile, D), row),
            _const_spec(mod.shape),
            _const_spec((1, D)),
            pl.BlockSpec((1, tile, Q_W), row),
            pl.BlockSpec((1, POOL_HALO, POOL_W), lambda b, t: (b, jnp.maximum(t * r - 1, 0), 0)),
            pl.BlockSpec((1, tile, POOL_W), row),
            pl.BlockSpec((1, POOL_HALO, POOL_W), lambda b, t: (b, jnp.minimum((t + 1) * r, n_halo - 1), 0)),
            pl.BlockSpec(memory_space=pl.ANY),
            _const_spec(gate_b.shape),
            pl.BlockSpec(memory_space=pl.ANY),
            _const_spec(pool_w.shape),
            _const_spec(pool_scale3.shape),
            pl.BlockSpec(memory_space=pl.ANY),
            pl.BlockSpec(memory_space=pl.ANY),
        ],
        out_specs=pl.BlockSpec((1, tile, D), row),
        out_shape=jax.ShapeDtypeStruct((B, S, D), F32),
        scratch_shapes=[
            pltpu.VMEM((tile + 2 * POOL_HALO, POOL_W), F32),
            pltpu.VMEM((D, 2 * D), BF16),
            pltpu.VMEM(w_ap.shape, BF16),
            pltpu.VMEM(w_out.shape, BF16),
            pltpu.VMEM(w_pool_proj.shape, BF16),
            pltpu.VMEM((STAGE_SLOTS, D, STAGE_CHUNK), F32),
            pltpu.VMEM((STAGE_SLOTS, STAGE_CHUNK, D), F32),
            pltpu.SemaphoreType.DMA((STAGE_SLOTS,)),
            pltpu.SemaphoreType.DMA((STAGE_SLOTS,)),
        ],
        compiler_params=_sequential_params(2),
        name="merge",
    )(x, mod, norm1_g, attn, u, u, u, w_all, gate_b, w_ap, pool_w, pool_scale3, w_pool_proj, w_out)


def _ffn_kernel(x_ref, mod_ref, g2_ref, wup_hbm, wdn_hbm, o_ref,
                wup_ref, wdn_ref, col_stage, row_stage, col_sem, row_sem, hmid_ref, *, d_ff, tile, sub):
    @pl.when((pl.program_id(0) == 0) & (pl.program_id(1) == 0))
    def _():
        _stage_weight_bf16(wup_hbm, wup_ref, col_stage, col_sem, axis=1)
        _stage_weight_bf16(wdn_hbm, wdn_ref, row_stage, row_sem, axis=0)

    m = mod_ref[pl.ds(pl.program_id(0), 1), :]
    sh = m[:, 3 * D_MODEL:4 * D_MODEL]
    gs = g2_ref[...] * (1.0 + m[:, 4 * D_MODEL:5 * D_MODEL])
    gate2 = m[:, 5 * D_MODEL:6 * D_MODEL]

    def up(r):
        rows = slice(r * sub, (r + 1) * sub)
        h = _rms_modulate(x_ref[0, rows, :], gs, sh).astype(BF16)
        for c in range(d_ff // MXU_TILE):
            a = _dot(h, wup_ref[:, c * MXU_TILE:(c + 1) * MXU_TILE])
            b = _dot(h, wup_ref[:, d_ff + c * MXU_TILE:d_ff + (c + 1) * MXU_TILE])
            hmid_ref[rows, c * MXU_TILE:(c + 1) * MXU_TILE] = (a * jax.nn.sigmoid(a) * b).astype(BF16)

    def down(r):
        rows = slice(r * sub, (r + 1) * sub)
        o_ref[0, rows, :] = x_ref[0, rows, :] + gate2 * _dot(hmid_ref[rows, :], wdn_ref[...])

    n_sub = tile // sub
    up(0)
    for r in range(1, n_sub):
        up(r)
        down(r - 1)
    down(n_sub - 1)


def _ffn_call(x, mod, norm2_g, w_up, w_down, tile):
    B, S, D = x.shape
    d_ff = w_down.shape[0]
    row = lambda b, t: (b, t, 0)
    return pl.pallas_call(
        functools.partial(_ffn_kernel, d_ff=d_ff, tile=tile, sub=SUB_TILE),
        grid=(B, S // tile),
        in_specs=[
            pl.BlockSpec((1, tile, D), row),
            _const_spec(mod.shape),
            _const_spec((1, D)),
            pl.BlockSpec(memory_space=pl.ANY),
            pl.BlockSpec(memory_space=pl.ANY),
        ],
        out_specs=pl.BlockSpec((1, tile, D), row),
        out_shape=jax.ShapeDtypeStruct((B, S, D), F32),
        scratch_shapes=[
            pltpu.VMEM(w_up.shape, BF16),
            pltpu.VMEM(w_down.shape, BF16),
            pltpu.VMEM((STAGE_SLOTS, D, STAGE_CHUNK), F32),
            pltpu.VMEM((STAGE_SLOTS, STAGE_CHUNK, D), F32),
            pltpu.SemaphoreType.DMA((STAGE_SLOTS,)),
            pltpu.SemaphoreType.DMA((STAGE_SLOTS,)),
            pltpu.VMEM((tile, d_ff), BF16),
        ],
        compiler_params=_sequential_params(2),
        name="ffn",
    )(x, mod, norm2_g, w_up, w_down)


def _rope_tables(seq):
    t = np.arange(seq)
    row = (t // GRID_W).astype(np.float64)
    col = (t % GRID_W).astype(np.float64)
    half = HEAD_DIM // 2
    inv_freq = 1.0 / (ROPE_THETA ** (np.arange(0, half, 2, dtype=np.float64) / half))
    ang_r = row[:, None] * inv_freq
    ang_c = col[:, None] * inv_freq
    cos = np.concatenate([np.cos(ang_r)] * 2 + [np.cos(ang_c)] * 2, axis=1)
    sin = np.concatenate([-np.sin(ang_r), np.sin(ang_r), -np.sin(ang_c), np.sin(ang_c)], axis=1)
    reps = (1, LANES // HEAD_DIM)
    return jnp.asarray(np.tile(cos, reps), dtype=F32), jnp.asarray(np.tile(sin, reps), dtype=F32)


def _band_bias():
    i = np.arange(BLOCK)[:, None]
    j = np.arange(3 * BLOCK)[None, :]
    band = np.abs(j - BLOCK - i) <= WINDOW
    first = band & (j >= BLOCK)
    last = band & (j < 2 * BLOCK)
    masks = np.stack([band, first, last])
    return jnp.asarray(np.where(masks, 0.0, NEG_BIG), dtype=F32)


def kernel(x, c, ctx, c_ctx, mod_w, mod_b, norm1_g, norm2_g, w_in, gate_b, q_norm_g, k_norm_g,
           sink, pool_w, pool_scale, w_attn_proj, w_pool_proj, w_out, w_up, w_down):
    B, S, D = x.shape
    depth = mod_w.shape[0]
    cos_t, sin_t = _rope_tables(S)
    bias = _band_bias()
    ones_bd = jnp.asarray(np.kron(np.eye(MXU_TILE // HEAD_DIM), np.ones((HEAD_DIM, HEAD_DIM))), dtype=BF16)
    assert depth == 1, "context-stream update between layers is not implemented"
    assert D == D_MODEL and S % ROW_TILE == 0 and B + 1 <= F32_SUBLANES

    for l in range(depth):
        wl = w_in[l]
        g1 = norm1_g[l][None, :]
        k_gain = k_norm_g[l][None, :]
        q_gain = q_norm_g[l][None, :]

        mod, kc, vc = _mod_ctx_call(c, c_ctx[None, :], mod_w[l], mod_b[l][None, :], ctx, g1, wl, k_gain, ones_bd)
        q, k, v, u = _inproj_call(x, mod, g1, wl, q_gain, k_gain, ones_bd, cos_t, sin_t, tile=ROW_TILE)
        attn = _attn_call(sink[l], q, k, v, kc, vc, bias, tile=ROW_TILE)
        x = _merge_call(
            x, mod, g1, attn, u, wl, gate_b[l][None, :], w_attn_proj[l], pool_w[l],
            pool_scale[l].reshape(POOL_GROUPS, 1, POOL_GROUP_W), w_pool_proj[l], w_out[l], tile=ROW_TILE)
        x = _ffn_call(x, mod, norm2_g[l][None, :], w_up[l], w_down[l], tile=ROW_TILE)
    return x
```
